```python
import math
import jax, jax.numpy as jnp
from jax import lax
import numpy as np

D_MODEL = 1024
BATCH = 8
SEQ = 8192
DEPTH = 2

CHUNK = 64
GM_BLOCK = 128
GM_WIDTH = D_MODEL // 2
GM_GROUPS = 4
GM_GROUP_CH = GM_WIDTH // GM_GROUPS
MLA_HEADS = 8
MLA_Q_RANK = D_MODEL // 4
MLA_KV_RANK = D_MODEL // 8
MLA_NOPE = 64
MLA_ROPE = 32
MLA_V = 64
MLA_WIDTH = MLA_HEADS * MLA_V
Q_BLOCK = 128
ROPE_BASE = 10000.0
LRU_WIDTH = D_MODEL // 2
LRU_BLOCKS = 8
LRU_BLOCK_W = LRU_WIDTH // LRU_BLOCKS
CONV_W = 4
LRU_C = 8.0
N_BRANCH = 3
BRANCH_W = GM_WIDTH
D_FF = 4 * D_MODEL
ALPHA = (2.0 * DEPTH) ** 0.25
BETA = (8.0 * DEPTH) ** -0.25
LN_EPS = 1e-5
RMS_EPS = 1e-6
OFF_GM = 0
OFF_QLAT = OFF_GM + 2 * GM_WIDTH
OFF_KVLAT = OFF_QLAT + MLA_Q_RANK
OFF_KROPE = OFF_KVLAT + MLA_KV_RANK
OFF_LRU_X = OFF_KROPE + MLA_ROPE
OFF_LRU_G = OFF_LRU_X + LRU_WIDTH
OFF_GATE = OFF_LRU_G + LRU_WIDTH
N_IN = OFF_GATE + N_BRANCH * D_MODEL

kernel_name = 'hybrid_gmlp_mla_rglru_deepnorm_adaln'


def layer_norm(x):
    xf = x.astype(jnp.float32)
    mu = jnp.mean(xf, axis=-1, keepdims=True)
    var = jnp.mean(jnp.square(xf - mu), axis=-1, keepdims=True)
    return ((xf - mu) * lax.rsqrt(var + LN_EPS)).astype(x.dtype)


def rms_norm(x, g):
    xf = x.astype(jnp.float32)
    y = xf * lax.rsqrt(jnp.mean(jnp.square(xf), axis=-1, keepdims=True) + RMS_EPS)
    return y.astype(x.dtype) * g


def rope_tables(seq, dtype):
    pos = jnp.arange(seq, dtype=jnp.float32)
    inv = ROPE_BASE ** (-jnp.arange(0, MLA_ROPE, 2, dtype=jnp.float32) / MLA_ROPE)
    ang = pos[:, None] * inv[None, :]
    return jnp.cos(ang).astype(dtype), jnp.sin(ang).astype(dtype)


def apply_rope(x, cos, sin):
    x1, x2 = jnp.split(x, 2, axis=-1)
    return jnp.concatenate([x1 * cos - x2 * sin, x2 * cos + x1 * sin], axis=-1)


def gmlp_mix(z, ln_g, ln_b, ws, bs):
    B, S, _ = z.shape
    z = jax.nn.gelu(z)
    u, v = jnp.split(z, 2, axis=-1)
    v = layer_norm(v) * ln_g + ln_b
    v = v.reshape(B, S // GM_BLOCK, GM_BLOCK, GM_GROUPS, GM_GROUP_CH)
    chunk_id = jnp.arange(GM_BLOCK) // CHUNK
    mask = chunk_id[:, None] >= chunk_id[None, :]
    w = jnp.where(mask[None], ws, 0.0)
    f = jnp.einsum('gij,bnjgc->bnigc', w, v) + bs.T[None, None, :, :, None]
    return u * f.reshape(B, S, GM_WIDTH)


def mla_mix(q_lat, kv_lat, k_rope, q_norm_g, w_uq, kv_norm_g, w_ukv, cos, sin):
    B, S, _ = q_lat.shape
    q = (rms_norm(q_lat, q_norm_g) @ w_uq).reshape(B, S, MLA_HEADS, MLA_NOPE + MLA_ROPE)
    q_nope = q[..., :MLA_NOPE]
    q_rope = apply_rope(q[..., MLA_NOPE:], cos[:, None, :], sin[:, None, :])
    kv = (rms_norm(kv_lat, kv_norm_g) @ w_ukv).reshape(B, S, MLA_HEADS, MLA_NOPE + MLA_V)
    k_nope = kv[..., :MLA_NOPE]
    v = kv[..., MLA_NOPE:]
    k_rope = apply_rope(k_rope, cos, sin)
    scale = (MLA_NOPE + MLA_ROPE) ** -0.5
    nq = S // Q_BLOCK
    k_chunk = jnp.arange(S) // CHUNK

    def to_blocks(t):
        return t.reshape(B, nq, Q_BLOCK, *t.shape[2:]).swapaxes(0, 1)

    def attend(args):
        qn, qr, blk = args
        s = (jnp.einsum('bqhd,bkhd->bhqk', qn, k_nope)
             + jnp.einsum('bqhr,bkr->bhqk', qr, k_rope)).astype(jnp.float32) * scale
        q_chunk = (blk * Q_BLOCK + jnp.arange(Q_BLOCK)) // CHUNK
        mask = k_chunk[None, :] <= q_chunk[:, None]
        s = jnp.where(mask[None, None], s, -jnp.inf)
        p = jax.nn.softmax(s, axis=-1).astype(v.dtype)
        return jnp.einsum('bhqk,bkhd->bqhd', p, v)

    o = lax.map(attend, (to_blocks(q_nope), to_blocks(q_rope), jnp.arange(nq)))
    return o.swapaxes(0, 1).reshape(B, S, MLA_WIDTH)


def rglru_mix(xb, gb, conv_w, conv_b, wr, br, wi, bi, lam):
    B, S, _ = xb.shape
    xp = jnp.pad(xb, ((0, 0), (CONV_W - 1, 0), (0, 0)))
    xc = conv_b + sum(xp[:, k:k + S] * conv_w[k] for k in range(CONV_W))
    xblk = xc.reshape(B, S, LRU_BLOCKS, LRU_BLOCK_W)
    r = jax.nn.sigmoid(jnp.einsum('bsnc,ncd->bsnd', xblk, wr).reshape(B, S, LRU_WIDTH) + br)
    i = jax.nn.sigmoid(jnp.einsum('bsnc,ncd->bsnd', xblk, wi).reshape(B, S, LRU_WIDTH) + bi)
    log_a = -LRU_C * r.astype(jnp.float32) * jax.nn.softplus(-lam.astype(jnp.float32))
    a = jnp.exp(log_a)
    b = jnp.sqrt(-jnp.expm1(2.0 * log_a)) * (i * xc).astype(jnp.float32)

    def combine(lhs, rhs):
        a1, b1 = lhs
        a2, b2 = rhs
        return a1 * a2, a2 * b1 + b2

    _, h = lax.associative_scan(combine, (a, b), axis=1)
    return h.astype(xb.dtype) * jax.nn.gelu(gb)


def _fwd_setup_inputs(seed: int = 0) -> dict:
    key = jax.random.key(seed)
    ks = iter(jax.random.split(key, 40))
    L = DEPTH

    def nrm(shape, scale):
        return jax.random.normal(next(ks), shape, jnp.float32) * scale

    a0 = jax.random.uniform(next(ks), (L, LRU_WIDTH), jnp.float32, 0.9, 0.999)
    return dict(
        x=nrm((BATCH, SEQ, D_MODEL), 1.0),
        c=nrm((BATCH, D_MODEL), 1.0),
        ada_w=nrm((L, D_MODEL, 6 * D_MODEL), 0.1 * D_MODEL ** -0.5),
        ada_b=nrm((L, 6 * D_MODEL), 0.01),
        in_w=nrm((L, D_MODEL, N_IN), D_MODEL ** -0.5),
        in_b=nrm((L, N_IN), 0.01),
        gm_ln_g=1.0 + nrm((L, GM_WIDTH), 0.05),
        gm_ln_b=nrm((L, GM_WIDTH), 0.01),
        gm_ws=nrm((L, GM_GROUPS, GM_BLOCK, GM_BLOCK), GM_BLOCK ** -0.5),
        gm_bs=1.0 + nrm((L, GM_GROUPS, GM_BLOCK), 0.05),
        mla_qnorm_g=1.0 + nrm((L, MLA_Q_RANK), 0.05),
        mla_wuq=nrm((L, MLA_Q_RANK, MLA_HEADS * (MLA_NOPE + MLA_ROPE)), MLA_Q_RANK ** -0.5),
        mla_kvnorm_g=1.0 + nrm((L, MLA_KV_RANK), 0.05),
        mla_wukv=nrm((L, MLA_KV_RANK, MLA_HEADS * (MLA_NOPE + MLA_V)), MLA_KV_RANK ** -0.5),
        lru_conv_w=nrm((L, CONV_W, LRU_WIDTH), CONV_W ** -0.5),
        lru_conv_b=nrm((L, LRU_WIDTH), 0.01),
        lru_wr=nrm((L, LRU_BLOCKS, LRU_BLOCK_W, LRU_BLOCK_W), LRU_BLOCK_W ** -0.5),
        lru_br=nrm((L, LRU_WIDTH), 0.01),
        lru_wi=nrm((L, LRU_BLOCKS, LRU_BLOCK_W, LRU_BLOCK_W), LRU_BLOCK_W ** -0.5),
        lru_bi=nrm((L, LRU_WIDTH), 0.01),
        lru_lambda=jnp.log(a0) - jnp.log1p(-a0),
        branch_w=nrm((L, N_BRANCH, BRANCH_W, D_MODEL), BETA * BRANCH_W ** -0.5),
        mix_out_w=nrm((L, D_MODEL, D_MODEL), BETA * D_MODEL ** -0.5),
        ffn_w1=nrm((L, D_MODEL, D_FF), D_MODEL ** -0.5),
        ffn_b1=nrm((L, D_FF), 0.01),
        ffn_w2=nrm((L, D_FF, D_MODEL), BETA * D_FF ** -0.5),
        ffn_b2=nrm((L, D_MODEL), 0.01),
        ln_g=1.0 + nrm((L, 2, D_MODEL), 0.05),
        ln_b=nrm((L, 2, D_MODEL), 0.01),
    )


def _fwd_reference(x, c, ada_w, ada_b, in_w, in_b, gm_ln_g, gm_ln_b, gm_ws, gm_bs,
              mla_qnorm_g, mla_wuq, mla_kvnorm_g, mla_wukv,
              lru_conv_w, lru_conv_b, lru_wr, lru_br, lru_wi, lru_bi, lru_lambda,
              branch_w, mix_out_w, ffn_w1, ffn_b1, ffn_w2, ffn_b2, ln_g, ln_b):
    B, S, D = x.shape
    cos, sin = rope_tables(S, x.dtype)
    c_act = jax.nn.silu(c)
    for l in range(DEPTH):
        mod = (c_act @ ada_w[l] + ada_b[l])[:, None, :]
        sh1, sc1, g1, sh2, sc2, g2 = jnp.split(mod, 6, axis=-1)
        h = layer_norm(x) * (1.0 + sc1) + sh1
        z = h @ in_w[l] + in_b[l]
        y_a = gmlp_mix(z[..., OFF_GM:OFF_QLAT], gm_ln_g[l], gm_ln_b[l], gm_ws[l], gm_bs[l])
        y_b = mla_mix(z[..., OFF_QLAT:OFF_KVLAT], z[..., OFF_KVLAT:OFF_KROPE],
                      z[..., OFF_KROPE:OFF_LRU_X], mla_qnorm_g[l], mla_wuq[l],
                      mla_kvnorm_g[l], mla_wukv[l], cos, sin)
        y_c = rglru_mix(z[..., OFF_LRU_X:OFF_LRU_G], z[..., OFF_LRU_G:OFF_GATE],
                        lru_conv_w[l], lru_conv_b[l], lru_wr[l], lru_br[l],
                        lru_wi[l], lru_bi[l], lru_lambda[l])
        gates = jax.nn.sigmoid(z[..., OFF_GATE:]).reshape(B, S, N_BRANCH, D)
        merged = sum(gates[:, :, n] * (y @ branch_w[l, n]) for n, y in enumerate((y_a, y_b, y_c)))
        mix = merged @ mix_out_w[l]
        x = layer_norm(ALPHA * x + (1.0 + g1) * mix) * ln_g[l, 0] + ln_b[l, 0]
        h = layer_norm(x) * (1.0 + sc2) + sh2
        f = jnp.square(jax.nn.relu(h @ ffn_w1[l] + ffn_b1[l])) @ ffn_w2[l] + ffn_b2[l]
        x = layer_norm(ALPHA * x + (1.0 + g2) * f) * ln_g[l, 1] + ln_b[l, 1]
    return x


import jax as _jax
import jax.numpy as _jnp

TWIN_FORMAT = 'train_step'
FWD_PARAMS = ['x', 'c', 'ada_w', 'ada_b', 'in_w', 'in_b', 'gm_ln_g', 'gm_ln_b', 'gm_ws', 'gm_bs', 'mla_qnorm_g', 'mla_wuq', 'mla_kvnorm_g', 'mla_wukv', 'lru_conv_w', 'lru_conv_b', 'lru_wr', 'lru_br', 'lru_wi', 'lru_bi', 'lru_lambda', 'branch_w', 'mix_out_w', 'ffn_w1', 'ffn_b1', 'ffn_w2', 'ffn_b2', 'ln_g', 'ln_b']
TWIN_WEIGHTS = ['ada_w', 'ada_b', 'in_w', 'in_b', 'gm_ln_g', 'gm_ln_b', 'gm_ws', 'gm_bs', 'mla_qnorm_g', 'mla_wuq', 'mla_kvnorm_g', 'mla_wukv', 'lru_conv_w', 'lru_conv_b', 'lru_wr', 'lru_br', 'lru_wi', 'lru_bi', 'lru_lambda', 'branch_w', 'mix_out_w', 'ffn_w1', 'ffn_b1', 'ffn_w2', 'ffn_b2', 'ln_g', 'ln_b']
TWIN_DIFF_INPUT = 'x'
TWIN_INPUTS = ['x', 'c', 'ada_w', 'ada_b', 'in_w', 'in_b', 'gm_ln_g', 'gm_ln_b', 'gm_ws', 'gm_bs', 'mla_qnorm_g', 'mla_wuq', 'mla_kvnorm_g', 'mla_wukv', 'lru_conv_w', 'lru_conv_b', 'lru_wr', 'lru_br', 'lru_wi', 'lru_bi', 'lru_lambda', 'branch_w', 'mix_out_w', 'ffn_w1', 'ffn_b1', 'ffn_w2', 'ffn_b2', 'ln_g', 'ln_b', 'loss_target', 'm_ada_w', 'm_ada_b', 'm_in_w', 'm_in_b', 'm_gm_ln_g', 'm_gm_ln_b', 'm_gm_ws', 'm_gm_bs', 'm_mla_qnorm_g', 'm_mla_wuq', 'm_mla_kvnorm_g', 'm_mla_wukv', 'm_lru_conv_w', 'm_lru_conv_b', 'm_lru_wr', 'm_lru_br', 'm_lru_wi', 'm_lru_bi', 'm_lru_lambda', 'm_branch_w', 'm_mix_out_w', 'm_ffn_w1', 'm_ffn_b1', 'm_ffn_w2', 'm_ffn_b2', 'm_ln_g', 'm_ln_b', 'v_ada_w', 'v_ada_b', 'v_in_w', 'v_in_b', 'v_gm_ln_g', 'v_gm_ln_b', 'v_gm_ws', 'v_gm_bs', 'v_mla_qnorm_g', 'v_mla_wuq', 'v_mla_kvnorm_g', 'v_mla_wukv', 'v_lru_conv_w', 'v_lru_conv_b', 'v_lru_wr', 'v_lru_br', 'v_lru_wi', 'v_lru_bi', 'v_lru_lambda', 'v_branch_w', 'v_mix_out_w', 'v_ffn_w1', 'v_ffn_b1', 'v_ffn_w2', 'v_ffn_b2', 'v_ln_g', 'v_ln_b']
TWIN_OUTPUTS = ['loss', 'grad_x', 'grad_ada_w', 'grad_ada_b', 'grad_in_w', 'grad_in_b', 'grad_gm_ln_g', 'grad_gm_ln_b', 'grad_gm_ws', 'grad_gm_bs', 'grad_mla_qnorm_g', 'grad_mla_wuq', 'grad_mla_kvnorm_g', 'grad_mla_wukv', 'grad_lru_conv_w', 'grad_lru_conv_b', 'grad_lru_wr', 'grad_lru_br', 'grad_lru_wi', 'grad_lru_bi', 'grad_lru_lambda', 'grad_branch_w', 'grad_mix_out_w', 'grad_ffn_w1', 'grad_ffn_b1', 'grad_ffn_w2', 'grad_ffn_b2', 'grad_ln_g', 'grad_ln_b', 'delta_ada_w', 'delta_ada_b', 'delta_in_w', 'delta_in_b', 'delta_gm_ln_g', 'delta_gm_ln_b', 'delta_gm_ws', 'delta_gm_bs', 'delta_mla_qnorm_g', 'delta_mla_wuq', 'delta_mla_kvnorm_g', 'delta_mla_wukv', 'delta_lru_conv_w', 'delta_lru_conv_b', 'delta_lru_wr', 'delta_lru_br', 'delta_lru_wi', 'delta_lru_bi', 'delta_lru_lambda', 'delta_branch_w', 'delta_mix_out_w', 'delta_ffn_w1', 'delta_ffn_b1', 'delta_ffn_w2', 'delta_ffn_b2', 'delta_ln_g', 'delta_ln_b', 'new_m_ada_w', 'new_m_ada_b', 'new_m_in_w', 'new_m_in_b', 'new_m_gm_ln_g', 'new_m_gm_ln_b', 'new_m_gm_ws', 'new_m_gm_bs', 'new_m_mla_qnorm_g', 'new_m_mla_wuq', 'new_m_mla_kvnorm_g', 'new_m_mla_wukv', 'new_m_lru_conv_w', 'new_m_lru_conv_b', 'new_m_lru_wr', 'new_m_lru_br', 'new_m_lru_wi', 'new_m_lru_bi', 'new_m_lru_lambda', 'new_m_branch_w', 'new_m_mix_out_w', 'new_m_ffn_w1', 'new_m_ffn_b1', 'new_m_ffn_w2', 'new_m_ffn_b2', 'new_m_ln_g', 'new_m_ln_b', 'new_v_ada_w', 'new_v_ada_b', 'new_v_in_w', 'new_v_in_b', 'new_v_gm_ln_g', 'new_v_gm_ln_b', 'new_v_gm_ws', 'new_v_gm_bs', 'new_v_mla_qnorm_g', 'new_v_mla_wuq', 'new_v_mla_kvnorm_g', 'new_v_mla_wukv', 'new_v_lru_conv_w', 'new_v_lru_conv_b', 'new_v_lru_wr', 'new_v_lru_br', 'new_v_lru_wi', 'new_v_lru_bi', 'new_v_lru_lambda', 'new_v_branch_w', 'new_v_mix_out_w', 'new_v_ffn_w1', 'new_v_ffn_b1', 'new_v_ffn_w2', 'new_v_ffn_b2', 'new_v_ln_g', 'new_v_ln_b']
TWIN_LEAF_KINDS = {'loss': 'loss', 'grad_x': 'grad_x', 'grad_ada_w': 'grad_w', 'grad_ada_b': 'grad_w', 'grad_in_w': 'grad_w', 'grad_in_b': 'grad_w', 'grad_gm_ln_g': 'grad_w', 'grad_gm_ln_b': 'grad_w', 'grad_gm_ws': 'grad_w', 'grad_gm_bs': 'grad_w', 'grad_mla_qnorm_g': 'grad_w', 'grad_mla_wuq': 'grad_w', 'grad_mla_kvnorm_g': 'grad_w', 'grad_mla_wukv': 'grad_w', 'grad_lru_conv_w': 'grad_w', 'grad_lru_conv_b': 'grad_w', 'grad_lru_wr': 'grad_w', 'grad_lru_br': 'grad_w', 'grad_lru_wi': 'grad_w', 'grad_lru_bi': 'grad_w', 'grad_lru_lambda': 'grad_w', 'grad_branch_w': 'grad_w', 'grad_mix_out_w': 'grad_w', 'grad_ffn_w1': 'grad_w', 'grad_ffn_b1': 'grad_w', 'grad_ffn_w2': 'grad_w', 'grad_ffn_b2': 'grad_w', 'grad_ln_g': 'grad_w', 'grad_ln_b': 'grad_w', 'delta_ada_w': 'delta_w', 'delta_ada_b': 'delta_w', 'delta_in_w': 'delta_w', 'delta_in_b': 'delta_w', 'delta_gm_ln_g': 'delta_w', 'delta_gm_ln_b': 'delta_w', 'delta_gm_ws': 'delta_w', 'delta_gm_bs': 'delta_w', 'delta_mla_qnorm_g': 'delta_w', 'delta_mla_wuq': 'delta_w', 'delta_mla_kvnorm_g': 'delta_w', 'delta_mla_wukv': 'delta_w', 'delta_lru_conv_w': 'delta_w', 'delta_lru_conv_b': 'delta_w', 'delta_lru_wr': 'delta_w', 'delta_lru_br': 'delta_w', 'delta_lru_wi': 'delta_w', 'delta_lru_bi': 'delta_w', 'delta_lru_lambda': 'delta_w', 'delta_branch_w': 'delta_w', 'delta_mix_out_w': 'delta_w', 'delta_ffn_w1': 'delta_w', 'delta_ffn_b1': 'delta_w', 'delta_ffn_w2': 'delta_w', 'delta_ffn_b2': 'delta_w', 'delta_ln_g': 'delta_w', 'delta_ln_b': 'delta_w', 'new_m_ada_w': 'new_m', 'new_m_ada_b': 'new_m', 'new_m_in_w': 'new_m', 'new_m_in_b': 'new_m', 'new_m_gm_ln_g': 'new_m', 'new_m_gm_ln_b': 'new_m', 'new_m_gm_ws': 'new_m', 'new_m_gm_bs': 'new_m', 'new_m_mla_qnorm_g': 'new_m', 'new_m_mla_wuq': 'new_m', 'new_m_mla_kvnorm_g': 'new_m', 'new_m_mla_wukv': 'new_m', 'new_m_lru_conv_w': 'new_m', 'new_m_lru_conv_b': 'new_m', 'new_m_lru_wr': 'new_m', 'new_m_lru_br': 'new_m', 'new_m_lru_wi': 'new_m', 'new_m_lru_bi': 'new_m', 'new_m_lru_lambda': 'new_m', 'new_m_branch_w': 'new_m', 'new_m_mix_out_w': 'new_m', 'new_m_ffn_w1': 'new_m', 'new_m_ffn_b1': 'new_m', 'new_m_ffn_w2': 'new_m', 'new_m_ffn_b2': 'new_m', 'new_m_ln_g': 'new_m', 'new_m_ln_b': 'new_m', 'new_v_ada_w': 'new_v', 'new_v_ada_b': 'new_v', 'new_v_in_w': 'new_v', 'new_v_in_b': 'new_v', 'new_v_gm_ln_g': 'new_v', 'new_v_gm_ln_b': 'new_v', 'new_v_gm_ws': 'new_v', 'new_v_gm_bs': 'new_v', 'new_v_mla_qnorm_g': 'new_v', 'new_v_mla_wuq': 'new_v', 'new_v_mla_kvnorm_g': 'new_v', 'new_v_mla_wukv': 'new_v', 'new_v_lru_conv_w': 'new_v', 'new_v_lru_conv_b': 'new_v', 'new_v_lru_wr': 'new_v', 'new_v_lru_br': 'new_v', 'new_v_lru_wi': 'new_v', 'new_v_lru_bi': 'new_v', 'new_v_lru_lambda': 'new_v', 'new_v_branch_w': 'new_v', 'new_v_mix_out_w': 'new_v', 'new_v_ffn_w1': 'new_v', 'new_v_ffn_b1': 'new_v', 'new_v_ffn_w2': 'new_v', 'new_v_ffn_b2': 'new_v', 'new_v_ln_g': 'new_v', 'new_v_ln_b': 'new_v'}


def _forward(args):
    return _fwd_reference(*[args[k] for k in FWD_PARAMS])


def _output_shape():
    def fwd():
        inp = _fwd_setup_inputs(0)
        return _fwd_reference(*[inp[k] for k in FWD_PARAMS])
    out = _jax.eval_shape(fwd)
    return out.shape, out.dtype

N_MICROBATCH = 1
ADAM_LR = 0.001
ADAM_B1 = 0.9
ADAM_B2 = 0.999
ADAM_EPS = 1e-08
ADAM_WD = 0.01
ADAM_STEP = 10
PER_EXAMPLE_BATCH_AXIS = {'x': 0, 'c': 0, 'loss_target': 0}
SHARED_INPUTS = []
_WEIGHT_DTYPES = {'ada_w': _jnp.float32, 'ada_b': _jnp.float32, 'in_w': _jnp.float32, 'in_b': _jnp.float32, 'gm_ln_g': _jnp.float32, 'gm_ln_b': _jnp.float32, 'gm_ws': _jnp.float32, 'gm_bs': _jnp.float32, 'mla_qnorm_g': _jnp.float32, 'mla_wuq': _jnp.float32, 'mla_kvnorm_g': _jnp.float32, 'mla_wukv': _jnp.float32, 'lru_conv_w': _jnp.float32, 'lru_conv_b': _jnp.float32, 'lru_wr': _jnp.float32, 'lru_br': _jnp.float32, 'lru_wi': _jnp.float32, 'lru_bi': _jnp.float32, 'lru_lambda': _jnp.float32, 'branch_w': _jnp.float32, 'mix_out_w': _jnp.float32, 'ffn_w1': _jnp.float32, 'ffn_b1': _jnp.float32, 'ffn_w2': _jnp.float32, 'ffn_b2': _jnp.float32, 'ln_g': _jnp.float32, 'ln_b': _jnp.float32}
MOMENT_SCALE = {'ada_w': 9.827396e-02, 'ada_b': 3.831485e-01, 'in_w': 1.565412e-02, 'in_b': 6.855825e-02, 'gm_ln_g': 1.974100e-02, 'gm_ln_b': 2.043233e-02, 'gm_ws': 2.014922e-02, 'gm_bs': 2.337807e-02, 'mla_qnorm_g': 7.577571e-03, 'mla_wuq': 4.456640e-03, 'mla_kvnorm_g': 2.768482e-02, 'mla_wukv': 1.167406e-02, 'lru_conv_w': 2.426515e-02, 'lru_conv_b': 1.401981e-01, 'lru_wr': 5.100124e-03, 'lru_br': 6.407591e-03, 'lru_wi': 9.516440e-03, 'lru_bi': 9.222476e-03, 'lru_lambda': 1.376321e-02, 'branch_w': 4.345949e-02, 'mix_out_w': 7.598435e-02, 'ffn_w1': 6.257690e-02, 'ffn_b1': 1.726060e-01, 'ffn_w2': 4.193381e-01, 'ffn_b2': 9.559919e-01, 'ln_g': 3.276372e+01, 'ln_b': 7.589380e+00}


def _to_microbatches(a, axis):
    t = _jnp.moveaxis(a, axis, 0)
    t = t.reshape((N_MICROBATCH, t.shape[0] // N_MICROBATCH) + t.shape[1:])
    return _jnp.moveaxis(t, 1, axis + 1)


def setup_inputs(seed: int = 0) -> dict:
    inp = _fwd_setup_inputs(seed)
    key = _jax.random.fold_in(_jax.random.key(seed), 7919)
    shape, _ = _output_shape()
    out = dict(inp)
    out["loss_target"] = _jax.random.normal(_jax.random.fold_in(key, 0), shape, _jnp.float32)
    for i, name in enumerate(TWIN_WEIGHTS):
        w = inp[name].astype(_jnp.float32)
        if MOMENT_SCALE is None:
            s = _jnp.sqrt(_jnp.mean(_jnp.square(w)) + 1e-30)
        else:
            s = MOMENT_SCALE[name]
        km, kv = _jax.random.split(_jax.random.fold_in(key, i + 1))
        out[name] = w
        out["m_" + name] = s * _jax.random.normal(km, w.shape, _jnp.float32)
        out["v_" + name] = (s * s) * _jax.random.uniform(kv, w.shape, _jnp.float32, 0.5, 1.5)
    if N_MICROBATCH > 1:
        for name, axis in PER_EXAMPLE_BATCH_AXIS.items():
            out[name] = _to_microbatches(out[name], axis)
    return {'x': out['x'], 'c': out['c'], 'ada_w': out['ada_w'], 'ada_b': out['ada_b'], 'in_w': out['in_w'], 'in_b': out['in_b'], 'gm_ln_g': out['gm_ln_g'], 'gm_ln_b': out['gm_ln_b'], 'gm_ws': out['gm_ws'], 'gm_bs': out['gm_bs'], 'mla_qnorm_g': out['mla_qnorm_g'], 'mla_wuq': out['mla_wuq'], 'mla_kvnorm_g': out['mla_kvnorm_g'], 'mla_wukv': out['mla_wukv'], 'lru_conv_w': out['lru_conv_w'], 'lru_conv_b': out['lru_conv_b'], 'lru_wr': out['lru_wr'], 'lru_br': out['lru_br'], 'lru_wi': out['lru_wi'], 'lru_bi': out['lru_bi'], 'lru_lambda': out['lru_lambda'], 'branch_w': out['branch_w'], 'mix_out_w': out['mix_out_w'], 'ffn_w1': out['ffn_w1'], 'ffn_b1': out['ffn_b1'], 'ffn_w2': out['ffn_w2'], 'ffn_b2': out['ffn_b2'], 'ln_g': out['ln_g'], 'ln_b': out['ln_b'], 'loss_target': out['loss_target'], 'm_ada_w': out['m_ada_w'], 'm_ada_b': out['m_ada_b'], 'm_in_w': out['m_in_w'], 'm_in_b': out['m_in_b'], 'm_gm_ln_g': out['m_gm_ln_g'], 'm_gm_ln_b': out['m_gm_ln_b'], 'm_gm_ws': out['m_gm_ws'], 'm_gm_bs': out['m_gm_bs'], 'm_mla_qnorm_g': out['m_mla_qnorm_g'], 'm_mla_wuq': out['m_mla_wuq'], 'm_mla_kvnorm_g': out['m_mla_kvnorm_g'], 'm_mla_wukv': out['m_mla_wukv'], 'm_lru_conv_w': out['m_lru_conv_w'], 'm_lru_conv_b': out['m_lru_conv_b'], 'm_lru_wr': out['m_lru_wr'], 'm_lru_br': out['m_lru_br'], 'm_lru_wi': out['m_lru_wi'], 'm_lru_bi': out['m_lru_bi'], 'm_lru_lambda': out['m_lru_lambda'], 'm_branch_w': out['m_branch_w'], 'm_mix_out_w': out['m_mix_out_w'], 'm_ffn_w1': out['m_ffn_w1'], 'm_ffn_b1': out['m_ffn_b1'], 'm_ffn_w2': out['m_ffn_w2'], 'm_ffn_b2': out['m_ffn_b2'], 'm_ln_g': out['m_ln_g'], 'm_ln_b': out['m_ln_b'], 'v_ada_w': out['v_ada_w'], 'v_ada_b': out['v_ada_b'], 'v_in_w': out['v_in_w'], 'v_in_b': out['v_in_b'], 'v_gm_ln_g': out['v_gm_ln_g'], 'v_gm_ln_b': out['v_gm_ln_b'], 'v_gm_ws': out['v_gm_ws'], 'v_gm_bs': out['v_gm_bs'], 'v_mla_qnorm_g': out['v_mla_qnorm_g'], 'v_mla_wuq': out['v_mla_wuq'], 'v_mla_kvnorm_g': out['v_mla_kvnorm_g'], 'v_mla_wukv': out['v_mla_wukv'], 'v_lru_conv_w': out['v_lru_conv_w'], 'v_lru_conv_b': out['v_lru_conv_b'], 'v_lru_wr': out['v_lru_wr'], 'v_lru_br': out['v_lru_br'], 'v_lru_wi': out['v_lru_wi'], 'v_lru_bi': out['v_lru_bi'], 'v_lru_lambda': out['v_lru_lambda'], 'v_branch_w': out['v_branch_w'], 'v_mix_out_w': out['v_mix_out_w'], 'v_ffn_w1': out['v_ffn_w1'], 'v_ffn_b1': out['v_ffn_b1'], 'v_ffn_w2': out['v_ffn_w2'], 'v_ffn_b2': out['v_ffn_b2'], 'v_ln_g': out['v_ln_g'], 'v_ln_b': out['v_ln_b']}


def _loss(weights, diff, rest, loss_target):
    with _jax.named_scope("forward"):
        args = {**rest, TWIN_DIFF_INPUT: diff, **{k: w.astype(_WEIGHT_DTYPES[k]) for k, w in weights.items()}}
        y = _forward(args)
    with _jax.named_scope("loss_head"):
        err = _jnp.square(y.astype(_jnp.float32) - loss_target)
        return 0.5 * _jnp.sum(_jnp.mean(err, axis=-1)) if err.ndim else 0.5 * err


def _adamw(w, g, m, v):
    m = ADAM_B1 * m + (1.0 - ADAM_B1) * g
    v = ADAM_B2 * v + (1.0 - ADAM_B2) * _jnp.square(g)
    m_hat = m / (1.0 - ADAM_B1 ** ADAM_STEP)
    v_hat = v / (1.0 - ADAM_B2 ** ADAM_STEP)
    delta = -ADAM_LR * (m_hat / (_jnp.sqrt(v_hat) + ADAM_EPS) + ADAM_WD * w)
    return delta, m, v


def reference(x, c, ada_w, ada_b, in_w, in_b, gm_ln_g, gm_ln_b, gm_ws, gm_bs, mla_qnorm_g, mla_wuq, mla_kvnorm_g, mla_wukv, lru_conv_w, lru_conv_b, lru_wr, lru_br, lru_wi, lru_bi, lru_lambda, branch_w, mix_out_w, ffn_w1, ffn_b1, ffn_w2, ffn_b2, ln_g, ln_b, loss_target, m_ada_w, m_ada_b, m_in_w, m_in_b, m_gm_ln_g, m_gm_ln_b, m_gm_ws, m_gm_bs, m_mla_qnorm_g, m_mla_wuq, m_mla_kvnorm_g, m_mla_wukv, m_lru_conv_w, m_lru_conv_b, m_lru_wr, m_lru_br, m_lru_wi, m_lru_bi, m_lru_lambda, m_branch_w, m_mix_out_w, m_ffn_w1, m_ffn_b1, m_ffn_w2, m_ffn_b2, m_ln_g, m_ln_b, v_ada_w, v_ada_b, v_in_w, v_in_b, v_gm_ln_g, v_gm_ln_b, v_gm_ws, v_gm_bs, v_mla_qnorm_g, v_mla_wuq, v_mla_kvnorm_g, v_mla_wukv, v_lru_conv_w, v_lru_conv_b, v_lru_wr, v_lru_br, v_lru_wi, v_lru_bi, v_lru_lambda, v_branch_w, v_mix_out_w, v_ffn_w1, v_ffn_b1, v_ffn_w2, v_ffn_b2, v_ln_g, v_ln_b):
    given = dict(x=x, c=c, ada_w=ada_w, ada_b=ada_b, in_w=in_w, in_b=in_b, gm_ln_g=gm_ln_g, gm_ln_b=gm_ln_b, gm_ws=gm_ws, gm_bs=gm_bs, mla_qnorm_g=mla_qnorm_g, mla_wuq=mla_wuq, mla_kvnorm_g=mla_kvnorm_g, mla_wukv=mla_wukv, lru_conv_w=lru_conv_w, lru_conv_b=lru_conv_b, lru_wr=lru_wr, lru_br=lru_br, lru_wi=lru_wi, lru_bi=lru_bi, lru_lambda=lru_lambda, branch_w=branch_w, mix_out_w=mix_out_w, ffn_w1=ffn_w1, ffn_b1=ffn_b1, ffn_w2=ffn_w2, ffn_b2=ffn_b2, ln_g=ln_g, ln_b=ln_b, loss_target=loss_target, m_ada_w=m_ada_w, m_ada_b=m_ada_b, m_in_w=m_in_w, m_in_b=m_in_b, m_gm_ln_g=m_gm_ln_g, m_gm_ln_b=m_gm_ln_b, m_gm_ws=m_gm_ws, m_gm_bs=m_gm_bs, m_mla_qnorm_g=m_mla_qnorm_g, m_mla_wuq=m_mla_wuq, m_mla_kvnorm_g=m_mla_kvnorm_g, m_mla_wukv=m_mla_wukv, m_lru_conv_w=m_lru_conv_w, m_lru_conv_b=m_lru_conv_b, m_lru_wr=m_lru_wr, m_lru_br=m_lru_br, m_lru_wi=m_lru_wi, m_lru_bi=m_lru_bi, m_lru_lambda=m_lru_lambda, m_branch_w=m_branch_w, m_mix_out_w=m_mix_out_w, m_ffn_w1=m_ffn_w1, m_ffn_b1=m_ffn_b1, m_ffn_w2=m_ffn_w2, m_ffn_b2=m_ffn_b2, m_ln_g=m_ln_g, m_ln_b=m_ln_b, v_ada_w=v_ada_w, v_ada_b=v_ada_b, v_in_w=v_in_w, v_in_b=v_in_b, v_gm_ln_g=v_gm_ln_g, v_gm_ln_b=v_gm_ln_b, v_gm_ws=v_gm_ws, v_gm_bs=v_gm_bs, v_mla_qnorm_g=v_mla_qnorm_g, v_mla_wuq=v_mla_wuq, v_mla_kvnorm_g=v_mla_kvnorm_g, v_mla_wukv=v_mla_wukv, v_lru_conv_w=v_lru_conv_w, v_lru_conv_b=v_lru_conv_b, v_lru_wr=v_lru_wr, v_lru_br=v_lru_br, v_lru_wi=v_lru_wi, v_lru_bi=v_lru_bi, v_lru_lambda=v_lru_lambda, v_branch_w=v_branch_w, v_mix_out_w=v_mix_out_w, v_ffn_w1=v_ffn_w1, v_ffn_b1=v_ffn_b1, v_ffn_w2=v_ffn_w2, v_ffn_b2=v_ffn_b2, v_ln_g=v_ln_g, v_ln_b=v_ln_b)
    weights = {n: given[n] for n in TWIN_WEIGHTS}
    shared = {n: given[n] for n in SHARED_INPUTS}
    per_example = {n: given[n] for n in ['x', 'c']}
    grad_fn = _jax.value_and_grad(_loss, argnums=(0, 1))

    def one_microbatch(ex, loss_target):
        ex = dict(ex)
        diff = ex.pop(TWIN_DIFF_INPUT)
        return grad_fn(weights, diff, {**shared, **ex}, loss_target)

    if N_MICROBATCH == 1:
        loss, (grad_w, grad_x) = one_microbatch(per_example, given["loss_target"])
    else:
        def body(carry, xs):
            loss_sum, grad_sum = carry
            l_k, (gw_k, gx_k) = one_microbatch(xs[0], xs[1])
            with _jax.named_scope("update"):
                return (loss_sum + l_k, _jax.tree.map(_jnp.add, grad_sum, gw_k)), gx_k

        init = (_jnp.zeros((), _jnp.float32), _jax.tree.map(_jnp.zeros_like, weights))
        (loss, grad_w), grad_x = _jax.lax.scan(body, init, (per_example, given["loss_target"]))
    with _jax.named_scope("update"):
        delta_w, new_m, new_v = {}, {}, {}
        for n in TWIN_WEIGHTS:
            delta_w[n], new_m[n], new_v[n] = _adamw(weights[n], grad_w[n], given["m_" + n], given["v_" + n])
    return (loss, grad_x, *[grad_w[n] for n in TWIN_WEIGHTS], *[delta_w[n] for n in TWIN_WEIGHTS],
            *[new_m[n] for n in TWIN_WEIGHTS], *[new_v[n] for n in TWIN_WEIGHTS])
```

```python
import functools
import math

import jax
import jax.numpy as jnp
from jax import lax
from jax.experimental import pallas as pl
from jax.experimental.pallas import tpu as pltpu
from jax.scipy.linalg import block_diag

F32 = jnp.float32
MXU = jnp.bfloat16
MESH = pl.DeviceIdType.MESH

D = 1024
DEPTH = 2
CHUNK = 64
GM_W = 512
GM_G = 4
HEADS = 8
Q_RANK = 256
KV_RANK = 128
NOPE = 64
ROPE = 32
VD = 64
LRU_W = 512
LRU_NB = 8
D_FF = 4096
ALPHA = (2.0 * DEPTH) ** 0.25
LN_EPS = 1e-5
RMS_EPS = 1e-6
ROPE_BASE = 10000.0
ATT_SCALE = (NOPE + ROPE) ** -0.5
N_IN = 5536
P_GM, P_GATE, P_LX, P_LG, P_Q, P_KV, P_KA, P_KB, NP = 0, 1024, 4096, 4608, 5120, 5376, 5504, 5632, 5760
O_GM, O_Q, O_KV, O_KR, O_LX, O_LG, O_GATE = 0, 1024, 1280, 1408, 1440, 1952, 2464

ADAM_LR, ADAM_B1, ADAM_B2, ADAM_EPS, ADAM_WD, ADAM_STEP = 0.001, 0.9, 0.999, 1e-08, 0.01, 10

ROW_TILE = 512
ATT_TILE = 512
VMEM_LIMIT = 56 * 1024 * 1024
FLAT_W = 1024
FLAT_ALIGN = 32


def _params(sem=None, vmem=VMEM_LIMIT):
    return pltpu.CompilerParams(dimension_semantics=sem, vmem_limit_bytes=vmem)


def _tile(dim, pref, mult=128):
    if dim <= pref:
        return dim
    t = (pref // mult) * mult
    while t >= mult:
        if dim % t == 0:
            return t
        t -= mult
    return dim


def _dot(a, b):
    return lax.dot_general(a, b, (((1,), (0,)), ((), ())), preferred_element_type=F32)


def _dot_nt(a, b):
    return lax.dot_general(a, b, (((1,), (1,)), ((), ())), preferred_element_type=F32)


def _dot_tn(a, b):
    return lax.dot_general(a, b, (((0,), (0,)), ((), ())), preferred_element_type=F32)


def _sigmoid(x):
    return 1.0 / (1.0 + jnp.exp(-x))


_GC = 0.7978845608028654


def _gelu(x):
    return 0.5 * x * (1.0 + jnp.tanh(_GC * (x + 0.044715 * x * x * x)))


def _gelu_grad(x):
    t = jnp.tanh(_GC * (x + 0.044715 * x * x * x))
    return 0.5 * (1.0 + t) + 0.5 * x * (1.0 - t * t) * _GC * (1.0 + 3.0 * 0.044715 * x * x)


def _ln_stats(x):
    mu = jnp.mean(x, axis=-1, keepdims=True)
    xc = x - mu
    r = lax.rsqrt(jnp.mean(xc * xc, axis=-1, keepdims=True) + LN_EPS)
    return xc * r, r


def _ln_bwd(dxh, xh, r):
    return r * (dxh - jnp.mean(dxh, axis=-1, keepdims=True) - xh * jnp.mean(dxh * xh, axis=-1, keepdims=True))


def _colsum(v):
    return jnp.sum(v, axis=0, keepdims=True)


def _acc(ref, first, val):
    @pl.when(first)
    def _():
        ref[...] = val

    @pl.when(jnp.logical_not(first))
    def _():
        ref[...] += val


def _row_spec(t, w, col=0):
    return pl.BlockSpec((t, w), lambda i, c=col: (i, c))


def _vec_spec(w):
    return pl.BlockSpec((1, w), lambda i: (0, 0))


def _full_spec(shape):
    nd = len(shape)
    return pl.BlockSpec(shape, lambda i, n=nd: (0,) * n)


def _mm(a, b, mode, name, out_dtypes=(F32,), extras=(), rows=(), epi=None, tm=1024, tn=1152, tk=1024):
    if mode == "nn":
        (M, K), N = a.shape, b.shape[1]
    elif mode == "nt":
        (M, K), N = a.shape, b.shape[0]
    else:
        (K, M), N = a.shape, b.shape[1]
    tm, tn, tk = _tile(M, tm), _tile(N, tn), _tile(K, tk)
    nk = K // tk
    dot = {"nn": _dot, "nt": _dot_nt, "tn": _dot_tn}[mode]
    a_spec = pl.BlockSpec((tk, tm), lambda i, j, k: (k, i)) if mode == "tn" else pl.BlockSpec((tm, tk), lambda i, j, k: (i, k))
    b_spec = pl.BlockSpec((tn, tk), lambda i, j, k: (j, k)) if mode == "nt" else pl.BlockSpec((tk, tn), lambda i, j, k: (k, j))
    o_spec = pl.BlockSpec((tm, tn), lambda i, j, k: (i, j))
    r_spec = pl.BlockSpec((1, tn), lambda i, j, k: (0, j))
    n_e, n_r, n_o = len(extras), len(rows), len(out_dtypes)
    if epi is None:
        epi = lambda acc: (acc,)

    def body(*refs):
        a_ref, b_ref = refs[0], refs[1]
        e_refs = refs[2:2 + n_e]
        r_refs = refs[2 + n_e:2 + n_e + n_r]
        o_refs = refs[2 + n_e + n_r:2 + n_e + n_r + n_o]
        p = dot(a_ref[...], b_ref[...])

        def finish(acc):
            outs = epi(acc, *[e[...] for e in e_refs], *[r[...] for r in r_refs])
            for o_ref, o in zip(o_refs, outs):
                o_ref[...] = o.astype(o_ref.dtype)

        if nk == 1:
            finish(p)
        else:
            acc_ref = refs[-1]
            k = pl.program_id(2)
            _acc(acc_ref, k == 0, p)

            @pl.when(k == nk - 1)
            def _():
                finish(acc_ref[...])

    outs = pl.pallas_call(
        body, name=name, grid=(M // tm, N // tn, nk),
        in_specs=[a_spec, b_spec] + [o_spec] * n_e + [r_spec] * n_r,
        out_specs=[o_spec] * n_o,
        out_shape=[jax.ShapeDtypeStruct((M, N), dt) for dt in out_dtypes],
        scratch_shapes=[pltpu.VMEM((tm, tn), F32)] if nk > 1 else [],
        compiler_params=_params(("parallel", "parallel", "arbitrary")),
    )(a, b, *extras, *rows)
    return outs[0] if n_o == 1 else outs


def _lnmod_fwd(x, sc, sh, name):
    S = x.shape[0]
    t = min(ROW_TILE, S)

    def body(x_ref, sc_ref, sh_ref, h_ref):
        xh, _ = _ln_stats(x_ref[...])
        h_ref[...] = (xh * (1.0 + sc_ref[...]) + sh_ref[...]).astype(h_ref.dtype)

    return pl.pallas_call(
        body, name=name, grid=(S // t,),
        in_specs=[_row_spec(t, D), _vec_spec(D), _vec_spec(D)], out_specs=_row_spec(t, D),
        out_shape=jax.ShapeDtypeStruct((S, D), MXU), compiler_params=_params(("parallel",)),
    )(x, sc, sh)


def _res_ln_fwd(xprev, br, gvec, lng, lnb, name, mod=None):
    S = xprev.shape[0]
    t = min(ROW_TILE, S)
    with_h = mod is not None

    def body(*refs):
        xp_ref, br_ref, g_ref, lg_ref, lb_ref = refs[:5]
        u = ALPHA * xp_ref[...] + (1.0 + g_ref[...]) * br_ref[...]
        uh, _ = _ln_stats(u)
        xn = uh * lg_ref[...] + lb_ref[...]
        if with_h:
            sc_ref, sh_ref, xn_ref, h_ref = refs[5:]
            xh, _ = _ln_stats(xn)
            h_ref[...] = (xh * (1.0 + sc_ref[...]) + sh_ref[...]).astype(h_ref.dtype)
        else:
            xn_ref = refs[5]
        xn_ref[...] = xn

    ins = [xprev, br, gvec, lng, lnb] + (list(mod) if with_h else [])
    return pl.pallas_call(
        body, name=name, grid=(S // t,),
        in_specs=[_row_spec(t, D), _row_spec(t, D)] + [_vec_spec(D)] * (len(ins) - 2),
        out_specs=[_row_spec(t, D)] * (2 if with_h else 1),
        out_shape=[jax.ShapeDtypeStruct((S, D), F32)] + ([jax.ShapeDtypeStruct((S, D), MXU)] if with_h else []),
        compiler_params=_params(("parallel",)),
    )(*ins)


def _loss_fwd(y, tgt):
    S = y.shape[0]
    t = min(ROW_TILE, S)

    def body(y_ref, t_ref, dy_ref, sq_ref):
        d = y_ref[...] - t_ref[...]
        dy_ref[...] = d * (1.0 / D)
        _acc(sq_ref, pl.program_id(0) == 0, _colsum(d * d))

    return pl.pallas_call(
        body, name="loss_head", grid=(S // t,),
        in_specs=[_row_spec(t, D), _row_spec(t, D)], out_specs=[_row_spec(t, D), _vec_spec(D)],
        out_shape=[jax.ShapeDtypeStruct((S, D), F32), jax.ShapeDtypeStruct((1, D), F32)],
        compiler_params=_params(("arbitrary",)),
    )(y, tgt)


def _node_bwd(name, dres, *, hpath=None, upath=None):
    S = dres.shape[0]
    t = min(ROW_TILE, S)
    has_h, has_u = hpath is not None, upath is not None

    def body(*refs):
        refs = list(refs)
        first = pl.program_id(0) == 0
        dxs = refs.pop(0)[...]
        if has_h:
            dh = refs.pop(0)[...]
            xs = refs.pop(0)[...]
            sc = refs.pop(0)[...]
        if has_u:
            xp = refs.pop(0)[...]
            br = refs.pop(0)[...]
            gv = refs.pop(0)[...]
            lg = refs.pop(0)[...]
        if has_h:
            dsc_ref, dsh_ref = refs.pop(0), refs.pop(0)
            xh, r = _ln_stats(xs)
            _acc(dsc_ref, first, _colsum(dh * xh))
            _acc(dsh_ref, first, _colsum(dh))
            dxs = dxs + _ln_bwd(dh * (1.0 + sc), xh, r)
        if has_u:
            du_ref, dbr_ref, dlg_ref, dlb_ref, dg_ref = refs
            uh, ru = _ln_stats(ALPHA * xp + (1.0 + gv) * br)
            _acc(dlg_ref, first, _colsum(dxs * uh))
            _acc(dlb_ref, first, _colsum(dxs))
            du = _ln_bwd(dxs * lg, uh, ru)
            _acc(dg_ref, first, _colsum(du * br))
            du_ref[...] = ALPHA * du
            dbr_ref[...] = ((1.0 + gv) * du).astype(dbr_ref.dtype)
        else:
            refs[0][...] = dxs

    ins, in_specs = [dres], [_row_spec(t, D)]
    outs, out_specs, names = [], [], []
    vec = jax.ShapeDtypeStruct((1, D), F32)
    if has_h:
        ins += list(hpath)
        in_specs += [_row_spec(t, D), _row_spec(t, D), _vec_spec(D)]
        outs += [vec, vec]
        out_specs += [_vec_spec(D), _vec_spec(D)]
        names += ["dsc", "dsh"]
    if has_u:
        ins += list(upath)
        in_specs += [_row_spec(t, D), _row_spec(t, D), _vec_spec(D), _vec_spec(D)]
        outs += [jax.ShapeDtypeStruct((S, D), F32), jax.ShapeDtypeStruct((S, D), MXU), vec, vec, vec]
        out_specs += [_row_spec(t, D), _row_spec(t, D), _vec_spec(D), _vec_spec(D), _vec_spec(D)]
        names += ["du", "dbr", "dlng", "dlnb", "dg"]
    else:
        outs += [jax.ShapeDtypeStruct((S, D), F32)]
        out_specs += [_row_spec(t, D)]
        names += ["dx"]
    res = pl.pallas_call(
        body, name=name, grid=(S // t,), in_specs=in_specs, out_specs=out_specs, out_shape=outs,
        compiler_params=_params(("arbitrary",)),
    )(*ins)
    return dict(zip(names, res))


def _colsum_call(a, name):
    S, N = a.shape
    t, tn = min(ROW_TILE, S), _tile(N, 1152)

    def body(a_ref, o_ref):
        _acc(o_ref, pl.program_id(1) == 0, _colsum(a_ref[...].astype(F32)))

    return pl.pallas_call(
        body, name=name, grid=(N // tn, S // t),
        in_specs=[pl.BlockSpec((t, tn), lambda j, i: (i, j))], out_specs=pl.BlockSpec((1, tn), lambda j, i: (0, j)),
        out_shape=jax.ShapeDtypeStruct((1, N), F32), compiler_params=_params(("parallel", "arbitrary")),
    )(a)


def _gm_mask():
    i = lax.broadcasted_iota(jnp.int32, (128, 128), 0) // CHUNK
    j = lax.broadcasted_iota(jnp.int32, (128, 128), 1) // CHUNK
    return i >= j


def _gmlp_common(z, lng, lnb):
    gz = _gelu(z)
    u, v = gz[:, :GM_W], gz[:, GM_W:]
    vh, r = _ln_stats(v)
    return u, vh, r, vh * lng + lnb


def _gmlp_fwd(z, lng, lnb, ws, bst, name):
    S = z.shape[0]
    t = min(ROW_TILE, S)

    def body(z_ref, lg_ref, lb_ref, ws_ref, bs_ref, y_ref):
        u, _, _, vn = _gmlp_common(z_ref[...], lg_ref[...], lb_ref[...])
        mask = _gm_mask()
        vb = vn.astype(MXU)
        for g in range(GM_G):
            w = jnp.where(mask, ws_ref[g], jnp.zeros_like(ws_ref[g]))
            bias = bs_ref[:, g:g + 1]
            for blk in range(t // 128):
                rs, cs = slice(blk * 128, (blk + 1) * 128), slice(g * 128, (g + 1) * 128)
                f = _dot(w, vb[rs, cs]) + bias
                y_ref[rs, cs] = (u[rs, cs] * f).astype(y_ref.dtype)

    return pl.pallas_call(
        body, name=name, grid=(S // t,),
        in_specs=[_row_spec(t, 2 * GM_W, P_GM // (2 * GM_W)), _vec_spec(GM_W), _vec_spec(GM_W),
                  _full_spec((GM_G, 128, 128)), _full_spec((128, GM_G))],
        out_specs=_row_spec(t, GM_W), out_shape=jax.ShapeDtypeStruct((S, GM_W), MXU),
        compiler_params=_params(("parallel",)),
    )(z, lng, lnb, ws, bst)


def _gmlp_bwd(z, dya, lng, lnb, ws, wst, bst, name):
    S = z.shape[0]
    t = min(ROW_TILE, S)

    def body(z_ref, dy_ref, lg_ref, lb_ref, ws_ref, wst_ref, bs_ref, dz_ref, dlg_ref, dlb_ref, dws_ref, dbs_ref):
        first = pl.program_id(0) == 0
        zz = z_ref[...]
        u, vh, r, vn = _gmlp_common(zz, lg_ref[...], lb_ref[...])
        dy = dy_ref[...]
        mask = _gm_mask()
        maskt = lax.broadcasted_iota(jnp.int32, (128, 128), 1) // CHUNK >= lax.broadcasted_iota(jnp.int32, (128, 128), 0) // CHUNK
        lane = lax.broadcasted_iota(jnp.int32, (128, 128), 1)
        vb = vn.astype(MXU)
        dfb = (dy * u).astype(MXU)
        df32 = dy * u
        dbs = jnp.zeros((128, 128), F32)
        du_cols, dvn_cols = [], []
        for g in range(GM_G):
            w = jnp.where(mask, ws_ref[g], jnp.zeros_like(ws_ref[g]))
            wt = jnp.where(maskt, wst_ref[g], jnp.zeros_like(wst_ref[g]))
            bias = bs_ref[:, g:g + 1]
            cs = slice(g * 128, (g + 1) * 128)
            dw = jnp.zeros((128, 128), F32)
            du_rows, dvn_rows = [], []
            for blk in range(t // 128):
                rs = slice(blk * 128, (blk + 1) * 128)
                f = _dot(w, vb[rs, cs]) + bias
                du_rows.append(dy[rs, cs] * f)
                dvn_rows.append(_dot(wt, dfb[rs, cs]))
                dw = dw + _dot_nt(dfb[rs, cs], vb[rs, cs])
                dbs = dbs + jnp.where(lane == g, jnp.sum(df32[rs, cs], axis=1, keepdims=True), 0.0)
            _acc(dws_ref.at[g], first, jnp.where(mask, dw, 0.0))
            du_cols.append(jnp.concatenate(du_rows, axis=0))
            dvn_cols.append(jnp.concatenate(dvn_rows, axis=0))
        _acc(dbs_ref, first, dbs)
        du = jnp.concatenate(du_cols, axis=1)
        dvn = jnp.concatenate(dvn_cols, axis=1)
        _acc(dlg_ref, first, _colsum(dvn * vh))
        _acc(dlb_ref, first, _colsum(dvn))
        dv = _ln_bwd(dvn * lg_ref[...], vh, r)
        dz_ref[...] = (jnp.concatenate([du, dv], axis=1) * _gelu_grad(zz)).astype(dz_ref.dtype)

    vec = jax.ShapeDtypeStruct((1, GM_W), F32)
    return pl.pallas_call(
        body, name=name, grid=(S // t,),
        in_specs=[_row_spec(t, 2 * GM_W, P_GM // (2 * GM_W)), _row_spec(t, GM_W), _vec_spec(GM_W), _vec_spec(GM_W),
                  _full_spec((GM_G, 128, 128)), _full_spec((GM_G, 128, 128)), _full_spec((128, GM_G))],
        out_specs=[_row_spec(t, 2 * GM_W), _vec_spec(GM_W), _vec_spec(GM_W), _full_spec((GM_G, 128, 128)), _full_spec((128, 128))],
        out_shape=[jax.ShapeDtypeStruct((S, 2 * GM_W), MXU), vec, vec,
                   jax.ShapeDtypeStruct((GM_G, 128, 128), F32), jax.ShapeDtypeStruct((128, 128), F32)],
        compiler_params=_params(("arbitrary",)),
    )(z, dya, lng, lnb, ws, wst, bst)


def _rms(x, g):
    r = lax.rsqrt(jnp.mean(x * x, axis=-1, keepdims=True) + RMS_EPS)
    xh = x * r
    return xh, r, xh * g


def _mla_specs(t):
    return [_row_spec(t, Q_RANK, P_Q // Q_RANK), _row_spec(t, KV_RANK, P_KV // 128), _row_spec(t, 128, P_KA // 128),
            _row_spec(t, 128, P_KB // 128), _row_spec(t, 128), _row_spec(t, 128), _vec_spec(Q_RANK), _vec_spec(KV_RANK),
            _full_spec((Q_RANK, D)), _full_spec((Q_RANK, D)), _full_spec((KV_RANK, D)), _full_spec((HEADS, KV_RANK, VD))]


def _mla_prep_fwd(z, cos, sin, gq, gkv, w1, w2, wk, wv, name):
    S = z.shape[0]
    t = min(ROW_TILE, S)

    def body(zq_ref, zkv_ref, zka_ref, zkb_ref, cos_ref, sin_ref, gq_ref, gkv_ref, w1_ref, w2_ref, wk_ref, wv_ref,
             q_ref, k_ref, v_ref):
        cos_, sin_ = cos_ref[...], sin_ref[...]
        cq = cos_ + jnp.where(lax.broadcasted_iota(jnp.int32, cos_.shape, 1) < NOPE, 1.0, 0.0)
        qn = _rms(zq_ref[...], gq_ref[...])[2].astype(MXU)
        q1, q2 = _dot(qn, w1_ref[...]), _dot(qn, w2_ref[...])
        kvn = _rms(zkv_ref[...], gkv_ref[...])[2].astype(MXU)
        kn = _dot(kvn, wk_ref[...])
        krot = zka_ref[...] * cos_ + zkb_ref[...] * sin_
        for h in range(HEADS):
            hs = slice(h * 128, (h + 1) * 128)
            q_ref[h] = (q1[:, hs] * cq + q2[:, hs] * sin_).astype(q_ref.dtype)
            k_ref[h] = (kn[:, hs] + krot).astype(k_ref.dtype)
            v_ref[h] = _dot(kvn, wv_ref[h]).astype(v_ref.dtype)

    hspec = lambda w: pl.BlockSpec((HEADS, t, w), lambda i: (0, i, 0))
    return pl.pallas_call(
        body, name=name, grid=(S // t,), in_specs=_mla_specs(t), out_specs=[hspec(128), hspec(128), hspec(VD)],
        out_shape=[jax.ShapeDtypeStruct((HEADS, S, 128), MXU), jax.ShapeDtypeStruct((HEADS, S, 128), MXU),
                   jax.ShapeDtypeStruct((HEADS, S, VD), MXU)],
        compiler_params=_params(("parallel",)),
    )(z, z, z, z, cos, sin, gq, gkv, w1, w2, wk, wv)


def _mla_prep_bwd(z, cos, sin, gq, gkv, w1, w2, wk, wv, dq, dk, dv, name):
    S = z.shape[0]
    t = min(ROW_TILE, S)

    def body(zq_ref, zkv_ref, zka_ref, zkb_ref, cos_ref, sin_ref, gq_ref, gkv_ref, w1_ref, w2_ref, wk_ref, wv_ref,
             dq_ref, dk_ref, dv_ref, dz_ref, dgq_ref, dgkv_ref, dw1_ref, dw2_ref, dwk_ref, dwv_ref):
        first = pl.program_id(0) == 0
        cos_, sin_ = cos_ref[...], sin_ref[...]
        cq = cos_ + jnp.where(lax.broadcasted_iota(jnp.int32, cos_.shape, 1) < NOPE, 1.0, 0.0)
        gq_, gkv_ = gq_ref[...], gkv_ref[...]
        xhq, rq, qn32 = _rms(zq_ref[...], gq_)
        qn = qn32.astype(MXU)
        dq1 = jnp.concatenate([dq_ref[h] * cq for h in range(HEADS)], axis=1).astype(MXU)
        dq2 = jnp.concatenate([dq_ref[h] * sin_ for h in range(HEADS)], axis=1).astype(MXU)
        dqn = _dot_nt(dq1, w1_ref[...]) + _dot_nt(dq2, w2_ref[...])
        _acc(dw1_ref, first, _dot_tn(qn, dq1))
        _acc(dw2_ref, first, _dot_tn(qn, dq2))
        _acc(dgq_ref, first, _colsum(dqn * xhq))
        dxn = dqn * gq_
        dzq = rq * (dxn - xhq * jnp.mean(dxn * xhq, axis=-1, keepdims=True))

        xhk, rk, kvn32 = _rms(zkv_ref[...], gkv_)
        kvn = kvn32.astype(MXU)
        dks = [dk_ref[h] for h in range(HEADS)]
        dkall = jnp.concatenate(dks, axis=1).astype(MXU)
        dkrot = functools.reduce(lambda p, q_: p + q_, dks)
        dkvn = _dot_nt(dkall, wk_ref[...])
        _acc(dwk_ref, first, _dot_tn(kvn, dkall))
        for h in range(HEADS):
            dvb = dv_ref[h].astype(MXU)
            dkvn = dkvn + _dot_nt(dvb, wv_ref[h])
            _acc(dwv_ref.at[h], first, _dot_tn(kvn, dvb))
        _acc(dgkv_ref, first, _colsum(dkvn * xhk))
        dxk = dkvn * gkv_
        dzkv = rk * (dxk - xhk * jnp.mean(dxk * xhk, axis=-1, keepdims=True))
        dz_ref[...] = jnp.concatenate([dzq, dzkv, dkrot * cos_, dkrot * sin_], axis=1).astype(dz_ref.dtype)

    hspec = lambda w: pl.BlockSpec((HEADS, t, w), lambda i: (0, i, 0))
    sds = jax.ShapeDtypeStruct
    return pl.pallas_call(
        body, name=name, grid=(S // t,),
        in_specs=_mla_specs(t) + [hspec(128), hspec(128), hspec(VD)],
        out_specs=[_row_spec(t, 640), _vec_spec(Q_RANK), _vec_spec(KV_RANK), _full_spec((Q_RANK, D)), _full_spec((Q_RANK, D)),
                   _full_spec((KV_RANK, D)), _full_spec((HEADS, KV_RANK, VD))],
        out_shape=[sds((S, 640), MXU), sds((1, Q_RANK), F32), sds((1, KV_RANK), F32), sds((Q_RANK, D), F32),
                   sds((Q_RANK, D), F32), sds((KV_RANK, D), F32), sds((HEADS, KV_RANK, VD), F32)],
        compiler_params=_params(("arbitrary",)),
    )(z, z, z, z, cos, sin, gq, gkv, w1, w2, wk, wv, dq, dk, dv)


def _chunk_mask(t, transposed):
    r = lax.broadcasted_iota(jnp.int32, (t, t), 0) // CHUNK
    c = lax.broadcasted_iota(jnp.int32, (t, t), 1) // CHUNK
    return (r <= c) if transposed else (c <= r)


def _attn_fwd(q, k, v, name):
    S = q.shape[1]
    t = min(ATT_TILE, S)
    n = S // t

    def body(q_ref, k_ref, v_ref, ob_ref, o_ref, lse_ref, m_s, l_s, acc_s):
        qi, ki = pl.program_id(1), pl.program_id(2)

        @pl.when(ki == 0)
        def _():
            m_s[...] = jnp.full(m_s.shape, -jnp.inf, F32)
            l_s[...] = jnp.zeros(l_s.shape, F32)
            acc_s[...] = jnp.zeros(acc_s.shape, F32)

        def step(masked):
            s = _dot_nt(q_ref[0], k_ref[0]) * ATT_SCALE
            if masked:
                s = jnp.where(_chunk_mask(t, False), s, -jnp.inf)
            m_prev = m_s[...]
            m_new = jnp.maximum(m_prev, jnp.max(s, axis=1, keepdims=True))
            p = jnp.exp(s - m_new)
            alpha = jnp.exp(m_prev - m_new)
            l_s[...] = alpha * l_s[...] + jnp.sum(p, axis=1, keepdims=True)
            acc_s[...] = alpha * acc_s[...] + _dot(p.astype(MXU), v_ref[0])
            m_s[...] = m_new

        @pl.when(ki < qi)
        def _():
            step(False)

        @pl.when(ki == qi)
        def _():
            step(True)
            o = acc_s[...] / l_s[...]
            o_ref[0] = o
            ob_ref[0] = o.astype(ob_ref.dtype)
            lse_ref[0] = m_s[...] + jnp.log(l_s[...])

    qspec = lambda w: pl.BlockSpec((1, t, w), lambda h, i, j: (h, i, 0))
    kspec = lambda w: pl.BlockSpec((1, t, w), lambda h, i, j: (h, jnp.minimum(i, j), 0))
    sds = jax.ShapeDtypeStruct
    return pl.pallas_call(
        body, name=name, grid=(HEADS, n, n),
        in_specs=[qspec(128), kspec(128), kspec(VD)], out_specs=[qspec(VD), qspec(VD), qspec(1)],
        out_shape=[sds((HEADS, S, VD), MXU), sds((HEADS, S, VD), F32), sds((HEADS, S, 1), F32)],
        scratch_shapes=[pltpu.VMEM((t, 1), F32), pltpu.VMEM((t, 1), F32), pltpu.VMEM((t, VD), F32)],
        compiler_params=_params(("parallel", "parallel", "arbitrary")),
    )(q, k, v)


def _attn_delta(o, do, name):
    S = o.shape[1]
    t = min(ROW_TILE, S)

    def body(o_ref, do_ref, d_ref):
        for h in range(HEADS):
            d_ref[h] = jnp.sum(o_ref[h] * do_ref[h], axis=1, keepdims=True)

    hspec = lambda w: pl.BlockSpec((HEADS, t, w), lambda i: (0, i, 0))
    return pl.pallas_call(
        body, name=name, grid=(S // t,), in_specs=[hspec(VD), hspec(VD)], out_specs=hspec(1),
        out_shape=jax.ShapeDtypeStruct((HEADS, S, 1), F32), compiler_params=_params(("parallel",)),
    )(o, do)


def _attn_bwd(q, k, v, do, lse_row, d_row, name):
    S = q.shape[1]
    t = min(ATT_TILE, S)
    n = S // t

    def body(q_ref, do_ref, lse_ref, d_ref, k_ref, v_ref, dq_ref, dk_ref, dv_ref, dk_s, dv_s):
        ki, qi = pl.program_id(1), pl.program_id(2)

        @pl.when(qi == ki)
        def _():
            dk_s[...] = jnp.zeros(dk_s.shape, F32)
            dv_s[...] = jnp.zeros(dv_s.shape, F32)

        def step(masked):
            st = _dot_nt(k_ref[0], q_ref[0]) * ATT_SCALE
            if masked:
                st = jnp.where(_chunk_mask(t, True), st, -jnp.inf)
            pt = jnp.exp(st - lse_ref[0])
            dv_s[...] += _dot(pt.astype(MXU), do_ref[0])
            dpt = _dot_nt(v_ref[0], do_ref[0])
            dst = (pt * (dpt - d_ref[0]) * ATT_SCALE).astype(MXU)
            dk_s[...] += _dot(dst, q_ref[0])
            dqc = _dot_tn(dst, k_ref[0])
            rows = pl.ds(pl.multiple_of(qi * t, t), t)

            @pl.when(ki == 0)
            def _():
                dq_ref[0, rows, :] = dqc

            @pl.when(ki > 0)
            def _():
                dq_ref[0, rows, :] += dqc

        @pl.when(qi > ki)
        def _():
            step(False)

        @pl.when(qi == ki)
        def _():
            step(True)

        @pl.when(qi == n - 1)
        def _():
            dk_ref[0] = dk_s[...]
            dv_ref[0] = dv_s[...]

    qmap = lambda h, j, i: (h, jnp.maximum(i, j), 0)
    rmap = lambda h, j, i: (h, 0, jnp.maximum(i, j))
    kmap = lambda h, j, i: (h, j, 0)
    sds = jax.ShapeDtypeStruct
    return pl.pallas_call(
        body, name=name, grid=(HEADS, n, n),
        in_specs=[pl.BlockSpec((1, t, 128), qmap), pl.BlockSpec((1, t, VD), qmap), pl.BlockSpec((1, 1, t), rmap),
                  pl.BlockSpec((1, 1, t), rmap), pl.BlockSpec((1, t, 128), kmap), pl.BlockSpec((1, t, VD), kmap)],
        out_specs=[pl.BlockSpec((1, S, 128), lambda h, j, i: (h, 0, 0)), pl.BlockSpec((1, t, 128), kmap),
                   pl.BlockSpec((1, t, VD), kmap)],
        out_shape=[sds((HEADS, S, 128), F32), sds((HEADS, S, 128), F32), sds((HEADS, S, VD), F32)],
        scratch_shapes=[pltpu.VMEM((t, 128), F32), pltpu.VMEM((t, VD), F32)],
        compiler_params=_params(("parallel", "arbitrary", "arbitrary")),
    )(q, do, lse_row, d_row, k, v)


def _shift_down(x, prev8, d):
    xr = pltpu.roll(x, d, 0)
    r8 = lax.broadcasted_iota(jnp.int32, prev8.shape, 0)
    top = jnp.where(r8 < d, pltpu.roll(prev8, d, 0), xr[0:8])
    return jnp.concatenate([top, xr[8:]], axis=0)


def _shift_up(x, next8, d):
    n = x.shape[0]
    xr = pltpu.roll(x, n - d, 0)
    r8 = lax.broadcasted_iota(jnp.int32, next8.shape, 0)
    bot = jnp.where(r8 >= 8 - d, pltpu.roll(next8, 8 - d, 0), xr[n - 8:])
    return jnp.concatenate([xr[:n - 8], bot], axis=0)


def _log1p(u):
    return jnp.where(u < 0.01, u * (1.0 - u * (0.5 - u * (1.0 / 3.0 - 0.25 * u))), jnp.log(1.0 + u))


def _neg_expm1(y):
    series = -y * (1.0 + 0.5 * y * (1.0 + (1.0 / 3.0) * y * (1.0 + 0.25 * y)))
    return jnp.where(y > -0.05, series, 1.0 - jnp.exp(y))


def _softplus_neg(lam):
    return jnp.maximum(-lam, 0.0) + _log1p(jnp.exp(-jnp.abs(lam)))


def _lru_gates(x, prev8, cw, cb, wr, br, wi, bi, lam):
    xs1, xs2, xs3 = _shift_down(x, prev8, 1), _shift_down(x, prev8, 2), _shift_down(x, prev8, 3)
    xc = cb + cw[0:1] * xs3 + cw[1:2] * xs2 + cw[2:3] * xs1 + cw[3:4] * x
    xcb = xc.astype(MXU)
    r = _sigmoid(_dot(xcb, wr) + br)
    ig = _sigmoid(_dot(xcb, wi) + bi)
    sp = _softplus_neg(lam)
    log_a = -8.0 * r * sp
    a = jnp.exp(log_a)
    gb = jnp.sqrt(_neg_expm1(2.0 * log_a))
    return (xs1, xs2, xs3), xc, xcb, r, ig, sp, a, gb


def _lru_fwd(z, cw, cb, wr, br, wi, bi, lam, name):
    S = z.shape[0]
    t = min(ROW_TILE, S)

    def body(zx_ref, zg_ref, cw_ref, cb_ref, wr_ref, br_ref, wi_ref, bi_ref, lam_ref, y_ref, h_ref, xp_s, hc_s):
        @pl.when(pl.program_id(0) == 0)
        def _():
            xp_s[...] = jnp.zeros(xp_s.shape, F32)
            hc_s[...] = jnp.zeros(hc_s.shape, F32)

        x = zx_ref[...]
        _, xc, _, _, ig, _, a, gb = _lru_gates(x, xp_s[...], cw_ref[...], cb_ref[...], wr_ref[...], br_ref[...],
                                               wi_ref[...], bi_ref[...], lam_ref[...])
        b = gb * (ig * xc)
        rows = lax.broadcasted_iota(jnp.int32, a.shape, 0)
        d = 1
        while d < t:
            ar, brr = pltpu.roll(a, d, 0), pltpu.roll(b, d, 0)
            ok = rows >= d
            b = jnp.where(ok, a * brr, 0.0) + b
            a = jnp.where(ok, a * ar, a)
            d *= 2
        h = a * hc_s[7:8, :] + b
        h_ref[...] = h
        y_ref[...] = (h * _gelu(zg_ref[...])).astype(y_ref.dtype)
        hc_s[...] = h[t - 8:, :]
        xp_s[...] = x[t - 8:, :]

    w = LRU_W
    return pl.pallas_call(
        body, name=name, grid=(S // t,),
        in_specs=[_row_spec(t, w, P_LX // w), _row_spec(t, w, P_LG // w), _full_spec((4, w)), _vec_spec(w), _full_spec((w, w)),
                  _vec_spec(w), _full_spec((w, w)), _vec_spec(w), _vec_spec(w)],
        out_specs=[_row_spec(t, w), _row_spec(t, w)],
        out_shape=[jax.ShapeDtypeStruct((S, w), MXU), jax.ShapeDtypeStruct((S, w), F32)],
        scratch_shapes=[pltpu.VMEM((8, w), F32), pltpu.VMEM((8, w), F32)],
        compiler_params=_params(("arbitrary",)),
    )(z, z, cw, cb, wr, br, wi, bi, lam)


def _lru_bwd(z, h, dy, cw, cb, wr, br, wi, bi, lam, name):
    S = z.shape[0]
    t = min(ROW_TILE, S)
    n = S // t
    w = LRU_W

    def body(zx_ref, zxp_ref, zg_ref, h_ref, hp_ref, dy_ref, cw_ref, cb_ref, wr_ref, br_ref, wi_ref, bi_ref, lam_ref,
             dz_ref, dcw_ref, dcb_ref, dwr_ref, dbr_ref, dwi_ref, dbi_ref, dlam_ref, gc_s, dn_s):
        i = pl.program_id(0)
        first = i == 0
        j = n - 1 - i

        @pl.when(first)
        def _():
            gc_s[...] = jnp.zeros(gc_s.shape, F32)
            dn_s[...] = jnp.zeros(dn_s.shape, F32)

        live = (j > 0).astype(F32)
        xprev8, hprev8 = zxp_ref[...] * live, hp_ref[...] * live
        x, zg, hh, dy_ = zx_ref[...], zg_ref[...], h_ref[...], dy_ref[...]
        cw_, lam_ = cw_ref[...], lam_ref[...]
        (xs1, xs2, xs3), xc, xcb, r, ig, sp, a, gb = _lru_gates(x, xprev8, cw_, cb_ref[...], wr_ref[...], br_ref[...],
                                                                wi_ref[...], bi_ref[...], lam_)
        hm1 = _shift_down(hh, hprev8, 1)
        dh = dy_ * _gelu(zg)
        dzg = dy_ * hh * _gelu_grad(zg)
        rows = lax.broadcasted_iota(jnp.int32, a.shape, 0)
        last = rows == t - 1
        ca = jnp.where(last, 0.0, pltpu.roll(a, t - 1, 0))
        g = dh + jnp.where(last, gc_s[0:1, :], 0.0)
        d = 1
        while d < t:
            ok = rows < t - d
            g = g + jnp.where(ok, ca * pltpu.roll(g, t - d, 0), 0.0)
            ca = jnp.where(ok, ca * pltpu.roll(ca, t - d, 0), 0.0)
            d *= 2
        gc_s[...] = a[0:8, :] * g[0:8, :]
        da = g * hm1
        dgb = g * (ig * xc)
        dub = g * gb
        di = dub * xc
        dxc = dub * ig
        dlog_a = da * a - dgb * (a * a) / gb
        dr = dlog_a * (-8.0 * sp)
        _acc(dlam_ref, first, _colsum(dlog_a * (-8.0 * r)) * (-_sigmoid(-lam_)))
        dpr = dr * r * (1.0 - r)
        dpi = di * ig * (1.0 - ig)
        _acc(dbr_ref, first, _colsum(dpr))
        _acc(dbi_ref, first, _colsum(dpi))
        dprb, dpib = dpr.astype(MXU), dpi.astype(MXU)
        _acc(dwr_ref, first, _dot_tn(xcb, dprb))
        _acc(dwi_ref, first, _dot_tn(xcb, dpib))
        dxc = dxc + _dot_nt(dprb, wr_ref[...]) + _dot_nt(dpib, wi_ref[...])
        _acc(dcb_ref, first, _colsum(dxc))
        _acc(dcw_ref, first, jnp.concatenate([_colsum(dxc * xs3), _colsum(dxc * xs2), _colsum(dxc * xs1), _colsum(dxc * x)], axis=0))
        nxt = dn_s[...]
        dx = cw_[3:4] * dxc + cw_[2:3] * _shift_up(dxc, nxt, 1) + cw_[1:2] * _shift_up(dxc, nxt, 2) + cw_[0:1] * _shift_up(dxc, nxt, 3)
        dn_s[...] = dxc[0:8, :]
        dz_ref[...] = jnp.concatenate([dx, dzg], axis=1).astype(dz_ref.dtype)

    rev = lambda col: pl.BlockSpec((t, w), lambda i, c=col: (n - 1 - i, c))
    prev8 = lambda col: pl.BlockSpec((8, w), lambda i, c=col: (jnp.maximum((n - 1 - i) * (t // 8) - 1, 0), c))
    vec = jax.ShapeDtypeStruct((1, w), F32)
    sds = jax.ShapeDtypeStruct
    return pl.pallas_call(
        body, name=name, grid=(n,),
        in_specs=[rev(P_LX // w), prev8(P_LX // w), rev(P_LG // w), rev(0), prev8(0), rev(0), _full_spec((4, w)), _vec_spec(w),
                  _full_spec((w, w)), _vec_spec(w), _full_spec((w, w)), _vec_spec(w), _vec_spec(w)],
        out_specs=[pl.BlockSpec((t, 2 * w), lambda i: (n - 1 - i, 0)), _full_spec((4, w)), _vec_spec(w), _full_spec((w, w)),
                   _vec_spec(w), _full_spec((w, w)), _vec_spec(w), _vec_spec(w)],
        out_shape=[sds((S, 2 * w), MXU), sds((4, w), F32), vec, sds((w, w), F32), vec, sds((w, w), F32), vec, vec],
        scratch_shapes=[pltpu.VMEM((8, w), F32), pltpu.VMEM((8, w), F32)],
        compiler_params=_params(("arbitrary",)),
    )(z, z, z, h, h, dy, cw, cb, wr, br, wi, bi, lam)


def _branch_fwd(z, ya, o, yc, wa, wb, wc, name):
    S = z.shape[0]
    t = min(ROW_TILE, S)

    def body(ga_ref, gb_ref, gc_ref, ya_ref, o_ref, yc_ref, wa_ref, wb_ref, wc_ref, m_ref, pa_ref, pb_ref, pc_ref):
        pa = _dot(ya_ref[...], wa_ref[...])
        pc = _dot(yc_ref[...], wc_ref[...])
        pb = _dot(o_ref[0], wb_ref[0])
        for h in range(1, HEADS):
            pb = pb + _dot(o_ref[h], wb_ref[h])
        pa_ref[...], pb_ref[...], pc_ref[...] = pa, pb, pc
        m = _sigmoid(ga_ref[...]) * pa + _sigmoid(gb_ref[...]) * pb + _sigmoid(gc_ref[...]) * pc
        m_ref[...] = m.astype(m_ref.dtype)

    g0 = P_GATE // D
    sds = jax.ShapeDtypeStruct
    return pl.pallas_call(
        body, name=name, grid=(S // t,),
        in_specs=[_row_spec(t, D, g0), _row_spec(t, D, g0 + 1), _row_spec(t, D, g0 + 2), _row_spec(t, GM_W),
                  pl.BlockSpec((HEADS, t, VD), lambda i: (0, i, 0)), _row_spec(t, LRU_W),
                  _full_spec((GM_W, D)), _full_spec((HEADS, VD, D)), _full_spec((LRU_W, D))],
        out_specs=[_row_spec(t, D)] * 4,
        out_shape=[sds((S, D), MXU), sds((S, D), F32), sds((S, D), F32), sds((S, D), F32)],
        compiler_params=_params(("parallel",)),
    )(z, z, z, ya, o, yc, wa, wb, wc)


def _branch_bwd(z, dm, pa, pb, pc, name):
    S = z.shape[0]
    t = min(ROW_TILE, S)

    def body(ga_ref, gb_ref, gc_ref, dm_ref, pa_ref, pb_ref, pc_ref, dz_ref, da_ref, db_ref, dc_ref):
        dm_ = dm_ref[...]
        for n_, (g_ref, p_ref, d_ref) in enumerate(((ga_ref, pa_ref, da_ref), (gb_ref, pb_ref, db_ref), (gc_ref, pc_ref, dc_ref))):
            gt = _sigmoid(g_ref[...])
            d_ref[...] = (dm_ * gt).astype(d_ref.dtype)
            dz_ref[:, n_ * D:(n_ + 1) * D] = (dm_ * p_ref[...] * gt * (1.0 - gt)).astype(dz_ref.dtype)

    g0 = P_GATE // D
    sds = jax.ShapeDtypeStruct
    return pl.pallas_call(
        body, name=name, grid=(S // t,),
        in_specs=[_row_spec(t, D, g0), _row_spec(t, D, g0 + 1), _row_spec(t, D, g0 + 2)] + [_row_spec(t, D)] * 4,
        out_specs=[_row_spec(t, 3 * D)] + [_row_spec(t, D)] * 3,
        out_shape=[sds((S, 3 * D), MXU)] + [sds((S, D), MXU)] * 3,
        compiler_params=_params(("parallel",)),
    )(z, z, z, dm, pa, pb, pc)


def _heads_bwd(dpb, o, wb, name):
    S = dpb.shape[0]
    t = min(ROW_TILE, S)

    def body(dp_ref, o_ref, wb_ref, dob_ref, do_ref, dwb_ref):
        first = pl.program_id(0) == 0
        dp = dp_ref[...]
        for h in range(HEADS):
            do = _dot_nt(dp, wb_ref[h])
            do_ref[h] = do
            dob_ref[h] = do.astype(dob_ref.dtype)
            _acc(dwb_ref.at[h], first, _dot_tn(o_ref[h], dp))

    hspec = pl.BlockSpec((HEADS, t, VD), lambda i: (0, i, 0))
    sds = jax.ShapeDtypeStruct
    return pl.pallas_call(
        body, name=name, grid=(S // t,),
        in_specs=[_row_spec(t, D), hspec, _full_spec((HEADS, VD, D))],
        out_specs=[hspec, hspec, _full_spec((HEADS, VD, D))],
        out_shape=[sds((HEADS, S, VD), MXU), sds((HEADS, S, VD), F32), sds((HEADS, VD, D), F32)],
        compiler_params=_params(("arbitrary",)),
    )(dpb, o, wb)


def _mod_fwd(c_all, ada_w, name):
    n = ada_w.shape[2]

    def body(c_ref, w_ref, o_ref):
        c = c_ref[...]
        ca = (c * _sigmoid(c)).astype(MXU)
        for l in range(DEPTH):
            o_ref[l] = _dot(ca, w_ref[l].astype(MXU))

    return pl.pallas_call(body, name=name, out_shape=jax.ShapeDtypeStruct((DEPTH, 8, n), F32), compiler_params=_params())(c_all, ada_w)


def _ada_w_grad(c_all_t, dmod, name):
    n = dmod.shape[2]

    def body(c_ref, d_ref, o_ref):
        c = c_ref[...]
        ca = c * _sigmoid(c)
        for l in range(DEPTH):
            dm = d_ref[l]
            acc = ca[:, 0:1] * dm[0:1, :]
            for b in range(1, 8):
                acc = acc + ca[:, b:b + 1] * dm[b:b + 1, :]
            o_ref[l] = acc

    return pl.pallas_call(body, name=name, out_shape=jax.ShapeDtypeStruct((DEPTH, D, n), F32), compiler_params=_params())(c_all_t, dmod)


def _adamw(w, g, m, v, name):
    R, C = w.shape
    t = _tile(R, 256) if R % 8 == 0 else R
    c1, c2 = 1.0 - ADAM_B1 ** ADAM_STEP, 1.0 - ADAM_B2 ** ADAM_STEP

    def body(w_ref, g_ref, m_ref, v_ref, d_ref, nm_ref, nv_ref):
        g_ = g_ref[...]
        m_ = ADAM_B1 * m_ref[...] + (1.0 - ADAM_B1) * g_
        v_ = ADAM_B2 * v_ref[...] + (1.0 - ADAM_B2) * (g_ * g_)
        nm_ref[...] = m_
        nv_ref[...] = v_
        d_ref[...] = -ADAM_LR * ((m_ / c1) / (jnp.sqrt(v_ / c2) + ADAM_EPS) + ADAM_WD * w_ref[...])

    spec = pl.BlockSpec((t, C), lambda i: (i, 0))
    return pl.pallas_call(
        body, name=name, grid=(R // t,), in_specs=[spec] * 4, out_specs=[spec] * 3,
        out_shape=[jax.ShapeDtypeStruct((R, C), F32)] * 3, compiler_params=_params(("parallel",)),
    )(w, g, m, v)


def _sum8(a, name):
    _, R, C = a.shape
    t = _tile(R, 512) if R % 8 == 0 else R

    def body(a_ref, o_ref):
        s = a_ref[0]
        for k in range(1, 8):
            s = s + a_ref[k]
        o_ref[...] = s

    return pl.pallas_call(
        body, name=name, grid=(R // t,), in_specs=[pl.BlockSpec((8, t, C), lambda i: (0, i, 0))],
        out_specs=pl.BlockSpec((t, C), lambda i: (i, 0)), out_shape=jax.ShapeDtypeStruct((R, C), F32),
        compiler_params=_params(("parallel",)),
    )(a)


def _pair_add(a, b, name):
    shape = a.shape
    a2, b2 = a.reshape(-1, shape[-1]), b.reshape(-1, shape[-1])
    R, C = a2.shape
    t = _tile(R, 512, 16)

    def body(a_ref, b_ref, o_ref, ob_ref):
        s = a_ref[...] + b_ref[...]
        o_ref[...] = s
        ob_ref[...] = s.astype(ob_ref.dtype)

    spec = pl.BlockSpec((t, C), lambda i: (i, 0))
    o, ob = pl.pallas_call(
        body, name=name, grid=(R // t,), in_specs=[spec, spec], out_specs=[spec, spec],
        out_shape=[jax.ShapeDtypeStruct((R, C), F32), jax.ShapeDtypeStruct((R, C), jnp.bfloat16)],
        compiler_params=_params(("parallel",)),
    )(a2, b2)
    return o.reshape(shape), ob.reshape(shape)


def _chip_sum(own, recv, name):
    R, C = own.shape
    t = _tile(R, 512, 16)

    def body(a_ref, r_ref, o_ref):
        s = a_ref[...]
        for k in range(3):
            s = s + r_ref[k].astype(F32)
        o_ref[...] = s

    return pl.pallas_call(
        body, name=name, grid=(R // t,),
        in_specs=[pl.BlockSpec((t, C), lambda i: (i, 0)), pl.BlockSpec((3, t, C), lambda i: (0, i, 0))],
        out_specs=pl.BlockSpec((t, C), lambda i: (i, 0)), out_shape=jax.ShapeDtypeStruct((R, C), F32),
        compiler_params=_params(("parallel",)),
    )(own, recv)


_ANY = pl.BlockSpec(memory_space=pl.ANY)


def _coords():
    return lax.axis_index("x"), lax.axis_index("y"), lax.axis_index("c")


def _allgather8(v, name):
    R, C = v.shape

    def body(v_ref, out_ref, send_sems, recv_sems, local_sem):
        x, y, c = _coords()
        me = 4 * x + 2 * y + c
        mine = pltpu.make_async_copy(v_ref, out_ref.at[me], local_sem)
        mine.start()
        sends = []
        for k in range(1, 8):
            bx, by, bc = (k >> 2) & 1, (k >> 1) & 1, k & 1
            peer = (x ^ bx, y ^ by, c ^ bc)
            cp = pltpu.make_async_remote_copy(src_ref=v_ref, dst_ref=out_ref.at[me], send_sem=send_sems.at[k - 1],
                                              recv_sem=recv_sems.at[k - 1], device_id=peer, device_id_type=MESH)
            cp.start()
            sends.append(cp)
        for k in range(1, 8):
            bx, by, bc = (k >> 2) & 1, (k >> 1) & 1, k & 1
            peer = (x ^ bx, y ^ by, c ^ bc)
            src = 4 * peer[0] + 2 * peer[1] + peer[2]
            pltpu.make_async_remote_copy(src_ref=v_ref, dst_ref=out_ref.at[src], send_sem=send_sems.at[k - 1],
                                         recv_sem=recv_sems.at[k - 1], device_id=peer, device_id_type=MESH).wait_recv()
        for cp in sends:
            cp.wait_send()
        mine.wait()

    return pl.pallas_call(
        body, name=name, in_specs=[_ANY], out_specs=_ANY, out_shape=jax.ShapeDtypeStruct((8, R, C), v.dtype),
        scratch_shapes=[pltpu.SemaphoreType.DMA((7,)), pltpu.SemaphoreType.DMA((7,)), pltpu.SemaphoreType.DMA(())],
    )(v)


def _chip_patterns(x, y):
    return [(1 - x, y), (x, 1 - y), (1 - x, 1 - y)]


def _allgather_weights(w, name):
    R, C = w.shape
    rh = R // 2

    def body(w_ref, out_ref, send_sems, recv_sems, local_sem):
        x, y, c = _coords()
        kme = 2 * x + y
        sibling = (x, y, 1 - c)
        mine_half = pl.ds(pl.multiple_of(c * rh, FLAT_ALIGN // 2), rh)
        other_half = pl.ds(pl.multiple_of((1 - c) * rh, FLAT_ALIGN // 2), rh)
        chips = _chip_patterns(x, y)
        mine = pltpu.make_async_copy(w_ref, out_ref.at[kme], local_sem)
        mine.start()

        def copy(k, chip_idx, half, to, src=None):
            dst = out_ref.at[chip_idx, half, :]
            return pltpu.make_async_remote_copy(src_ref=dst if src is None else src, dst_ref=dst, send_sem=send_sems.at[k],
                                                recv_sem=recv_sems.at[k], device_id=to, device_id_type=MESH)

        first = [copy(j, kme, mine_half, (cx, cy, c), src=w_ref.at[mine_half, :]) for j, (cx, cy) in enumerate(chips)]
        for cp in first:
            cp.start()
        passed = []
        for j, (cx, cy) in enumerate(chips):
            kj = 2 * cx + cy
            copy(j, kj, mine_half, (x, y, c)).wait_recv()
            fw = copy(3 + j, kj, mine_half, sibling)
            fw.start()
            passed.append(fw)
        for j, (cx, cy) in enumerate(chips):
            copy(3 + j, 2 * cx + cy, other_half, (x, y, c)).wait_recv()
        for cp in first + passed:
            cp.wait_send()
        mine.wait()

    return pl.pallas_call(
        body, name=name, in_specs=[_ANY], out_specs=_ANY, out_shape=jax.ShapeDtypeStruct((4, R, C), w.dtype),
        scratch_shapes=[pltpu.SemaphoreType.DMA((6,)), pltpu.SemaphoreType.DMA((6,)), pltpu.SemaphoreType.DMA(())],
    )(w)


def _sibling_swap(g, name):
    _, R, C = g.shape
    rh = R // 2

    def body(g_ref, out_ref, send_sem, recv_sem):
        x, y, c = _coords()
        other_half = pl.ds(pl.multiple_of((1 - c) * rh, 8), rh)
        cp = pltpu.make_async_remote_copy(src_ref=g_ref.at[:, other_half, :], dst_ref=out_ref, send_sem=send_sem,
                                          recv_sem=recv_sem, device_id=(x, y, 1 - c), device_id_type=MESH)
        cp.start()
        cp.wait()

    return pl.pallas_call(
        body, name=name, in_specs=[_ANY], out_specs=_ANY, out_shape=jax.ShapeDtypeStruct((4, rh, C), g.dtype),
        scratch_shapes=[pltpu.SemaphoreType.DMA(()), pltpu.SemaphoreType.DMA(())],
    )(g)


def _chip_scatter(a, name):
    _, R, C = a.shape

    def body(a_ref, out_ref, send_sems, recv_sems):
        x, y, c = _coords()
        cps = []
        for j, (cx, cy) in enumerate(_chip_patterns(x, y)):
            cp = pltpu.make_async_remote_copy(src_ref=a_ref.at[2 * cx + cy], dst_ref=out_ref.at[j], send_sem=send_sems.at[j],
                                              recv_sem=recv_sems.at[j], device_id=(cx, cy, c), device_id_type=MESH)
            cp.start()
            cps.append(cp)
        for cp in cps:
            cp.wait()

    return pl.pallas_call(
        body, name=name, in_specs=[_ANY], out_specs=_ANY, out_shape=jax.ShapeDtypeStruct((3, R, C), a.dtype),
        scratch_shapes=[pltpu.SemaphoreType.DMA((3,)), pltpu.SemaphoreType.DMA((3,))],
    )(a)


def _sibling_join(f, name):
    rh, C = f.shape

    def body(f_ref, out_ref, send_sem, recv_sem, local_sem):
        x, y, c = _coords()
        mine_half = pl.ds(pl.multiple_of(c * rh, 8), rh)
        mine = pltpu.make_async_copy(f_ref, out_ref.at[mine_half, :], local_sem)
        mine.start()
        cp = pltpu.make_async_remote_copy(src_ref=f_ref, dst_ref=out_ref.at[mine_half, :], send_sem=send_sem, recv_sem=recv_sem,
                                          device_id=(x, y, 1 - c), device_id_type=MESH)
        cp.start()
        cp.wait()
        mine.wait()

    return pl.pallas_call(
        body, name=name, in_specs=[_ANY], out_specs=_ANY, out_shape=jax.ShapeDtypeStruct((2 * rh, C), f.dtype),
        scratch_shapes=[pltpu.SemaphoreType.DMA(()), pltpu.SemaphoreType.DMA(()), pltpu.SemaphoreType.DMA(())],
    )(f)


_BIG = (("in_w", 2), ("mla_wuq", 2), ("mla_wukv", 2), ("branch_w", 3), ("mix_out_w", 1), ("ffn_w1", 2), ("ffn_w2", 1))


def _pack(shards):
    flat = jnp.concatenate([shards[n].reshape(-1, FLAT_W) for n, _ in _BIG], axis=0)
    pad = (-flat.shape[0]) % FLAT_ALIGN
    return jnp.pad(flat, ((0, pad), (0, 0)))


def _unpack(flat, shapes):
    out, r = {}, 0
    for n, _ in _BIG:
        rows = math.prod(shapes[n]) // FLAT_W
        out[n] = flat[r:r + rows].reshape(shapes[n])
        r += rows
    return out


def _swap16(a):
    return jnp.concatenate([a[..., 16:32], a[..., 0:16]], axis=-1)


def _pad_in_cols(w):
    lead = w.shape[:-1]
    z = lambda n: jnp.zeros(lead + (n,), w.dtype)
    kr = w[..., O_KR:O_LX]
    return jnp.concatenate([w[..., O_GM:O_Q], w[..., O_GATE:], w[..., O_LX:O_LG], w[..., O_LG:O_GATE], w[..., O_Q:O_KV],
                            w[..., O_KV:O_KR], z(64), kr, z(32), z(64), _swap16(kr), z(32)], axis=-1)


def _unpad_in_cols(g):
    kr = g[..., P_KA + 64:P_KA + 96] + _swap16(g[..., P_KB + 64:P_KB + 96])
    return jnp.concatenate([g[..., P_GM:P_GATE], g[..., P_Q:P_KV], g[..., P_KV:P_KA], kr, g[..., P_LX:P_LG], g[..., P_LG:P_Q],
                            g[..., P_GATE:P_LX]], axis=-1)


def _flat128(vecs):
    flat = jnp.concatenate([v.reshape(-1) for v in vecs])
    pad = (-flat.shape[0]) % 1024
    return jnp.pad(flat, (0, pad)).reshape(-1, 128)


def _unflat(flat, shapes):
    flat = flat.reshape(-1)
    out, r = [], 0
    for s in shapes:
        n = math.prod(s)
        out.append(flat[r:r + n].reshape(s))
        r += n
    return out


_SMALL = ("ada_b", "in_b", "gm_ln_g", "gm_ln_b", "gm_ws", "gm_bs", "mla_qnorm_g", "mla_kvnorm_g", "lru_conv_w", "lru_conv_b",
          "lru_wr", "lru_br", "lru_wi", "lru_bi", "lru_lambda", "ffn_b1", "ffn_b2", "ln_g", "ln_b")
_NAMES = ("ada_w", "ada_b", "in_w", "in_b", "gm_ln_g", "gm_ln_b", "gm_ws", "gm_bs", "mla_qnorm_g", "mla_wuq", "mla_kvnorm_g",
          "mla_wukv", "lru_conv_w", "lru_conv_b", "lru_wr", "lru_br", "lru_wi", "lru_bi", "lru_lambda", "branch_w", "mix_out_w",
          "ffn_w1", "ffn_b1", "ffn_w2", "ffn_b2", "ln_g", "ln_b")


def kernel(x, c, ada_w, ada_b, in_w, in_b, gm_ln_g, gm_ln_b, gm_ws, gm_bs, mla_qnorm_g, mla_wuq, mla_kvnorm_g, mla_wukv, lru_conv_w, lru_conv_b, lru_wr, lru_br, lru_wi, lru_bi, lru_lambda, branch_w, mix_out_w, ffn_w1, ffn_b1, ffn_w2, ffn_b2, ln_g, ln_b, loss_target, m_ada_w, m_ada_b, m_in_w, m_in_b, m_gm_ln_g, m_gm_ln_b, m_gm_ws, m_gm_bs, m_mla_qnorm_g, m_mla_wuq, m_mla_kvnorm_g, m_mla_wukv, m_lru_conv_w, m_lru_conv_b, m_lru_wr, m_lru_br, m_lru_wi, m_lru_bi, m_lru_lambda, m_branch_w, m_mix_out_w, m_ffn_w1, m_ffn_b1, m_ffn_w2, m_ffn_b2, m_ln_g, m_ln_b, v_ada_w, v_ada_b, v_in_w, v_in_b, v_gm_ln_g, v_gm_ln_b, v_gm_ws, v_gm_bs, v_mla_qnorm_g, v_mla_wuq, v_mla_kvnorm_g, v_mla_wukv, v_lru_conv_w, v_lru_conv_b, v_lru_wr, v_lru_br, v_lru_wi, v_lru_bi, v_lru_lambda, v_branch_w, v_mix_out_w, v_ffn_w1, v_ffn_b1, v_ffn_w2, v_ffn_b2, v_ln_g, v_ln_b):
    W = dict(ada_w=ada_w, ada_b=ada_b, in_w=in_w, in_b=in_b, gm_ln_g=gm_ln_g, gm_ln_b=gm_ln_b, gm_ws=gm_ws, gm_bs=gm_bs,
             mla_qnorm_g=mla_qnorm_g, mla_wuq=mla_wuq, mla_kvnorm_g=mla_kvnorm_g, mla_wukv=mla_wukv, lru_conv_w=lru_conv_w,
             lru_conv_b=lru_conv_b, lru_wr=lru_wr, lru_br=lru_br, lru_wi=lru_wi, lru_bi=lru_bi, lru_lambda=lru_lambda,
             branch_w=branch_w, mix_out_w=mix_out_w, ffn_w1=ffn_w1, ffn_b1=ffn_b1, ffn_w2=ffn_w2, ffn_b2=ffn_b2, ln_g=ln_g, ln_b=ln_b)
    M = dict(ada_w=m_ada_w, ada_b=m_ada_b, in_w=m_in_w, in_b=m_in_b, gm_ln_g=m_gm_ln_g, gm_ln_b=m_gm_ln_b, gm_ws=m_gm_ws,
             gm_bs=m_gm_bs, mla_qnorm_g=m_mla_qnorm_g, mla_wuq=m_mla_wuq, mla_kvnorm_g=m_mla_kvnorm_g, mla_wukv=m_mla_wukv,
             lru_conv_w=m_lru_conv_w, lru_conv_b=m_lru_conv_b, lru_wr=m_lru_wr, lru_br=m_lru_br, lru_wi=m_lru_wi, lru_bi=m_lru_bi,
             lru_lambda=m_lru_lambda, branch_w=m_branch_w, mix_out_w=m_mix_out_w, ffn_w1=m_ffn_w1, ffn_b1=m_ffn_b1, ffn_w2=m_ffn_w2,
             ffn_b2=m_ffn_b2, ln_g=m_ln_g, ln_b=m_ln_b)
    V = dict(ada_w=v_ada_w, ada_b=v_ada_b, in_w=v_in_w, in_b=v_in_b, gm_ln_g=v_gm_ln_g, gm_ln_b=v_gm_ln_b, gm_ws=v_gm_ws,
             gm_bs=v_gm_bs, mla_qnorm_g=v_mla_qnorm_g, mla_wuq=v_mla_wuq, mla_kvnorm_g=v_mla_kvnorm_g, mla_wukv=v_mla_wukv,
             lru_conv_w=v_lru_conv_w, lru_conv_b=v_lru_conv_b, lru_wr=v_lru_wr, lru_br=v_lru_br, lru_wi=v_lru_wi, lru_bi=v_lru_bi,
             lru_lambda=v_lru_lambda, branch_w=v_branch_w, mix_out_w=v_mix_out_w, ffn_w1=v_ffn_w1, ffn_b1=v_ffn_b1, ffn_w2=v_ffn_w2,
             ffn_b2=v_ffn_b2, ln_g=v_ln_g, ln_b=v_ln_b)

    S = x.shape[1]
    xi, yi, ci = _coords()
    kme = 2 * xi + yi
    me = 4 * xi + 2 * yi + ci
    x0 = x[0]
    tgt = loss_target[0]

    shard_shapes = {n: W[n].shape for n, _ in _BIG}
    gathered = _allgather_weights(_pack({n: W[n].astype(MXU) for n, _ in _BIG}), "ag_weights")
    parts = [_unpack(gathered[k], shard_shapes) for k in range(4)]
    full = {n: jnp.concatenate([p[n] for p in parts], axis=ax) for n, ax in _BIG}

    small1 = _allgather8(_flat128([c, ln_g, ln_b, lru_conv_w]), "ag_small")
    per_dev = [_unflat(small1[d], [c.shape, ln_g.shape, ln_b.shape, lru_conv_w.shape]) for d in range(8)]
    c_all = jnp.concatenate([p[0] for p in per_dev], axis=0)
    chip = lambda i: [per_dev[2 * k][i] for k in range(4)]
    ln_g_f, ln_b_f, conv_w_f = (jnp.concatenate(chip(1), axis=2), jnp.concatenate(chip(2), axis=2), jnp.concatenate(chip(3), axis=2))

    mod_sh = _mod_fwd(c_all, ada_w, "mod_fwd")
    mod_all = _allgather8(_flat128([mod_sh]), "ag_mod")
    mod_parts = [_unflat(mod_all[2 * k], [mod_sh.shape])[0] for k in range(4)]
    mod = jnp.concatenate(mod_parts, axis=2)
    mod = lax.dynamic_index_in_dim(mod, me, axis=1, keepdims=False) + ada_b

    pos = jnp.arange(S, dtype=F32)
    inv = ROPE_BASE ** (-jnp.arange(0, ROPE, 2, dtype=F32) / ROPE)
    ang = pos[:, None] * inv[None, :]
    cs, sn = jnp.cos(ang), jnp.sin(ang)
    zc = lambda n: jnp.zeros((S, n), F32)
    cos_t = jnp.concatenate([zc(64), cs, cs, zc(32)], axis=1)
    sin_t = jnp.concatenate([zc(64), -sn, sn, zc(32)], axis=1)

    win_p = _pad_in_cols(full["in_w"])
    bin_p = _pad_in_cols(in_b)
    wuq = full["mla_wuq"].reshape(DEPTH, Q_RANK, HEADS, NOPE + ROPE)
    zq = lambda n: jnp.zeros((DEPTH, Q_RANK, HEADS, n), MXU)
    w1q = jnp.concatenate([wuq, zq(32)], axis=-1).reshape(DEPTH, Q_RANK, D)
    w2q = jnp.concatenate([zq(64), _swap16(wuq[..., NOPE:]), zq(32)], axis=-1).reshape(DEPTH, Q_RANK, D)
    wukv = full["mla_wukv"].reshape(DEPTH, KV_RANK, HEADS, NOPE + VD)
    wk_p = jnp.concatenate([wukv[..., :NOPE], jnp.zeros((DEPTH, KV_RANK, HEADS, 64), MXU)], axis=-1).reshape(DEPTH, KV_RANK, D)
    wv_h = wukv[..., NOPE:].transpose(0, 2, 1, 3)
    wb_h = full["branch_w"][:, 1].reshape(DEPTH, HEADS, VD, D)
    wr_f = jnp.stack([block_diag(*[lru_wr[l, b] for b in range(LRU_NB)]) for l in range(DEPTH)]).astype(MXU)
    wi_f = jnp.stack([block_diag(*[lru_wi[l, b] for b in range(LRU_NB)]) for l in range(DEPTH)]).astype(MXU)
    ws_b = gm_ws.astype(MXU)
    wst_b = gm_ws.transpose(0, 1, 3, 2).astype(MXU)
    bs_t = gm_bs.transpose(0, 2, 1)
    row = lambda v: v.reshape(1, -1)

    saved = []
    xs = x0
    h1 = None
    for l in range(DEPTH):
        sh1, sc1, g1, sh2, sc2, g2 = [row(mod[l, i * D:(i + 1) * D]) for i in range(6)]
        if l == 0:
            h1 = _lnmod_fwd(xs, sc1, sh1, f"lnmod_fwd{l}")
        z = _mm(h1, win_p[l], "nn", f"in_proj{l}", rows=[row(bin_p[l])], epi=lambda acc, b: (acc + b,))
        ya = _gmlp_fwd(z, row(gm_ln_g[l]), row(gm_ln_b[l]), ws_b[l], bs_t[l], f"gmlp_fwd{l}")
        qf, kf, vf = _mla_prep_fwd(z, cos_t, sin_t, row(mla_qnorm_g[l]), row(mla_kvnorm_g[l]), w1q[l], w2q[l], wk_p[l], wv_h[l],
                                   f"mla_prep_fwd{l}")
        ob, o32, lse = _attn_fwd(qf, kf, vf, f"attn_fwd{l}")
        lru_args = (conv_w_f[l], row(lru_conv_b[l]), wr_f[l], row(lru_br[l]), wi_f[l], row(lru_bi[l]), row(lru_lambda[l]))
        yc, hl = _lru_fwd(z, *lru_args, f"lru_fwd{l}")
        merged, pa, pb, pc = _branch_fwd(z, ya, ob, yc, full["branch_w"][l, 0], wb_h[l], full["branch_w"][l, 2], f"branch_fwd{l}")
        mix = _mm(merged, full["mix_out_w"][l], "nn", f"mix_out{l}")
        x1, h2 = _res_ln_fwd(xs, mix, g1, row(ln_g_f[l, 0]), row(ln_b_f[l, 0]), f"res_ln_a{l}", mod=(sc2, sh2))
        a1, r2 = _mm(h2, full["ffn_w1"][l], "nn", f"ffn_up{l}", out_dtypes=(F32, MXU), rows=[row(ffn_b1[l])],
                     epi=lambda acc, b: (acc + b, jnp.square(jnp.maximum(acc + b, 0.0))))
        f = _mm(r2, full["ffn_w2"][l], "nn", f"ffn_down{l}", rows=[row(ffn_b2[l])], epi=lambda acc, b: (acc + b,))
        if l + 1 < DEPTH:
            nsh1, nsc1 = row(mod[l + 1, 0:D]), row(mod[l + 1, D:2 * D])
            x2, h1n = _res_ln_fwd(x1, f, g2, row(ln_g_f[l, 1]), row(ln_b_f[l, 1]), f"res_ln_b{l}", mod=(nsc1, nsh1))
        else:
            (x2,), h1n = _res_ln_fwd(x1, f, g2, row(ln_g_f[l, 1]), row(ln_b_f[l, 1]), f"res_ln_b{l}"), None
        saved.append(dict(x_in=xs, h1=h1, z=z, ya=ya, qf=qf, kf=kf, vf=vf, ob=ob, o32=o32, lse=lse, yc=yc, hl=hl, merged=merged,
                          pa=pa, pb=pb, pc=pc, mix=mix, x1=x1, h2=h2, a1=a1, r2=r2, f=f, x2=x2, lru_args=lru_args,
                          mods=(sh1, sc1, g1, sh2, sc2, g2)))
        xs, h1 = x2, h1n

    dy, sq = _loss_fwd(xs, tgt)
    loss = lax.psum(0.5 / D * jnp.sum(sq), ("x", "y", "c"))

    G = {}
    dmods = []
    dres, hpath = dy, None
    for l in reversed(range(DEPTH)):
        sv = saved[l]
        sh1, sc1, g1, sh2, sc2, g2 = sv["mods"]
        gl = {}
        nb = _node_bwd(f"node_b{l}", dres, hpath=hpath, upath=(sv["x1"], sv["f"], g2, row(ln_g_f[l, 1])))
        if hpath is not None:
            dmods[-1]["sc1"], dmods[-1]["sh1"] = nb["dsc"], nb["dsh"]
        dm = dict(g2=nb["dg"])
        ln_g_l1, ln_b_l1 = nb["dlng"], nb["dlnb"]
        df = nb["dbr"]
        gl["ffn_b2"] = _colsum_call(df, f"d_ffn_b2_{l}")[0]
        gl["ffn_w2"] = _mm(sv["r2"], df, "tn", f"d_ffn_w2_{l}")
        da1 = _mm(df, full["ffn_w2"][l], "nt", f"d_ffn_act{l}", out_dtypes=(MXU,), extras=[sv["a1"]],
                  epi=lambda acc, a: (acc * (2.0 * jnp.maximum(a, 0.0)),))
        gl["ffn_b1"] = _colsum_call(da1, f"d_ffn_b1_{l}")[0]
        gl["ffn_w1"] = _mm(sv["h2"], da1, "tn", f"d_ffn_w1_{l}")
        dh2 = _mm(da1, full["ffn_w1"][l], "nt", f"d_ffn_in{l}")
        na = _node_bwd(f"node_a{l}", nb["du"], hpath=(dh2, sv["x1"], sc2), upath=(sv["x_in"], sv["mix"], g1, row(ln_g_f[l, 0])))
        dm.update(sc2=na["dsc"], sh2=na["dsh"], g1=na["dg"])
        gl["ln_g"] = jnp.concatenate([na["dlng"], ln_g_l1], axis=0)
        gl["ln_b"] = jnp.concatenate([na["dlnb"], ln_b_l1], axis=0)
        dmix = na["dbr"]
        gl["mix_out_w"] = _mm(sv["merged"], dmix, "tn", f"d_mix_w{l}")
        dmerged = _mm(dmix, full["mix_out_w"][l], "nt", f"d_merged{l}")
        dzg, dpa, dpb, dpc = _branch_bwd(sv["z"], dmerged, sv["pa"], sv["pb"], sv["pc"], f"branch_bwd{l}")
        wa, wc = full["branch_w"][l, 0], full["branch_w"][l, 2]
        dya = _mm(dpa, wa, "nt", f"d_ya{l}")
        dyc = _mm(dpc, wc, "nt", f"d_yc{l}")
        dwa = _mm(sv["ya"], dpa, "tn", f"d_wa{l}")
        dwc = _mm(sv["yc"], dpc, "tn", f"d_wc{l}")
        dob, do32, dwb = _heads_bwd(dpb, sv["ob"], wb_h[l], f"heads_bwd{l}")
        gl["branch_w"] = jnp.stack([dwa, dwb.reshape(GM_W, D), dwc])
        dzl, dcw, dcb, dwr, dbr_, dwi, dbi, dlam = _lru_bwd(sv["z"], sv["hl"], dyc, *sv["lru_args"], f"lru_bwd{l}")
        gl["lru_conv_w"], gl["lru_conv_b"], gl["lru_br"], gl["lru_bi"], gl["lru_lambda"] = dcw, dcb[0], dbr_[0], dbi[0], dlam[0]
        blocks = lambda m: jnp.stack([m[b * 64:(b + 1) * 64, b * 64:(b + 1) * 64] for b in range(LRU_NB)])
        gl["lru_wr"], gl["lru_wi"] = blocks(dwr), blocks(dwi)
        delta = _attn_delta(sv["o32"], do32, f"attn_delta{l}")
        dqf, dkf, dvf = _attn_bwd(sv["qf"], sv["kf"], sv["vf"], dob, sv["lse"].reshape(HEADS, 1, S), delta.reshape(HEADS, 1, S),
                                  f"attn_bwd{l}")
        dzm, dgq, dgkv, dw1, dw2, dwk, dwv = _mla_prep_bwd(sv["z"], cos_t, sin_t, row(mla_qnorm_g[l]), row(mla_kvnorm_g[l]), w1q[l],
                                                           w2q[l], wk_p[l], wv_h[l], dqf, dkf, dvf, f"mla_prep_bwd{l}")
        gl["mla_qnorm_g"], gl["mla_kvnorm_g"] = dgq[0], dgkv[0]
        dw1 = dw1.reshape(Q_RANK, HEADS, 128)
        dw2 = dw2.reshape(Q_RANK, HEADS, 128)
        gl["mla_wuq"] = jnp.concatenate([dw1[..., :NOPE], dw1[..., NOPE:NOPE + ROPE] + _swap16(dw2[..., NOPE:NOPE + ROPE])],
                                        axis=-1).reshape(Q_RANK, HEADS * (NOPE + ROPE))
        gl["mla_wukv"] = jnp.concatenate([dwk.reshape(KV_RANK, HEADS, 128)[..., :NOPE], dwv.transpose(1, 0, 2)],
                                         axis=-1).reshape(KV_RANK, HEADS * (NOPE + VD))
        dzgm, dglg, dglb, dws, dbs = _gmlp_bwd(sv["z"], dya, row(gm_ln_g[l]), row(gm_ln_b[l]), ws_b[l], wst_b[l], bs_t[l],
                                               f"gmlp_bwd{l}")
        gl["gm_ln_g"], gl["gm_ln_b"], gl["gm_ws"], gl["gm_bs"] = dglg[0], dglb[0], dws, dbs[:, :GM_G].T
        dz = jnp.concatenate([dzgm, dzg, dzl, dzm], axis=1)
        gl["in_b"] = _unpad_in_cols(_colsum_call(dz, f"d_in_b{l}"))[0]
        gl["in_w"] = _unpad_in_cols(_mm(sv["h1"], dz, "tn", f"d_in_w{l}"))
        dh1 = _mm(dz, win_p[l], "nt", f"d_h1_{l}")
        dmods.append(dm)
        dres, hpath = na["du"], (dh1, sv["x_in"], sc1)
        G[l] = gl
    n0 = _node_bwd("node_in", dres, hpath=hpath)
    dmods[-1]["sc1"], dmods[-1]["sh1"] = n0["dsc"], n0["dsh"]
    grad_x = n0["dx"][None]
    dmods = dmods[::-1]
    dmod = jnp.stack([jnp.concatenate([dmods[l][k] for k in ("sh1", "sc1", "g1", "sh2", "sc2", "g2")], axis=1)[0]
                      for l in range(DEPTH)])
    grads = {n: jnp.stack([G[l][n] for l in range(DEPTH)]) for n in G[0]}
    grads["ada_b"] = dmod

    gsh = jnp.stack([_pack({n: jnp.split(grads[n], 4, axis=ax)[k] for n, ax in _BIG}) for k in range(4)])
    R = gsh.shape[1]
    rh = R // 2
    from_sib = _sibling_swap(gsh, "rs_pair")
    mine_half = lax.dynamic_slice_in_dim(gsh, ci * rh, rh, axis=1)
    pair, pair_b = _pair_add(mine_half, from_sib, "rs_pair_add")
    from_chips = _chip_scatter(pair_b, "rs_chips")
    own = lax.dynamic_index_in_dim(pair, kme, axis=0, keepdims=False)
    half_sum = _chip_sum(own, from_chips, "rs_chip_sum")
    gflat = _sibling_join(half_sum, "rs_join")
    gbig = _unpack(gflat, shard_shapes)

    small_shapes = [grads[n].shape for n in _SMALL]
    gsm_all = _allgather8(_flat128([grads[n] for n in _SMALL]), "ag_small_grads")
    gsm = _unflat(_sum8(gsm_all, "sum_small_grads"), small_shapes)
    gsmall = dict(zip(_SMALL, gsm))
    dmod_all = jnp.stack([_unflat(gsm_all[d], small_shapes)[0] for d in range(8)], axis=1)
    dmod_sh = lax.dynamic_slice_in_dim(dmod_all, kme * (6 * D // 4), 6 * D // 4, axis=2)
    g_ada_w = _ada_w_grad(c_all.T, dmod_sh, "d_ada_w")
    quarter = lambda g, ax: lax.dynamic_slice_in_dim(g, kme * (g.shape[ax] // 4), g.shape[ax] // 4, axis=ax)
    gsmall["lru_conv_w"] = quarter(gsmall["lru_conv_w"], 2)
    gsmall["ln_g"] = quarter(gsmall["ln_g"], 2)
    gsmall["ln_b"] = quarter(gsmall["ln_b"], 2)

    grad = dict(gbig)
    grad.update(gsmall)
    grad["ada_w"] = g_ada_w

    delta, new_m, new_v = {}, {}, {}
    for n in ("ada_w",) + tuple(n for n, _ in _BIG):
        shp = W[n].shape
        two = lambda a: a.reshape(-1, shp[-1])
        d_, m_, v_ = _adamw(two(W[n]), two(grad[n]), two(M[n]), two(V[n]), f"adamw_{n}")
        delta[n], new_m[n], new_v[n] = d_.reshape(shp), m_.reshape(shp), v_.reshape(shp)
    sm_shapes = [W[n].shape for n in _SMALL]
    d_, m_, v_ = _adamw(_flat128([W[n] for n in _SMALL]), _flat128([grad[n] for n in _SMALL]), _flat128([M[n] for n in _SMALL]),
                        _flat128([V[n] for n in _SMALL]), "adamw_small")
    for n, a, b, c_ in zip(_SMALL, _unflat(d_, sm_shapes), _unflat(m_, sm_shapes), _unflat(v_, sm_shapes)):
        delta[n], new_m[n], new_v[n] = a, b, c_

    return (loss, grad_x, *[grad[n] for n in _NAMES], *[delta[n] for n in _NAMES], *[new_m[n] for n in _NAMES],
            *[new_v[n] for n in _NAMES])
```

```python
import functools
import math

import jax
import jax.numpy as jnp
from jax import lax
from jax.experimental import pallas as pl
from jax.experimental.pallas import tpu as pltpu
from jax.scipy.linalg import block_diag

F32 = jnp.float32
MXU = jnp.bfloat16
MESH = pl.DeviceIdType.MESH

D = 1024
DEPTH = 2
CHUNK = 64
GM_W = 512
GM_G = 4
HEADS = 8
Q_RANK = 256
KV_RANK = 128
NOPE = 64
ROPE = 32
VD = 64
LRU_W = 512
LRU_NB = 8
D_FF = 4096
ALPHA = (2.0 * DEPTH) ** 0.25
LN_EPS = 1e-5
RMS_EPS = 1e-6
ROPE_BASE = 10000.0
ATT_SCALE = (NOPE + ROPE) ** -0.5
N_IN = 5536
P_GATE, P_GM, P_LX, P_LG, P_Q, P_KV, P_KA, P_KB, NP = 0, 3072, 4096, 4608, 5120, 5376, 5504, 5632, 5760
O_GM, O_Q, O_KV, O_KR, O_LX, O_LG, O_GATE = 0, 1024, 1280, 1408, 1440, 1952, 2464

ADAM_LR, ADAM_B1, ADAM_B2, ADAM_EPS, ADAM_WD, ADAM_STEP = 0.001, 0.9, 0.999, 1e-08, 0.01, 10

ROW_TILE = 512
ATT_TILE = 512
VMEM_LIMIT = 56 * 1024 * 1024
FLAT_W = 1024
FLAT_ALIGN = 32


def _params(sem=None, vmem=VMEM_LIMIT):
    return pltpu.CompilerParams(dimension_semantics=sem, vmem_limit_bytes=vmem)


def _tile(dim, pref, mult=128):
    if dim <= pref:
        return dim
    t = (pref // mult) * mult
    while t >= mult:
        if dim % t == 0:
            return t
        t -= mult
    return dim


def _dot(a, b):
    return lax.dot_general(a, b, (((1,), (0,)), ((), ())), preferred_element_type=F32)


def _dot_nt(a, b):
    return lax.dot_general(a, b, (((1,), (1,)), ((), ())), preferred_element_type=F32)


def _dot_tn(a, b):
    return lax.dot_general(a, b, (((0,), (0,)), ((), ())), preferred_element_type=F32)


def _sigmoid(x):
    return 1.0 / (1.0 + jnp.exp(-x))


_GC = 0.7978845608028654


def _gelu(x):
    return 0.5 * x * (1.0 + jnp.tanh(_GC * (x + 0.044715 * x * x * x)))


def _gelu_grad(x):
    t = jnp.tanh(_GC * (x + 0.044715 * x * x * x))
    return 0.5 * (1.0 + t) + 0.5 * x * (1.0 - t * t) * _GC * (1.0 + 3.0 * 0.044715 * x * x)


def _ln_stats(x):
    mu = jnp.mean(x, axis=-1, keepdims=True)
    xc = x - mu
    r = lax.rsqrt(jnp.mean(xc * xc, axis=-1, keepdims=True) + LN_EPS)
    return xc * r, r


def _ln_bwd(dxh, xh, r):
    return r * (dxh - jnp.mean(dxh, axis=-1, keepdims=True) - xh * jnp.mean(dxh * xh, axis=-1, keepdims=True))


def _colsum(v):
    return jnp.sum(v, axis=0, keepdims=True)


def _acc(ref, first, val):
    @pl.when(first)
    def _():
        ref[...] = val

    @pl.when(jnp.logical_not(first))
    def _():
        ref[...] += val


def _row_spec(t, w, col=0):
    return pl.BlockSpec((t, w), lambda i, c=col: (i, c))


def _vec_spec(w):
    return pl.BlockSpec((1, w), lambda i: (0, 0))


def _full_spec(shape):
    nd = len(shape)
    return pl.BlockSpec(shape, lambda i, n=nd: (0,) * n)


def _mm(a, b, mode, name, out_dtypes=(F32,), extras=(), rows=(), epi=None, tm=1024, tn=1152, tk=1024):
    if mode == "nn":
        (M, K), N = a.shape, b.shape[1]
    elif mode == "nt":
        (M, K), N = a.shape, b.shape[0]
    else:
        (K, M), N = a.shape, b.shape[1]
    tm, tn, tk = _tile(M, tm), _tile(N, tn), _tile(K, tk)
    nk = K // tk
    dot = {"nn": _dot, "nt": _dot_nt, "tn": _dot_tn}[mode]
    a_spec = pl.BlockSpec((tk, tm), lambda i, j, k: (k, i)) if mode == "tn" else pl.BlockSpec((tm, tk), lambda i, j, k: (i, k))
    b_spec = pl.BlockSpec((tn, tk), lambda i, j, k: (j, k)) if mode == "nt" else pl.BlockSpec((tk, tn), lambda i, j, k: (k, j))
    o_spec = pl.BlockSpec((tm, tn), lambda i, j, k: (i, j))
    r_spec = pl.BlockSpec((1, tn), lambda i, j, k: (0, j))
    n_e, n_r, n_o = len(extras), len(rows), len(out_dtypes)
    if epi is None:
        epi = lambda acc: (acc,)

    def body(*refs):
        a_ref, b_ref = refs[0], refs[1]
        e_refs = refs[2:2 + n_e]
        r_refs = refs[2 + n_e:2 + n_e + n_r]
        o_refs = refs[2 + n_e + n_r:2 + n_e + n_r + n_o]
        p = dot(a_ref[...], b_ref[...])

        def finish(acc):
            outs = epi(acc, *[e[...] for e in e_refs], *[r[...] for r in r_refs])
            for o_ref, o in zip(o_refs, outs):
                o_ref[...] = o.astype(o_ref.dtype)

        if nk == 1:
            finish(p)
        else:
            acc_ref = refs[-1]
            k = pl.program_id(2)
            _acc(acc_ref, k == 0, p)

            @pl.when(k == nk - 1)
            def _():
                finish(acc_ref[...])

    outs = pl.pallas_call(
        body, name=name, grid=(M // tm, N // tn, nk),
        in_specs=[a_spec, b_spec] + [o_spec] * n_e + [r_spec] * n_r,
        out_specs=[o_spec] * n_o,
        out_shape=[jax.ShapeDtypeStruct((M, N), dt) for dt in out_dtypes],
        scratch_shapes=[pltpu.VMEM((tm, tn), F32)] if nk > 1 else [],
        compiler_params=_params(("parallel", "parallel", "arbitrary")),
    )(a, b, *extras, *rows)
    return outs[0] if n_o == 1 else outs


def _lnmod_fwd(x, sc, sh, name):
    S = x.shape[0]
    t = min(ROW_TILE, S)

    def body(x_ref, sc_ref, sh_ref, h_ref):
        xh, _ = _ln_stats(x_ref[...])
        h_ref[...] = (xh * (1.0 + sc_ref[...]) + sh_ref[...]).astype(h_ref.dtype)

    return pl.pallas_call(
        body, name=name, grid=(S // t,),
        in_specs=[_row_spec(t, D), _vec_spec(D), _vec_spec(D)], out_specs=_row_spec(t, D),
        out_shape=jax.ShapeDtypeStruct((S, D), MXU), compiler_params=_params(("parallel",)),
    )(x, sc, sh)


def _res_ln_fwd(xprev, br, gvec, lng, lnb, name, mod=None):
    S = xprev.shape[0]
    t = min(ROW_TILE, S)
    with_h = mod is not None

    def body(*refs):
        xp_ref, br_ref, g_ref, lg_ref, lb_ref = refs[:5]
        u = ALPHA * xp_ref[...] + (1.0 + g_ref[...]) * br_ref[...]
        uh, _ = _ln_stats(u)
        xn = uh * lg_ref[...] + lb_ref[...]
        if with_h:
            sc_ref, sh_ref, xn_ref, h_ref = refs[5:]
            xh, _ = _ln_stats(xn)
            h_ref[...] = (xh * (1.0 + sc_ref[...]) + sh_ref[...]).astype(h_ref.dtype)
        else:
            xn_ref = refs[5]
        xn_ref[...] = xn

    ins = [xprev, br, gvec, lng, lnb] + (list(mod) if with_h else [])
    return pl.pallas_call(
        body, name=name, grid=(S // t,),
        in_specs=[_row_spec(t, D), _row_spec(t, D)] + [_vec_spec(D)] * (len(ins) - 2),
        out_specs=[_row_spec(t, D)] * (2 if with_h else 1),
        out_shape=[jax.ShapeDtypeStruct((S, D), F32)] + ([jax.ShapeDtypeStruct((S, D), MXU)] if with_h else []),
        compiler_params=_params(("parallel",)),
    )(*ins)


def _loss_fwd(y, tgt):
    S = y.shape[0]
    t = min(ROW_TILE, S)

    def body(y_ref, t_ref, dy_ref, sq_ref):
        d = y_ref[...] - t_ref[...]
        dy_ref[...] = d * (1.0 / D)
        _acc(sq_ref, pl.program_id(0) == 0, _colsum(d * d))

    return pl.pallas_call(
        body, name="loss_head", grid=(S // t,),
        in_specs=[_row_spec(t, D), _row_spec(t, D)], out_specs=[_row_spec(t, D), _vec_spec(D)],
        out_shape=[jax.ShapeDtypeStruct((S, D), F32), jax.ShapeDtypeStruct((1, D), F32)],
        compiler_params=_params(("arbitrary",)),
    )(y, tgt)


def _node_bwd(name, dres, *, hpath=None, upath=None):
    S = dres.shape[0]
    t = min(ROW_TILE, S)
    has_h, has_u = hpath is not None, upath is not None

    def body(*refs):
        refs = list(refs)
        first = pl.program_id(0) == 0
        dxs = refs.pop(0)[...]
        if has_h:
            dh = refs.pop(0)[...]
            xs = refs.pop(0)[...]
            sc = refs.pop(0)[...]
        if has_u:
            xp = refs.pop(0)[...]
            br = refs.pop(0)[...]
            gv = refs.pop(0)[...]
            lg = refs.pop(0)[...]
        if has_h:
            dsc_ref, dsh_ref = refs.pop(0), refs.pop(0)
            xh, r = _ln_stats(xs)
            _acc(dsc_ref, first, _colsum(dh * xh))
            _acc(dsh_ref, first, _colsum(dh))
            dxs = dxs + _ln_bwd(dh * (1.0 + sc), xh, r)
        if has_u:
            du_ref, dbr_ref, dlg_ref, dlb_ref, dg_ref = refs
            uh, ru = _ln_stats(ALPHA * xp + (1.0 + gv) * br)
            _acc(dlg_ref, first, _colsum(dxs * uh))
            _acc(dlb_ref, first, _colsum(dxs))
            du = _ln_bwd(dxs * lg, uh, ru)
            _acc(dg_ref, first, _colsum(du * br))
            du_ref[...] = ALPHA * du
            dbr_ref[...] = ((1.0 + gv) * du).astype(dbr_ref.dtype)
        else:
            refs[0][...] = dxs

    ins, in_specs = [dres], [_row_spec(t, D)]
    outs, out_specs, names = [], [], []
    vec = jax.ShapeDtypeStruct((1, D), F32)
    if has_h:
        ins += list(hpath)
        in_specs += [_row_spec(t, D), _row_spec(t, D), _vec_spec(D)]
        outs += [vec, vec]
        out_specs += [_vec_spec(D), _vec_spec(D)]
        names += ["dsc", "dsh"]
    if has_u:
        ins += list(upath)
        in_specs += [_row_spec(t, D), _row_spec(t, D), _vec_spec(D), _vec_spec(D)]
        outs += [jax.ShapeDtypeStruct((S, D), F32), jax.ShapeDtypeStruct((S, D), MXU), vec, vec, vec]
        out_specs += [_row_spec(t, D), _row_spec(t, D), _vec_spec(D), _vec_spec(D), _vec_spec(D)]
        names += ["du", "dbr", "dlng", "dlnb", "dg"]
    else:
        outs += [jax.ShapeDtypeStruct((S, D), F32)]
        out_specs += [_row_spec(t, D)]
        names += ["dx"]
    res = pl.pallas_call(
        body, name=name, grid=(S // t,), in_specs=in_specs, out_specs=out_specs, out_shape=outs,
        compiler_params=_params(("arbitrary",)),
    )(*ins)
    return dict(zip(names, res))


def _colsum_call(a, name):
    S, N = a.shape
    t, tn = min(ROW_TILE, S), _tile(N, 1152)

    def body(a_ref, o_ref):
        _acc(o_ref, pl.program_id(1) == 0, _colsum(a_ref[...].astype(F32)))

    return pl.pallas_call(
        body, name=name, grid=(N // tn, S // t),
        in_specs=[pl.BlockSpec((t, tn), lambda j, i: (i, j))], out_specs=pl.BlockSpec((1, tn), lambda j, i: (0, j)),
        out_shape=jax.ShapeDtypeStruct((1, N), F32), compiler_params=_params(("parallel", "arbitrary")),
    )(a)


def _gm_mask():
    i = lax.broadcasted_iota(jnp.int32, (128, 128), 0) // CHUNK
    j = lax.broadcasted_iota(jnp.int32, (128, 128), 1) // CHUNK
    return i >= j


def _gmlp_common(z, lng, lnb):
    gz = _gelu(z)
    u, v = gz[:, :GM_W], gz[:, GM_W:]
    vh, r = _ln_stats(v)
    return u, vh, r, vh * lng + lnb


def _gmlp_fwd(z, lng, lnb, ws, bst, name):
    S = z.shape[0]
    t = min(ROW_TILE, S)

    def body(z_ref, lg_ref, lb_ref, ws_ref, bs_ref, y_ref):
        u, _, _, vn = _gmlp_common(z_ref[...], lg_ref[...], lb_ref[...])
        mask = _gm_mask()
        vb = vn.astype(MXU)
        for g in range(GM_G):
            w = jnp.where(mask, ws_ref[g], jnp.zeros_like(ws_ref[g]))
            bias = bs_ref[:, g:g + 1]
            for blk in range(t // 128):
                rs, cs = slice(blk * 128, (blk + 1) * 128), slice(g * 128, (g + 1) * 128)
                f = _dot(w, vb[rs, cs]) + bias
                y_ref[rs, cs] = (u[rs, cs] * f).astype(y_ref.dtype)

    return pl.pallas_call(
        body, name=name, grid=(S // t,),
        in_specs=[_row_spec(t, 2 * GM_W, P_GM // (2 * GM_W)), _vec_spec(GM_W), _vec_spec(GM_W),
                  _full_spec((GM_G, 128, 128)), _full_spec((128, GM_G))],
        out_specs=_row_spec(t, GM_W), out_shape=jax.ShapeDtypeStruct((S, GM_W), MXU),
        compiler_params=_params(("parallel",)),
    )(z, lng, lnb, ws, bst)


def _gmlp_bwd(z, dya, lng, lnb, ws, wst, bst, dz_in, name):
    S = z.shape[0]
    t = min(ROW_TILE, S)

    def body(z_ref, dy_ref, lg_ref, lb_ref, ws_ref, wst_ref, bs_ref, _dz_in, dz_ref, dlg_ref, dlb_ref, dws_ref, dbs_ref):
        first = pl.program_id(0) == 0
        zz = z_ref[...]
        u, vh, r, vn = _gmlp_common(zz, lg_ref[...], lb_ref[...])
        dy = dy_ref[...]
        mask = _gm_mask()
        maskt = lax.broadcasted_iota(jnp.int32, (128, 128), 1) // CHUNK >= lax.broadcasted_iota(jnp.int32, (128, 128), 0) // CHUNK
        lane = lax.broadcasted_iota(jnp.int32, (128, 128), 1)
        vb = vn.astype(MXU)
        dfb = (dy * u).astype(MXU)
        df32 = dy * u
        dbs = jnp.zeros((128, 128), F32)
        du_cols, dvn_cols = [], []
        for g in range(GM_G):
            w = jnp.where(mask, ws_ref[g], jnp.zeros_like(ws_ref[g]))
            wt = jnp.where(maskt, wst_ref[g], jnp.zeros_like(wst_ref[g]))
            bias = bs_ref[:, g:g + 1]
            cs = slice(g * 128, (g + 1) * 128)
            dw = jnp.zeros((128, 128), F32)
            du_rows, dvn_rows = [], []
            for blk in range(t // 128):
                rs = slice(blk * 128, (blk + 1) * 128)
                f = _dot(w, vb[rs, cs]) + bias
                du_rows.append(dy[rs, cs] * f)
                dvn_rows.append(_dot(wt, dfb[rs, cs]))
                dw = dw + _dot_nt(dfb[rs, cs], vb[rs, cs])
                dbs = dbs + jnp.where(lane == g, jnp.sum(df32[rs, cs], axis=1, keepdims=True), 0.0)
            _acc(dws_ref.at[g], first, jnp.where(mask, dw, 0.0))
            du_cols.append(jnp.concatenate(du_rows, axis=0))
            dvn_cols.append(jnp.concatenate(dvn_rows, axis=0))
        _acc(dbs_ref, first, dbs)
        du = jnp.concatenate(du_cols, axis=1)
        dvn = jnp.concatenate(dvn_cols, axis=1)
        _acc(dlg_ref, first, _colsum(dvn * vh))
        _acc(dlb_ref, first, _colsum(dvn))
        dv = _ln_bwd(dvn * lg_ref[...], vh, r)
        dz_ref[...] = (jnp.concatenate([du, dv], axis=1) * _gelu_grad(zz)).astype(dz_ref.dtype)

    vec = jax.ShapeDtypeStruct((1, GM_W), F32)
    return pl.pallas_call(
        body, name=name, grid=(S // t,),
        in_specs=[_row_spec(t, 2 * GM_W, P_GM // (2 * GM_W)), _row_spec(t, GM_W), _vec_spec(GM_W), _vec_spec(GM_W),
                  _full_spec((GM_G, 128, 128)), _full_spec((GM_G, 128, 128)), _full_spec((128, GM_G)), _ANY],
        out_specs=[_row_spec(t, 2 * GM_W, P_GM // (2 * GM_W)), _vec_spec(GM_W), _vec_spec(GM_W), _full_spec((GM_G, 128, 128)),
                   _full_spec((128, 128))],
        out_shape=[jax.ShapeDtypeStruct((S, NP), MXU), vec, vec,
                   jax.ShapeDtypeStruct((GM_G, 128, 128), F32), jax.ShapeDtypeStruct((128, 128), F32)],
        input_output_aliases=_dz_alias(8), compiler_params=_params(("arbitrary",)),
    )(z, dya, lng, lnb, ws, wst, bst, dz_in)


def _rms(x, g):
    r = lax.rsqrt(jnp.mean(x * x, axis=-1, keepdims=True) + RMS_EPS)
    xh = x * r
    return xh, r, xh * g


def _mla_specs(t):
    return [_row_spec(t, Q_RANK, P_Q // Q_RANK), _row_spec(t, KV_RANK, P_KV // 128), _row_spec(t, 128, P_KA // 128),
            _row_spec(t, 128, P_KB // 128), _row_spec(t, 128), _row_spec(t, 128), _vec_spec(Q_RANK), _vec_spec(KV_RANK),
            _full_spec((Q_RANK, D)), _full_spec((Q_RANK, D)), _full_spec((KV_RANK, D)), _full_spec((HEADS, KV_RANK, VD))]


def _mla_prep_fwd(z, cos, sin, gq, gkv, w1, w2, wk, wv, wvt, name):
    S = z.shape[0]
    t = min(ROW_TILE, S)

    def body(zq_ref, zkv_ref, zka_ref, zkb_ref, cos_ref, sin_ref, gq_ref, gkv_ref, w1_ref, w2_ref, wk_ref, wv_ref, wvt_ref,
             q_ref, k_ref, v_ref, vt_ref):
        cos_, sin_ = cos_ref[...], sin_ref[...]
        cq = cos_ + jnp.where(lax.broadcasted_iota(jnp.int32, cos_.shape, 1) < NOPE, 1.0, 0.0)
        qn = _rms(zq_ref[...], gq_ref[...])[2].astype(MXU)
        q1, q2 = _dot(qn, w1_ref[...]), _dot(qn, w2_ref[...])
        kvn = _rms(zkv_ref[...], gkv_ref[...])[2].astype(MXU)
        kn = _dot(kvn, wk_ref[...])
        krot = zka_ref[...] * cos_ + zkb_ref[...] * sin_
        for h in range(HEADS):
            hs = slice(h * 128, (h + 1) * 128)
            q_ref[h] = (q1[:, hs] * cq + q2[:, hs] * sin_).astype(q_ref.dtype)
            k_ref[h] = (kn[:, hs] + krot).astype(k_ref.dtype)
            v_ref[h] = _dot(kvn, wv_ref[h]).astype(v_ref.dtype)
            vt_ref[h] = _dot_nt(wvt_ref[h], kvn).astype(vt_ref.dtype)

    hspec = lambda w: pl.BlockSpec((HEADS, t, w), lambda i: (0, i, 0))
    return pl.pallas_call(
        body, name=name, grid=(S // t,), in_specs=_mla_specs(t) + [_full_spec((HEADS, VD, KV_RANK))],
        out_specs=[hspec(128), hspec(128), hspec(VD), pl.BlockSpec((HEADS, VD, t), lambda i: (0, 0, i))],
        out_shape=[jax.ShapeDtypeStruct((HEADS, S, 128), MXU), jax.ShapeDtypeStruct((HEADS, S, 128), MXU),
                   jax.ShapeDtypeStruct((HEADS, S, VD), MXU), jax.ShapeDtypeStruct((HEADS, VD, S), MXU)],
        compiler_params=_params(("parallel",)),
    )(z, z, z, z, cos, sin, gq, gkv, w1, w2, wk, wv, wvt)


def _mla_prep_bwd(z, cos, sin, gq, gkv, w1, w2, wk, wv, dq, dk, dv, dz_in, name):
    S = z.shape[0]
    t = min(ROW_TILE, S)

    def body(zq_ref, zkv_ref, zka_ref, zkb_ref, cos_ref, sin_ref, gq_ref, gkv_ref, w1_ref, w2_ref, wk_ref, wv_ref,
             dq_ref, dk_ref, dv_ref, _dz_in, dz_ref, dgq_ref, dgkv_ref, dw1_ref, dw2_ref, dwk_ref, dwv_ref):
        first = pl.program_id(0) == 0
        cos_, sin_ = cos_ref[...], sin_ref[...]
        cq = cos_ + jnp.where(lax.broadcasted_iota(jnp.int32, cos_.shape, 1) < NOPE, 1.0, 0.0)
        gq_, gkv_ = gq_ref[...], gkv_ref[...]
        xhq, rq, qn32 = _rms(zq_ref[...], gq_)
        qn = qn32.astype(MXU)
        dq1 = jnp.concatenate([dq_ref[h] * cq for h in range(HEADS)], axis=1).astype(MXU)
        dq2 = jnp.concatenate([dq_ref[h] * sin_ for h in range(HEADS)], axis=1).astype(MXU)
        dqn = _dot_nt(dq1, w1_ref[...]) + _dot_nt(dq2, w2_ref[...])
        _acc(dw1_ref, first, _dot_tn(qn, dq1))
        _acc(dw2_ref, first, _dot_tn(qn, dq2))
        _acc(dgq_ref, first, _colsum(dqn * xhq))
        dxn = dqn * gq_
        dzq = rq * (dxn - xhq * jnp.mean(dxn * xhq, axis=-1, keepdims=True))

        xhk, rk, kvn32 = _rms(zkv_ref[...], gkv_)
        kvn = kvn32.astype(MXU)
        dks = [dk_ref[h] for h in range(HEADS)]
        dkall = jnp.concatenate(dks, axis=1).astype(MXU)
        dkrot = functools.reduce(lambda p, q_: p + q_, dks)
        dkvn = _dot_nt(dkall, wk_ref[...])
        _acc(dwk_ref, first, _dot_tn(kvn, dkall))
        for h in range(HEADS):
            dvb = dv_ref[h].astype(MXU)
            dkvn = dkvn + _dot_nt(dvb, wv_ref[h])
            _acc(dwv_ref.at[h], first, _dot_tn(kvn, dvb))
        _acc(dgkv_ref, first, _colsum(dkvn * xhk))
        dxk = dkvn * gkv_
        dzkv = rk * (dxk - xhk * jnp.mean(dxk * xhk, axis=-1, keepdims=True))
        dz_ref[...] = jnp.concatenate([dzq, dzkv, dkrot * cos_, dkrot * sin_], axis=1).astype(dz_ref.dtype)

    hspec = lambda w: pl.BlockSpec((HEADS, t, w), lambda i: (0, i, 0))
    sds = jax.ShapeDtypeStruct
    return pl.pallas_call(
        body, name=name, grid=(S // t,),
        in_specs=_mla_specs(t) + [hspec(128), hspec(128), hspec(VD), _ANY],
        out_specs=[_row_spec(t, 640, P_Q // 640), _vec_spec(Q_RANK), _vec_spec(KV_RANK), _full_spec((Q_RANK, D)),
                   _full_spec((Q_RANK, D)), _full_spec((KV_RANK, D)), _full_spec((HEADS, KV_RANK, VD))],
        out_shape=[sds((S, NP), MXU), sds((1, Q_RANK), F32), sds((1, KV_RANK), F32), sds((Q_RANK, D), F32),
                   sds((Q_RANK, D), F32), sds((KV_RANK, D), F32), sds((HEADS, KV_RANK, VD), F32)],
        input_output_aliases=_dz_alias(16), compiler_params=_params(("arbitrary",)),
    )(z, z, z, z, cos, sin, gq, gkv, w1, w2, wk, wv, dq, dk, dv, dz_in)


def _chunk_mask(t, transposed):
    r = lax.broadcasted_iota(jnp.int32, (t, t), 0) // CHUNK
    c = lax.broadcasted_iota(jnp.int32, (t, t), 1) // CHUNK
    return (r <= c) if transposed else (c <= r)


def _attn_fwd(q, k, vt, name):
    S = q.shape[1]
    t = min(ATT_TILE, S)
    n = S // t

    def body(q_ref, k_ref, vt_ref, ob_ref, o_ref, lse_ref):
        qi = pl.program_id(1)
        qb = q_ref[0]

        def block(j, carry, masked):
            m, l, acc = carry
            cols = pl.ds(pl.multiple_of(j * t, t), t)
            st = _dot_nt(k_ref[0, cols, :], qb) * ATT_SCALE
            if masked:
                st = jnp.where(_chunk_mask(t, True), st, -jnp.inf)
            m_new = jnp.maximum(m, jnp.max(st, axis=0, keepdims=True))
            p = jnp.exp(st - m_new)
            alpha = jnp.exp(m - m_new)
            l = alpha * l + jnp.sum(p, axis=0, keepdims=True)
            acc = alpha * acc + _dot(vt_ref[0, :, cols], p.astype(MXU))
            return m_new, l, acc

        init = (jnp.full((1, t), -jnp.inf, F32), jnp.zeros((1, t), F32), jnp.zeros((VD, t), F32))
        carry = lax.fori_loop(0, qi, lambda j, c: block(j, c, False), init)
        m, l, acc = block(qi, carry, True)
        o = (acc / l).T
        o_ref[0] = o
        ob_ref[0] = o.astype(ob_ref.dtype)
        lse_ref[0] = m + jnp.log(l)

    qspec = lambda w: pl.BlockSpec((1, t, w), lambda h, i: (h, i, 0))
    sds = jax.ShapeDtypeStruct
    return pl.pallas_call(
        body, name=name, grid=(HEADS, n),
        in_specs=[qspec(128), pl.BlockSpec((1, S, 128), lambda h, i: (h, 0, 0)), pl.BlockSpec((1, VD, S), lambda h, i: (h, 0, 0))],
        out_specs=[qspec(VD), qspec(VD), pl.BlockSpec((1, 1, t), lambda h, i: (h, 0, i))],
        out_shape=[sds((HEADS, S, VD), MXU), sds((HEADS, S, VD), F32), sds((HEADS, 1, S), F32)],
        compiler_params=_params(("parallel", "arbitrary")),
    )(q, k, vt)


def _attn_delta(o, do, name):
    S = o.shape[1]
    t = min(ROW_TILE, S)

    def body(o_ref, do_ref, d_ref):
        for h in range(HEADS):
            d_ref[h] = jnp.sum(o_ref[h] * do_ref[h], axis=1, keepdims=True)

    hspec = lambda w: pl.BlockSpec((HEADS, t, w), lambda i: (0, i, 0))
    return pl.pallas_call(
        body, name=name, grid=(S // t,), in_specs=[hspec(VD), hspec(VD)], out_specs=hspec(1),
        out_shape=jax.ShapeDtypeStruct((HEADS, S, 1), F32), compiler_params=_params(("parallel",)),
    )(o, do)


def _attn_bwd(q, k, v, do, lse_row, d_row, name):
    S = q.shape[1]
    t = min(ATT_TILE, S)
    n = S // t

    def body(q_ref, do_ref, lse_ref, d_ref, k_ref, v_ref, dq_ref, dk_ref, dv_ref, dk_s, dv_s):
        ki = pl.program_id(1)

        @pl.when(ki == 0)
        def _():
            dq_ref[...] = jnp.zeros(dq_ref.shape, F32)

        dk_s[...] = jnp.zeros(dk_s.shape, F32)
        dv_s[...] = jnp.zeros(dv_s.shape, F32)
        kb, vb = k_ref[0], v_ref[0]

        def step(qi, masked):
            rows = pl.ds(pl.multiple_of(qi * t, t), t)
            qb, dob = q_ref[0, rows, :], do_ref[0, rows, :]
            st = _dot_nt(kb, qb) * ATT_SCALE
            if masked:
                st = jnp.where(_chunk_mask(t, True), st, -jnp.inf)
            pt = jnp.exp(st - lse_ref[0, :, rows])
            dv_s[...] += _dot(pt.astype(MXU), dob)
            dpt = _dot_nt(vb, dob)
            dst = (pt * (dpt - d_ref[0, :, rows]) * ATT_SCALE).astype(MXU)
            dk_s[...] += _dot(dst, qb)
            dq_ref[0, rows, :] += _dot_tn(dst, kb)

        step(ki, True)

        def loop(qi, c):
            step(qi, False)
            return c

        lax.fori_loop(ki + 1, n, loop, 0)
        dk_ref[0] = dk_s[...]
        dv_ref[0] = dv_s[...]

    head = lambda *shape: pl.BlockSpec((1,) + shape, lambda h, j: (h, 0, 0))
    kmap = lambda h, j: (h, j, 0)
    sds = jax.ShapeDtypeStruct
    return pl.pallas_call(
        body, name=name, grid=(HEADS, n),
        in_specs=[head(S, 128), head(S, VD), head(1, S), head(1, S), pl.BlockSpec((1, t, 128), kmap), pl.BlockSpec((1, t, VD), kmap)],
        out_specs=[head(S, 128), pl.BlockSpec((1, t, 128), kmap), pl.BlockSpec((1, t, VD), kmap)],
        out_shape=[sds((HEADS, S, 128), F32), sds((HEADS, S, 128), F32), sds((HEADS, S, VD), F32)],
        scratch_shapes=[pltpu.VMEM((t, 128), F32), pltpu.VMEM((t, VD), F32)],
        compiler_params=_params(("parallel", "arbitrary")),
    )(q, do, lse_row, d_row, k, v)


def _shift_down(x, prev8, d):
    xr = pltpu.roll(x, d, 0)
    r8 = lax.broadcasted_iota(jnp.int32, prev8.shape, 0)
    top = jnp.where(r8 < d, pltpu.roll(prev8, d, 0), xr[0:8])
    return jnp.concatenate([top, xr[8:]], axis=0)


def _shift_up(x, next8, d):
    n = x.shape[0]
    xr = pltpu.roll(x, n - d, 0)
    r8 = lax.broadcasted_iota(jnp.int32, next8.shape, 0)
    bot = jnp.where(r8 >= 8 - d, pltpu.roll(next8, 8 - d, 0), xr[n - 8:])
    return jnp.concatenate([xr[:n - 8], bot], axis=0)


def _log1p(u):
    return jnp.where(u < 0.01, u * (1.0 - u * (0.5 - u * (1.0 / 3.0 - 0.25 * u))), jnp.log(1.0 + u))


def _neg_expm1(y):
    series = -y * (1.0 + 0.5 * y * (1.0 + (1.0 / 3.0) * y * (1.0 + 0.25 * y)))
    return jnp.where(y > -0.05, series, 1.0 - jnp.exp(y))


def _softplus_neg(lam):
    return jnp.maximum(-lam, 0.0) + _log1p(jnp.exp(-jnp.abs(lam)))


def _lru_gates(x, prev8, cw, cb, wr, br, wi, bi, lam):
    xs1, xs2, xs3 = _shift_down(x, prev8, 1), _shift_down(x, prev8, 2), _shift_down(x, prev8, 3)
    xc = cb + cw[0:1] * xs3 + cw[1:2] * xs2 + cw[2:3] * xs1 + cw[3:4] * x
    xcb = xc.astype(MXU)
    r = _sigmoid(_dot(xcb, wr) + br)
    ig = _sigmoid(_dot(xcb, wi) + bi)
    sp = _softplus_neg(lam)
    log_a = -8.0 * r * sp
    a = jnp.exp(log_a)
    gb = jnp.sqrt(_neg_expm1(2.0 * log_a))
    return (xs1, xs2, xs3), xc, xcb, r, ig, sp, a, gb


def _lru_fwd(z, cw, cb, wr, br, wi, bi, lam, name):
    S = z.shape[0]
    t = min(ROW_TILE, S)

    def body(zx_ref, zg_ref, cw_ref, cb_ref, wr_ref, br_ref, wi_ref, bi_ref, lam_ref, y_ref, h_ref, xp_s, hc_s):
        @pl.when(pl.program_id(0) == 0)
        def _():
            xp_s[...] = jnp.zeros(xp_s.shape, F32)
            hc_s[...] = jnp.zeros(hc_s.shape, F32)

        x = zx_ref[...]
        _, xc, _, _, ig, _, a, gb = _lru_gates(x, xp_s[...], cw_ref[...], cb_ref[...], wr_ref[...], br_ref[...],
                                               wi_ref[...], bi_ref[...], lam_ref[...])
        b = gb * (ig * xc)
        rows = lax.broadcasted_iota(jnp.int32, a.shape, 0)
        d = 1
        while d < t:
            ar, brr = pltpu.roll(a, d, 0), pltpu.roll(b, d, 0)
            ok = rows >= d
            b = jnp.where(ok, a * brr, 0.0) + b
            a = jnp.where(ok, a * ar, a)
            d *= 2
        h = a * hc_s[7:8, :] + b
        h_ref[...] = h
        y_ref[...] = (h * _gelu(zg_ref[...])).astype(y_ref.dtype)
        hc_s[...] = h[t - 8:, :]
        xp_s[...] = x[t - 8:, :]

    w = LRU_W
    return pl.pallas_call(
        body, name=name, grid=(S // t,),
        in_specs=[_row_spec(t, w, P_LX // w), _row_spec(t, w, P_LG // w), _full_spec((4, w)), _vec_spec(w), _full_spec((w, w)),
                  _vec_spec(w), _full_spec((w, w)), _vec_spec(w), _vec_spec(w)],
        out_specs=[_row_spec(t, w), _row_spec(t, w)],
        out_shape=[jax.ShapeDtypeStruct((S, w), MXU), jax.ShapeDtypeStruct((S, w), F32)],
        scratch_shapes=[pltpu.VMEM((8, w), F32), pltpu.VMEM((8, w), F32)],
        compiler_params=_params(("arbitrary",)),
    )(z, z, cw, cb, wr, br, wi, bi, lam)


def _lru_bwd(z, h, dy, cw, cb, wr, br, wi, bi, lam, dz_in, name):
    S = z.shape[0]
    t = min(ROW_TILE, S)
    n = S // t
    w = LRU_W

    def body(zx_ref, zxp_ref, zg_ref, h_ref, hp_ref, dy_ref, cw_ref, cb_ref, wr_ref, br_ref, wi_ref, bi_ref, lam_ref, _dz_in,
             dz_ref, dcw_ref, dcb_ref, dwr_ref, dbr_ref, dwi_ref, dbi_ref, dlam_ref, gc_s, dn_s):
        i = pl.program_id(0)
        first = i == 0
        j = n - 1 - i

        @pl.when(first)
        def _():
            gc_s[...] = jnp.zeros(gc_s.shape, F32)
            dn_s[...] = jnp.zeros(dn_s.shape, F32)

        live = (j > 0).astype(F32)
        xprev8, hprev8 = zxp_ref[...] * live, hp_ref[...] * live
        x, zg, hh, dy_ = zx_ref[...], zg_ref[...], h_ref[...], dy_ref[...]
        cw_, lam_ = cw_ref[...], lam_ref[...]
        (xs1, xs2, xs3), xc, xcb, r, ig, sp, a, gb = _lru_gates(x, xprev8, cw_, cb_ref[...], wr_ref[...], br_ref[...],
                                                                wi_ref[...], bi_ref[...], lam_)
        hm1 = _shift_down(hh, hprev8, 1)
        dh = dy_ * _gelu(zg)
        dzg = dy_ * hh * _gelu_grad(zg)
        rows = lax.broadcasted_iota(jnp.int32, a.shape, 0)
        last = rows == t - 1
        ca = jnp.where(last, 0.0, pltpu.roll(a, t - 1, 0))
        g = dh + jnp.where(last, gc_s[0:1, :], 0.0)
        d = 1
        while d < t:
            ok = rows < t - d
            g = g + jnp.where(ok, ca * pltpu.roll(g, t - d, 0), 0.0)
            ca = jnp.where(ok, ca * pltpu.roll(ca, t - d, 0), 0.0)
            d *= 2
        gc_s[...] = a[0:8, :] * g[0:8, :]
        da = g * hm1
        dgb = g * (ig * xc)
        dub = g * gb
        di = dub * xc
        dxc = dub * ig
        dlog_a = da * a - dgb * (a * a) / gb
        dr = dlog_a * (-8.0 * sp)
        _acc(dlam_ref, first, _colsum(dlog_a * (-8.0 * r)) * (-_sigmoid(-lam_)))
        dpr = dr * r * (1.0 - r)
        dpi = di * ig * (1.0 - ig)
        _acc(dbr_ref, first, _colsum(dpr))
        _acc(dbi_ref, first, _colsum(dpi))
        dprb, dpib = dpr.astype(MXU), dpi.astype(MXU)
        _acc(dwr_ref, first, _dot_tn(xcb, dprb))
        _acc(dwi_ref, first, _dot_tn(xcb, dpib))
        dxc = dxc + _dot_nt(dprb, wr_ref[...]) + _dot_nt(dpib, wi_ref[...])
        _acc(dcb_ref, first, _colsum(dxc))
        _acc(dcw_ref, first, jnp.concatenate([_colsum(dxc * xs3), _colsum(dxc * xs2), _colsum(dxc * xs1), _colsum(dxc * x)], axis=0))
        nxt = dn_s[...]
        dx = cw_[3:4] * dxc + cw_[2:3] * _shift_up(dxc, nxt, 1) + cw_[1:2] * _shift_up(dxc, nxt, 2) + cw_[0:1] * _shift_up(dxc, nxt, 3)
        dn_s[...] = dxc[0:8, :]
        dz_ref[...] = jnp.concatenate([dx, dzg], axis=1).astype(dz_ref.dtype)

    rev = lambda col: pl.BlockSpec((t, w), lambda i, c=col: (n - 1 - i, c))
    prev8 = lambda col: pl.BlockSpec((8, w), lambda i, c=col: (jnp.maximum((n - 1 - i) * (t // 8) - 1, 0), c))
    vec = jax.ShapeDtypeStruct((1, w), F32)
    sds = jax.ShapeDtypeStruct
    return pl.pallas_call(
        body, name=name, grid=(n,),
        in_specs=[rev(P_LX // w), prev8(P_LX // w), rev(P_LG // w), rev(0), prev8(0), rev(0), _full_spec((4, w)), _vec_spec(w),
                  _full_spec((w, w)), _vec_spec(w), _full_spec((w, w)), _vec_spec(w), _vec_spec(w), _ANY],
        out_specs=[pl.BlockSpec((t, 2 * w), lambda i: (n - 1 - i, P_LX // (2 * w))), _full_spec((4, w)), _vec_spec(w),
                   _full_spec((w, w)), _vec_spec(w), _full_spec((w, w)), _vec_spec(w), _vec_spec(w)],
        out_shape=[sds((S, NP), MXU), sds((4, w), F32), vec, sds((w, w), F32), vec, sds((w, w), F32), vec, vec],
        scratch_shapes=[pltpu.VMEM((8, w), F32), pltpu.VMEM((8, w), F32)],
        input_output_aliases=_dz_alias(14), compiler_params=_params(("arbitrary",)),
    )(z, z, z, h, h, dy, cw, cb, wr, br, wi, bi, lam, dz_in)


def _branch_fwd(z, ya, o, yc, wa, wb, wc, name):
    S = z.shape[0]
    t = min(ROW_TILE, S)

    def body(ga_ref, gb_ref, gc_ref, ya_ref, o_ref, yc_ref, wa_ref, wb_ref, wc_ref, m_ref, pa_ref, pb_ref, pc_ref):
        pa = _dot(ya_ref[...], wa_ref[...])
        pc = _dot(yc_ref[...], wc_ref[...])
        pb = _dot(o_ref[0], wb_ref[0])
        for h in range(1, HEADS):
            pb = pb + _dot(o_ref[h], wb_ref[h])
        pa_ref[...], pb_ref[...], pc_ref[...] = pa, pb, pc
        m = _sigmoid(ga_ref[...]) * pa + _sigmoid(gb_ref[...]) * pb + _sigmoid(gc_ref[...]) * pc
        m_ref[...] = m.astype(m_ref.dtype)

    g0 = P_GATE // D
    sds = jax.ShapeDtypeStruct
    return pl.pallas_call(
        body, name=name, grid=(S // t,),
        in_specs=[_row_spec(t, D, g0), _row_spec(t, D, g0 + 1), _row_spec(t, D, g0 + 2), _row_spec(t, GM_W),
                  pl.BlockSpec((HEADS, t, VD), lambda i: (0, i, 0)), _row_spec(t, LRU_W),
                  _full_spec((GM_W, D)), _full_spec((HEADS, VD, D)), _full_spec((LRU_W, D))],
        out_specs=[_row_spec(t, D)] * 4,
        out_shape=[sds((S, D), MXU), sds((S, D), F32), sds((S, D), F32), sds((S, D), F32)],
        compiler_params=_params(("parallel",)),
    )(z, z, z, ya, o, yc, wa, wb, wc)


def _dz_alias(n_inputs):
    return {n_inputs - 1: 0}


def _branch_bwd(z, dm, pa, pb, pc, name):
    S = z.shape[0]
    t = min(ROW_TILE, S)

    def body(ga_ref, gb_ref, gc_ref, dm_ref, pa_ref, pb_ref, pc_ref, dz_ref, da_ref, db_ref, dc_ref):
        dm_ = dm_ref[...]
        for n_, (g_ref, p_ref, d_ref) in enumerate(((ga_ref, pa_ref, da_ref), (gb_ref, pb_ref, db_ref), (gc_ref, pc_ref, dc_ref))):
            gt = _sigmoid(g_ref[...])
            d_ref[...] = (dm_ * gt).astype(d_ref.dtype)
            dz_ref[:, n_ * D:(n_ + 1) * D] = (dm_ * p_ref[...] * gt * (1.0 - gt)).astype(dz_ref.dtype)

    g0 = P_GATE // D
    sds = jax.ShapeDtypeStruct
    return pl.pallas_call(
        body, name=name, grid=(S // t,),
        in_specs=[_row_spec(t, D, g0), _row_spec(t, D, g0 + 1), _row_spec(t, D, g0 + 2)] + [_row_spec(t, D)] * 4,
        out_specs=[_row_spec(t, 3 * D, P_GATE // (3 * D))] + [_row_spec(t, D)] * 3,
        out_shape=[sds((S, NP), MXU)] + [sds((S, D), MXU)] * 3,
        compiler_params=_params(("parallel",)),
    )(z, z, z, dm, pa, pb, pc)


def _heads_bwd(dpb, o, wb, name):
    S = dpb.shape[0]
    t = min(ROW_TILE, S)

    def body(dp_ref, o_ref, wb_ref, dob_ref, do_ref, dwb_ref):
        first = pl.program_id(0) == 0
        dp = dp_ref[...]
        for h in range(HEADS):
            do = _dot_nt(dp, wb_ref[h])
            do_ref[h] = do
            dob_ref[h] = do.astype(dob_ref.dtype)
            _acc(dwb_ref.at[h], first, _dot_tn(o_ref[h], dp))

    hspec = pl.BlockSpec((HEADS, t, VD), lambda i: (0, i, 0))
    sds = jax.ShapeDtypeStruct
    return pl.pallas_call(
        body, name=name, grid=(S // t,),
        in_specs=[_row_spec(t, D), hspec, _full_spec((HEADS, VD, D))],
        out_specs=[hspec, hspec, _full_spec((HEADS, VD, D))],
        out_shape=[sds((HEADS, S, VD), MXU), sds((HEADS, S, VD), F32), sds((HEADS, VD, D), F32)],
        compiler_params=_params(("arbitrary",)),
    )(dpb, o, wb)


def _mod_fwd(c_all, ada_w, name):
    n = ada_w.shape[2]

    def body(c_ref, w_ref, o_ref):
        c = c_ref[...]
        ca = (c * _sigmoid(c)).astype(MXU)
        for l in range(DEPTH):
            o_ref[l] = _dot(ca, w_ref[l].astype(MXU))

    return pl.pallas_call(body, name=name, out_shape=jax.ShapeDtypeStruct((DEPTH, 8, n), F32), compiler_params=_params())(c_all, ada_w)


def _ada_w_grad(c_all_t, dmod, name):
    n = dmod.shape[2]

    def body(c_ref, d_ref, o_ref):
        c = c_ref[...]
        ca = c * _sigmoid(c)
        for l in range(DEPTH):
            dm = d_ref[l]
            acc = ca[:, 0:1] * dm[0:1, :]
            for b in range(1, 8):
                acc = acc + ca[:, b:b + 1] * dm[b:b + 1, :]
            o_ref[l] = acc

    return pl.pallas_call(body, name=name, out_shape=jax.ShapeDtypeStruct((DEPTH, D, n), F32), compiler_params=_params())(c_all_t, dmod)


def _adamw(w, g, m, v, name):
    R, C = w.shape
    t = _tile(R, 256) if R % 8 == 0 else R
    c1, c2 = 1.0 - ADAM_B1 ** ADAM_STEP, 1.0 - ADAM_B2 ** ADAM_STEP

    def body(w_ref, g_ref, m_ref, v_ref, d_ref, nm_ref, nv_ref):
        g_ = g_ref[...]
        m_ = ADAM_B1 * m_ref[...] + (1.0 - ADAM_B1) * g_
        v_ = ADAM_B2 * v_ref[...] + (1.0 - ADAM_B2) * (g_ * g_)
        nm_ref[...] = m_
        nv_ref[...] = v_
        d_ref[...] = -ADAM_LR * ((m_ / c1) / (jnp.sqrt(v_ / c2) + ADAM_EPS) + ADAM_WD * w_ref[...])

    spec = pl.BlockSpec((t, C), lambda i: (i, 0))
    return pl.pallas_call(
        body, name=name, grid=(R // t,), in_specs=[spec] * 4, out_specs=[spec] * 3,
        out_shape=[jax.ShapeDtypeStruct((R, C), F32)] * 3, compiler_params=_params(("parallel",)),
    )(w, g, m, v)


def _sum8(a, name):
    _, R, C = a.shape
    t = _tile(R, 512) if R % 8 == 0 else R

    def body(a_ref, o_ref):
        s = a_ref[0]
        for k in range(1, 8):
            s = s + a_ref[k]
        o_ref[...] = s

    return pl.pallas_call(
        body, name=name, grid=(R // t,), in_specs=[pl.BlockSpec((8, t, C), lambda i: (0, i, 0))],
        out_specs=pl.BlockSpec((t, C), lambda i: (i, 0)), out_shape=jax.ShapeDtypeStruct((R, C), F32),
        compiler_params=_params(("parallel",)),
    )(a)


def _pair_add(a, b, name):
    shape = a.shape
    a2, b2 = a.reshape(-1, shape[-1]), b.reshape(-1, shape[-1])
    R, C = a2.shape
    t = _tile(R, 512, 16)

    def body(a_ref, b_ref, o_ref, ob_ref):
        s = a_ref[...] + b_ref[...]
        o_ref[...] = s
        ob_ref[...] = s.astype(ob_ref.dtype)

    spec = pl.BlockSpec((t, C), lambda i: (i, 0))
    o, ob = pl.pallas_call(
        body, name=name, grid=(R // t,), in_specs=[spec, spec], out_specs=[spec, spec],
        out_shape=[jax.ShapeDtypeStruct((R, C), F32), jax.ShapeDtypeStruct((R, C), jnp.bfloat16)],
        compiler_params=_params(("parallel",)),
    )(a2, b2)
    return o.reshape(shape), ob.reshape(shape)


def _chip_sum(own, recv, name):
    R, C = own.shape
    t = _tile(R, 512, 16)

    def body(a_ref, r_ref, o_ref):
        s = a_ref[...]
        for k in range(3):
            s = s + r_ref[k].astype(F32)
        o_ref[...] = s

    return pl.pallas_call(
        body, name=name, grid=(R // t,),
        in_specs=[pl.BlockSpec((t, C), lambda i: (i, 0)), pl.BlockSpec((3, t, C), lambda i: (0, i, 0))],
        out_specs=pl.BlockSpec((t, C), lambda i: (i, 0)), out_shape=jax.ShapeDtypeStruct((R, C), F32),
        compiler_params=_params(("parallel",)),
    )(own, recv)


_ANY = pl.BlockSpec(memory_space=pl.ANY)


def _coords():
    return lax.axis_index("x"), lax.axis_index("y"), lax.axis_index("c")


def _allgather8(v, name):
    R, C = v.shape

    def body(v_ref, out_ref, send_sems, recv_sems):
        x, y, c = _coords()
        me = 4 * x + 2 * y + c
        sends = []
        for k in range(1, 8):
            bx, by, bc = (k >> 2) & 1, (k >> 1) & 1, k & 1
            peer = (x ^ bx, y ^ by, c ^ bc)
            cp = pltpu.make_async_remote_copy(src_ref=v_ref, dst_ref=out_ref.at[me], send_sem=send_sems.at[k - 1],
                                              recv_sem=recv_sems.at[k - 1], device_id=peer, device_id_type=MESH)
            cp.start()
            sends.append(cp)
        for k in range(1, 8):
            bx, by, bc = (k >> 2) & 1, (k >> 1) & 1, k & 1
            peer = (x ^ bx, y ^ by, c ^ bc)
            src = 4 * peer[0] + 2 * peer[1] + peer[2]
            pltpu.make_async_remote_copy(src_ref=v_ref, dst_ref=out_ref.at[src], send_sem=send_sems.at[k - 1],
                                         recv_sem=recv_sems.at[k - 1], device_id=peer, device_id_type=MESH).wait_recv()
        for cp in sends:
            cp.wait_send()

    out = pl.pallas_call(
        body, name=name, in_specs=[_ANY], out_specs=_ANY, out_shape=jax.ShapeDtypeStruct((8, R, C), v.dtype),
        scratch_shapes=[pltpu.SemaphoreType.DMA((7,)), pltpu.SemaphoreType.DMA((7,))],
    )(v)
    x, y, c = _coords()
    return lax.dynamic_update_slice(out, v[None], (4 * x + 2 * y + c, 0, 0))


def _chip_patterns(x, y):
    return [(1 - x, y), (x, 1 - y), (1 - x, 1 - y)]


def _allgather_weights(w, name):
    R, C = w.shape
    rh = R // 2

    def body(w_ref, out_ref, send_sems, recv_sems):
        x, y, c = _coords()
        kme = 2 * x + y
        sibling = (x, y, 1 - c)
        mine_half = pl.ds(pl.multiple_of(c * rh, FLAT_ALIGN // 2), rh)
        other_half = pl.ds(pl.multiple_of((1 - c) * rh, FLAT_ALIGN // 2), rh)
        chips = _chip_patterns(x, y)

        def copy(k, chip_idx, half, to, src=None):
            dst = out_ref.at[chip_idx, half, :]
            return pltpu.make_async_remote_copy(src_ref=dst if src is None else src, dst_ref=dst, send_sem=send_sems.at[k],
                                                recv_sem=recv_sems.at[k], device_id=to, device_id_type=MESH)

        first = [copy(j, kme, mine_half, (cx, cy, c), src=w_ref.at[mine_half, :]) for j, (cx, cy) in enumerate(chips)]
        for cp in first:
            cp.start()
        passed = []
        for j, (cx, cy) in enumerate(chips):
            kj = 2 * cx + cy
            copy(j, kj, mine_half, (x, y, c)).wait_recv()
            fw = copy(3 + j, kj, mine_half, sibling)
            fw.start()
            passed.append(fw)
        for j, (cx, cy) in enumerate(chips):
            copy(3 + j, 2 * cx + cy, other_half, (x, y, c)).wait_recv()
        for cp in first + passed:
            cp.wait_send()

    out = pl.pallas_call(
        body, name=name, in_specs=[_ANY], out_specs=_ANY, out_shape=jax.ShapeDtypeStruct((4, R, C), w.dtype),
        scratch_shapes=[pltpu.SemaphoreType.DMA((6,)), pltpu.SemaphoreType.DMA((6,))],
    )(w)
    x, y, _ = _coords()
    return lax.dynamic_update_slice(out, w[None], (2 * x + y, 0, 0))


def _sibling_swap(g, name):
    _, R, C = g.shape
    rh = R // 2

    def body(g_ref, out_ref, send_sem, recv_sem):
        x, y, c = _coords()
        other_half = pl.ds(pl.multiple_of((1 - c) * rh, 8), rh)
        cp = pltpu.make_async_remote_copy(src_ref=g_ref.at[:, other_half, :], dst_ref=out_ref, send_sem=send_sem,
                                          recv_sem=recv_sem, device_id=(x, y, 1 - c), device_id_type=MESH)
        cp.start()
        cp.wait()

    return pl.pallas_call(
        body, name=name, in_specs=[_ANY], out_specs=_ANY, out_shape=jax.ShapeDtypeStruct((4, rh, C), g.dtype),
        scratch_shapes=[pltpu.SemaphoreType.DMA(()), pltpu.SemaphoreType.DMA(())],
    )(g)


def _chip_scatter(a, name):
    _, R, C = a.shape

    def body(a_ref, out_ref, send_sems, recv_sems):
        x, y, c = _coords()
        cps = []
        for j, (cx, cy) in enumerate(_chip_patterns(x, y)):
            cp = pltpu.make_async_remote_copy(src_ref=a_ref.at[2 * cx + cy], dst_ref=out_ref.at[j], send_sem=send_sems.at[j],
                                              recv_sem=recv_sems.at[j], device_id=(cx, cy, c), device_id_type=MESH)
            cp.start()
            cps.append(cp)
        for cp in cps:
            cp.wait()

    return pl.pallas_call(
        body, name=name, in_specs=[_ANY], out_specs=_ANY, out_shape=jax.ShapeDtypeStruct((3, R, C), a.dtype),
        scratch_shapes=[pltpu.SemaphoreType.DMA((3,)), pltpu.SemaphoreType.DMA((3,))],
    )(a)


def _sibling_join(f, name):
    rh, C = f.shape

    def body(f_ref, out_ref, send_sem, recv_sem):
        x, y, c = _coords()
        mine_half = pl.ds(pl.multiple_of(c * rh, 8), rh)
        cp = pltpu.make_async_remote_copy(src_ref=f_ref, dst_ref=out_ref.at[mine_half, :], send_sem=send_sem, recv_sem=recv_sem,
                                          device_id=(x, y, 1 - c), device_id_type=MESH)
        cp.start()
        cp.wait()

    out = pl.pallas_call(
        body, name=name, in_specs=[_ANY], out_specs=_ANY, out_shape=jax.ShapeDtypeStruct((2 * rh, C), f.dtype),
        scratch_shapes=[pltpu.SemaphoreType.DMA(()), pltpu.SemaphoreType.DMA(())],
    )(f)
    return lax.dynamic_update_slice(out, f, (lax.axis_index("c") * rh, 0))


_BIG = (("in_w", 2), ("mla_wuq", 2), ("mla_wukv", 2), ("branch_w", 3), ("mix_out_w", 1), ("ffn_w1", 2), ("ffn_w2", 1))


def _pack(shards):
    flat = jnp.concatenate([shards[n].reshape(-1, FLAT_W) for n, _ in _BIG], axis=0)
    pad = (-flat.shape[0]) % FLAT_ALIGN
    return jnp.pad(flat, ((0, pad), (0, 0)))


def _unpack(flat, shapes):
    out, r = {}, 0
    for n, _ in _BIG:
        rows = math.prod(shapes[n]) // FLAT_W
        out[n] = flat[r:r + rows].reshape(shapes[n])
        r += rows
    return out


def _swap16(a):
    return jnp.concatenate([a[..., 16:32], a[..., 0:16]], axis=-1)


def _pad_in_cols(w):
    lead = w.shape[:-1]
    z = lambda n: jnp.zeros(lead + (n,), w.dtype)
    kr = w[..., O_KR:O_LX]
    return jnp.concatenate([w[..., O_GATE:], w[..., O_GM:O_Q], w[..., O_LX:O_LG], w[..., O_LG:O_GATE], w[..., O_Q:O_KV],
                            w[..., O_KV:O_KR], z(64), kr, z(32), z(64), _swap16(kr), z(32)], axis=-1)


def _unpad_in_cols(g):
    kr = g[..., P_KA + 64:P_KA + 96] + _swap16(g[..., P_KB + 64:P_KB + 96])
    return jnp.concatenate([g[..., P_GM:P_LX], g[..., P_Q:P_KV], g[..., P_KV:P_KA], kr, g[..., P_LX:P_LG], g[..., P_LG:P_Q],
                            g[..., P_GATE:P_GM]], axis=-1)


def _flat128(vecs):
    flat = jnp.concatenate([v.reshape(-1) for v in vecs])
    pad = (-flat.shape[0]) % 1024
    return jnp.pad(flat, (0, pad)).reshape(-1, 128)


def _unflat(flat, shapes):
    flat = flat.reshape(-1)
    out, r = [], 0
    for s in shapes:
        n = math.prod(s)
        out.append(flat[r:r + n].reshape(s))
        r += n
    return out


_SMALL = ("ada_b", "in_b", "gm_ln_g", "gm_ln_b", "gm_ws", "gm_bs", "mla_qnorm_g", "mla_kvnorm_g", "lru_conv_w", "lru_conv_b",
          "lru_wr", "lru_br", "lru_wi", "lru_bi", "lru_lambda", "ffn_b1", "ffn_b2", "ln_g", "ln_b")
_NAMES = ("ada_w", "ada_b", "in_w", "in_b", "gm_ln_g", "gm_ln_b", "gm_ws", "gm_bs", "mla_qnorm_g", "mla_wuq", "mla_kvnorm_g",
          "mla_wukv", "lru_conv_w", "lru_conv_b", "lru_wr", "lru_br", "lru_wi", "lru_bi", "lru_lambda", "branch_w", "mix_out_w",
          "ffn_w1", "ffn_b1", "ffn_w2", "ffn_b2", "ln_g", "ln_b")


def kernel(x, c, ada_w, ada_b, in_w, in_b, gm_ln_g, gm_ln_b, gm_ws, gm_bs, mla_qnorm_g, mla_wuq, mla_kvnorm_g, mla_wukv, lru_conv_w, lru_conv_b, lru_wr, lru_br, lru_wi, lru_bi, lru_lambda, branch_w, mix_out_w, ffn_w1, ffn_b1, ffn_w2, ffn_b2, ln_g, ln_b, loss_target, m_ada_w, m_ada_b, m_in_w, m_in_b, m_gm_ln_g, m_gm_ln_b, m_gm_ws, m_gm_bs, m_mla_qnorm_g, m_mla_wuq, m_mla_kvnorm_g, m_mla_wukv, m_lru_conv_w, m_lru_conv_b, m_lru_wr, m_lru_br, m_lru_wi, m_lru_bi, m_lru_lambda, m_branch_w, m_mix_out_w, m_ffn_w1, m_ffn_b1, m_ffn_w2, m_ffn_b2, m_ln_g, m_ln_b, v_ada_w, v_ada_b, v_in_w, v_in_b, v_gm_ln_g, v_gm_ln_b, v_gm_ws, v_gm_bs, v_mla_qnorm_g, v_mla_wuq, v_mla_kvnorm_g, v_mla_wukv, v_lru_conv_w, v_lru_conv_b, v_lru_wr, v_lru_br, v_lru_wi, v_lru_bi, v_lru_lambda, v_branch_w, v_mix_out_w, v_ffn_w1, v_ffn_b1, v_ffn_w2, v_ffn_b2, v_ln_g, v_ln_b):
    W = dict(ada_w=ada_w, ada_b=ada_b, in_w=in_w, in_b=in_b, gm_ln_g=gm_ln_g, gm_ln_b=gm_ln_b, gm_ws=gm_ws, gm_bs=gm_bs,
             mla_qnorm_g=mla_qnorm_g, mla_wuq=mla_wuq, mla_kvnorm_g=mla_kvnorm_g, mla_wukv=mla_wukv, lru_conv_w=lru_conv_w,
             lru_conv_b=lru_conv_b, lru_wr=lru_wr, lru_br=lru_br, lru_wi=lru_wi, lru_bi=lru_bi, lru_lambda=lru_lambda,
             branch_w=branch_w, mix_out_w=mix_out_w, ffn_w1=ffn_w1, ffn_b1=ffn_b1, ffn_w2=ffn_w2, ffn_b2=ffn_b2, ln_g=ln_g, ln_b=ln_b)
    M = dict(ada_w=m_ada_w, ada_b=m_ada_b, in_w=m_in_w, in_b=m_in_b, gm_ln_g=m_gm_ln_g, gm_ln_b=m_gm_ln_b, gm_ws=m_gm_ws,
             gm_bs=m_gm_bs, mla_qnorm_g=m_mla_qnorm_g, mla_wuq=m_mla_wuq, mla_kvnorm_g=m_mla_kvnorm_g, mla_wukv=m_mla_wukv,
             lru_conv_w=m_lru_conv_w, lru_conv_b=m_lru_conv_b, lru_wr=m_lru_wr, lru_br=m_lru_br, lru_wi=m_lru_wi, lru_bi=m_lru_bi,
             lru_lambda=m_lru_lambda, branch_w=m_branch_w, mix_out_w=m_mix_out_w, ffn_w1=m_ffn_w1, ffn_b1=m_ffn_b1, ffn_w2=m_ffn_w2,
             ffn_b2=m_ffn_b2, ln_g=m_ln_g, ln_b=m_ln_b)
    V = dict(ada_w=v_ada_w, ada_b=v_ada_b, in_w=v_in_w, in_b=v_in_b, gm_ln_g=v_gm_ln_g, gm_ln_b=v_gm_ln_b, gm_ws=v_gm_ws,
             gm_bs=v_gm_bs, mla_qnorm_g=v_mla_qnorm_g, mla_wuq=v_mla_wuq, mla_kvnorm_g=v_mla_kvnorm_g, mla_wukv=v_mla_wukv,
             lru_conv_w=v_lru_conv_w, lru_conv_b=v_lru_conv_b, lru_wr=v_lru_wr, lru_br=v_lru_br, lru_wi=v_lru_wi, lru_bi=v_lru_bi,
             lru_lambda=v_lru_lambda, branch_w=v_branch_w, mix_out_w=v_mix_out_w, ffn_w1=v_ffn_w1, ffn_b1=v_ffn_b1, ffn_w2=v_ffn_w2,
             ffn_b2=v_ffn_b2, ln_g=v_ln_g, ln_b=v_ln_b)

    S = x.shape[1]
    xi, yi, ci = _coords()
    kme = 2 * xi + yi
    me = 4 * xi + 2 * yi + ci
    x0 = x[0]
    tgt = loss_target[0]

    shard_shapes = {n: W[n].shape for n, _ in _BIG}
    gathered = _allgather_weights(_pack({n: W[n].astype(MXU) for n, _ in _BIG}), "ag_weights")
    parts = [_unpack(gathered[k], shard_shapes) for k in range(4)]
    full = {n: jnp.concatenate([p[n] for p in parts], axis=ax) for n, ax in _BIG}

    small1 = _allgather8(_flat128([c, ln_g, ln_b, lru_conv_w]), "ag_small")
    per_dev = [_unflat(small1[d], [c.shape, ln_g.shape, ln_b.shape, lru_conv_w.shape]) for d in range(8)]
    c_all = jnp.concatenate([p[0] for p in per_dev], axis=0)
    chip = lambda i: [per_dev[2 * k][i] for k in range(4)]
    ln_g_f, ln_b_f, conv_w_f = (jnp.concatenate(chip(1), axis=2), jnp.concatenate(chip(2), axis=2), jnp.concatenate(chip(3), axis=2))

    mod_sh = _mod_fwd(c_all, ada_w, "mod_fwd")
    mod_all = _allgather8(_flat128([mod_sh]), "ag_mod")
    mod_parts = [_unflat(mod_all[2 * k], [mod_sh.shape])[0] for k in range(4)]
    mod = jnp.concatenate(mod_parts, axis=2)
    mod = lax.dynamic_index_in_dim(mod, me, axis=1, keepdims=False) + ada_b

    pos = jnp.arange(S, dtype=F32)
    inv = ROPE_BASE ** (-jnp.arange(0, ROPE, 2, dtype=F32) / ROPE)
    ang = pos[:, None] * inv[None, :]
    cs, sn = jnp.cos(ang), jnp.sin(ang)
    zc = lambda n: jnp.zeros((S, n), F32)
    cos_t = jnp.concatenate([zc(64), cs, cs, zc(32)], axis=1)
    sin_t = jnp.concatenate([zc(64), -sn, sn, zc(32)], axis=1)

    win_p = _pad_in_cols(full["in_w"])
    bin_p = _pad_in_cols(in_b)
    wuq = full["mla_wuq"].reshape(DEPTH, Q_RANK, HEADS, NOPE + ROPE)
    zq = lambda n: jnp.zeros((DEPTH, Q_RANK, HEADS, n), MXU)
    w1q = jnp.concatenate([wuq, zq(32)], axis=-1).reshape(DEPTH, Q_RANK, D)
    w2q = jnp.concatenate([zq(64), _swap16(wuq[..., NOPE:]), zq(32)], axis=-1).reshape(DEPTH, Q_RANK, D)
    wukv = full["mla_wukv"].reshape(DEPTH, KV_RANK, HEADS, NOPE + VD)
    wk_p = jnp.concatenate([wukv[..., :NOPE], jnp.zeros((DEPTH, KV_RANK, HEADS, 64), MXU)], axis=-1).reshape(DEPTH, KV_RANK, D)
    wv_h = wukv[..., NOPE:].transpose(0, 2, 1, 3)
    wvt_h = wukv[..., NOPE:].transpose(0, 2, 3, 1)
    wb_h = full["branch_w"][:, 1].reshape(DEPTH, HEADS, VD, D)
    wr_f = jnp.stack([block_diag(*[lru_wr[l, b] for b in range(LRU_NB)]) for l in range(DEPTH)]).astype(MXU)
    wi_f = jnp.stack([block_diag(*[lru_wi[l, b] for b in range(LRU_NB)]) for l in range(DEPTH)]).astype(MXU)
    ws_b = gm_ws.astype(MXU)
    wst_b = gm_ws.transpose(0, 1, 3, 2).astype(MXU)
    bs_t = gm_bs.transpose(0, 2, 1)
    row = lambda v: v.reshape(1, -1)

    saved = []
    xs = x0
    h1 = None
    for l in range(DEPTH):
        sh1, sc1, g1, sh2, sc2, g2 = [row(mod[l, i * D:(i + 1) * D]) for i in range(6)]
        if l == 0:
            h1 = _lnmod_fwd(xs, sc1, sh1, f"lnmod_fwd{l}")
        z = _mm(h1, win_p[l], "nn", f"in_proj{l}", rows=[row(bin_p[l])], epi=lambda acc, b: (acc + b,))
        ya = _gmlp_fwd(z, row(gm_ln_g[l]), row(gm_ln_b[l]), ws_b[l], bs_t[l], f"gmlp_fwd{l}")
        qf, kf, vf, vtf = _mla_prep_fwd(z, cos_t, sin_t, row(mla_qnorm_g[l]), row(mla_kvnorm_g[l]), w1q[l], w2q[l], wk_p[l], wv_h[l],
                                        wvt_h[l], f"mla_prep_fwd{l}")
        ob, o32, lse = _attn_fwd(qf, kf, vtf, f"attn_fwd{l}")
        lru_args = (conv_w_f[l], row(lru_conv_b[l]), wr_f[l], row(lru_br[l]), wi_f[l], row(lru_bi[l]), row(lru_lambda[l]))
        yc, hl = _lru_fwd(z, *lru_args, f"lru_fwd{l}")
        merged, pa, pb, pc = _branch_fwd(z, ya, ob, yc, full["branch_w"][l, 0], wb_h[l], full["branch_w"][l, 2], f"branch_fwd{l}")
        mix = _mm(merged, full["mix_out_w"][l], "nn", f"mix_out{l}")
        x1, h2 = _res_ln_fwd(xs, mix, g1, row(ln_g_f[l, 0]), row(ln_b_f[l, 0]), f"res_ln_a{l}", mod=(sc2, sh2))
        a1, r2 = _mm(h2, full["ffn_w1"][l], "nn", f"ffn_up{l}", out_dtypes=(F32, MXU), rows=[row(ffn_b1[l])],
                     epi=lambda acc, b: (acc + b, jnp.square(jnp.maximum(acc + b, 0.0))))
        f = _mm(r2, full["ffn_w2"][l], "nn", f"ffn_down{l}", rows=[row(ffn_b2[l])], epi=lambda acc, b: (acc + b,))
        if l + 1 < DEPTH:
            nsh1, nsc1 = row(mod[l + 1, 0:D]), row(mod[l + 1, D:2 * D])
            x2, h1n = _res_ln_fwd(x1, f, g2, row(ln_g_f[l, 1]), row(ln_b_f[l, 1]), f"res_ln_b{l}", mod=(nsc1, nsh1))
        else:
            (x2,), h1n = _res_ln_fwd(x1, f, g2, row(ln_g_f[l, 1]), row(ln_b_f[l, 1]), f"res_ln_b{l}"), None
        saved.append(dict(x_in=xs, h1=h1, z=z, ya=ya, qf=qf, kf=kf, vf=vf, ob=ob, o32=o32, lse=lse, yc=yc, hl=hl, merged=merged,
                          pa=pa, pb=pb, pc=pc, mix=mix, x1=x1, h2=h2, a1=a1, r2=r2, f=f, x2=x2, lru_args=lru_args,
                          mods=(sh1, sc1, g1, sh2, sc2, g2)))
        xs, h1 = x2, h1n

    dy, sq = _loss_fwd(xs, tgt)
    loss = lax.psum(0.5 / D * jnp.sum(sq), ("x", "y", "c"))

    G = {}
    dmods = []
    dres, hpath = dy, None
    for l in reversed(range(DEPTH)):
        sv = saved[l]
        sh1, sc1, g1, sh2, sc2, g2 = sv["mods"]
        gl = {}
        nb = _node_bwd(f"node_b{l}", dres, hpath=hpath, upath=(sv["x1"], sv["f"], g2, row(ln_g_f[l, 1])))
        if hpath is not None:
            dmods[-1]["sc1"], dmods[-1]["sh1"] = nb["dsc"], nb["dsh"]
        dm = dict(g2=nb["dg"])
        ln_g_l1, ln_b_l1 = nb["dlng"], nb["dlnb"]
        df = nb["dbr"]
        gl["ffn_b2"] = _colsum_call(df, f"d_ffn_b2_{l}")[0]
        gl["ffn_w2"] = _mm(sv["r2"], df, "tn", f"d_ffn_w2_{l}")
        da1 = _mm(df, full["ffn_w2"][l], "nt", f"d_ffn_act{l}", out_dtypes=(MXU,), extras=[sv["a1"]],
                  epi=lambda acc, a: (acc * (2.0 * jnp.maximum(a, 0.0)),))
        gl["ffn_b1"] = _colsum_call(da1, f"d_ffn_b1_{l}")[0]
        gl["ffn_w1"] = _mm(sv["h2"], da1, "tn", f"d_ffn_w1_{l}")
        dh2 = _mm(da1, full["ffn_w1"][l], "nt", f"d_ffn_in{l}")
        na = _node_bwd(f"node_a{l}", nb["du"], hpath=(dh2, sv["x1"], sc2), upath=(sv["x_in"], sv["mix"], g1, row(ln_g_f[l, 0])))
        dm.update(sc2=na["dsc"], sh2=na["dsh"], g1=na["dg"])
        gl["ln_g"] = jnp.concatenate([na["dlng"], ln_g_l1], axis=0)
        gl["ln_b"] = jnp.concatenate([na["dlnb"], ln_b_l1], axis=0)
        dmix = na["dbr"]
        gl["mix_out_w"] = _mm(sv["merged"], dmix, "tn", f"d_mix_w{l}")
        dmerged = _mm(dmix, full["mix_out_w"][l], "nt", f"d_merged{l}")
        dz, dpa, dpb, dpc = _branch_bwd(sv["z"], dmerged, sv["pa"], sv["pb"], sv["pc"], f"branch_bwd{l}")
        wa, wc = full["branch_w"][l, 0], full["branch_w"][l, 2]
        dya = _mm(dpa, wa, "nt", f"d_ya{l}")
        dyc = _mm(dpc, wc, "nt", f"d_yc{l}")
        dwa = _mm(sv["ya"], dpa, "tn", f"d_wa{l}")
        dwc = _mm(sv["yc"], dpc, "tn", f"d_wc{l}")
        dob, do32, dwb = _heads_bwd(dpb, sv["ob"], wb_h[l], f"heads_bwd{l}")
        gl["branch_w"] = jnp.stack([dwa, dwb.reshape(GM_W, D), dwc])
        dz, dcw, dcb, dwr, dbr_, dwi, dbi, dlam = _lru_bwd(sv["z"], sv["hl"], dyc, *sv["lru_args"], dz, f"lru_bwd{l}")
        gl["lru_conv_w"], gl["lru_conv_b"], gl["lru_br"], gl["lru_bi"], gl["lru_lambda"] = dcw, dcb[0], dbr_[0], dbi[0], dlam[0]
        blocks = lambda m: jnp.stack([m[b * 64:(b + 1) * 64, b * 64:(b + 1) * 64] for b in range(LRU_NB)])
        gl["lru_wr"], gl["lru_wi"] = blocks(dwr), blocks(dwi)
        delta = _attn_delta(sv["o32"], do32, f"attn_delta{l}")
        dqf, dkf, dvf = _attn_bwd(sv["qf"], sv["kf"], sv["vf"], dob, sv["lse"], delta.reshape(HEADS, 1, S), f"attn_bwd{l}")
        dz, dgq, dgkv, dw1, dw2, dwk, dwv = _mla_prep_bwd(sv["z"], cos_t, sin_t, row(mla_qnorm_g[l]), row(mla_kvnorm_g[l]), w1q[l],
                                                          w2q[l], wk_p[l], wv_h[l], dqf, dkf, dvf, dz, f"mla_prep_bwd{l}")
        gl["mla_qnorm_g"], gl["mla_kvnorm_g"] = dgq[0], dgkv[0]
        dw1 = dw1.reshape(Q_RANK, HEADS, 128)
        dw2 = dw2.reshape(Q_RANK, HEADS, 128)
        gl["mla_wuq"] = jnp.concatenate([dw1[..., :NOPE], dw1[..., NOPE:NOPE + ROPE] + _swap16(dw2[..., NOPE:NOPE + ROPE])],
                                        axis=-1).reshape(Q_RANK, HEADS * (NOPE + ROPE))
        gl["mla_wukv"] = jnp.concatenate([dwk.reshape(KV_RANK, HEADS, 128)[..., :NOPE], dwv.transpose(1, 0, 2)],
                                         axis=-1).reshape(KV_RANK, HEADS * (NOPE + VD))
        dz, dglg, dglb, dws, dbs = _gmlp_bwd(sv["z"], dya, row(gm_ln_g[l]), row(gm_ln_b[l]), ws_b[l], wst_b[l], bs_t[l], dz,
                                             f"gmlp_bwd{l}")
        gl["gm_ln_g"], gl["gm_ln_b"], gl["gm_ws"], gl["gm_bs"] = dglg[0], dglb[0], dws, dbs[:, :GM_G].T
        gl["in_b"] = _unpad_in_cols(_colsum_call(dz, f"d_in_b{l}"))[0]
        gl["in_w"] = _unpad_in_cols(_mm(sv["h1"], dz, "tn", f"d_in_w{l}"))
        dh1 = _mm(dz, win_p[l], "nt", f"d_h1_{l}")
        dmods.append(dm)
        dres, hpath = na["du"], (dh1, sv["x_in"], sc1)
        G[l] = gl
    n0 = _node_bwd("node_in", dres, hpath=hpath)
    dmods[-1]["sc1"], dmods[-1]["sh1"] = n0["dsc"], n0["dsh"]
    grad_x = n0["dx"][None]
    dmods = dmods[::-1]
    dmod = jnp.stack([jnp.concatenate([dmods[l][k] for k in ("sh1", "sc1", "g1", "sh2", "sc2", "g2")], axis=1)[0]
                      for l in range(DEPTH)])
    grads = {n: jnp.stack([G[l][n] for l in range(DEPTH)]) for n in G[0]}
    grads["ada_b"] = dmod

    gsh = jnp.stack([_pack({n: jnp.split(grads[n], 4, axis=ax)[k] for n, ax in _BIG}) for k in range(4)])
    R = gsh.shape[1]
    rh = R // 2
    from_sib = _sibling_swap(gsh, "rs_pair")
    mine_half = lax.dynamic_slice_in_dim(gsh, ci * rh, rh, axis=1)
    pair, pair_b = _pair_add(mine_half, from_sib, "rs_pair_add")
    from_chips = _chip_scatter(pair_b, "rs_chips")
    own = lax.dynamic_index_in_dim(pair, kme, axis=0, keepdims=False)
    half_sum = _chip_sum(own, from_chips, "rs_chip_sum")
    gflat = _sibling_join(half_sum, "rs_join")
    gbig = _unpack(gflat, shard_shapes)

    small_shapes = [grads[n].shape for n in _SMALL]
    gsm_all = _allgather8(_flat128([grads[n] for n in _SMALL]), "ag_small_grads")
    gsm = _unflat(_sum8(gsm_all, "sum_small_grads"), small_shapes)
    gsmall = dict(zip(_SMALL, gsm))
    dmod_all = jnp.stack([_unflat(gsm_all[d], small_shapes)[0] for d in range(8)], axis=1)
    dmod_sh = lax.dynamic_slice_in_dim(dmod_all, kme * (6 * D // 4), 6 * D // 4, axis=2)
    g_ada_w = _ada_w_grad(c_all.T, dmod_sh, "d_ada_w")
    quarter = lambda g, ax: lax.dynamic_slice_in_dim(g, kme * (g.shape[ax] // 4), g.shape[ax] // 4, axis=ax)
    gsmall["lru_conv_w"] = quarter(gsmall["lru_conv_w"], 2)
    gsmall["ln_g"] = quarter(gsmall["ln_g"], 2)
    gsmall["ln_b"] = quarter(gsmall["ln_b"], 2)

    grad = dict(gbig)
    grad.update(gsmall)
    grad["ada_w"] = g_ada_w

    delta, new_m, new_v = {}, {}, {}
    for n in ("ada_w",) + tuple(n for n, _ in _BIG):
        shp = W[n].shape
        two = lambda a: a.reshape(-1, shp[-1])
        d_, m_, v_ = _adamw(two(W[n]), two(grad[n]), two(M[n]), two(V[n]), f"adamw_{n}")
        delta[n], new_m[n], new_v[n] = d_.reshape(shp), m_.reshape(shp), v_.reshape(shp)
    sm_shapes = [W[n].shape for n in _SMALL]
    d_, m_, v_ = _adamw(_flat128([W[n] for n in _SMALL]), _flat128([grad[n] for n in _SMALL]), _flat128([M[n] for n in _SMALL]),
                        _flat128([V[n] for n in _SMALL]), "adamw_small")
    for n, a, b, c_ in zip(_SMALL, _unflat(d_, sm_shapes), _unflat(m_, sm_shapes), _unflat(v_, sm_shapes)):
        delta[n], new_m[n], new_v[n] = a, b, c_

    return (loss, grad_x, *[grad[n] for n in _NAMES], *[delta[n] for n in _NAMES], *[new_m[n] for n in _NAMES],
            *[new_v[n] for n in _NAMES])
```

```python
import functools
import math

import jax
import jax.numpy as jnp
from jax import lax
from jax.experimental import pallas as pl
from jax.experimental.pallas import tpu as pltpu
from jax.scipy.linalg import block_diag

F32 = jnp.float32
MXU = jnp.bfloat16
MESH = pl.DeviceIdType.MESH

D = 1024
DEPTH = 2
CHUNK = 64
GM_W = 512
GM_G = 4
HEADS = 8
Q_RANK = 256
KV_RANK = 128
NOPE = 64
ROPE = 32
VD = 64
LRU_W = 512
LRU_NB = 8
D_FF = 4096
ALPHA = (2.0 * DEPTH) ** 0.25
LN_EPS = 1e-5
RMS_EPS = 1e-6
ROPE_BASE = 10000.0
ATT_SCALE = (NOPE + ROPE) ** -0.5
N_IN = 5536
P_GATE, P_GM, P_LX, P_LG, P_Q, P_KV, P_KA, P_KB, NP = 0, 3072, 4096, 4608, 5120, 5376, 5504, 5632, 5760
O_GM, O_Q, O_KV, O_KR, O_LX, O_LG, O_GATE = 0, 1024, 1280, 1408, 1440, 1952, 2464

ADAM_LR, ADAM_B1, ADAM_B2, ADAM_EPS, ADAM_WD, ADAM_STEP = 0.001, 0.9, 0.999, 1e-08, 0.01, 10

ROW_TILE = 512
ATT_TILE = 512
VMEM_LIMIT = 56 * 1024 * 1024
FLAT_W = 1024
FLAT_ALIGN = 32


def _params(sem=None, vmem=VMEM_LIMIT):
    return pltpu.CompilerParams(dimension_semantics=sem, vmem_limit_bytes=vmem)


def _tile(dim, pref, mult=128):
    if dim <= pref:
        return dim
    t = (pref // mult) * mult
    while t >= mult:
        if dim % t == 0:
            return t
        t -= mult
    return dim


def _dot(a, b):
    return lax.dot_general(a, b, (((1,), (0,)), ((), ())), preferred_element_type=F32)


def _dot_nt(a, b):
    return lax.dot_general(a, b, (((1,), (1,)), ((), ())), preferred_element_type=F32)


def _dot_tn(a, b):
    return lax.dot_general(a, b, (((0,), (0,)), ((), ())), preferred_element_type=F32)


def _sigmoid(x):
    return 1.0 / (1.0 + jnp.exp(-x))


_GC = 0.7978845608028654


def _gelu(x):
    return 0.5 * x * (1.0 + jnp.tanh(_GC * (x + 0.044715 * x * x * x)))


def _gelu_grad(x):
    t = jnp.tanh(_GC * (x + 0.044715 * x * x * x))
    return 0.5 * (1.0 + t) + 0.5 * x * (1.0 - t * t) * _GC * (1.0 + 3.0 * 0.044715 * x * x)


def _ln_stats(x):
    mu = jnp.mean(x, axis=-1, keepdims=True)
    xc = x - mu
    r = lax.rsqrt(jnp.mean(xc * xc, axis=-1, keepdims=True) + LN_EPS)
    return xc * r, r


def _ln_bwd(dxh, xh, r):
    return r * (dxh - jnp.mean(dxh, axis=-1, keepdims=True) - xh * jnp.mean(dxh * xh, axis=-1, keepdims=True))


def _colsum(v):
    return jnp.sum(v, axis=0, keepdims=True)


def _acc(ref, first, val):
    @pl.when(first)
    def _():
        ref[...] = val

    @pl.when(jnp.logical_not(first))
    def _():
        ref[...] += val


def _row_spec(t, w, col=0):
    return pl.BlockSpec((t, w), lambda i, c=col: (i, c))


def _vec_spec(w):
    return pl.BlockSpec((1, w), lambda i: (0, 0))


def _full_spec(shape):
    nd = len(shape)
    return pl.BlockSpec(shape, lambda i, n=nd: (0,) * n)


def _mm(a, b, mode, name, out_dtypes=(F32,), extras=(), rows=(), epi=None, colsum=False, tm=1024, tn=1152, tk=1024):
    if mode == "nn":
        (M, K), N = a.shape, b.shape[1]
    elif mode == "nt":
        (M, K), N = a.shape, b.shape[0]
    else:
        (K, M), N = a.shape, b.shape[1]
    tm, tn, tk = _tile(M, tm), _tile(N, tn), _tile(K, tk)
    nk = K // tk
    dot = {"nn": _dot, "nt": _dot_nt, "tn": _dot_tn}[mode]
    a_spec = pl.BlockSpec((tk, tm), lambda i, j, k: (k, i)) if mode == "tn" else pl.BlockSpec((tm, tk), lambda i, j, k: (i, k))
    b_spec = pl.BlockSpec((tn, tk), lambda i, j, k: (j, k)) if mode == "nt" else pl.BlockSpec((tk, tn), lambda i, j, k: (k, j))
    o_spec = pl.BlockSpec((tm, tn), lambda i, j, k: (i, j))
    r_spec = pl.BlockSpec((1, tn), lambda i, j, k: (0, j))
    n_e, n_r, n_o = len(extras), len(rows), len(out_dtypes)
    if epi is None:
        epi = lambda acc: (acc,)

    def body(*refs):
        a_ref, b_ref = refs[0], refs[1]
        e_refs = refs[2:2 + n_e]
        r_refs = refs[2 + n_e:2 + n_e + n_r]
        o_refs = refs[2 + n_e + n_r:2 + n_e + n_r + n_o]
        p = dot(a_ref[...], b_ref[...])

        def finish(acc):
            outs = epi(acc, *[e[...] for e in e_refs], *[r[...] for r in r_refs])
            for o_ref, o in zip(o_refs, outs):
                o_ref[...] = o.astype(o_ref.dtype)
            if colsum:
                cs_ref = refs[2 + n_e + n_r + n_o]
                cols = pl.ds(pl.multiple_of(pl.program_id(1) * tn, 128), tn)
                _acc(cs_ref.at[:, cols], pl.program_id(0) == 0, _colsum(outs[0]))

        if nk == 1:
            finish(p)
        else:
            acc_ref = refs[-1]
            k = pl.program_id(2)
            _acc(acc_ref, k == 0, p)

            @pl.when(k == nk - 1)
            def _():
                finish(acc_ref[...])

    outs = pl.pallas_call(
        body, name=name, grid=(M // tm, N // tn, nk),
        in_specs=[a_spec, b_spec] + [o_spec] * n_e + [r_spec] * n_r,
        out_specs=[o_spec] * n_o + ([pl.BlockSpec((1, N), lambda i, j, k: (0, 0))] if colsum else []),
        out_shape=[jax.ShapeDtypeStruct((M, N), dt) for dt in out_dtypes] + ([jax.ShapeDtypeStruct((1, N), F32)] if colsum else []),
        scratch_shapes=[pltpu.VMEM((tm, tn), F32)] if nk > 1 else [],
        compiler_params=_params(("arbitrary",) * 3 if colsum else ("parallel", "parallel", "arbitrary")),
    )(a, b, *extras, *rows)
    return outs[0] if len(outs) == 1 else outs


def _lnmod_fwd(x, sc, sh, name):
    S = x.shape[0]
    t = min(ROW_TILE, S)

    def body(x_ref, sc_ref, sh_ref, h_ref):
        xh, _ = _ln_stats(x_ref[...])
        h_ref[...] = (xh * (1.0 + sc_ref[...]) + sh_ref[...]).astype(h_ref.dtype)

    return pl.pallas_call(
        body, name=name, grid=(S // t,),
        in_specs=[_row_spec(t, D), _vec_spec(D), _vec_spec(D)], out_specs=_row_spec(t, D),
        out_shape=jax.ShapeDtypeStruct((S, D), MXU), compiler_params=_params(("parallel",)),
    )(x, sc, sh)


def _res_ln_fwd(xprev, br, gvec, lng, lnb, name, mod=None):
    S = xprev.shape[0]
    t = min(ROW_TILE, S)
    with_h = mod is not None

    def body(*refs):
        xp_ref, br_ref, g_ref, lg_ref, lb_ref = refs[:5]
        u = ALPHA * xp_ref[...] + (1.0 + g_ref[...]) * br_ref[...]
        uh, _ = _ln_stats(u)
        xn = uh * lg_ref[...] + lb_ref[...]
        if with_h:
            sc_ref, sh_ref, xn_ref, h_ref = refs[5:]
            xh, _ = _ln_stats(xn)
            h_ref[...] = (xh * (1.0 + sc_ref[...]) + sh_ref[...]).astype(h_ref.dtype)
        else:
            xn_ref = refs[5]
        xn_ref[...] = xn

    ins = [xprev, br, gvec, lng, lnb] + (list(mod) if with_h else [])
    return pl.pallas_call(
        body, name=name, grid=(S // t,),
        in_specs=[_row_spec(t, D), _row_spec(t, D)] + [_vec_spec(D)] * (len(ins) - 2),
        out_specs=[_row_spec(t, D)] * (2 if with_h else 1),
        out_shape=[jax.ShapeDtypeStruct((S, D), F32)] + ([jax.ShapeDtypeStruct((S, D), MXU)] if with_h else []),
        compiler_params=_params(("parallel",)),
    )(*ins)


def _loss_fwd(y, tgt):
    S = y.shape[0]
    t = min(ROW_TILE, S)

    def body(y_ref, t_ref, dy_ref, sq_ref):
        d = y_ref[...] - t_ref[...]
        dy_ref[...] = d * (1.0 / D)
        _acc(sq_ref, pl.program_id(0) == 0, _colsum(d * d))

    return pl.pallas_call(
        body, name="loss_head", grid=(S // t,),
        in_specs=[_row_spec(t, D), _row_spec(t, D)], out_specs=[_row_spec(t, D), _vec_spec(D)],
        out_shape=[jax.ShapeDtypeStruct((S, D), F32), jax.ShapeDtypeStruct((1, D), F32)],
        compiler_params=_params(("arbitrary",)),
    )(y, tgt)


def _node_bwd(name, dres, *, hpath=None, upath=None):
    S = dres.shape[0]
    t = min(ROW_TILE, S)
    has_h, has_u = hpath is not None, upath is not None

    def body(*refs):
        refs = list(refs)
        first = pl.program_id(0) == 0
        dxs = refs.pop(0)[...]
        if has_h:
            dh = refs.pop(0)[...]
            xs = refs.pop(0)[...]
            sc = refs.pop(0)[...]
        if has_u:
            xp = refs.pop(0)[...]
            br = refs.pop(0)[...]
            gv = refs.pop(0)[...]
            lg = refs.pop(0)[...]
        if has_h:
            dsc_ref, dsh_ref = refs.pop(0), refs.pop(0)
            xh, r = _ln_stats(xs)
            _acc(dsc_ref, first, _colsum(dh * xh))
            _acc(dsh_ref, first, _colsum(dh))
            dxs = dxs + _ln_bwd(dh * (1.0 + sc), xh, r)
        if has_u:
            du_ref, dbr_ref, dlg_ref, dlb_ref, dg_ref, dbs_ref = refs
            uh, ru = _ln_stats(ALPHA * xp + (1.0 + gv) * br)
            _acc(dlg_ref, first, _colsum(dxs * uh))
            _acc(dlb_ref, first, _colsum(dxs))
            du = _ln_bwd(dxs * lg, uh, ru)
            _acc(dg_ref, first, _colsum(du * br))
            du_ref[...] = ALPHA * du
            dbr = (1.0 + gv) * du
            _acc(dbs_ref, first, _colsum(dbr))
            dbr_ref[...] = dbr.astype(dbr_ref.dtype)
        else:
            refs[0][...] = dxs

    ins, in_specs = [dres], [_row_spec(t, D)]
    outs, out_specs, names = [], [], []
    vec = jax.ShapeDtypeStruct((1, D), F32)
    if has_h:
        ins += list(hpath)
        in_specs += [_row_spec(t, D), _row_spec(t, D), _vec_spec(D)]
        outs += [vec, vec]
        out_specs += [_vec_spec(D), _vec_spec(D)]
        names += ["dsc", "dsh"]
    if has_u:
        ins += list(upath)
        in_specs += [_row_spec(t, D), _row_spec(t, D), _vec_spec(D), _vec_spec(D)]
        outs += [jax.ShapeDtypeStruct((S, D), F32), jax.ShapeDtypeStruct((S, D), MXU), vec, vec, vec, vec]
        out_specs += [_row_spec(t, D), _row_spec(t, D), _vec_spec(D), _vec_spec(D), _vec_spec(D), _vec_spec(D)]
        names += ["du", "dbr", "dlng", "dlnb", "dg", "dbrsum"]
    else:
        outs += [jax.ShapeDtypeStruct((S, D), F32)]
        out_specs += [_row_spec(t, D)]
        names += ["dx"]
    res = pl.pallas_call(
        body, name=name, grid=(S // t,), in_specs=in_specs, out_specs=out_specs, out_shape=outs,
        compiler_params=_params(("arbitrary",)),
    )(*ins)
    return dict(zip(names, res))


def _colsum_call(a, name):
    S, N = a.shape
    t, tn = min(ROW_TILE, S), _tile(N, 1152)

    def body(a_ref, o_ref):
        _acc(o_ref, pl.program_id(1) == 0, _colsum(a_ref[...].astype(F32)))

    return pl.pallas_call(
        body, name=name, grid=(N // tn, S // t),
        in_specs=[pl.BlockSpec((t, tn), lambda j, i: (i, j))], out_specs=pl.BlockSpec((1, tn), lambda j, i: (0, j)),
        out_shape=jax.ShapeDtypeStruct((1, N), F32), compiler_params=_params(("parallel", "arbitrary")),
    )(a)


def _gm_mask():
    i = lax.broadcasted_iota(jnp.int32, (128, 128), 0) // CHUNK
    j = lax.broadcasted_iota(jnp.int32, (128, 128), 1) // CHUNK
    return i >= j


def _gmlp_common(z, lng, lnb):
    gz = _gelu(z)
    u, v = gz[:, :GM_W], gz[:, GM_W:]
    vh, r = _ln_stats(v)
    return u, vh, r, vh * lng + lnb


def _gmlp_fwd(z, lng, lnb, ws, bst, name):
    S = z.shape[0]
    t = min(ROW_TILE, S)

    def body(z_ref, lg_ref, lb_ref, ws_ref, bs_ref, y_ref):
        u, _, _, vn = _gmlp_common(z_ref[...], lg_ref[...], lb_ref[...])
        mask = _gm_mask()
        vb = vn.astype(MXU)
        for g in range(GM_G):
            w = jnp.where(mask, ws_ref[g], jnp.zeros_like(ws_ref[g]))
            bias = bs_ref[:, g:g + 1]
            for blk in range(t // 128):
                rs, cs = slice(blk * 128, (blk + 1) * 128), slice(g * 128, (g + 1) * 128)
                f = _dot(w, vb[rs, cs]) + bias
                y_ref[rs, cs] = (u[rs, cs] * f).astype(y_ref.dtype)

    return pl.pallas_call(
        body, name=name, grid=(S // t,),
        in_specs=[_row_spec(t, 2 * GM_W, P_GM // (2 * GM_W)), _vec_spec(GM_W), _vec_spec(GM_W),
                  _full_spec((GM_G, 128, 128)), _full_spec((128, GM_G))],
        out_specs=_row_spec(t, GM_W), out_shape=jax.ShapeDtypeStruct((S, GM_W), MXU),
        compiler_params=_params(("parallel",)),
    )(z, lng, lnb, ws, bst)


def _gmlp_bwd(z, dya, lng, lnb, ws, wst, bst, dz_in, name):
    S = z.shape[0]
    t = min(ROW_TILE, S)

    def body(z_ref, dy_ref, lg_ref, lb_ref, ws_ref, wst_ref, bs_ref, _dz_in, dz_ref, dlg_ref, dlb_ref, dws_ref, dbs_ref, sum_ref):
        first = pl.program_id(0) == 0
        zz = z_ref[...]
        u, vh, r, vn = _gmlp_common(zz, lg_ref[...], lb_ref[...])
        dy = dy_ref[...]
        mask = _gm_mask()
        maskt = lax.broadcasted_iota(jnp.int32, (128, 128), 1) // CHUNK >= lax.broadcasted_iota(jnp.int32, (128, 128), 0) // CHUNK
        lane = lax.broadcasted_iota(jnp.int32, (128, 128), 1)
        vb = vn.astype(MXU)
        dfb = (dy * u).astype(MXU)
        df32 = dy * u
        dbs = jnp.zeros((128, 128), F32)
        du_cols, dvn_cols = [], []
        for g in range(GM_G):
            w = jnp.where(mask, ws_ref[g], jnp.zeros_like(ws_ref[g]))
            wt = jnp.where(maskt, wst_ref[g], jnp.zeros_like(wst_ref[g]))
            bias = bs_ref[:, g:g + 1]
            cs = slice(g * 128, (g + 1) * 128)
            dw = jnp.zeros((128, 128), F32)
            du_rows, dvn_rows = [], []
            for blk in range(t // 128):
                rs = slice(blk * 128, (blk + 1) * 128)
                f = _dot(w, vb[rs, cs]) + bias
                du_rows.append(dy[rs, cs] * f)
                dvn_rows.append(_dot(wt, dfb[rs, cs]))
                dw = dw + _dot_nt(dfb[rs, cs], vb[rs, cs])
                dbs = dbs + jnp.where(lane == g, jnp.sum(df32[rs, cs], axis=1, keepdims=True), 0.0)
            _acc(dws_ref.at[g], first, jnp.where(mask, dw, 0.0))
            du_cols.append(jnp.concatenate(du_rows, axis=0))
            dvn_cols.append(jnp.concatenate(dvn_rows, axis=0))
        _acc(dbs_ref, first, dbs)
        du = jnp.concatenate(du_cols, axis=1)
        dvn = jnp.concatenate(dvn_cols, axis=1)
        _acc(dlg_ref, first, _colsum(dvn * vh))
        _acc(dlb_ref, first, _colsum(dvn))
        dv = _ln_bwd(dvn * lg_ref[...], vh, r)
        dzz = jnp.concatenate([du, dv], axis=1) * _gelu_grad(zz)
        dz_ref[...] = dzz.astype(dz_ref.dtype)
        _acc(sum_ref, first, _colsum(dzz))

    vec = jax.ShapeDtypeStruct((1, GM_W), F32)
    return pl.pallas_call(
        body, name=name, grid=(S // t,),
        in_specs=[_row_spec(t, 2 * GM_W, P_GM // (2 * GM_W)), _row_spec(t, GM_W), _vec_spec(GM_W), _vec_spec(GM_W),
                  _full_spec((GM_G, 128, 128)), _full_spec((GM_G, 128, 128)), _full_spec((128, GM_G)), _ANY],
        out_specs=[_row_spec(t, 2 * GM_W, P_GM // (2 * GM_W)), _vec_spec(GM_W), _vec_spec(GM_W), _full_spec((GM_G, 128, 128)),
                   _full_spec((128, 128)), _vec_spec(2 * GM_W)],
        out_shape=[jax.ShapeDtypeStruct((S, NP), MXU), vec, vec,
                   jax.ShapeDtypeStruct((GM_G, 128, 128), F32), jax.ShapeDtypeStruct((128, 128), F32),
                   jax.ShapeDtypeStruct((1, 2 * GM_W), F32)],
        input_output_aliases=_dz_alias(8), compiler_params=_params(("arbitrary",)),
    )(z, dya, lng, lnb, ws, wst, bst, dz_in)


def _rms(x, g):
    r = lax.rsqrt(jnp.mean(x * x, axis=-1, keepdims=True) + RMS_EPS)
    xh = x * r
    return xh, r, xh * g


def _mla_specs(t):
    return [_row_spec(t, Q_RANK, P_Q // Q_RANK), _row_spec(t, KV_RANK, P_KV // 128), _row_spec(t, 128, P_KA // 128),
            _row_spec(t, 128, P_KB // 128), _row_spec(t, 128), _row_spec(t, 128), _vec_spec(Q_RANK), _vec_spec(KV_RANK),
            _full_spec((Q_RANK, D)), _full_spec((Q_RANK, D)), _full_spec((KV_RANK, D)), _full_spec((HEADS, KV_RANK, VD))]


def _mla_prep_fwd(z, cos, sin, gq, gkv, w1, w2, wk, wv, wvt, name):
    S = z.shape[0]
    t = min(ROW_TILE, S)

    def body(zq_ref, zkv_ref, zka_ref, zkb_ref, cos_ref, sin_ref, gq_ref, gkv_ref, w1_ref, w2_ref, wk_ref, wv_ref, wvt_ref,
             q_ref, k_ref, v_ref, vt_ref):
        cos_, sin_ = cos_ref[...], sin_ref[...]
        cq = cos_ + jnp.where(lax.broadcasted_iota(jnp.int32, cos_.shape, 1) < NOPE, 1.0, 0.0)
        qn = _rms(zq_ref[...], gq_ref[...])[2].astype(MXU)
        q1, q2 = _dot(qn, w1_ref[...]), _dot(qn, w2_ref[...])
        kvn = _rms(zkv_ref[...], gkv_ref[...])[2].astype(MXU)
        kn = _dot(kvn, wk_ref[...])
        krot = zka_ref[...] * cos_ + zkb_ref[...] * sin_
        for h in range(HEADS):
            hs = slice(h * 128, (h + 1) * 128)
            q_ref[h] = (q1[:, hs] * cq + q2[:, hs] * sin_).astype(q_ref.dtype)
            k_ref[h] = (kn[:, hs] + krot).astype(k_ref.dtype)
            v_ref[h] = _dot(kvn, wv_ref[h]).astype(v_ref.dtype)
            vt_ref[h] = _dot_nt(wvt_ref[h], kvn).astype(vt_ref.dtype)

    hspec = lambda w: pl.BlockSpec((HEADS, t, w), lambda i: (0, i, 0))
    return pl.pallas_call(
        body, name=name, grid=(S // t,), in_specs=_mla_specs(t) + [_full_spec((HEADS, VD, KV_RANK))],
        out_specs=[hspec(128), hspec(128), hspec(VD), pl.BlockSpec((HEADS, VD, t), lambda i: (0, 0, i))],
        out_shape=[jax.ShapeDtypeStruct((HEADS, S, 128), MXU), jax.ShapeDtypeStruct((HEADS, S, 128), MXU),
                   jax.ShapeDtypeStruct((HEADS, S, VD), MXU), jax.ShapeDtypeStruct((HEADS, VD, S), MXU)],
        compiler_params=_params(("parallel",)),
    )(z, z, z, z, cos, sin, gq, gkv, w1, w2, wk, wv, wvt)


def _mla_prep_bwd(z, cos, sin, gq, gkv, w1, w2, wk, wv, dq, dk, dv, dz_in, name):
    S = z.shape[0]
    t = min(ROW_TILE, S)

    def body(zq_ref, zkv_ref, zka_ref, zkb_ref, cos_ref, sin_ref, gq_ref, gkv_ref, w1_ref, w2_ref, wk_ref, wv_ref,
             dq_ref, dk_ref, dv_ref, _dz_in, dz_ref, dgq_ref, dgkv_ref, dw1_ref, dw2_ref, dwk_ref, dwv_ref, sum_ref):
        first = pl.program_id(0) == 0
        cos_, sin_ = cos_ref[...], sin_ref[...]
        cq = cos_ + jnp.where(lax.broadcasted_iota(jnp.int32, cos_.shape, 1) < NOPE, 1.0, 0.0)
        gq_, gkv_ = gq_ref[...], gkv_ref[...]
        xhq, rq, qn32 = _rms(zq_ref[...], gq_)
        qn = qn32.astype(MXU)
        dq1 = jnp.concatenate([dq_ref[h] * cq for h in range(HEADS)], axis=1).astype(MXU)
        dq2 = jnp.concatenate([dq_ref[h] * sin_ for h in range(HEADS)], axis=1).astype(MXU)
        dqn = _dot_nt(dq1, w1_ref[...]) + _dot_nt(dq2, w2_ref[...])
        _acc(dw1_ref, first, _dot_tn(qn, dq1))
        _acc(dw2_ref, first, _dot_tn(qn, dq2))
        _acc(dgq_ref, first, _colsum(dqn * xhq))
        dxn = dqn * gq_
        dzq = rq * (dxn - xhq * jnp.mean(dxn * xhq, axis=-1, keepdims=True))

        xhk, rk, kvn32 = _rms(zkv_ref[...], gkv_)
        kvn = kvn32.astype(MXU)
        dks = [dk_ref[h] for h in range(HEADS)]
        dkall = jnp.concatenate(dks, axis=1).astype(MXU)
        dkrot = functools.reduce(lambda p, q_: p + q_, dks)
        dkvn = _dot_nt(dkall, wk_ref[...])
        _acc(dwk_ref, first, _dot_tn(kvn, dkall))
        for h in range(HEADS):
            dvb = dv_ref[h].astype(MXU)
            dkvn = dkvn + _dot_nt(dvb, wv_ref[h])
            _acc(dwv_ref.at[h], first, _dot_tn(kvn, dvb))
        _acc(dgkv_ref, first, _colsum(dkvn * xhk))
        dxk = dkvn * gkv_
        dzkv = rk * (dxk - xhk * jnp.mean(dxk * xhk, axis=-1, keepdims=True))
        dzm = jnp.concatenate([dzq, dzkv, dkrot * cos_, dkrot * sin_], axis=1)
        dz_ref[...] = dzm.astype(dz_ref.dtype)
        _acc(sum_ref, first, _colsum(dzm))

    hspec = lambda w: pl.BlockSpec((HEADS, t, w), lambda i: (0, i, 0))
    sds = jax.ShapeDtypeStruct
    return pl.pallas_call(
        body, name=name, grid=(S // t,),
        in_specs=_mla_specs(t) + [hspec(128), hspec(128), hspec(VD), _ANY],
        out_specs=[_row_spec(t, 640, P_Q // 640), _vec_spec(Q_RANK), _vec_spec(KV_RANK), _full_spec((Q_RANK, D)),
                   _full_spec((Q_RANK, D)), _full_spec((KV_RANK, D)), _full_spec((HEADS, KV_RANK, VD)), _vec_spec(640)],
        out_shape=[sds((S, NP), MXU), sds((1, Q_RANK), F32), sds((1, KV_RANK), F32), sds((Q_RANK, D), F32),
                   sds((Q_RANK, D), F32), sds((KV_RANK, D), F32), sds((HEADS, KV_RANK, VD), F32), sds((1, 640), F32)],
        input_output_aliases=_dz_alias(16), compiler_params=_params(("arbitrary",)),
    )(z, z, z, z, cos, sin, gq, gkv, w1, w2, wk, wv, dq, dk, dv, dz_in)


def _chunk_mask(t, transposed):
    r = lax.broadcasted_iota(jnp.int32, (t, t), 0) // CHUNK
    c = lax.broadcasted_iota(jnp.int32, (t, t), 1) // CHUNK
    return (r <= c) if transposed else (c <= r)


def _attn_fwd(q, k, vt, name):
    S = q.shape[1]
    t = min(ATT_TILE, S)
    n = S // t

    hb = 2

    def body(q_ref, k_ref, vt_ref, ob_ref, o_ref, lse_ref):
        qi = pl.program_id(1)
        qbs = [q_ref[g] for g in range(hb)]

        def block(j, carries, masked):
            cols = pl.ds(pl.multiple_of(j * t, t), t)
            out = []
            for g in range(hb):
                m, l, acc = carries[g]
                st = _dot_nt(k_ref[g, cols, :], qbs[g]) * ATT_SCALE
                if masked:
                    st = jnp.where(_chunk_mask(t, True), st, -jnp.inf)
                m_new = jnp.maximum(m, jnp.max(st, axis=0, keepdims=True))
                p = jnp.exp(st - m_new)
                alpha = jnp.exp(m - m_new)
                l = alpha * l + jnp.sum(p, axis=0, keepdims=True)
                acc = alpha * acc + _dot(vt_ref[g, :, cols], p.astype(MXU))
                out.append((m_new, l, acc))
            return tuple(out)

        init = tuple((jnp.full((1, t), -jnp.inf, F32), jnp.zeros((1, t), F32), jnp.zeros((VD, t), F32)) for _ in range(hb))
        carries = lax.fori_loop(0, qi, lambda j, c: block(j, c, False), init)
        for g, (m, l, acc) in enumerate(block(qi, carries, True)):
            o = (acc / l).T
            o_ref[g] = o
            ob_ref[g] = o.astype(ob_ref.dtype)
            lse_ref[g] = m + jnp.log(l)

    qspec = lambda w: pl.BlockSpec((hb, t, w), lambda h, i: (h, i, 0))
    sds = jax.ShapeDtypeStruct
    return pl.pallas_call(
        body, name=name, grid=(HEADS // hb, n),
        in_specs=[qspec(128), pl.BlockSpec((hb, S, 128), lambda h, i: (h, 0, 0)), pl.BlockSpec((hb, VD, S), lambda h, i: (h, 0, 0))],
        out_specs=[qspec(VD), qspec(VD), pl.BlockSpec((hb, 1, t), lambda h, i: (h, 0, i))],
        out_shape=[sds((HEADS, S, VD), MXU), sds((HEADS, S, VD), F32), sds((HEADS, 1, S), F32)],
        compiler_params=_params(("parallel", "arbitrary")),
    )(q, k, vt)


def _attn_delta(o, do, name):
    S = o.shape[1]
    t = min(ROW_TILE, S)

    def body(o_ref, do_ref, d_ref):
        for h in range(HEADS):
            d_ref[h] = jnp.sum(o_ref[h] * do_ref[h], axis=1, keepdims=True)

    hspec = lambda w: pl.BlockSpec((HEADS, t, w), lambda i: (0, i, 0))
    return pl.pallas_call(
        body, name=name, grid=(S // t,), in_specs=[hspec(VD), hspec(VD)], out_specs=hspec(1),
        out_shape=jax.ShapeDtypeStruct((HEADS, S, 1), F32), compiler_params=_params(("parallel",)),
    )(o, do)


def _attn_bwd(q, k, v, do, lse_row, d_row, name):
    S = q.shape[1]
    t = min(ATT_TILE, S)
    n = S // t

    def body(q_ref, do_ref, lse_ref, d_ref, k_ref, v_ref, dq_ref, dk_ref, dv_ref, dk_s, dv_s):
        ki = pl.program_id(1)

        @pl.when(ki == 0)
        def _():
            dq_ref[...] = jnp.zeros(dq_ref.shape, F32)

        dk_s[...] = jnp.zeros(dk_s.shape, F32)
        dv_s[...] = jnp.zeros(dv_s.shape, F32)
        kb, vb = k_ref[0], v_ref[0]

        def step(qi, masked):
            rows = pl.ds(pl.multiple_of(qi * t, t), t)
            qb, dob = q_ref[0, rows, :], do_ref[0, rows, :]
            st = _dot_nt(kb, qb) * ATT_SCALE
            if masked:
                st = jnp.where(_chunk_mask(t, True), st, -jnp.inf)
            pt = jnp.exp(st - lse_ref[0, :, rows])
            dv_s[...] += _dot(pt.astype(MXU), dob)
            dpt = _dot_nt(vb, dob)
            dst = (pt * (dpt - d_ref[0, :, rows]) * ATT_SCALE).astype(MXU)
            dk_s[...] += _dot(dst, qb)
            dq_ref[0, rows, :] += _dot_tn(dst, kb)

        step(ki, True)

        def loop(qi, c):
            step(qi, False)
            return c

        lax.fori_loop(ki + 1, n, loop, 0)
        dk_ref[0] = dk_s[...]
        dv_ref[0] = dv_s[...]

    head = lambda *shape: pl.BlockSpec((1,) + shape, lambda h, j: (h, 0, 0))
    kmap = lambda h, j: (h, j, 0)
    sds = jax.ShapeDtypeStruct
    return pl.pallas_call(
        body, name=name, grid=(HEADS, n),
        in_specs=[head(S, 128), head(S, VD), head(1, S), head(1, S), pl.BlockSpec((1, t, 128), kmap), pl.BlockSpec((1, t, VD), kmap)],
        out_specs=[head(S, 128), pl.BlockSpec((1, t, 128), kmap), pl.BlockSpec((1, t, VD), kmap)],
        out_shape=[sds((HEADS, S, 128), F32), sds((HEADS, S, 128), F32), sds((HEADS, S, VD), F32)],
        scratch_shapes=[pltpu.VMEM((t, 128), F32), pltpu.VMEM((t, VD), F32)],
        compiler_params=_params(("parallel", "arbitrary")),
    )(q, do, lse_row, d_row, k, v)


def _shift_down(x, prev8, d):
    xr = pltpu.roll(x, d, 0)
    r8 = lax.broadcasted_iota(jnp.int32, prev8.shape, 0)
    top = jnp.where(r8 < d, pltpu.roll(prev8, d, 0), xr[0:8])
    return jnp.concatenate([top, xr[8:]], axis=0)


def _shift_up(x, next8, d):
    n = x.shape[0]
    xr = pltpu.roll(x, n - d, 0)
    r8 = lax.broadcasted_iota(jnp.int32, next8.shape, 0)
    bot = jnp.where(r8 >= 8 - d, pltpu.roll(next8, 8 - d, 0), xr[n - 8:])
    return jnp.concatenate([xr[:n - 8], bot], axis=0)


def _log1p(u):
    return jnp.where(u < 0.01, u * (1.0 - u * (0.5 - u * (1.0 / 3.0 - 0.25 * u))), jnp.log(1.0 + u))


def _neg_expm1(y):
    series = -y * (1.0 + 0.5 * y * (1.0 + (1.0 / 3.0) * y * (1.0 + 0.25 * y)))
    return jnp.where(y > -0.05, series, 1.0 - jnp.exp(y))


def _softplus_neg(lam):
    return jnp.maximum(-lam, 0.0) + _log1p(jnp.exp(-jnp.abs(lam)))


def _lru_gates(x, prev8, cw, cb, wr, br, wi, bi, lam):
    xs1, xs2, xs3 = _shift_down(x, prev8, 1), _shift_down(x, prev8, 2), _shift_down(x, prev8, 3)
    xc = cb + cw[0:1] * xs3 + cw[1:2] * xs2 + cw[2:3] * xs1 + cw[3:4] * x
    xcb = xc.astype(MXU)
    r = _sigmoid(_dot(xcb, wr) + br)
    ig = _sigmoid(_dot(xcb, wi) + bi)
    sp = _softplus_neg(lam)
    log_a = -8.0 * r * sp
    a = jnp.exp(log_a)
    gb = jnp.sqrt(_neg_expm1(2.0 * log_a))
    return (xs1, xs2, xs3), xc, xcb, r, ig, sp, a, gb


def _lru_fwd(z, cw, cb, wr, br, wi, bi, lam, name):
    S = z.shape[0]
    t = min(ROW_TILE, S)

    def body(zx_ref, zg_ref, cw_ref, cb_ref, wr_ref, br_ref, wi_ref, bi_ref, lam_ref, y_ref, h_ref, xp_s, hc_s):
        @pl.when(pl.program_id(0) == 0)
        def _():
            xp_s[...] = jnp.zeros(xp_s.shape, F32)
            hc_s[...] = jnp.zeros(hc_s.shape, F32)

        x = zx_ref[...]
        _, xc, _, _, ig, _, a, gb = _lru_gates(x, xp_s[...], cw_ref[...], cb_ref[...], wr_ref[...], br_ref[...],
                                               wi_ref[...], bi_ref[...], lam_ref[...])
        b = gb * (ig * xc)
        rows = lax.broadcasted_iota(jnp.int32, a.shape, 0)
        d = 1
        while d < t:
            ar, brr = pltpu.roll(a, d, 0), pltpu.roll(b, d, 0)
            ok = rows >= d
            b = jnp.where(ok, a * brr, 0.0) + b
            a = jnp.where(ok, a * ar, a)
            d *= 2
        h = a * hc_s[7:8, :] + b
        h_ref[...] = h
        y_ref[...] = (h * _gelu(zg_ref[...])).astype(y_ref.dtype)
        hc_s[...] = h[t - 8:, :]
        xp_s[...] = x[t - 8:, :]

    w = LRU_W
    return pl.pallas_call(
        body, name=name, grid=(S // t,),
        in_specs=[_row_spec(t, w, P_LX // w), _row_spec(t, w, P_LG // w), _full_spec((4, w)), _vec_spec(w), _full_spec((w, w)),
                  _vec_spec(w), _full_spec((w, w)), _vec_spec(w), _vec_spec(w)],
        out_specs=[_row_spec(t, w), _row_spec(t, w)],
        out_shape=[jax.ShapeDtypeStruct((S, w), MXU), jax.ShapeDtypeStruct((S, w), F32)],
        scratch_shapes=[pltpu.VMEM((8, w), F32), pltpu.VMEM((8, w), F32)],
        compiler_params=_params(("arbitrary",)),
    )(z, z, cw, cb, wr, br, wi, bi, lam)


def _lru_bwd(z, h, dy, cw, cb, wr, br, wi, bi, lam, dz_in, name):
    S = z.shape[0]
    t = min(ROW_TILE, S)
    n = S // t
    w = LRU_W

    def body(zx_ref, zxp_ref, zg_ref, h_ref, hp_ref, dy_ref, cw_ref, cb_ref, wr_ref, br_ref, wi_ref, bi_ref, lam_ref, _dz_in,
             dz_ref, dcw_ref, dcb_ref, dwr_ref, dbr_ref, dwi_ref, dbi_ref, dlam_ref, sum_ref, gc_s, dn_s):
        i = pl.program_id(0)
        first = i == 0
        j = n - 1 - i

        @pl.when(first)
        def _():
            gc_s[...] = jnp.zeros(gc_s.shape, F32)
            dn_s[...] = jnp.zeros(dn_s.shape, F32)

        live = (j > 0).astype(F32)
        xprev8, hprev8 = zxp_ref[...] * live, hp_ref[...] * live
        x, zg, hh, dy_ = zx_ref[...], zg_ref[...], h_ref[...], dy_ref[...]
        cw_, lam_ = cw_ref[...], lam_ref[...]
        (xs1, xs2, xs3), xc, xcb, r, ig, sp, a, gb = _lru_gates(x, xprev8, cw_, cb_ref[...], wr_ref[...], br_ref[...],
                                                                wi_ref[...], bi_ref[...], lam_)
        hm1 = _shift_down(hh, hprev8, 1)
        dh = dy_ * _gelu(zg)
        dzg = dy_ * hh * _gelu_grad(zg)
        rows = lax.broadcasted_iota(jnp.int32, a.shape, 0)
        last = rows == t - 1
        ca = jnp.where(last, 0.0, pltpu.roll(a, t - 1, 0))
        g = dh + jnp.where(last, gc_s[0:1, :], 0.0)
        d = 1
        while d < t:
            ok = rows < t - d
            g = g + jnp.where(ok, ca * pltpu.roll(g, t - d, 0), 0.0)
            ca = jnp.where(ok, ca * pltpu.roll(ca, t - d, 0), 0.0)
            d *= 2
        gc_s[...] = a[0:8, :] * g[0:8, :]
        da = g * hm1
        dgb = g * (ig * xc)
        dub = g * gb
        di = dub * xc
        dxc = dub * ig
        dlog_a = da * a - dgb * (a * a) / gb
        dr = dlog_a * (-8.0 * sp)
        _acc(dlam_ref, first, _colsum(dlog_a * (-8.0 * r)) * (-_sigmoid(-lam_)))
        dpr = dr * r * (1.0 - r)
        dpi = di * ig * (1.0 - ig)
        _acc(dbr_ref, first, _colsum(dpr))
        _acc(dbi_ref, first, _colsum(dpi))
        dprb, dpib = dpr.astype(MXU), dpi.astype(MXU)
        _acc(dwr_ref, first, _dot_tn(xcb, dprb))
        _acc(dwi_ref, first, _dot_tn(xcb, dpib))
        dxc = dxc + _dot_nt(dprb, wr_ref[...]) + _dot_nt(dpib, wi_ref[...])
        _acc(dcb_ref, first, _colsum(dxc))
        _acc(dcw_ref, first, jnp.concatenate([_colsum(dxc * xs3), _colsum(dxc * xs2), _colsum(dxc * xs1), _colsum(dxc * x)], axis=0))
        nxt = dn_s[...]
        dx = cw_[3:4] * dxc + cw_[2:3] * _shift_up(dxc, nxt, 1) + cw_[1:2] * _shift_up(dxc, nxt, 2) + cw_[0:1] * _shift_up(dxc, nxt, 3)
        dn_s[...] = dxc[0:8, :]
        dz_ref[...] = jnp.concatenate([dx, dzg], axis=1).astype(dz_ref.dtype)
        _acc(sum_ref, first, jnp.concatenate([_colsum(dx), _colsum(dzg)], axis=1))

    rev = lambda col: pl.BlockSpec((t, w), lambda i, c=col: (n - 1 - i, c))
    prev8 = lambda col: pl.BlockSpec((8, w), lambda i, c=col: (jnp.maximum((n - 1 - i) * (t // 8) - 1, 0), c))
    vec = jax.ShapeDtypeStruct((1, w), F32)
    sds = jax.ShapeDtypeStruct
    return pl.pallas_call(
        body, name=name, grid=(n,),
        in_specs=[rev(P_LX // w), prev8(P_LX // w), rev(P_LG // w), rev(0), prev8(0), rev(0), _full_spec((4, w)), _vec_spec(w),
                  _full_spec((w, w)), _vec_spec(w), _full_spec((w, w)), _vec_spec(w), _vec_spec(w), _ANY],
        out_specs=[pl.BlockSpec((t, 2 * w), lambda i: (n - 1 - i, P_LX // (2 * w))), _full_spec((4, w)), _vec_spec(w),
                   _full_spec((w, w)), _vec_spec(w), _full_spec((w, w)), _vec_spec(w), _vec_spec(w), _vec_spec(2 * w)],
        out_shape=[sds((S, NP), MXU), sds((4, w), F32), vec, sds((w, w), F32), vec, sds((w, w), F32), vec, vec, sds((1, 2 * w), F32)],
        scratch_shapes=[pltpu.VMEM((8, w), F32), pltpu.VMEM((8, w), F32)],
        input_output_aliases=_dz_alias(14), compiler_params=_params(("arbitrary",)),
    )(z, z, z, h, h, dy, cw, cb, wr, br, wi, bi, lam, dz_in)


def _branch_fwd(z, ya, o, yc, wa, wb, wc, name):
    S = z.shape[0]
    t = min(ROW_TILE, S)

    def body(ga_ref, gb_ref, gc_ref, ya_ref, o_ref, yc_ref, wa_ref, wb_ref, wc_ref, m_ref, pa_ref, pb_ref, pc_ref):
        pa = _dot(ya_ref[...], wa_ref[...])
        pc = _dot(yc_ref[...], wc_ref[...])
        pb = _dot(o_ref[0], wb_ref[0])
        for h in range(1, HEADS):
            pb = pb + _dot(o_ref[h], wb_ref[h])
        pa_ref[...], pb_ref[...], pc_ref[...] = pa, pb, pc
        m = _sigmoid(ga_ref[...]) * pa + _sigmoid(gb_ref[...]) * pb + _sigmoid(gc_ref[...]) * pc
        m_ref[...] = m.astype(m_ref.dtype)

    g0 = P_GATE // D
    sds = jax.ShapeDtypeStruct
    return pl.pallas_call(
        body, name=name, grid=(S // t,),
        in_specs=[_row_spec(t, D, g0), _row_spec(t, D, g0 + 1), _row_spec(t, D, g0 + 2), _row_spec(t, GM_W),
                  pl.BlockSpec((HEADS, t, VD), lambda i: (0, i, 0)), _row_spec(t, LRU_W),
                  _full_spec((GM_W, D)), _full_spec((HEADS, VD, D)), _full_spec((LRU_W, D))],
        out_specs=[_row_spec(t, D)] * 4,
        out_shape=[sds((S, D), MXU), sds((S, D), F32), sds((S, D), F32), sds((S, D), F32)],
        compiler_params=_params(("parallel",)),
    )(z, z, z, ya, o, yc, wa, wb, wc)


def _dz_alias(n_inputs):
    return {n_inputs - 1: 0}


def _branch_bwd(z, dm, pa, pb, pc, name):
    S = z.shape[0]
    t = min(ROW_TILE, S)

    def body(ga_ref, gb_ref, gc_ref, dm_ref, pa_ref, pb_ref, pc_ref, dz_ref, da_ref, db_ref, dc_ref, sum_ref):
        first = pl.program_id(0) == 0
        dm_ = dm_ref[...]
        for n_, (g_ref, p_ref, d_ref) in enumerate(((ga_ref, pa_ref, da_ref), (gb_ref, pb_ref, db_ref), (gc_ref, pc_ref, dc_ref))):
            gt = _sigmoid(g_ref[...])
            d_ref[...] = (dm_ * gt).astype(d_ref.dtype)
            dzg = dm_ * p_ref[...] * gt * (1.0 - gt)
            dz_ref[:, n_ * D:(n_ + 1) * D] = dzg.astype(dz_ref.dtype)
            _acc(sum_ref.at[:, n_ * D:(n_ + 1) * D], first, _colsum(dzg))

    g0 = P_GATE // D
    sds = jax.ShapeDtypeStruct
    return pl.pallas_call(
        body, name=name, grid=(S // t,),
        in_specs=[_row_spec(t, D, g0), _row_spec(t, D, g0 + 1), _row_spec(t, D, g0 + 2)] + [_row_spec(t, D)] * 4,
        out_specs=[_row_spec(t, 3 * D, P_GATE // (3 * D))] + [_row_spec(t, D)] * 3 + [_vec_spec(3 * D)],
        out_shape=[sds((S, NP), MXU)] + [sds((S, D), MXU)] * 3 + [sds((1, 3 * D), F32)],
        compiler_params=_params(("arbitrary",)),
    )(z, z, z, dm, pa, pb, pc)


def _heads_bwd(dpb, o, wb, name):
    S = dpb.shape[0]
    t = min(ROW_TILE, S)

    def body(dp_ref, o_ref, wb_ref, dob_ref, do_ref, dwb_ref):
        first = pl.program_id(0) == 0
        dp = dp_ref[...]
        for h in range(HEADS):
            do = _dot_nt(dp, wb_ref[h])
            do_ref[h] = do
            dob_ref[h] = do.astype(dob_ref.dtype)
            _acc(dwb_ref.at[h], first, _dot_tn(o_ref[h], dp))

    hspec = pl.BlockSpec((HEADS, t, VD), lambda i: (0, i, 0))
    sds = jax.ShapeDtypeStruct
    return pl.pallas_call(
        body, name=name, grid=(S // t,),
        in_specs=[_row_spec(t, D), hspec, _full_spec((HEADS, VD, D))],
        out_specs=[hspec, hspec, _full_spec((HEADS, VD, D))],
        out_shape=[sds((HEADS, S, VD), MXU), sds((HEADS, S, VD), F32), sds((HEADS, VD, D), F32)],
        compiler_params=_params(("arbitrary",)),
    )(dpb, o, wb)


def _mod_fwd(c_all, ada_w, name):
    n = ada_w.shape[2]

    def body(c_ref, w_ref, o_ref):
        c = c_ref[...]
        ca = (c * _sigmoid(c)).astype(MXU)
        for l in range(DEPTH):
            o_ref[l] = _dot(ca, w_ref[l].astype(MXU))

    return pl.pallas_call(body, name=name, out_shape=jax.ShapeDtypeStruct((DEPTH, 8, n), F32), compiler_params=_params())(c_all, ada_w)


def _ada_w_grad(c_all_t, dmod, name):
    n = dmod.shape[2]

    def body(c_ref, d_ref, o_ref):
        c = c_ref[...]
        ca = c * _sigmoid(c)
        for l in range(DEPTH):
            dm = d_ref[l]
            acc = ca[:, 0:1] * dm[0:1, :]
            for b in range(1, 8):
                acc = acc + ca[:, b:b + 1] * dm[b:b + 1, :]
            o_ref[l] = acc

    return pl.pallas_call(body, name=name, out_shape=jax.ShapeDtypeStruct((DEPTH, D, n), F32), compiler_params=_params())(c_all_t, dmod)


def _adamw(w, g, m, v, name):
    R, C = w.shape
    t = _tile(R, 256) if R % 8 == 0 else R
    c1, c2 = 1.0 - ADAM_B1 ** ADAM_STEP, 1.0 - ADAM_B2 ** ADAM_STEP

    def body(w_ref, g_ref, m_ref, v_ref, d_ref, nm_ref, nv_ref):
        g_ = g_ref[...]
        m_ = ADAM_B1 * m_ref[...] + (1.0 - ADAM_B1) * g_
        v_ = ADAM_B2 * v_ref[...] + (1.0 - ADAM_B2) * (g_ * g_)
        nm_ref[...] = m_
        nv_ref[...] = v_
        d_ref[...] = -ADAM_LR * ((m_ / c1) / (jnp.sqrt(v_ / c2) + ADAM_EPS) + ADAM_WD * w_ref[...])

    spec = pl.BlockSpec((t, C), lambda i: (i, 0))
    return pl.pallas_call(
        body, name=name, grid=(R // t,), in_specs=[spec] * 4, out_specs=[spec] * 3,
        out_shape=[jax.ShapeDtypeStruct((R, C), F32)] * 3, compiler_params=_params(("parallel",)),
    )(w, g, m, v)


def _sum8(a, name):
    _, R, C = a.shape
    t = _tile(R, 512) if R % 8 == 0 else R

    def body(a_ref, o_ref):
        s = a_ref[0]
        for k in range(1, 8):
            s = s + a_ref[k]
        o_ref[...] = s

    return pl.pallas_call(
        body, name=name, grid=(R // t,), in_specs=[pl.BlockSpec((8, t, C), lambda i: (0, i, 0))],
        out_specs=pl.BlockSpec((t, C), lambda i: (i, 0)), out_shape=jax.ShapeDtypeStruct((R, C), F32),
        compiler_params=_params(("parallel",)),
    )(a)


def _pair_add(a, b, name):
    shape = a.shape
    a2, b2 = a.reshape(-1, shape[-1]), b.reshape(-1, shape[-1])
    R, C = a2.shape
    t = _tile(R, 512, 16)

    def body(a_ref, b_ref, o_ref, ob_ref):
        s = a_ref[...] + b_ref[...]
        o_ref[...] = s
        ob_ref[...] = s.astype(ob_ref.dtype)

    spec = pl.BlockSpec((t, C), lambda i: (i, 0))
    o, ob = pl.pallas_call(
        body, name=name, grid=(R // t,), in_specs=[spec, spec], out_specs=[spec, spec],
        out_shape=[jax.ShapeDtypeStruct((R, C), F32), jax.ShapeDtypeStruct((R, C), jnp.bfloat16)],
        compiler_params=_params(("parallel",)),
    )(a2, b2)
    return o.reshape(shape), ob.reshape(shape)


def _chip_sum(own, recv, name):
    R, C = own.shape
    t = _tile(R, 512, 16)

    def body(a_ref, r_ref, o_ref):
        s = a_ref[...]
        for k in range(3):
            s = s + r_ref[k].astype(F32)
        o_ref[...] = s

    return pl.pallas_call(
        body, name=name, grid=(R // t,),
        in_specs=[pl.BlockSpec((t, C), lambda i: (i, 0)), pl.BlockSpec((3, t, C), lambda i: (0, i, 0))],
        out_specs=pl.BlockSpec((t, C), lambda i: (i, 0)), out_shape=jax.ShapeDtypeStruct((R, C), F32),
        compiler_params=_params(("parallel",)),
    )(own, recv)


_ANY = pl.BlockSpec(memory_space=pl.ANY)


def _coords():
    return lax.axis_index("x"), lax.axis_index("y"), lax.axis_index("c")


def _allgather8(v, name):
    R, C = v.shape

    def body(v_ref, out_ref, send_sems, recv_sems):
        x, y, c = _coords()
        me = 4 * x + 2 * y + c
        sends = []
        for k in range(1, 8):
            bx, by, bc = (k >> 2) & 1, (k >> 1) & 1, k & 1
            peer = (x ^ bx, y ^ by, c ^ bc)
            cp = pltpu.make_async_remote_copy(src_ref=v_ref, dst_ref=out_ref.at[me], send_sem=send_sems.at[k - 1],
                                              recv_sem=recv_sems.at[k - 1], device_id=peer, device_id_type=MESH)
            cp.start()
            sends.append(cp)
        for k in range(1, 8):
            bx, by, bc = (k >> 2) & 1, (k >> 1) & 1, k & 1
            peer = (x ^ bx, y ^ by, c ^ bc)
            src = 4 * peer[0] + 2 * peer[1] + peer[2]
            pltpu.make_async_remote_copy(src_ref=v_ref, dst_ref=out_ref.at[src], send_sem=send_sems.at[k - 1],
                                         recv_sem=recv_sems.at[k - 1], device_id=peer, device_id_type=MESH).wait_recv()
        for cp in sends:
            cp.wait_send()

    out = pl.pallas_call(
        body, name=name, in_specs=[_ANY], out_specs=_ANY, out_shape=jax.ShapeDtypeStruct((8, R, C), v.dtype),
        scratch_shapes=[pltpu.SemaphoreType.DMA((7,)), pltpu.SemaphoreType.DMA((7,))],
    )(v)
    x, y, c = _coords()
    return lax.dynamic_update_slice(out, v[None], (4 * x + 2 * y + c, 0, 0))


def _chip_patterns(x, y):
    return [(1 - x, y), (x, 1 - y), (1 - x, 1 - y)]


def _allgather_weights(w, name):
    R, C = w.shape
    rh = R // 2

    def body(w_ref, out_ref, send_sems, recv_sems):
        x, y, c = _coords()
        kme = 2 * x + y
        sibling = (x, y, 1 - c)
        mine_half = pl.ds(pl.multiple_of(c * rh, FLAT_ALIGN // 2), rh)
        other_half = pl.ds(pl.multiple_of((1 - c) * rh, FLAT_ALIGN // 2), rh)
        chips = _chip_patterns(x, y)

        def copy(k, chip_idx, half, to, src=None):
            dst = out_ref.at[chip_idx, half, :]
            return pltpu.make_async_remote_copy(src_ref=dst if src is None else src, dst_ref=dst, send_sem=send_sems.at[k],
                                                recv_sem=recv_sems.at[k], device_id=to, device_id_type=MESH)

        first = [copy(j, kme, mine_half, (cx, cy, c), src=w_ref.at[mine_half, :]) for j, (cx, cy) in enumerate(chips)]
        own = pltpu.make_async_remote_copy(src_ref=w_ref, dst_ref=out_ref.at[kme], send_sem=send_sems.at[6], recv_sem=recv_sems.at[6],
                                           device_id=sibling, device_id_type=MESH)
        first.append(own)
        for cp in first:
            cp.start()
        passed = []
        for j, (cx, cy) in enumerate(chips):
            kj = 2 * cx + cy
            copy(j, kj, mine_half, (x, y, c)).wait_recv()
            fw = copy(3 + j, kj, mine_half, sibling)
            fw.start()
            passed.append(fw)
        for j, (cx, cy) in enumerate(chips):
            copy(3 + j, 2 * cx + cy, other_half, (x, y, c)).wait_recv()
        own.wait_recv()
        for cp in first + passed:
            cp.wait_send()

    return pl.pallas_call(
        body, name=name, in_specs=[_ANY], out_specs=_ANY, out_shape=jax.ShapeDtypeStruct((4, R, C), w.dtype),
        scratch_shapes=[pltpu.SemaphoreType.DMA((7,)), pltpu.SemaphoreType.DMA((7,))],
    )(w)


def _sibling_swap(g, name):
    _, R, C = g.shape
    rh = R // 2

    def body(g_ref, out_ref, send_sem, recv_sem):
        x, y, c = _coords()
        other_half = pl.ds(pl.multiple_of((1 - c) * rh, 8), rh)
        cp = pltpu.make_async_remote_copy(src_ref=g_ref.at[:, other_half, :], dst_ref=out_ref, send_sem=send_sem,
                                          recv_sem=recv_sem, device_id=(x, y, 1 - c), device_id_type=MESH)
        cp.start()
        cp.wait()

    return pl.pallas_call(
        body, name=name, in_specs=[_ANY], out_specs=_ANY, out_shape=jax.ShapeDtypeStruct((4, rh, C), g.dtype),
        scratch_shapes=[pltpu.SemaphoreType.DMA(()), pltpu.SemaphoreType.DMA(())],
    )(g)


def _chip_scatter(a, name):
    _, R, C = a.shape

    def body(a_ref, out_ref, send_sems, recv_sems):
        x, y, c = _coords()
        cps = []
        for j, (cx, cy) in enumerate(_chip_patterns(x, y)):
            cp = pltpu.make_async_remote_copy(src_ref=a_ref.at[2 * cx + cy], dst_ref=out_ref.at[j], send_sem=send_sems.at[j],
                                              recv_sem=recv_sems.at[j], device_id=(cx, cy, c), device_id_type=MESH)
            cp.start()
            cps.append(cp)
        for cp in cps:
            cp.wait()

    return pl.pallas_call(
        body, name=name, in_specs=[_ANY], out_specs=_ANY, out_shape=jax.ShapeDtypeStruct((3, R, C), a.dtype),
        scratch_shapes=[pltpu.SemaphoreType.DMA((3,)), pltpu.SemaphoreType.DMA((3,))],
    )(a)


def _sibling_join(f, name):
    rh, C = f.shape

    def body(f_ref, out_ref, send_sem, recv_sem):
        x, y, c = _coords()
        mine_half = pl.ds(pl.multiple_of(c * rh, 8), rh)
        cp = pltpu.make_async_remote_copy(src_ref=f_ref, dst_ref=out_ref.at[mine_half, :], send_sem=send_sem, recv_sem=recv_sem,
                                          device_id=(x, y, 1 - c), device_id_type=MESH)
        cp.start()
        cp.wait()

    out = pl.pallas_call(
        body, name=name, in_specs=[_ANY], out_specs=_ANY, out_shape=jax.ShapeDtypeStruct((2 * rh, C), f.dtype),
        scratch_shapes=[pltpu.SemaphoreType.DMA(()), pltpu.SemaphoreType.DMA(())],
    )(f)
    return lax.dynamic_update_slice(out, f, (lax.axis_index("c") * rh, 0))


_BIG = (("in_w", 2), ("mla_wuq", 2), ("mla_wukv", 2), ("branch_w", 3), ("mix_out_w", 1), ("ffn_w1", 2), ("ffn_w2", 1))


def _pack(shards):
    flat = jnp.concatenate([shards[n].reshape(-1, FLAT_W) for n, _ in _BIG], axis=0)
    pad = (-flat.shape[0]) % FLAT_ALIGN
    return jnp.pad(flat, ((0, pad), (0, 0)))


def _unpack(flat, shapes):
    out, r = {}, 0
    for n, _ in _BIG:
        rows = math.prod(shapes[n]) // FLAT_W
        out[n] = flat[r:r + rows].reshape(shapes[n])
        r += rows
    return out


def _swap16(a):
    return jnp.concatenate([a[..., 16:32], a[..., 0:16]], axis=-1)


def _pad_in_cols(w):
    lead = w.shape[:-1]
    z = lambda n: jnp.zeros(lead + (n,), w.dtype)
    kr = w[..., O_KR:O_LX]
    return jnp.concatenate([w[..., O_GATE:], w[..., O_GM:O_Q], w[..., O_LX:O_LG], w[..., O_LG:O_GATE], w[..., O_Q:O_KV],
                            w[..., O_KV:O_KR], z(64), kr, z(32), z(64), _swap16(kr), z(32)], axis=-1)


def _unpad_in_cols(g):
    kr = g[..., P_KA + 64:P_KA + 96] + _swap16(g[..., P_KB + 64:P_KB + 96])
    return jnp.concatenate([g[..., P_GM:P_LX], g[..., P_Q:P_KV], g[..., P_KV:P_KA], kr, g[..., P_LX:P_LG], g[..., P_LG:P_Q],
                            g[..., P_GATE:P_GM]], axis=-1)


def _flat128(vecs):
    flat = jnp.concatenate([v.reshape(-1) for v in vecs])
    pad = (-flat.shape[0]) % 1024
    return jnp.pad(flat, (0, pad)).reshape(-1, 128)


def _unflat(flat, shapes):
    flat = flat.reshape(-1)
    out, r = [], 0
    for s in shapes:
        n = math.prod(s)
        out.append(flat[r:r + n].reshape(s))
        r += n
    return out


_SMALL = ("ada_b", "in_b", "gm_ln_g", "gm_ln_b", "gm_ws", "gm_bs", "mla_qnorm_g", "mla_kvnorm_g", "lru_conv_w", "lru_conv_b",
          "lru_wr", "lru_br", "lru_wi", "lru_bi", "lru_lambda", "ffn_b1", "ffn_b2", "ln_g", "ln_b")
_NAMES = ("ada_w", "ada_b", "in_w", "in_b", "gm_ln_g", "gm_ln_b", "gm_ws", "gm_bs", "mla_qnorm_g", "mla_wuq", "mla_kvnorm_g",
          "mla_wukv", "lru_conv_w", "lru_conv_b", "lru_wr", "lru_br", "lru_wi", "lru_bi", "lru_lambda", "branch_w", "mix_out_w",
          "ffn_w1", "ffn_b1", "ffn_w2", "ffn_b2", "ln_g", "ln_b")


def kernel(x, c, ada_w, ada_b, in_w, in_b, gm_ln_g, gm_ln_b, gm_ws, gm_bs, mla_qnorm_g, mla_wuq, mla_kvnorm_g, mla_wukv, lru_conv_w, lru_conv_b, lru_wr, lru_br, lru_wi, lru_bi, lru_lambda, branch_w, mix_out_w, ffn_w1, ffn_b1, ffn_w2, ffn_b2, ln_g, ln_b, loss_target, m_ada_w, m_ada_b, m_in_w, m_in_b, m_gm_ln_g, m_gm_ln_b, m_gm_ws, m_gm_bs, m_mla_qnorm_g, m_mla_wuq, m_mla_kvnorm_g, m_mla_wukv, m_lru_conv_w, m_lru_conv_b, m_lru_wr, m_lru_br, m_lru_wi, m_lru_bi, m_lru_lambda, m_branch_w, m_mix_out_w, m_ffn_w1, m_ffn_b1, m_ffn_w2, m_ffn_b2, m_ln_g, m_ln_b, v_ada_w, v_ada_b, v_in_w, v_in_b, v_gm_ln_g, v_gm_ln_b, v_gm_ws, v_gm_bs, v_mla_qnorm_g, v_mla_wuq, v_mla_kvnorm_g, v_mla_wukv, v_lru_conv_w, v_lru_conv_b, v_lru_wr, v_lru_br, v_lru_wi, v_lru_bi, v_lru_lambda, v_branch_w, v_mix_out_w, v_ffn_w1, v_ffn_b1, v_ffn_w2, v_ffn_b2, v_ln_g, v_ln_b):
    W = dict(ada_w=ada_w, ada_b=ada_b, in_w=in_w, in_b=in_b, gm_ln_g=gm_ln_g, gm_ln_b=gm_ln_b, gm_ws=gm_ws, gm_bs=gm_bs,
             mla_qnorm_g=mla_qnorm_g, mla_wuq=mla_wuq, mla_kvnorm_g=mla_kvnorm_g, mla_wukv=mla_wukv, lru_conv_w=lru_conv_w,
             lru_conv_b=lru_conv_b, lru_wr=lru_wr, lru_br=lru_br, lru_wi=lru_wi, lru_bi=lru_bi, lru_lambda=lru_lambda,
             branch_w=branch_w, mix_out_w=mix_out_w, ffn_w1=ffn_w1, ffn_b1=ffn_b1, ffn_w2=ffn_w2, ffn_b2=ffn_b2, ln_g=ln_g, ln_b=ln_b)
    M = dict(ada_w=m_ada_w, ada_b=m_ada_b, in_w=m_in_w, in_b=m_in_b, gm_ln_g=m_gm_ln_g, gm_ln_b=m_gm_ln_b, gm_ws=m_gm_ws,
             gm_bs=m_gm_bs, mla_qnorm_g=m_mla_qnorm_g, mla_wuq=m_mla_wuq, mla_kvnorm_g=m_mla_kvnorm_g, mla_wukv=m_mla_wukv,
             lru_conv_w=m_lru_conv_w, lru_conv_b=m_lru_conv_b, lru_wr=m_lru_wr, lru_br=m_lru_br, lru_wi=m_lru_wi, lru_bi=m_lru_bi,
             lru_lambda=m_lru_lambda, branch_w=m_branch_w, mix_out_w=m_mix_out_w, ffn_w1=m_ffn_w1, ffn_b1=m_ffn_b1, ffn_w2=m_ffn_w2,
             ffn_b2=m_ffn_b2, ln_g=m_ln_g, ln_b=m_ln_b)
    V = dict(ada_w=v_ada_w, ada_b=v_ada_b, in_w=v_in_w, in_b=v_in_b, gm_ln_g=v_gm_ln_g, gm_ln_b=v_gm_ln_b, gm_ws=v_gm_ws,
             gm_bs=v_gm_bs, mla_qnorm_g=v_mla_qnorm_g, mla_wuq=v_mla_wuq, mla_kvnorm_g=v_mla_kvnorm_g, mla_wukv=v_mla_wukv,
             lru_conv_w=v_lru_conv_w, lru_conv_b=v_lru_conv_b, lru_wr=v_lru_wr, lru_br=v_lru_br, lru_wi=v_lru_wi, lru_bi=v_lru_bi,
             lru_lambda=v_lru_lambda, branch_w=v_branch_w, mix_out_w=v_mix_out_w, ffn_w1=v_ffn_w1, ffn_b1=v_ffn_b1, ffn_w2=v_ffn_w2,
             ffn_b2=v_ffn_b2, ln_g=v_ln_g, ln_b=v_ln_b)

    S = x.shape[1]
    xi, yi, ci = _coords()
    kme = 2 * xi + yi
    me = 4 * xi + 2 * yi + ci
    x0 = x[0]
    tgt = loss_target[0]

    shard_shapes = {n: W[n].shape for n, _ in _BIG}
    gathered = _allgather_weights(_pack({n: W[n].astype(MXU) for n, _ in _BIG}), "ag_weights")
    parts = [_unpack(gathered[k], shard_shapes) for k in range(4)]
    full = {n: jnp.concatenate([p[n] for p in parts], axis=ax) for n, ax in _BIG}

    small1 = _allgather8(_flat128([c, ln_g, ln_b, lru_conv_w]), "ag_small")
    per_dev = [_unflat(small1[d], [c.shape, ln_g.shape, ln_b.shape, lru_conv_w.shape]) for d in range(8)]
    c_all = jnp.concatenate([p[0] for p in per_dev], axis=0)
    chip = lambda i: [per_dev[2 * k][i] for k in range(4)]
    ln_g_f, ln_b_f, conv_w_f = (jnp.concatenate(chip(1), axis=2), jnp.concatenate(chip(2), axis=2), jnp.concatenate(chip(3), axis=2))

    mod_sh = _mod_fwd(c_all, ada_w, "mod_fwd")
    mod_all = _allgather8(_flat128([mod_sh]), "ag_mod")
    mod_parts = [_unflat(mod_all[2 * k], [mod_sh.shape])[0] for k in range(4)]
    mod = jnp.concatenate(mod_parts, axis=2)
    mod = lax.dynamic_index_in_dim(mod, me, axis=1, keepdims=False) + ada_b

    pos = jnp.arange(S, dtype=F32)
    inv = ROPE_BASE ** (-jnp.arange(0, ROPE, 2, dtype=F32) / ROPE)
    ang = pos[:, None] * inv[None, :]
    cs, sn = jnp.cos(ang), jnp.sin(ang)
    zc = lambda n: jnp.zeros((S, n), F32)
    cos_t = jnp.concatenate([zc(64), cs, cs, zc(32)], axis=1)
    sin_t = jnp.concatenate([zc(64), -sn, sn, zc(32)], axis=1)

    win_p = _pad_in_cols(full["in_w"])
    bin_p = _pad_in_cols(in_b)
    wuq = full["mla_wuq"].reshape(DEPTH, Q_RANK, HEADS, NOPE + ROPE)
    zq = lambda n: jnp.zeros((DEPTH, Q_RANK, HEADS, n), MXU)
    w1q = jnp.concatenate([wuq, zq(32)], axis=-1).reshape(DEPTH, Q_RANK, D)
    w2q = jnp.concatenate([zq(64), _swap16(wuq[..., NOPE:]), zq(32)], axis=-1).reshape(DEPTH, Q_RANK, D)
    wukv = full["mla_wukv"].reshape(DEPTH, KV_RANK, HEADS, NOPE + VD)
    wk_p = jnp.concatenate([wukv[..., :NOPE], jnp.zeros((DEPTH, KV_RANK, HEADS, 64), MXU)], axis=-1).reshape(DEPTH, KV_RANK, D)
    wv_h = wukv[..., NOPE:].transpose(0, 2, 1, 3)
    wvt_h = wukv[..., NOPE:].transpose(0, 2, 3, 1)
    wb_h = full["branch_w"][:, 1].reshape(DEPTH, HEADS, VD, D)
    wr_f = jnp.stack([block_diag(*[lru_wr[l, b] for b in range(LRU_NB)]) for l in range(DEPTH)]).astype(MXU)
    wi_f = jnp.stack([block_diag(*[lru_wi[l, b] for b in range(LRU_NB)]) for l in range(DEPTH)]).astype(MXU)
    ws_b = gm_ws.astype(MXU)
    wst_b = gm_ws.transpose(0, 1, 3, 2).astype(MXU)
    bs_t = gm_bs.transpose(0, 2, 1)
    row = lambda v: v.reshape(1, -1)

    saved = []
    xs = x0
    h1 = None
    for l in range(DEPTH):
        sh1, sc1, g1, sh2, sc2, g2 = [row(mod[l, i * D:(i + 1) * D]) for i in range(6)]
        if l == 0:
            h1 = _lnmod_fwd(xs, sc1, sh1, f"lnmod_fwd{l}")
        z = _mm(h1, win_p[l], "nn", f"in_proj{l}", rows=[row(bin_p[l])], epi=lambda acc, b: (acc + b,))
        ya = _gmlp_fwd(z, row(gm_ln_g[l]), row(gm_ln_b[l]), ws_b[l], bs_t[l], f"gmlp_fwd{l}")
        qf, kf, vf, vtf = _mla_prep_fwd(z, cos_t, sin_t, row(mla_qnorm_g[l]), row(mla_kvnorm_g[l]), w1q[l], w2q[l], wk_p[l], wv_h[l],
                                        wvt_h[l], f"mla_prep_fwd{l}")
        ob, o32, lse = _attn_fwd(qf, kf, vtf, f"attn_fwd{l}")
        lru_args = (conv_w_f[l], row(lru_conv_b[l]), wr_f[l], row(lru_br[l]), wi_f[l], row(lru_bi[l]), row(lru_lambda[l]))
        yc, hl = _lru_fwd(z, *lru_args, f"lru_fwd{l}")
        merged, pa, pb, pc = _branch_fwd(z, ya, ob, yc, full["branch_w"][l, 0], wb_h[l], full["branch_w"][l, 2], f"branch_fwd{l}")
        mix = _mm(merged, full["mix_out_w"][l], "nn", f"mix_out{l}")
        x1, h2 = _res_ln_fwd(xs, mix, g1, row(ln_g_f[l, 0]), row(ln_b_f[l, 0]), f"res_ln_a{l}", mod=(sc2, sh2))
        a1, r2 = _mm(h2, full["ffn_w1"][l], "nn", f"ffn_up{l}", out_dtypes=(F32, MXU), rows=[row(ffn_b1[l])],
                     epi=lambda acc, b: (acc + b, jnp.square(jnp.maximum(acc + b, 0.0))))
        f = _mm(r2, full["ffn_w2"][l], "nn", f"ffn_down{l}", rows=[row(ffn_b2[l])], epi=lambda acc, b: (acc + b,))
        if l + 1 < DEPTH:
            nsh1, nsc1 = row(mod[l + 1, 0:D]), row(mod[l + 1, D:2 * D])
            x2, h1n = _res_ln_fwd(x1, f, g2, row(ln_g_f[l, 1]), row(ln_b_f[l, 1]), f"res_ln_b{l}", mod=(nsc1, nsh1))
        else:
            (x2,), h1n = _res_ln_fwd(x1, f, g2, row(ln_g_f[l, 1]), row(ln_b_f[l, 1]), f"res_ln_b{l}"), None
        saved.append(dict(x_in=xs, h1=h1, z=z, ya=ya, qf=qf, kf=kf, vf=vf, ob=ob, o32=o32, lse=lse, yc=yc, hl=hl, merged=merged,
                          pa=pa, pb=pb, pc=pc, mix=mix, x1=x1, h2=h2, a1=a1, r2=r2, f=f, x2=x2, lru_args=lru_args,
                          mods=(sh1, sc1, g1, sh2, sc2, g2)))
        xs, h1 = x2, h1n

    dy, sq = _loss_fwd(xs, tgt)
    loss = lax.psum(0.5 / D * jnp.sum(sq), ("x", "y", "c"))

    G = {}
    dmods = []
    dres, hpath = dy, None
    for l in reversed(range(DEPTH)):
        sv = saved[l]
        sh1, sc1, g1, sh2, sc2, g2 = sv["mods"]
        gl = {}
        nb = _node_bwd(f"node_b{l}", dres, hpath=hpath, upath=(sv["x1"], sv["f"], g2, row(ln_g_f[l, 1])))
        if hpath is not None:
            dmods[-1]["sc1"], dmods[-1]["sh1"] = nb["dsc"], nb["dsh"]
        dm = dict(g2=nb["dg"])
        ln_g_l1, ln_b_l1 = nb["dlng"], nb["dlnb"]
        df = nb["dbr"]
        gl["ffn_b2"] = nb["dbrsum"][0]
        gl["ffn_w2"] = _mm(sv["r2"], df, "tn", f"d_ffn_w2_{l}")
        da1, db1 = _mm(df, full["ffn_w2"][l], "nt", f"d_ffn_act{l}", out_dtypes=(MXU,), extras=[sv["a1"]], colsum=True,
                       epi=lambda acc, a: (acc * (2.0 * jnp.maximum(a, 0.0)),))
        gl["ffn_b1"] = db1[0]
        gl["ffn_w1"] = _mm(sv["h2"], da1, "tn", f"d_ffn_w1_{l}")
        dh2 = _mm(da1, full["ffn_w1"][l], "nt", f"d_ffn_in{l}")
        na = _node_bwd(f"node_a{l}", nb["du"], hpath=(dh2, sv["x1"], sc2), upath=(sv["x_in"], sv["mix"], g1, row(ln_g_f[l, 0])))
        dm.update(sc2=na["dsc"], sh2=na["dsh"], g1=na["dg"])
        gl["ln_g"] = jnp.concatenate([na["dlng"], ln_g_l1], axis=0)
        gl["ln_b"] = jnp.concatenate([na["dlnb"], ln_b_l1], axis=0)
        dmix = na["dbr"]
        gl["mix_out_w"] = _mm(sv["merged"], dmix, "tn", f"d_mix_w{l}")
        dmerged = _mm(dmix, full["mix_out_w"][l], "nt", f"d_merged{l}")
        dz, dpa, dpb, dpc, s_gate = _branch_bwd(sv["z"], dmerged, sv["pa"], sv["pb"], sv["pc"], f"branch_bwd{l}")
        wa, wc = full["branch_w"][l, 0], full["branch_w"][l, 2]
        dya = _mm(dpa, wa, "nt", f"d_ya{l}")
        dyc = _mm(dpc, wc, "nt", f"d_yc{l}")
        dwa = _mm(sv["ya"], dpa, "tn", f"d_wa{l}")
        dwc = _mm(sv["yc"], dpc, "tn", f"d_wc{l}")
        dob, do32, dwb = _heads_bwd(dpb, sv["ob"], wb_h[l], f"heads_bwd{l}")
        gl["branch_w"] = jnp.stack([dwa, dwb.reshape(GM_W, D), dwc])
        dz, dcw, dcb, dwr, dbr_, dwi, dbi, dlam, s_lru = _lru_bwd(sv["z"], sv["hl"], dyc, *sv["lru_args"], dz, f"lru_bwd{l}")
        gl["lru_conv_w"], gl["lru_conv_b"], gl["lru_br"], gl["lru_bi"], gl["lru_lambda"] = dcw, dcb[0], dbr_[0], dbi[0], dlam[0]
        blocks = lambda m: jnp.stack([m[b * 64:(b + 1) * 64, b * 64:(b + 1) * 64] for b in range(LRU_NB)])
        gl["lru_wr"], gl["lru_wi"] = blocks(dwr), blocks(dwi)
        delta = _attn_delta(sv["o32"], do32, f"attn_delta{l}")
        dqf, dkf, dvf = _attn_bwd(sv["qf"], sv["kf"], sv["vf"], dob, sv["lse"], delta.reshape(HEADS, 1, S), f"attn_bwd{l}")
        dz, dgq, dgkv, dw1, dw2, dwk, dwv, s_mla = _mla_prep_bwd(sv["z"], cos_t, sin_t, row(mla_qnorm_g[l]), row(mla_kvnorm_g[l]),
                                                                 w1q[l], w2q[l], wk_p[l], wv_h[l], dqf, dkf, dvf, dz,
                                                                 f"mla_prep_bwd{l}")
        gl["mla_qnorm_g"], gl["mla_kvnorm_g"] = dgq[0], dgkv[0]
        dw1 = dw1.reshape(Q_RANK, HEADS, 128)
        dw2 = dw2.reshape(Q_RANK, HEADS, 128)
        gl["mla_wuq"] = jnp.concatenate([dw1[..., :NOPE], dw1[..., NOPE:NOPE + ROPE] + _swap16(dw2[..., NOPE:NOPE + ROPE])],
                                        axis=-1).reshape(Q_RANK, HEADS * (NOPE + ROPE))
        gl["mla_wukv"] = jnp.concatenate([dwk.reshape(KV_RANK, HEADS, 128)[..., :NOPE], dwv.transpose(1, 0, 2)],
                                         axis=-1).reshape(KV_RANK, HEADS * (NOPE + VD))
        dz, dglg, dglb, dws, dbs, s_gm = _gmlp_bwd(sv["z"], dya, row(gm_ln_g[l]), row(gm_ln_b[l]), ws_b[l], wst_b[l], bs_t[l], dz,
                                                   f"gmlp_bwd{l}")
        gl["gm_ln_g"], gl["gm_ln_b"], gl["gm_ws"], gl["gm_bs"] = dglg[0], dglb[0], dws, dbs[:, :GM_G].T
        gl["in_b"] = _unpad_in_cols(jnp.concatenate([s_gate, s_gm, s_lru, s_mla], axis=1))[0]
        gl["in_w"] = _unpad_in_cols(_mm(sv["h1"], dz, "tn", f"d_in_w{l}"))
        dh1 = _mm(dz, win_p[l], "nt", f"d_h1_{l}")
        dmods.append(dm)
        dres, hpath = na["du"], (dh1, sv["x_in"], sc1)
        G[l] = gl
    n0 = _node_bwd("node_in", dres, hpath=hpath)
    dmods[-1]["sc1"], dmods[-1]["sh1"] = n0["dsc"], n0["dsh"]
    grad_x = n0["dx"][None]
    dmods = dmods[::-1]
    dmod = jnp.stack([jnp.concatenate([dmods[l][k] for k in ("sh1", "sc1", "g1", "sh2", "sc2", "g2")], axis=1)[0]
                      for l in range(DEPTH)])
    grads = {n: jnp.stack([G[l][n] for l in range(DEPTH)]) for n in G[0]}
    grads["ada_b"] = dmod

    gsh = jnp.stack([_pack({n: jnp.split(grads[n], 4, axis=ax)[k] for n, ax in _BIG}) for k in range(4)])
    R = gsh.shape[1]
    rh = R // 2
    from_sib = _sibling_swap(gsh, "rs_pair")
    mine_half = lax.dynamic_slice_in_dim(gsh, ci * rh, rh, axis=1)
    pair, pair_b = _pair_add(mine_half, from_sib, "rs_pair_add")
    from_chips = _chip_scatter(pair_b, "rs_chips")
    own = lax.dynamic_index_in_dim(pair, kme, axis=0, keepdims=False)
    half_sum = _chip_sum(own, from_chips, "rs_chip_sum")
    gflat = _sibling_join(half_sum, "rs_join")
    gbig = _unpack(gflat, shard_shapes)

    small_shapes = [grads[n].shape for n in _SMALL]
    gsm_all = _allgather8(_flat128([grads[n] for n in _SMALL]), "ag_small_grads")
    gsm = _unflat(_sum8(gsm_all, "sum_small_grads"), small_shapes)
    gsmall = dict(zip(_SMALL, gsm))
    dmod_all = jnp.stack([_unflat(gsm_all[d], small_shapes)[0] for d in range(8)], axis=1)
    dmod_sh = lax.dynamic_slice_in_dim(dmod_all, kme * (6 * D // 4), 6 * D // 4, axis=2)
    g_ada_w = _ada_w_grad(c_all.T, dmod_sh, "d_ada_w")
    quarter = lambda g, ax: lax.dynamic_slice_in_dim(g, kme * (g.shape[ax] // 4), g.shape[ax] // 4, axis=ax)
    gsmall["lru_conv_w"] = quarter(gsmall["lru_conv_w"], 2)
    gsmall["ln_g"] = quarter(gsmall["ln_g"], 2)
    gsmall["ln_b"] = quarter(gsmall["ln_b"], 2)

    grad = dict(gbig)
    grad.update(gsmall)
    grad["ada_w"] = g_ada_w

    delta, new_m, new_v = {}, {}, {}
    for n in ("ada_w",) + tuple(n for n, _ in _BIG):
        shp = W[n].shape
        two = lambda a: a.reshape(-1, shp[-1])
        d_, m_, v_ = _adamw(two(W[n]), two(grad[n]), two(M[n]), two(V[n]), f"adamw_{n}")
        delta[n], new_m[n], new_v[n] = d_.reshape(shp), m_.reshape(shp), v_.reshape(shp)
    sm_shapes = [W[n].shape for n in _SMALL]
    d_, m_, v_ = _adamw(_flat128([W[n] for n in _SMALL]), _flat128([grad[n] for n in _SMALL]), _flat128([M[n] for n in _SMALL]),
                        _flat128([V[n] for n in _SMALL]), "adamw_small")
    for n, a, b, c_ in zip(_SMALL, _unflat(d_, sm_shapes), _unflat(m_, sm_shapes), _unflat(v_, sm_shapes)):
        delta[n], new_m[n], new_v[n] = a, b, c_

    return (loss, grad_x, *[grad[n] for n in _NAMES], *[delta[n] for n in _NAMES], *[new_m[n] for n in _NAMES],
            *[new_v[n] for n in _NAMES])
```

```python
import functools
import math

import jax
import jax.numpy as jnp
from jax import lax
from jax.experimental import pallas as pl
from jax.experimental.pallas import tpu as pltpu
from jax.scipy.linalg import block_diag

F32 = jnp.float32
MXU = jnp.bfloat16
MESH = pl.DeviceIdType.MESH

D = 1024
DEPTH = 2
CHUNK = 64
GM_W = 512
GM_G = 4
HEADS = 8
Q_RANK = 256
KV_RANK = 128
NOPE = 64
ROPE = 32
VD = 64
LRU_W = 512
LRU_NB = 8
D_FF = 4096
ALPHA = (2.0 * DEPTH) ** 0.25
LN_EPS = 1e-5
RMS_EPS = 1e-6
ROPE_BASE = 10000.0
ATT_SCALE = (NOPE + ROPE) ** -0.5
AUX = NOPE + ROPE
N_IN = 5536
P_GATE, P_GM, P_LX, P_LG, P_Q, P_KV, P_KA, P_KB, NP = 0, 3072, 4096, 4608, 5120, 5376, 5504, 5632, 5760
O_GM, O_Q, O_KV, O_KR, O_LX, O_LG, O_GATE = 0, 1024, 1280, 1408, 1440, 1952, 2464

ADAM_LR, ADAM_B1, ADAM_B2, ADAM_EPS, ADAM_WD, ADAM_STEP = 0.001, 0.9, 0.999, 1e-08, 0.01, 10

ROW_TILE = 512
ATT_TILE = 512
VMEM_LIMIT = 56 * 1024 * 1024
FLAT_W = 1024
FLAT_ALIGN = 32


def _params(sem=None, vmem=VMEM_LIMIT):
    return pltpu.CompilerParams(dimension_semantics=sem, vmem_limit_bytes=vmem)


def _tile(dim, pref, mult=128):
    if dim <= pref:
        return dim
    t = (pref // mult) * mult
    while t >= mult:
        if dim % t == 0:
            return t
        t -= mult
    return dim


def _dot(a, b):
    return lax.dot_general(a, b, (((1,), (0,)), ((), ())), preferred_element_type=F32)


def _dot_nt(a, b):
    return lax.dot_general(a, b, (((1,), (1,)), ((), ())), preferred_element_type=F32)


def _dot_tn(a, b):
    return lax.dot_general(a, b, (((0,), (0,)), ((), ())), preferred_element_type=F32)


def _sigmoid(x):
    return 1.0 / (1.0 + jnp.exp(-x))


_GC = 0.7978845608028654


def _gelu(x):
    return 0.5 * x * (1.0 + jnp.tanh(_GC * (x + 0.044715 * x * x * x)))


def _gelu_grad(x):
    t = jnp.tanh(_GC * (x + 0.044715 * x * x * x))
    return 0.5 * (1.0 + t) + 0.5 * x * (1.0 - t * t) * _GC * (1.0 + 3.0 * 0.044715 * x * x)


def _ln_stats(x):
    mu = jnp.mean(x, axis=-1, keepdims=True)
    xc = x - mu
    r = lax.rsqrt(jnp.mean(xc * xc, axis=-1, keepdims=True) + LN_EPS)
    return xc * r, r


def _ln_bwd(dxh, xh, r):
    return r * (dxh - jnp.mean(dxh, axis=-1, keepdims=True) - xh * jnp.mean(dxh * xh, axis=-1, keepdims=True))


def _colsum(v):
    return jnp.sum(v, axis=0, keepdims=True)


def _acc(ref, first, val):
    @pl.when(first)
    def _():
        ref[...] = val

    @pl.when(jnp.logical_not(first))
    def _():
        ref[...] += val


def _row_spec(t, w, col=0):
    return pl.BlockSpec((t, w), lambda i, c=col: (i, c))


def _vec_spec(w):
    return pl.BlockSpec((1, w), lambda i: (0, 0))


def _full_spec(shape):
    nd = len(shape)
    return pl.BlockSpec(shape, lambda i, n=nd: (0,) * n)


def _mm(a, b, mode, name, out_dtypes=(F32,), extras=(), rows=(), epi=None, colsum=False, tm=1024, tn=1152, tk=2048):
    if mode == "nn":
        (M, K), N = a.shape, b.shape[1]
    elif mode == "nt":
        (M, K), N = a.shape, b.shape[0]
    else:
        (K, M), N = a.shape, b.shape[1]
    tm, tn, tk = _tile(M, tm), _tile(N, tn), _tile(K, tk)
    nk = K // tk
    dot = {"nn": _dot, "nt": _dot_nt, "tn": _dot_tn}[mode]
    a_spec = pl.BlockSpec((tk, tm), lambda i, j, k: (k, i)) if mode == "tn" else pl.BlockSpec((tm, tk), lambda i, j, k: (i, k))
    b_spec = pl.BlockSpec((tn, tk), lambda i, j, k: (j, k)) if mode == "nt" else pl.BlockSpec((tk, tn), lambda i, j, k: (k, j))
    o_spec = pl.BlockSpec((tm, tn), lambda i, j, k: (i, j))
    r_spec = pl.BlockSpec((1, tn), lambda i, j, k: (0, j))
    n_e, n_r, n_o = len(extras), len(rows), len(out_dtypes)
    if epi is None:
        epi = lambda acc: (acc,)

    def body(*refs):
        a_ref, b_ref = refs[0], refs[1]
        e_refs = refs[2:2 + n_e]
        r_refs = refs[2 + n_e:2 + n_e + n_r]
        o_refs = refs[2 + n_e + n_r:2 + n_e + n_r + n_o]
        p = dot(a_ref[...], b_ref[...])

        def finish(acc):
            outs = epi(acc, *[e[...] for e in e_refs], *[r[...] for r in r_refs])
            for o_ref, o in zip(o_refs, outs):
                o_ref[...] = o.astype(o_ref.dtype)
            if colsum:
                cs_ref = refs[2 + n_e + n_r + n_o]
                cols = pl.ds(pl.multiple_of(pl.program_id(1) * tn, 128), tn)
                _acc(cs_ref.at[:, cols], pl.program_id(0) == 0, _colsum(outs[0]))

        if nk == 1:
            finish(p)
        else:
            acc_ref = refs[-1]
            k = pl.program_id(2)
            _acc(acc_ref, k == 0, p)

            @pl.when(k == nk - 1)
            def _():
                finish(acc_ref[...])

    outs = pl.pallas_call(
        body, name=name, grid=(M // tm, N // tn, nk),
        in_specs=[a_spec, b_spec] + [o_spec] * n_e + [r_spec] * n_r,
        out_specs=[o_spec] * n_o + ([pl.BlockSpec((1, N), lambda i, j, k: (0, 0))] if colsum else []),
        out_shape=[jax.ShapeDtypeStruct((M, N), dt) for dt in out_dtypes] + ([jax.ShapeDtypeStruct((1, N), F32)] if colsum else []),
        scratch_shapes=[pltpu.VMEM((tm, tn), F32)] if nk > 1 else [],
        compiler_params=_params(("arbitrary",) * 3 if colsum else ("parallel", "parallel", "arbitrary")),
    )(a, b, *extras, *rows)
    return outs[0] if len(outs) == 1 else outs


def _lnmod_fwd(x, sc, sh, name):
    S = x.shape[0]
    t = min(ROW_TILE, S)

    def body(x_ref, sc_ref, sh_ref, h_ref):
        xh, _ = _ln_stats(x_ref[...])
        h_ref[...] = (xh * (1.0 + sc_ref[...]) + sh_ref[...]).astype(h_ref.dtype)

    return pl.pallas_call(
        body, name=name, grid=(S // t,),
        in_specs=[_row_spec(t, D), _vec_spec(D), _vec_spec(D)], out_specs=_row_spec(t, D),
        out_shape=jax.ShapeDtypeStruct((S, D), MXU), compiler_params=_params(("parallel",)),
    )(x, sc, sh)


def _res_ln_fwd(xprev, br, gvec, lng, lnb, name, mod=None):
    S = xprev.shape[0]
    t = min(ROW_TILE, S)
    with_h = mod is not None

    def body(*refs):
        xp_ref, br_ref, g_ref, lg_ref, lb_ref = refs[:5]
        u = ALPHA * xp_ref[...] + (1.0 + g_ref[...]) * br_ref[...]
        uh, _ = _ln_stats(u)
        xn = uh * lg_ref[...] + lb_ref[...]
        if with_h:
            sc_ref, sh_ref, xn_ref, h_ref = refs[5:]
            xh, _ = _ln_stats(xn)
            h_ref[...] = (xh * (1.0 + sc_ref[...]) + sh_ref[...]).astype(h_ref.dtype)
        else:
            xn_ref = refs[5]
        xn_ref[...] = xn

    ins = [xprev, br, gvec, lng, lnb] + (list(mod) if with_h else [])
    return pl.pallas_call(
        body, name=name, grid=(S // t,),
        in_specs=[_row_spec(t, D), _row_spec(t, D)] + [_vec_spec(D)] * (len(ins) - 2),
        out_specs=[_row_spec(t, D)] * (2 if with_h else 1),
        out_shape=[jax.ShapeDtypeStruct((S, D), F32)] + ([jax.ShapeDtypeStruct((S, D), MXU)] if with_h else []),
        compiler_params=_params(("parallel",)),
    )(*ins)


def _loss_fwd(y, tgt):
    S = y.shape[0]
    t = min(ROW_TILE, S)

    def body(y_ref, t_ref, dy_ref, sq_ref):
        d = y_ref[...] - t_ref[...]
        dy_ref[...] = d * (1.0 / D)
        _acc(sq_ref, pl.program_id(0) == 0, _colsum(d * d))

    return pl.pallas_call(
        body, name="loss_head", grid=(S // t,),
        in_specs=[_row_spec(t, D), _row_spec(t, D)], out_specs=[_row_spec(t, D), _vec_spec(D)],
        out_shape=[jax.ShapeDtypeStruct((S, D), F32), jax.ShapeDtypeStruct((1, D), F32)],
        compiler_params=_params(("arbitrary",)),
    )(y, tgt)


def _node_bwd(name, dres, *, hpath=None, upath=None):
    S = dres.shape[0]
    t = min(ROW_TILE, S)
    has_h, has_u = hpath is not None, upath is not None

    def body(*refs):
        refs = list(refs)
        first = pl.program_id(0) == 0
        dxs = refs.pop(0)[...]
        if has_h:
            dh = refs.pop(0)[...]
            xs = refs.pop(0)[...]
            sc = refs.pop(0)[...]
        if has_u:
            xp = refs.pop(0)[...]
            br = refs.pop(0)[...]
            gv = refs.pop(0)[...]
            lg = refs.pop(0)[...]
        if has_h:
            dsc_ref, dsh_ref = refs.pop(0), refs.pop(0)
            xh, r = _ln_stats(xs)
            _acc(dsc_ref, first, _colsum(dh * xh))
            _acc(dsh_ref, first, _colsum(dh))
            dxs = dxs + _ln_bwd(dh * (1.0 + sc), xh, r)
        if has_u:
            du_ref, dbr_ref, dlg_ref, dlb_ref, dg_ref, dbs_ref = refs
            uh, ru = _ln_stats(ALPHA * xp + (1.0 + gv) * br)
            _acc(dlg_ref, first, _colsum(dxs * uh))
            _acc(dlb_ref, first, _colsum(dxs))
            du = _ln_bwd(dxs * lg, uh, ru)
            _acc(dg_ref, first, _colsum(du * br))
            du_ref[...] = ALPHA * du
            dbr = (1.0 + gv) * du
            _acc(dbs_ref, first, _colsum(dbr))
            dbr_ref[...] = dbr.astype(dbr_ref.dtype)
        else:
            refs[0][...] = dxs

    ins, in_specs = [dres], [_row_spec(t, D)]
    outs, out_specs, names = [], [], []
    vec = jax.ShapeDtypeStruct((1, D), F32)
    if has_h:
        ins += list(hpath)
        in_specs += [_row_spec(t, D), _row_spec(t, D), _vec_spec(D)]
        outs += [vec, vec]
        out_specs += [_vec_spec(D), _vec_spec(D)]
        names += ["dsc", "dsh"]
    if has_u:
        ins += list(upath)
        in_specs += [_row_spec(t, D), _row_spec(t, D), _vec_spec(D), _vec_spec(D)]
        outs += [jax.ShapeDtypeStruct((S, D), F32), jax.ShapeDtypeStruct((S, D), MXU), vec, vec, vec, vec]
        out_specs += [_row_spec(t, D), _row_spec(t, D), _vec_spec(D), _vec_spec(D), _vec_spec(D), _vec_spec(D)]
        names += ["du", "dbr", "dlng", "dlnb", "dg", "dbrsum"]
    else:
        outs += [jax.ShapeDtypeStruct((S, D), F32)]
        out_specs += [_row_spec(t, D)]
        names += ["dx"]
    res = pl.pallas_call(
        body, name=name, grid=(S // t,), in_specs=in_specs, out_specs=out_specs, out_shape=outs,
        compiler_params=_params(("arbitrary",)),
    )(*ins)
    return dict(zip(names, res))


def _colsum_call(a, name):
    S, N = a.shape
    t, tn = min(ROW_TILE, S), _tile(N, 1152)

    def body(a_ref, o_ref):
        _acc(o_ref, pl.program_id(1) == 0, _colsum(a_ref[...].astype(F32)))

    return pl.pallas_call(
        body, name=name, grid=(N // tn, S // t),
        in_specs=[pl.BlockSpec((t, tn), lambda j, i: (i, j))], out_specs=pl.BlockSpec((1, tn), lambda j, i: (0, j)),
        out_shape=jax.ShapeDtypeStruct((1, N), F32), compiler_params=_params(("parallel", "arbitrary")),
    )(a)


def _gm_mask():
    i = lax.broadcasted_iota(jnp.int32, (128, 128), 0) // CHUNK
    j = lax.broadcasted_iota(jnp.int32, (128, 128), 1) // CHUNK
    return i >= j


def _gmlp_common(z, lng, lnb):
    gz = _gelu(z)
    u, v = gz[:, :GM_W], gz[:, GM_W:]
    vh, r = _ln_stats(v)
    return u, vh, r, vh * lng + lnb


def _gmlp_fwd(z, lng, lnb, ws, bst, name):
    S = z.shape[0]
    t = min(ROW_TILE, S)

    def body(z_ref, lg_ref, lb_ref, ws_ref, bs_ref, y_ref):
        u, _, _, vn = _gmlp_common(z_ref[...], lg_ref[...], lb_ref[...])
        mask = _gm_mask()
        vb = vn.astype(MXU)
        for g in range(GM_G):
            w = jnp.where(mask, ws_ref[g], jnp.zeros_like(ws_ref[g]))
            bias = bs_ref[:, g:g + 1]
            for blk in range(t // 128):
                rs, cs = slice(blk * 128, (blk + 1) * 128), slice(g * 128, (g + 1) * 128)
                f = _dot(w, vb[rs, cs]) + bias
                y_ref[rs, cs] = (u[rs, cs] * f).astype(y_ref.dtype)

    return pl.pallas_call(
        body, name=name, grid=(S // t,),
        in_specs=[_row_spec(t, 2 * GM_W, P_GM // (2 * GM_W)), _vec_spec(GM_W), _vec_spec(GM_W),
                  _full_spec((GM_G, 128, 128)), _full_spec((128, GM_G))],
        out_specs=_row_spec(t, GM_W), out_shape=jax.ShapeDtypeStruct((S, GM_W), MXU),
        compiler_params=_params(("parallel",)),
    )(z, lng, lnb, ws, bst)


def _gmlp_bwd(z, dya, lng, lnb, ws, wst, bst, dz_in, name):
    S = z.shape[0]
    t = min(ROW_TILE, S)

    def body(z_ref, dy_ref, lg_ref, lb_ref, ws_ref, wst_ref, bs_ref, _dz_in, dz_ref, dlg_ref, dlb_ref, dws_ref, dbs_ref, sum_ref):
        first = pl.program_id(0) == 0
        zz = z_ref[...]
        u, vh, r, vn = _gmlp_common(zz, lg_ref[...], lb_ref[...])
        dy = dy_ref[...]
        mask = _gm_mask()
        maskt = lax.broadcasted_iota(jnp.int32, (128, 128), 1) // CHUNK >= lax.broadcasted_iota(jnp.int32, (128, 128), 0) // CHUNK
        lane = lax.broadcasted_iota(jnp.int32, (128, 128), 1)
        vb = vn.astype(MXU)
        dfb = (dy * u).astype(MXU)
        df32 = dy * u
        dbs = jnp.zeros((128, 128), F32)
        du_cols, dvn_cols = [], []
        for g in range(GM_G):
            w = jnp.where(mask, ws_ref[g], jnp.zeros_like(ws_ref[g]))
            wt = jnp.where(maskt, wst_ref[g], jnp.zeros_like(wst_ref[g]))
            bias = bs_ref[:, g:g + 1]
            cs = slice(g * 128, (g + 1) * 128)
            dw = jnp.zeros((128, 128), F32)
            du_rows, dvn_rows = [], []
            for blk in range(t // 128):
                rs = slice(blk * 128, (blk + 1) * 128)
                f = _dot(w, vb[rs, cs]) + bias
                du_rows.append(dy[rs, cs] * f)
                dvn_rows.append(_dot(wt, dfb[rs, cs]))
                dw = dw + _dot_nt(dfb[rs, cs], vb[rs, cs])
                dbs = dbs + jnp.where(lane == g, jnp.sum(df32[rs, cs], axis=1, keepdims=True), 0.0)
            _acc(dws_ref.at[g], first, jnp.where(mask, dw, 0.0))
            du_cols.append(jnp.concatenate(du_rows, axis=0))
            dvn_cols.append(jnp.concatenate(dvn_rows, axis=0))
        _acc(dbs_ref, first, dbs)
        du = jnp.concatenate(du_cols, axis=1)
        dvn = jnp.concatenate(dvn_cols, axis=1)
        _acc(dlg_ref, first, _colsum(dvn * vh))
        _acc(dlb_ref, first, _colsum(dvn))
        dv = _ln_bwd(dvn * lg_ref[...], vh, r)
        dzz = jnp.concatenate([du, dv], axis=1) * _gelu_grad(zz)
        dz_ref[...] = dzz.astype(dz_ref.dtype)
        _acc(sum_ref, first, _colsum(dzz))

    vec = jax.ShapeDtypeStruct((1, GM_W), F32)
    return pl.pallas_call(
        body, name=name, grid=(S // t,),
        in_specs=[_row_spec(t, 2 * GM_W, P_GM // (2 * GM_W)), _row_spec(t, GM_W), _vec_spec(GM_W), _vec_spec(GM_W),
                  _full_spec((GM_G, 128, 128)), _full_spec((GM_G, 128, 128)), _full_spec((128, GM_G)), _ANY],
        out_specs=[_row_spec(t, 2 * GM_W, P_GM // (2 * GM_W)), _vec_spec(GM_W), _vec_spec(GM_W), _full_spec((GM_G, 128, 128)),
                   _full_spec((128, 128)), _vec_spec(2 * GM_W)],
        out_shape=[jax.ShapeDtypeStruct((S, NP), MXU), vec, vec,
                   jax.ShapeDtypeStruct((GM_G, 128, 128), F32), jax.ShapeDtypeStruct((128, 128), F32),
                   jax.ShapeDtypeStruct((1, 2 * GM_W), F32)],
        input_output_aliases=_dz_alias(8), compiler_params=_params(("arbitrary",)),
    )(z, dya, lng, lnb, ws, wst, bst, dz_in)


def _rms(x, g):
    r = lax.rsqrt(jnp.mean(x * x, axis=-1, keepdims=True) + RMS_EPS)
    xh = x * r
    return xh, r, xh * g


def _mla_specs(t):
    return [_row_spec(t, Q_RANK, P_Q // Q_RANK), _row_spec(t, KV_RANK, P_KV // 128), _row_spec(t, 128, P_KA // 128),
            _row_spec(t, 128, P_KB // 128), _row_spec(t, 128), _row_spec(t, 128), _vec_spec(Q_RANK), _vec_spec(KV_RANK),
            _full_spec((Q_RANK, D)), _full_spec((Q_RANK, D)), _full_spec((KV_RANK, D)), _full_spec((HEADS, KV_RANK, VD))]


def _mla_prep_fwd(z, cos, sin, gq, gkv, w1, w2, wk, wv, wvt, name):
    S = z.shape[0]
    t = min(ROW_TILE, S)

    def body(zq_ref, zkv_ref, zka_ref, zkb_ref, cos_ref, sin_ref, gq_ref, gkv_ref, w1_ref, w2_ref, wk_ref, wv_ref, wvt_ref,
             q_ref, k_ref, vt_ref):
        cos_, sin_ = cos_ref[...], sin_ref[...]
        cq = cos_ + jnp.where(lax.broadcasted_iota(jnp.int32, cos_.shape, 1) < NOPE, 1.0, 0.0)
        qn = _rms(zq_ref[...], gq_ref[...])[2].astype(MXU)
        q1, q2 = _dot(qn, w1_ref[...]), _dot(qn, w2_ref[...])
        kvn = _rms(zkv_ref[...], gkv_ref[...])[2].astype(MXU)
        kn = _dot(kvn, wk_ref[...])
        lane = lax.broadcasted_iota(jnp.int32, cos_.shape, 1)
        krot = zka_ref[...] * cos_ + zkb_ref[...] * sin_ + jnp.where((lane >= AUX) & (lane < AUX + 3), 1.0, 0.0)
        rowi = lax.broadcasted_iota(jnp.int32, (128 - VD, t), 0)
        vt_aux = jnp.where(rowi < 3, 1.0, 0.0).astype(vt_ref.dtype)
        for h in range(HEADS):
            hs = slice(h * 128, (h + 1) * 128)
            q_ref[h] = (q1[:, hs] * cq + q2[:, hs] * sin_).astype(q_ref.dtype)
            k_ref[h] = (kn[:, hs] + krot).astype(k_ref.dtype)
            vt_ref[h, 0:VD, :] = _dot_nt(wvt_ref[h], kvn).astype(vt_ref.dtype)
            vt_ref[h, VD:128, :] = vt_aux

    hspec = lambda w: pl.BlockSpec((HEADS, t, w), lambda i: (0, i, 0))
    return pl.pallas_call(
        body, name=name, grid=(S // t,), in_specs=_mla_specs(t) + [_full_spec((HEADS, VD, KV_RANK))],
        out_specs=[hspec(128), hspec(128), pl.BlockSpec((HEADS, 128, t), lambda i: (0, 0, i))],
        out_shape=[jax.ShapeDtypeStruct((HEADS, S, 128), MXU), jax.ShapeDtypeStruct((HEADS, S, 128), MXU),
                   jax.ShapeDtypeStruct((HEADS, 128, S), MXU)],
        compiler_params=_params(("parallel",)),
    )(z, z, z, z, cos, sin, gq, gkv, w1, w2, wk, wv, wvt)


def _mla_prep_bwd(z, cos, sin, gq, gkv, w1, w2, wk, wv, dq, dk, dv, dz_in, name):
    S = z.shape[0]
    t = min(ROW_TILE, S)

    def body(zq_ref, zkv_ref, zka_ref, zkb_ref, cos_ref, sin_ref, gq_ref, gkv_ref, w1_ref, w2_ref, wk_ref, wv_ref,
             dq_ref, dk_ref, dv_ref, _dz_in, dz_ref, dgq_ref, dgkv_ref, dw1_ref, dw2_ref, dwk_ref, dwv_ref, sum_ref):
        first = pl.program_id(0) == 0
        cos_, sin_ = cos_ref[...], sin_ref[...]
        cq = cos_ + jnp.where(lax.broadcasted_iota(jnp.int32, cos_.shape, 1) < NOPE, 1.0, 0.0)
        gq_, gkv_ = gq_ref[...], gkv_ref[...]
        xhq, rq, qn32 = _rms(zq_ref[...], gq_)
        qn = qn32.astype(MXU)
        dq1 = jnp.concatenate([dq_ref[h] * cq for h in range(HEADS)], axis=1).astype(MXU)
        dq2 = jnp.concatenate([dq_ref[h] * sin_ for h in range(HEADS)], axis=1).astype(MXU)
        dqn = _dot_nt(dq1, w1_ref[...]) + _dot_nt(dq2, w2_ref[...])
        _acc(dw1_ref, first, _dot_tn(qn, dq1))
        _acc(dw2_ref, first, _dot_tn(qn, dq2))
        _acc(dgq_ref, first, _colsum(dqn * xhq))
        dxn = dqn * gq_
        dzq = rq * (dxn - xhq * jnp.mean(dxn * xhq, axis=-1, keepdims=True))

        xhk, rk, kvn32 = _rms(zkv_ref[...], gkv_)
        kvn = kvn32.astype(MXU)
        dks = [dk_ref[h].T for h in range(HEADS)]
        dkall = jnp.concatenate(dks, axis=1).astype(MXU)
        dkrot = functools.reduce(lambda p, q_: p + q_, dks)
        dkvn = _dot_nt(dkall, wk_ref[...])
        _acc(dwk_ref, first, _dot_tn(kvn, dkall))
        for h in range(HEADS):
            dvb = dv_ref[h].T[:, 0:VD].astype(MXU)
            dkvn = dkvn + _dot_nt(dvb, wv_ref[h])
            _acc(dwv_ref.at[h], first, _dot_tn(kvn, dvb))
        _acc(dgkv_ref, first, _colsum(dkvn * xhk))
        dxk = dkvn * gkv_
        dzkv = rk * (dxk - xhk * jnp.mean(dxk * xhk, axis=-1, keepdims=True))
        dzm = jnp.concatenate([dzq, dzkv, dkrot * cos_, dkrot * sin_], axis=1)
        dz_ref[...] = dzm.astype(dz_ref.dtype)
        _acc(sum_ref, first, _colsum(dzm))

    hspec = lambda w: pl.BlockSpec((HEADS, t, w), lambda i: (0, i, 0))
    tspec = pl.BlockSpec((HEADS, 128, t), lambda i: (0, 0, i))
    sds = jax.ShapeDtypeStruct
    return pl.pallas_call(
        body, name=name, grid=(S // t,),
        in_specs=_mla_specs(t) + [hspec(128), tspec, tspec, _ANY],
        out_specs=[_row_spec(t, 640, P_Q // 640), _vec_spec(Q_RANK), _vec_spec(KV_RANK), _full_spec((Q_RANK, D)),
                   _full_spec((Q_RANK, D)), _full_spec((KV_RANK, D)), _full_spec((HEADS, KV_RANK, VD)), _vec_spec(640)],
        out_shape=[sds((S, NP), MXU), sds((1, Q_RANK), F32), sds((1, KV_RANK), F32), sds((Q_RANK, D), F32),
                   sds((Q_RANK, D), F32), sds((KV_RANK, D), F32), sds((HEADS, KV_RANK, VD), F32), sds((1, 640), F32)],
        input_output_aliases=_dz_alias(16), compiler_params=_params(("arbitrary",)),
    )(z, z, z, z, cos, sin, gq, gkv, w1, w2, wk, wv, dq, dk, dv, dz_in)


def _chunk_mask(t, transposed):
    r = lax.broadcasted_iota(jnp.int32, (t, t), 0) // CHUNK
    c = lax.broadcasted_iota(jnp.int32, (t, t), 1) // CHUNK
    return (r <= c) if transposed else (c <= r)


def _attn_fwd(q, k, vt, name):
    S = q.shape[1]
    t = min(ATT_TILE, S)
    n = S // t

    hb = 2

    def body(q_ref, k_ref, vt_ref, ob_ref, o_ref):
        qi = pl.program_id(1)
        qbs = [q_ref[g] for g in range(hb)]

        def block(j, carries, masked):
            cols = pl.ds(pl.multiple_of(j * t, t), t)
            out = []
            for g in range(hb):
                m, l, acc = carries[g]
                st = _dot_nt(k_ref[g, cols, :], qbs[g]) * ATT_SCALE
                if masked:
                    st = jnp.where(_chunk_mask(t, True), st, -jnp.inf)
                m_new = jnp.maximum(m, jnp.max(st, axis=0, keepdims=True))
                p = jnp.exp(st - m_new)
                alpha = jnp.exp(m - m_new)
                l = alpha * l + jnp.sum(p, axis=0, keepdims=True)
                acc = alpha * acc + _dot(vt_ref[g, 0:VD, cols], p.astype(MXU))
                out.append((m_new, l, acc))
            return tuple(out)

        init = tuple((jnp.full((1, t), -jnp.inf, F32), jnp.zeros((1, t), F32), jnp.zeros((VD, t), F32)) for _ in range(hb))
        carries = lax.fori_loop(0, qi, lambda j, c: block(j, c, False), init)
        for g, (m, l, acc) in enumerate(block(qi, carries, True)):
            lse = jnp.broadcast_to(m + jnp.log(l), (8, t))
            ext = jnp.concatenate([acc / l, lse, jnp.zeros((128 - VD - 8, t), F32)], axis=0).T
            o_ref[g] = ext
            ob_ref[g] = ext[:, 0:VD].astype(ob_ref.dtype)

    qspec = lambda w: pl.BlockSpec((hb, t, w), lambda h, i: (h, i, 0))
    sds = jax.ShapeDtypeStruct
    return pl.pallas_call(
        body, name=name, grid=(HEADS // hb, n),
        in_specs=[qspec(128), pl.BlockSpec((hb, S, 128), lambda h, i: (h, 0, 0)), pl.BlockSpec((hb, 128, S), lambda h, i: (h, 0, 0))],
        out_specs=[qspec(VD), qspec(128)],
        out_shape=[sds((HEADS, S, VD), MXU), sds((HEADS, S, 128), F32)],
        compiler_params=_params(("parallel", "arbitrary")),
    )(q, k, vt)


def _split3(x):
    hi = x.astype(jnp.bfloat16).astype(F32)
    mid = (x - hi).astype(jnp.bfloat16).astype(F32)
    lo = (x - hi - mid).astype(jnp.bfloat16).astype(F32)
    return hi, mid, lo


def _lane_parts(lane, first, parts):
    out = 0.0
    for i, p_ in enumerate(parts):
        out = out + jnp.where(lane == first + i, p_, 0.0)
    return out


def _attn_bwd_prep(q, o_ext, do, name):
    S = q.shape[1]
    t = min(ROW_TILE, S)

    def body(q_ref, o_ref, do_ref, qa_ref, qt_ref, doa_ref, dot_ref):
        lane = lax.broadcasted_iota(jnp.int32, (t, 128), 1)
        for h in range(HEADS):
            oe = o_ref[h]
            do_ = jnp.concatenate([do_ref[h], jnp.zeros((t, 128 - VD), F32)], axis=1)
            delta = jnp.sum(jnp.where(lane < VD, oe, 0.0) * do_, axis=1, keepdims=True)
            lse = jnp.sum(jnp.where(lane == VD, oe, 0.0), axis=1, keepdims=True)
            qa = q_ref[h].astype(F32) + _lane_parts(lane, AUX, _split3(lse * (-1.0 / ATT_SCALE)))
            doa = do_ + _lane_parts(lane, VD, _split3(-delta))
            qa_ref[h] = qa.astype(qa_ref.dtype)
            qt_ref[h] = qa.T.astype(qt_ref.dtype)
            doa_ref[h] = doa.astype(doa_ref.dtype)
            dot_ref[h] = doa.T.astype(dot_ref.dtype)

    rspec = lambda w: pl.BlockSpec((HEADS, t, w), lambda i: (0, i, 0))
    tspec = pl.BlockSpec((HEADS, 128, t), lambda i: (0, 0, i))
    sds = jax.ShapeDtypeStruct
    return pl.pallas_call(
        body, name=name, grid=(S // t,), in_specs=[rspec(128), rspec(128), rspec(VD)], out_specs=[rspec(128), tspec, rspec(128), tspec],
        out_shape=[sds((HEADS, S, 128), MXU), sds((HEADS, 128, S), MXU), sds((HEADS, S, 128), MXU), sds((HEADS, 128, S), MXU)],
        compiler_params=_params(("parallel",)),
    )(q, o_ext, do)


def _attn_bwd(qa, qt, doa, dot_, k, vt, name):
    S = qa.shape[1]
    t = min(ATT_TILE, S)
    n = S // t

    def body(qa_ref, qt_ref, doa_ref, dot_ref, k_ref, vt_ref, dq_ref, dkt_ref, dvt_ref, dk_s, dv_s):
        ki = pl.program_id(1)

        @pl.when(ki == 0)
        def _():
            dq_ref[...] = jnp.zeros(dq_ref.shape, F32)

        dk_s[...] = jnp.zeros(dk_s.shape, F32)
        dv_s[...] = jnp.zeros(dv_s.shape, F32)
        kb, vtb = k_ref[0], vt_ref[0]

        def step(qi, masked):
            rows = pl.ds(pl.multiple_of(qi * t, t), t)
            s = _dot_nt(qa_ref[0, rows, :], kb) * ATT_SCALE
            if masked:
                s = jnp.where(_chunk_mask(t, False), s, -jnp.inf)
            p = jnp.exp(s)
            dv_s[...] += _dot(dot_ref[0, :, rows], p.astype(MXU))
            dp = _dot(doa_ref[0, rows, :], vtb)
            ds = (p * dp * ATT_SCALE).astype(MXU)
            dk_s[...] += _dot(qt_ref[0, :, rows], ds)
            dq_ref[0, rows, :] += _dot(ds, kb)

        step(ki, True)

        def loop(qi, c):
            step(qi, False)
            return c

        lax.fori_loop(ki + 1, n, loop, 0)
        dkt_ref[0] = dk_s[...]
        dvt_ref[0] = dv_s[...]

    head = lambda *shape: pl.BlockSpec((1,) + shape, lambda h, j: (h, 0, 0))
    tmap = lambda h, j: (h, 0, j)
    sds = jax.ShapeDtypeStruct
    return pl.pallas_call(
        body, name=name, grid=(HEADS, n),
        in_specs=[head(S, 128), head(128, S), head(S, 128), head(128, S), pl.BlockSpec((1, t, 128), lambda h, j: (h, j, 0)),
                  pl.BlockSpec((1, 128, t), tmap)],
        out_specs=[head(S, 128), pl.BlockSpec((1, 128, t), tmap), pl.BlockSpec((1, 128, t), tmap)],
        out_shape=[sds((HEADS, S, 128), F32), sds((HEADS, 128, S), F32), sds((HEADS, 128, S), F32)],
        scratch_shapes=[pltpu.VMEM((128, t), F32), pltpu.VMEM((128, t), F32)],
        compiler_params=_params(("parallel", "arbitrary")),
    )(qa, qt, doa, dot_, k, vt)


def _shift_down(x, prev8, d):
    xr = pltpu.roll(x, d, 0)
    r8 = lax.broadcasted_iota(jnp.int32, prev8.shape, 0)
    top = jnp.where(r8 < d, pltpu.roll(prev8, d, 0), xr[0:8])
    return jnp.concatenate([top, xr[8:]], axis=0)


def _shift_up(x, next8, d):
    n = x.shape[0]
    xr = pltpu.roll(x, n - d, 0)
    r8 = lax.broadcasted_iota(jnp.int32, next8.shape, 0)
    bot = jnp.where(r8 >= 8 - d, pltpu.roll(next8, 8 - d, 0), xr[n - 8:])
    return jnp.concatenate([xr[:n - 8], bot], axis=0)


def _log1p(u):
    return jnp.where(u < 0.01, u * (1.0 - u * (0.5 - u * (1.0 / 3.0 - 0.25 * u))), jnp.log(1.0 + u))


def _neg_expm1(y):
    series = -y * (1.0 + 0.5 * y * (1.0 + (1.0 / 3.0) * y * (1.0 + 0.25 * y)))
    return jnp.where(y > -0.05, series, 1.0 - jnp.exp(y))


def _softplus_neg(lam):
    return jnp.maximum(-lam, 0.0) + _log1p(jnp.exp(-jnp.abs(lam)))


def _lru_gates(x, prev8, cw, cb, wr, br, wi, bi, lam):
    xs1, xs2, xs3 = _shift_down(x, prev8, 1), _shift_down(x, prev8, 2), _shift_down(x, prev8, 3)
    xc = cb + cw[0:1] * xs3 + cw[1:2] * xs2 + cw[2:3] * xs1 + cw[3:4] * x
    xcb = xc.astype(MXU)
    r = _sigmoid(_dot(xcb, wr) + br)
    ig = _sigmoid(_dot(xcb, wi) + bi)
    sp = _softplus_neg(lam)
    log_a = -8.0 * r * sp
    a = jnp.exp(log_a)
    gb = jnp.sqrt(_neg_expm1(2.0 * log_a))
    return (xs1, xs2, xs3), xc, xcb, r, ig, sp, a, gb


def _lru_fwd(z, cw, cb, wr, br, wi, bi, lam, name):
    S = z.shape[0]
    t = min(ROW_TILE, S)

    def body(zx_ref, zg_ref, cw_ref, cb_ref, wr_ref, br_ref, wi_ref, bi_ref, lam_ref, y_ref, h_ref, xp_s, hc_s):
        @pl.when(pl.program_id(0) == 0)
        def _():
            xp_s[...] = jnp.zeros(xp_s.shape, F32)
            hc_s[...] = jnp.zeros(hc_s.shape, F32)

        x = zx_ref[...]
        _, xc, _, _, ig, _, a, gb = _lru_gates(x, xp_s[...], cw_ref[...], cb_ref[...], wr_ref[...], br_ref[...],
                                               wi_ref[...], bi_ref[...], lam_ref[...])
        b = gb * (ig * xc)
        rows = lax.broadcasted_iota(jnp.int32, a.shape, 0)
        d = 1
        while d < t:
            ar, brr = pltpu.roll(a, d, 0), pltpu.roll(b, d, 0)
            ok = rows >= d
            b = jnp.where(ok, a * brr, 0.0) + b
            a = jnp.where(ok, a * ar, a)
            d *= 2
        h = a * hc_s[7:8, :] + b
        h_ref[...] = h
        y_ref[...] = (h * _gelu(zg_ref[...])).astype(y_ref.dtype)
        hc_s[...] = h[t - 8:, :]
        xp_s[...] = x[t - 8:, :]

    w = LRU_W
    return pl.pallas_call(
        body, name=name, grid=(S // t,),
        in_specs=[_row_spec(t, w, P_LX // w), _row_spec(t, w, P_LG // w), _full_spec((4, w)), _vec_spec(w), _full_spec((w, w)),
                  _vec_spec(w), _full_spec((w, w)), _vec_spec(w), _vec_spec(w)],
        out_specs=[_row_spec(t, w), _row_spec(t, w)],
        out_shape=[jax.ShapeDtypeStruct((S, w), MXU), jax.ShapeDtypeStruct((S, w), F32)],
        scratch_shapes=[pltpu.VMEM((8, w), F32), pltpu.VMEM((8, w), F32)],
        compiler_params=_params(("arbitrary",)),
    )(z, z, cw, cb, wr, br, wi, bi, lam)


def _lru_bwd(z, h, dy, cw, cb, wr, br, wi, bi, lam, dz_in, name):
    S = z.shape[0]
    t = min(ROW_TILE, S)
    n = S // t
    w = LRU_W

    def body(zx_ref, zxp_ref, zg_ref, h_ref, hp_ref, dy_ref, cw_ref, cb_ref, wr_ref, br_ref, wi_ref, bi_ref, lam_ref, _dz_in,
             dz_ref, dcw_ref, dcb_ref, dwr_ref, dbr_ref, dwi_ref, dbi_ref, dlam_ref, sum_ref, gc_s, dn_s):
        i = pl.program_id(0)
        first = i == 0
        j = n - 1 - i

        @pl.when(first)
        def _():
            gc_s[...] = jnp.zeros(gc_s.shape, F32)
            dn_s[...] = jnp.zeros(dn_s.shape, F32)

        live = (j > 0).astype(F32)
        xprev8, hprev8 = zxp_ref[...] * live, hp_ref[...] * live
        x, zg, hh, dy_ = zx_ref[...], zg_ref[...], h_ref[...], dy_ref[...]
        cw_, lam_ = cw_ref[...], lam_ref[...]
        (xs1, xs2, xs3), xc, xcb, r, ig, sp, a, gb = _lru_gates(x, xprev8, cw_, cb_ref[...], wr_ref[...], br_ref[...],
                                                                wi_ref[...], bi_ref[...], lam_)
        hm1 = _shift_down(hh, hprev8, 1)
        dh = dy_ * _gelu(zg)
        dzg = dy_ * hh * _gelu_grad(zg)
        rows = lax.broadcasted_iota(jnp.int32, a.shape, 0)
        last = rows == t - 1
        ca = jnp.where(last, 0.0, pltpu.roll(a, t - 1, 0))
        g = dh + jnp.where(last, gc_s[0:1, :], 0.0)
        d = 1
        while d < t:
            ok = rows < t - d
            g = g + jnp.where(ok, ca * pltpu.roll(g, t - d, 0), 0.0)
            ca = jnp.where(ok, ca * pltpu.roll(ca, t - d, 0), 0.0)
            d *= 2
        gc_s[...] = a[0:8, :] * g[0:8, :]
        da = g * hm1
        dgb = g * (ig * xc)
        dub = g * gb
        di = dub * xc
        dxc = dub * ig
        dlog_a = da * a - dgb * (a * a) / gb
        dr = dlog_a * (-8.0 * sp)
        _acc(dlam_ref, first, _colsum(dlog_a * (-8.0 * r)) * (-_sigmoid(-lam_)))
        dpr = dr * r * (1.0 - r)
        dpi = di * ig * (1.0 - ig)
        _acc(dbr_ref, first, _colsum(dpr))
        _acc(dbi_ref, first, _colsum(dpi))
        dprb, dpib = dpr.astype(MXU), dpi.astype(MXU)
        _acc(dwr_ref, first, _dot_tn(xcb, dprb))
        _acc(dwi_ref, first, _dot_tn(xcb, dpib))
        dxc = dxc + _dot_nt(dprb, wr_ref[...]) + _dot_nt(dpib, wi_ref[...])
        _acc(dcb_ref, first, _colsum(dxc))
        _acc(dcw_ref, first, jnp.concatenate([_colsum(dxc * xs3), _colsum(dxc * xs2), _colsum(dxc * xs1), _colsum(dxc * x)], axis=0))
        nxt = dn_s[...]
        dx = cw_[3:4] * dxc + cw_[2:3] * _shift_up(dxc, nxt, 1) + cw_[1:2] * _shift_up(dxc, nxt, 2) + cw_[0:1] * _shift_up(dxc, nxt, 3)
        dn_s[...] = dxc[0:8, :]
        dz_ref[...] = jnp.concatenate([dx, dzg], axis=1).astype(dz_ref.dtype)
        _acc(sum_ref, first, jnp.concatenate([_colsum(dx), _colsum(dzg)], axis=1))

    rev = lambda col: pl.BlockSpec((t, w), lambda i, c=col: (n - 1 - i, c))
    prev8 = lambda col: pl.BlockSpec((8, w), lambda i, c=col: (jnp.maximum((n - 1 - i) * (t // 8) - 1, 0), c))
    vec = jax.ShapeDtypeStruct((1, w), F32)
    sds = jax.ShapeDtypeStruct
    return pl.pallas_call(
        body, name=name, grid=(n,),
        in_specs=[rev(P_LX // w), prev8(P_LX // w), rev(P_LG // w), rev(0), prev8(0), rev(0), _full_spec((4, w)), _vec_spec(w),
                  _full_spec((w, w)), _vec_spec(w), _full_spec((w, w)), _vec_spec(w), _vec_spec(w), _ANY],
        out_specs=[pl.BlockSpec((t, 2 * w), lambda i: (n - 1 - i, P_LX // (2 * w))), _full_spec((4, w)), _vec_spec(w),
                   _full_spec((w, w)), _vec_spec(w), _full_spec((w, w)), _vec_spec(w), _vec_spec(w), _vec_spec(2 * w)],
        out_shape=[sds((S, NP), MXU), sds((4, w), F32), vec, sds((w, w), F32), vec, sds((w, w), F32), vec, vec, sds((1, 2 * w), F32)],
        scratch_shapes=[pltpu.VMEM((8, w), F32), pltpu.VMEM((8, w), F32)],
        input_output_aliases=_dz_alias(14), compiler_params=_params(("arbitrary",)),
    )(z, z, z, h, h, dy, cw, cb, wr, br, wi, bi, lam, dz_in)


def _branch_fwd(z, ya, o, yc, wa, wb, wc, name):
    S = z.shape[0]
    t = min(ROW_TILE, S)

    def body(ga_ref, gb_ref, gc_ref, ya_ref, o_ref, yc_ref, wa_ref, wb_ref, wc_ref, m_ref, pa_ref, pb_ref, pc_ref):
        pa = _dot(ya_ref[...], wa_ref[...])
        pc = _dot(yc_ref[...], wc_ref[...])
        pb = _dot(o_ref[0], wb_ref[0])
        for h in range(1, HEADS):
            pb = pb + _dot(o_ref[h], wb_ref[h])
        pa_ref[...], pb_ref[...], pc_ref[...] = pa, pb, pc
        m = _sigmoid(ga_ref[...]) * pa + _sigmoid(gb_ref[...]) * pb + _sigmoid(gc_ref[...]) * pc
        m_ref[...] = m.astype(m_ref.dtype)

    g0 = P_GATE // D
    sds = jax.ShapeDtypeStruct
    return pl.pallas_call(
        body, name=name, grid=(S // t,),
        in_specs=[_row_spec(t, D, g0), _row_spec(t, D, g0 + 1), _row_spec(t, D, g0 + 2), _row_spec(t, GM_W),
                  pl.BlockSpec((HEADS, t, VD), lambda i: (0, i, 0)), _row_spec(t, LRU_W),
                  _full_spec((GM_W, D)), _full_spec((HEADS, VD, D)), _full_spec((LRU_W, D))],
        out_specs=[_row_spec(t, D)] * 4,
        out_shape=[sds((S, D), MXU), sds((S, D), F32), sds((S, D), F32), sds((S, D), F32)],
        compiler_params=_params(("parallel",)),
    )(z, z, z, ya, o, yc, wa, wb, wc)


def _dz_alias(n_inputs):
    return {n_inputs - 1: 0}


def _branch_bwd(z, dm, pa, pb, pc, name):
    S = z.shape[0]
    t = min(ROW_TILE, S)

    def body(ga_ref, gb_ref, gc_ref, dm_ref, pa_ref, pb_ref, pc_ref, dz_ref, da_ref, db_ref, dc_ref, sum_ref):
        first = pl.program_id(0) == 0
        dm_ = dm_ref[...]
        for n_, (g_ref, p_ref, d_ref) in enumerate(((ga_ref, pa_ref, da_ref), (gb_ref, pb_ref, db_ref), (gc_ref, pc_ref, dc_ref))):
            gt = _sigmoid(g_ref[...])
            d_ref[...] = (dm_ * gt).astype(d_ref.dtype)
            dzg = dm_ * p_ref[...] * gt * (1.0 - gt)
            dz_ref[:, n_ * D:(n_ + 1) * D] = dzg.astype(dz_ref.dtype)
            _acc(sum_ref.at[:, n_ * D:(n_ + 1) * D], first, _colsum(dzg))

    g0 = P_GATE // D
    sds = jax.ShapeDtypeStruct
    return pl.pallas_call(
        body, name=name, grid=(S // t,),
        in_specs=[_row_spec(t, D, g0), _row_spec(t, D, g0 + 1), _row_spec(t, D, g0 + 2)] + [_row_spec(t, D)] * 4,
        out_specs=[_row_spec(t, 3 * D, P_GATE // (3 * D))] + [_row_spec(t, D)] * 3 + [_vec_spec(3 * D)],
        out_shape=[sds((S, NP), MXU)] + [sds((S, D), MXU)] * 3 + [sds((1, 3 * D), F32)],
        compiler_params=_params(("arbitrary",)),
    )(z, z, z, dm, pa, pb, pc)


def _heads_bwd(dpb, o, wb, name):
    S = dpb.shape[0]
    t = min(ROW_TILE, S)

    def body(dp_ref, o_ref, wb_ref, do_ref, dwb_ref):
        first = pl.program_id(0) == 0
        dp = dp_ref[...]
        for h in range(HEADS):
            do_ref[h] = _dot_nt(dp, wb_ref[h])
            _acc(dwb_ref.at[h], first, _dot_tn(o_ref[h], dp))

    hspec = pl.BlockSpec((HEADS, t, VD), lambda i: (0, i, 0))
    sds = jax.ShapeDtypeStruct
    return pl.pallas_call(
        body, name=name, grid=(S // t,),
        in_specs=[_row_spec(t, D), hspec, _full_spec((HEADS, VD, D))],
        out_specs=[hspec, _full_spec((HEADS, VD, D))],
        out_shape=[sds((HEADS, S, VD), F32), sds((HEADS, VD, D), F32)],
        compiler_params=_params(("arbitrary",)),
    )(dpb, o, wb)


def _mod_fwd(c_all, ada_w, name):
    n = ada_w.shape[2]

    def body(c_ref, w_ref, o_ref):
        c = c_ref[...]
        ca = (c * _sigmoid(c)).astype(MXU)
        for l in range(DEPTH):
            o_ref[l] = _dot(ca, w_ref[l].astype(MXU))

    return pl.pallas_call(body, name=name, out_shape=jax.ShapeDtypeStruct((DEPTH, 8, n), F32), compiler_params=_params())(c_all, ada_w)


def _ada_w_grad(c_all_t, dmod, name):
    n = dmod.shape[2]

    def body(c_ref, d_ref, o_ref):
        c = c_ref[...]
        ca = c * _sigmoid(c)
        for l in range(DEPTH):
            dm = d_ref[l]
            acc = ca[:, 0:1] * dm[0:1, :]
            for b in range(1, 8):
                acc = acc + ca[:, b:b + 1] * dm[b:b + 1, :]
            o_ref[l] = acc

    return pl.pallas_call(body, name=name, out_shape=jax.ShapeDtypeStruct((DEPTH, D, n), F32), compiler_params=_params())(c_all_t, dmod)


def _adamw(w, g, m, v, name):
    R, C = w.shape
    t = _tile(R, 256) if R % 8 == 0 else R
    c1, c2 = 1.0 - ADAM_B1 ** ADAM_STEP, 1.0 - ADAM_B2 ** ADAM_STEP

    def body(w_ref, g_ref, m_ref, v_ref, d_ref, nm_ref, nv_ref):
        g_ = g_ref[...]
        m_ = ADAM_B1 * m_ref[...] + (1.0 - ADAM_B1) * g_
        v_ = ADAM_B2 * v_ref[...] + (1.0 - ADAM_B2) * (g_ * g_)
        nm_ref[...] = m_
        nv_ref[...] = v_
        d_ref[...] = -ADAM_LR * ((m_ / c1) / (jnp.sqrt(v_ / c2) + ADAM_EPS) + ADAM_WD * w_ref[...])

    spec = pl.BlockSpec((t, C), lambda i: (i, 0))
    return pl.pallas_call(
        body, name=name, grid=(R // t,), in_specs=[spec] * 4, out_specs=[spec] * 3,
        out_shape=[jax.ShapeDtypeStruct((R, C), F32)] * 3, compiler_params=_params(("parallel",)),
    )(w, g, m, v)


def _sum8(a, name):
    _, R, C = a.shape
    t = _tile(R, 512) if R % 8 == 0 else R

    def body(a_ref, o_ref):
        s = a_ref[0]
        for k in range(1, 8):
            s = s + a_ref[k]
        o_ref[...] = s

    return pl.pallas_call(
        body, name=name, grid=(R // t,), in_specs=[pl.BlockSpec((8, t, C), lambda i: (0, i, 0))],
        out_specs=pl.BlockSpec((t, C), lambda i: (i, 0)), out_shape=jax.ShapeDtypeStruct((R, C), F32),
        compiler_params=_params(("parallel",)),
    )(a)


def _pair_add(shards, from_sib, half, name):
    _, rh, C = from_sib.shape
    t = _tile(rh, 384, 16)
    nb = rh // t

    def body(half_ref, a0, a1, a2, a3, b_ref, o_ref, ob_ref):
        for k, a_ref in enumerate((a0, a1, a2, a3)):
            s = a_ref[...] + b_ref[k]
            o_ref[k] = s
            ob_ref[k] = s.astype(ob_ref.dtype)

    a_spec = pl.BlockSpec((t, C), lambda i, half_ref: (half_ref[0] * nb + i, 0))
    s_spec = pl.BlockSpec((4, t, C), lambda i, half_ref: (0, i, 0))
    return pl.pallas_call(
        body, name=name,
        grid_spec=pltpu.PrefetchScalarGridSpec(num_scalar_prefetch=1, grid=(nb,), in_specs=[a_spec] * 4 + [s_spec],
                                               out_specs=[s_spec, s_spec]),
        out_shape=[jax.ShapeDtypeStruct((4, rh, C), F32), jax.ShapeDtypeStruct((4, rh, C), jnp.bfloat16)],
        compiler_params=_params(("parallel",)),
    )(half, *shards, from_sib)


def _chip_sum(own, recv, name):
    R, C = own.shape
    t = _tile(R, 512, 16)

    def body(a_ref, r_ref, o_ref):
        s = a_ref[...]
        for k in range(3):
            s = s + r_ref[k].astype(F32)
        o_ref[...] = s

    return pl.pallas_call(
        body, name=name, grid=(R // t,),
        in_specs=[pl.BlockSpec((t, C), lambda i: (i, 0)), pl.BlockSpec((3, t, C), lambda i: (0, i, 0))],
        out_specs=pl.BlockSpec((t, C), lambda i: (i, 0)), out_shape=jax.ShapeDtypeStruct((R, C), F32),
        compiler_params=_params(("parallel",)),
    )(own, recv)


_ANY = pl.BlockSpec(memory_space=pl.ANY)


def _coords():
    return lax.axis_index("x"), lax.axis_index("y"), lax.axis_index("c")


def _allgather8(v, name):
    R, C = v.shape

    def body(v_ref, out_ref, send_sems, recv_sems):
        x, y, c = _coords()
        me = 4 * x + 2 * y + c
        sends = []
        for k in range(1, 8):
            bx, by, bc = (k >> 2) & 1, (k >> 1) & 1, k & 1
            peer = (x ^ bx, y ^ by, c ^ bc)
            cp = pltpu.make_async_remote_copy(src_ref=v_ref, dst_ref=out_ref.at[me], send_sem=send_sems.at[k - 1],
                                              recv_sem=recv_sems.at[k - 1], device_id=peer, device_id_type=MESH)
            cp.start()
            sends.append(cp)
        for k in range(1, 8):
            bx, by, bc = (k >> 2) & 1, (k >> 1) & 1, k & 1
            peer = (x ^ bx, y ^ by, c ^ bc)
            src = 4 * peer[0] + 2 * peer[1] + peer[2]
            pltpu.make_async_remote_copy(src_ref=v_ref, dst_ref=out_ref.at[src], send_sem=send_sems.at[k - 1],
                                         recv_sem=recv_sems.at[k - 1], device_id=peer, device_id_type=MESH).wait_recv()
        for cp in sends:
            cp.wait_send()

    out = pl.pallas_call(
        body, name=name, in_specs=[_ANY], out_specs=_ANY, out_shape=jax.ShapeDtypeStruct((8, R, C), v.dtype),
        scratch_shapes=[pltpu.SemaphoreType.DMA((7,)), pltpu.SemaphoreType.DMA((7,))],
    )(v)
    x, y, c = _coords()
    return lax.dynamic_update_slice(out, v[None], (4 * x + 2 * y + c, 0, 0))


def _chip_patterns(x, y):
    return [(1 - x, y), (x, 1 - y), (1 - x, 1 - y)]


def _allgather_weights(w, name):
    R, C = w.shape
    rh = R // 2

    def body(w_ref, out_ref, send_sems, recv_sems):
        x, y, c = _coords()
        kme = 2 * x + y
        sibling = (x, y, 1 - c)
        mine_half = pl.ds(pl.multiple_of(c * rh, FLAT_ALIGN // 2), rh)
        other_half = pl.ds(pl.multiple_of((1 - c) * rh, FLAT_ALIGN // 2), rh)
        chips = _chip_patterns(x, y)

        def copy(k, chip_idx, half, to, src=None):
            dst = out_ref.at[chip_idx, half, :]
            return pltpu.make_async_remote_copy(src_ref=dst if src is None else src, dst_ref=dst, send_sem=send_sems.at[k],
                                                recv_sem=recv_sems.at[k], device_id=to, device_id_type=MESH)

        first = [copy(j, kme, mine_half, (cx, cy, c), src=w_ref.at[mine_half, :]) for j, (cx, cy) in enumerate(chips)]
        own = pltpu.make_async_remote_copy(src_ref=w_ref, dst_ref=out_ref.at[kme], send_sem=send_sems.at[6], recv_sem=recv_sems.at[6],
                                           device_id=sibling, device_id_type=MESH)
        first.append(own)
        for cp in first:
            cp.start()
        passed = []
        for j, (cx, cy) in enumerate(chips):
            kj = 2 * cx + cy
            copy(j, kj, mine_half, (x, y, c)).wait_recv()
            fw = copy(3 + j, kj, mine_half, sibling)
            fw.start()
            passed.append(fw)
        for j, (cx, cy) in enumerate(chips):
            copy(3 + j, 2 * cx + cy, other_half, (x, y, c)).wait_recv()
        own.wait_recv()
        for cp in first + passed:
            cp.wait_send()

    return pl.pallas_call(
        body, name=name, in_specs=[_ANY], out_specs=_ANY, out_shape=jax.ShapeDtypeStruct((4, R, C), w.dtype),
        scratch_shapes=[pltpu.SemaphoreType.DMA((7,)), pltpu.SemaphoreType.DMA((7,))],
    )(w)


def _sibling_swap(shards, name):
    R, C = shards[0].shape
    rh = R // 2

    def body(g0, g1, g2, g3, out_ref, send_sems, recv_sems):
        x, y, c = _coords()
        other_half = pl.ds(pl.multiple_of((1 - c) * rh, 8), rh)
        cps = []
        for k, g_ref in enumerate((g0, g1, g2, g3)):
            cp = pltpu.make_async_remote_copy(src_ref=g_ref.at[other_half, :], dst_ref=out_ref.at[k], send_sem=send_sems.at[k],
                                              recv_sem=recv_sems.at[k], device_id=(x, y, 1 - c), device_id_type=MESH)
            cp.start()
            cps.append(cp)
        for cp in cps:
            cp.wait()

    return pl.pallas_call(
        body, name=name, in_specs=[_ANY] * 4, out_specs=_ANY, out_shape=jax.ShapeDtypeStruct((4, rh, C), shards[0].dtype),
        scratch_shapes=[pltpu.SemaphoreType.DMA((4,)), pltpu.SemaphoreType.DMA((4,))],
    )(*shards)


def _chip_scatter(a, name):
    _, R, C = a.shape

    def body(a_ref, out_ref, send_sems, recv_sems):
        x, y, c = _coords()
        cps = []
        for j, (cx, cy) in enumerate(_chip_patterns(x, y)):
            cp = pltpu.make_async_remote_copy(src_ref=a_ref.at[2 * cx + cy], dst_ref=out_ref.at[j], send_sem=send_sems.at[j],
                                              recv_sem=recv_sems.at[j], device_id=(cx, cy, c), device_id_type=MESH)
            cp.start()
            cps.append(cp)
        for cp in cps:
            cp.wait()

    return pl.pallas_call(
        body, name=name, in_specs=[_ANY], out_specs=_ANY, out_shape=jax.ShapeDtypeStruct((3, R, C), a.dtype),
        scratch_shapes=[pltpu.SemaphoreType.DMA((3,)), pltpu.SemaphoreType.DMA((3,))],
    )(a)


def _sibling_join(f, name):
    rh, C = f.shape

    def body(f_ref, out_ref, send_sem, recv_sem):
        x, y, c = _coords()
        mine_half = pl.ds(pl.multiple_of(c * rh, 8), rh)
        cp = pltpu.make_async_remote_copy(src_ref=f_ref, dst_ref=out_ref.at[mine_half, :], send_sem=send_sem, recv_sem=recv_sem,
                                          device_id=(x, y, 1 - c), device_id_type=MESH)
        cp.start()
        cp.wait()

    out = pl.pallas_call(
        body, name=name, in_specs=[_ANY], out_specs=_ANY, out_shape=jax.ShapeDtypeStruct((2 * rh, C), f.dtype),
        scratch_shapes=[pltpu.SemaphoreType.DMA(()), pltpu.SemaphoreType.DMA(())],
    )(f)
    return lax.dynamic_update_slice(out, f, (lax.axis_index("c") * rh, 0))


_BIG = (("in_w", 2), ("mla_wuq", 2), ("mla_wukv", 2), ("branch_w", 3), ("mix_out_w", 1), ("ffn_w1", 2), ("ffn_w2", 1))


def _pack(shards):
    flat = jnp.concatenate([shards[n].reshape(-1, FLAT_W) for n, _ in _BIG], axis=0)
    pad = (-flat.shape[0]) % FLAT_ALIGN
    return jnp.pad(flat, ((0, pad), (0, 0)))


def _unpack(flat, shapes):
    out, r = {}, 0
    for n, _ in _BIG:
        rows = math.prod(shapes[n]) // FLAT_W
        out[n] = flat[r:r + rows].reshape(shapes[n])
        r += rows
    return out


def _swap16(a):
    return jnp.concatenate([a[..., 16:32], a[..., 0:16]], axis=-1)


def _pad_in_cols(w):
    lead = w.shape[:-1]
    z = lambda n: jnp.zeros(lead + (n,), w.dtype)
    kr = w[..., O_KR:O_LX]
    return jnp.concatenate([w[..., O_GATE:], w[..., O_GM:O_Q], w[..., O_LX:O_LG], w[..., O_LG:O_GATE], w[..., O_Q:O_KV],
                            w[..., O_KV:O_KR], z(64), kr, z(32), z(64), _swap16(kr), z(32)], axis=-1)


def _unpad_in_cols(g):
    kr = g[..., P_KA + 64:P_KA + 96] + _swap16(g[..., P_KB + 64:P_KB + 96])
    return jnp.concatenate([g[..., P_GM:P_LX], g[..., P_Q:P_KV], g[..., P_KV:P_KA], kr, g[..., P_LX:P_LG], g[..., P_LG:P_Q],
                            g[..., P_GATE:P_GM]], axis=-1)


def _flat128(vecs):
    flat = jnp.concatenate([v.reshape(-1) for v in vecs])
    pad = (-flat.shape[0]) % 1024
    return jnp.pad(flat, (0, pad)).reshape(-1, 128)


def _unflat(flat, shapes):
    flat = flat.reshape(-1)
    out, r = [], 0
    for s in shapes:
        n = math.prod(s)
        out.append(flat[r:r + n].reshape(s))
        r += n
    return out


_SMALL = ("ada_b", "in_b", "gm_ln_g", "gm_ln_b", "gm_ws", "gm_bs", "mla_qnorm_g", "mla_kvnorm_g", "lru_conv_w", "lru_conv_b",
          "lru_wr", "lru_br", "lru_wi", "lru_bi", "lru_lambda", "ffn_b1", "ffn_b2", "ln_g", "ln_b")
_NAMES = ("ada_w", "ada_b", "in_w", "in_b", "gm_ln_g", "gm_ln_b", "gm_ws", "gm_bs", "mla_qnorm_g", "mla_wuq", "mla_kvnorm_g",
          "mla_wukv", "lru_conv_w", "lru_conv_b", "lru_wr", "lru_br", "lru_wi", "lru_bi", "lru_lambda", "branch_w", "mix_out_w",
          "ffn_w1", "ffn_b1", "ffn_w2", "ffn_b2", "ln_g", "ln_b")


def kernel(x, c, ada_w, ada_b, in_w, in_b, gm_ln_g, gm_ln_b, gm_ws, gm_bs, mla_qnorm_g, mla_wuq, mla_kvnorm_g, mla_wukv, lru_conv_w, lru_conv_b, lru_wr, lru_br, lru_wi, lru_bi, lru_lambda, branch_w, mix_out_w, ffn_w1, ffn_b1, ffn_w2, ffn_b2, ln_g, ln_b, loss_target, m_ada_w, m_ada_b, m_in_w, m_in_b, m_gm_ln_g, m_gm_ln_b, m_gm_ws, m_gm_bs, m_mla_qnorm_g, m_mla_wuq, m_mla_kvnorm_g, m_mla_wukv, m_lru_conv_w, m_lru_conv_b, m_lru_wr, m_lru_br, m_lru_wi, m_lru_bi, m_lru_lambda, m_branch_w, m_mix_out_w, m_ffn_w1, m_ffn_b1, m_ffn_w2, m_ffn_b2, m_ln_g, m_ln_b, v_ada_w, v_ada_b, v_in_w, v_in_b, v_gm_ln_g, v_gm_ln_b, v_gm_ws, v_gm_bs, v_mla_qnorm_g, v_mla_wuq, v_mla_kvnorm_g, v_mla_wukv, v_lru_conv_w, v_lru_conv_b, v_lru_wr, v_lru_br, v_lru_wi, v_lru_bi, v_lru_lambda, v_branch_w, v_mix_out_w, v_ffn_w1, v_ffn_b1, v_ffn_w2, v_ffn_b2, v_ln_g, v_ln_b):
    W = dict(ada_w=ada_w, ada_b=ada_b, in_w=in_w, in_b=in_b, gm_ln_g=gm_ln_g, gm_ln_b=gm_ln_b, gm_ws=gm_ws, gm_bs=gm_bs,
             mla_qnorm_g=mla_qnorm_g, mla_wuq=mla_wuq, mla_kvnorm_g=mla_kvnorm_g, mla_wukv=mla_wukv, lru_conv_w=lru_conv_w,
             lru_conv_b=lru_conv_b, lru_wr=lru_wr, lru_br=lru_br, lru_wi=lru_wi, lru_bi=lru_bi, lru_lambda=lru_lambda,
             branch_w=branch_w, mix_out_w=mix_out_w, ffn_w1=ffn_w1, ffn_b1=ffn_b1, ffn_w2=ffn_w2, ffn_b2=ffn_b2, ln_g=ln_g, ln_b=ln_b)
    M = dict(ada_w=m_ada_w, ada_b=m_ada_b, in_w=m_in_w, in_b=m_in_b, gm_ln_g=m_gm_ln_g, gm_ln_b=m_gm_ln_b, gm_ws=m_gm_ws,
             gm_bs=m_gm_bs, mla_qnorm_g=m_mla_qnorm_g, mla_wuq=m_mla_wuq, mla_kvnorm_g=m_mla_kvnorm_g, mla_wukv=m_mla_wukv,
             lru_conv_w=m_lru_conv_w, lru_conv_b=m_lru_conv_b, lru_wr=m_lru_wr, lru_br=m_lru_br, lru_wi=m_lru_wi, lru_bi=m_lru_bi,
             lru_lambda=m_lru_lambda, branch_w=m_branch_w, mix_out_w=m_mix_out_w, ffn_w1=m_ffn_w1, ffn_b1=m_ffn_b1, ffn_w2=m_ffn_w2,
             ffn_b2=m_ffn_b2, ln_g=m_ln_g, ln_b=m_ln_b)
    V = dict(ada_w=v_ada_w, ada_b=v_ada_b, in_w=v_in_w, in_b=v_in_b, gm_ln_g=v_gm_ln_g, gm_ln_b=v_gm_ln_b, gm_ws=v_gm_ws,
             gm_bs=v_gm_bs, mla_qnorm_g=v_mla_qnorm_g, mla_wuq=v_mla_wuq, mla_kvnorm_g=v_mla_kvnorm_g, mla_wukv=v_mla_wukv,
             lru_conv_w=v_lru_conv_w, lru_conv_b=v_lru_conv_b, lru_wr=v_lru_wr, lru_br=v_lru_br, lru_wi=v_lru_wi, lru_bi=v_lru_bi,
             lru_lambda=v_lru_lambda, branch_w=v_branch_w, mix_out_w=v_mix_out_w, ffn_w1=v_ffn_w1, ffn_b1=v_ffn_b1, ffn_w2=v_ffn_w2,
             ffn_b2=v_ffn_b2, ln_g=v_ln_g, ln_b=v_ln_b)

    S = x.shape[1]
    xi, yi, ci = _coords()
    kme = 2 * xi + yi
    me = 4 * xi + 2 * yi + ci
    x0 = x[0]
    tgt = loss_target[0]

    shard_shapes = {n: W[n].shape for n, _ in _BIG}
    gathered = _allgather_weights(_pack({n: W[n].astype(MXU) for n, _ in _BIG}), "ag_weights")
    parts = [_unpack(gathered[k], shard_shapes) for k in range(4)]
    full = {n: jnp.concatenate([p[n] for p in parts], axis=ax) for n, ax in _BIG}

    small1 = _allgather8(_flat128([c, ln_g, ln_b, lru_conv_w]), "ag_small")
    per_dev = [_unflat(small1[d], [c.shape, ln_g.shape, ln_b.shape, lru_conv_w.shape]) for d in range(8)]
    c_all = jnp.concatenate([p[0] for p in per_dev], axis=0)
    chip = lambda i: [per_dev[2 * k][i] for k in range(4)]
    ln_g_f, ln_b_f, conv_w_f = (jnp.concatenate(chip(1), axis=2), jnp.concatenate(chip(2), axis=2), jnp.concatenate(chip(3), axis=2))

    mod_sh = _mod_fwd(c_all, ada_w, "mod_fwd")
    mod_all = _allgather8(_flat128([mod_sh]), "ag_mod")
    mod_parts = [_unflat(mod_all[2 * k], [mod_sh.shape])[0] for k in range(4)]
    mod = jnp.concatenate(mod_parts, axis=2)
    mod = lax.dynamic_index_in_dim(mod, me, axis=1, keepdims=False) + ada_b

    pos = jnp.arange(S, dtype=F32)
    inv = ROPE_BASE ** (-jnp.arange(0, ROPE, 2, dtype=F32) / ROPE)
    ang = pos[:, None] * inv[None, :]
    cs, sn = jnp.cos(ang), jnp.sin(ang)
    zc = lambda n: jnp.zeros((S, n), F32)
    cos_t = jnp.concatenate([zc(64), cs, cs, zc(32)], axis=1)
    sin_t = jnp.concatenate([zc(64), -sn, sn, zc(32)], axis=1)

    win_p = _pad_in_cols(full["in_w"])
    bin_p = _pad_in_cols(in_b)
    wuq = full["mla_wuq"].reshape(DEPTH, Q_RANK, HEADS, NOPE + ROPE)
    zq = lambda n: jnp.zeros((DEPTH, Q_RANK, HEADS, n), MXU)
    w1q = jnp.concatenate([wuq, zq(32)], axis=-1).reshape(DEPTH, Q_RANK, D)
    w2q = jnp.concatenate([zq(64), _swap16(wuq[..., NOPE:]), zq(32)], axis=-1).reshape(DEPTH, Q_RANK, D)
    wukv = full["mla_wukv"].reshape(DEPTH, KV_RANK, HEADS, NOPE + VD)
    wk_p = jnp.concatenate([wukv[..., :NOPE], jnp.zeros((DEPTH, KV_RANK, HEADS, 64), MXU)], axis=-1).reshape(DEPTH, KV_RANK, D)
    wv_h = wukv[..., NOPE:].transpose(0, 2, 1, 3)
    wvt_h = wukv[..., NOPE:].transpose(0, 2, 3, 1)
    wb_h = full["branch_w"][:, 1].reshape(DEPTH, HEADS, VD, D)
    wr_f = jnp.stack([block_diag(*[lru_wr[l, b] for b in range(LRU_NB)]) for l in range(DEPTH)]).astype(MXU)
    wi_f = jnp.stack([block_diag(*[lru_wi[l, b] for b in range(LRU_NB)]) for l in range(DEPTH)]).astype(MXU)
    ws_b = gm_ws.astype(MXU)
    wst_b = gm_ws.transpose(0, 1, 3, 2).astype(MXU)
    bs_t = gm_bs.transpose(0, 2, 1)
    row = lambda v: v.reshape(1, -1)

    saved = []
    xs = x0
    h1 = None
    for l in range(DEPTH):
        sh1, sc1, g1, sh2, sc2, g2 = [row(mod[l, i * D:(i + 1) * D]) for i in range(6)]
        if l == 0:
            h1 = _lnmod_fwd(xs, sc1, sh1, f"lnmod_fwd{l}")
        z = _mm(h1, win_p[l], "nn", f"in_proj{l}", rows=[row(bin_p[l])], epi=lambda acc, b: (acc + b,))
        ya = _gmlp_fwd(z, row(gm_ln_g[l]), row(gm_ln_b[l]), ws_b[l], bs_t[l], f"gmlp_fwd{l}")
        qf, kf, vtf = _mla_prep_fwd(z, cos_t, sin_t, row(mla_qnorm_g[l]), row(mla_kvnorm_g[l]), w1q[l], w2q[l], wk_p[l], wv_h[l],
                                    wvt_h[l], f"mla_prep_fwd{l}")
        ob, oext = _attn_fwd(qf, kf, vtf, f"attn_fwd{l}")
        lru_args = (conv_w_f[l], row(lru_conv_b[l]), wr_f[l], row(lru_br[l]), wi_f[l], row(lru_bi[l]), row(lru_lambda[l]))
        yc, hl = _lru_fwd(z, *lru_args, f"lru_fwd{l}")
        merged, pa, pb, pc = _branch_fwd(z, ya, ob, yc, full["branch_w"][l, 0], wb_h[l], full["branch_w"][l, 2], f"branch_fwd{l}")
        mix = _mm(merged, full["mix_out_w"][l], "nn", f"mix_out{l}")
        x1, h2 = _res_ln_fwd(xs, mix, g1, row(ln_g_f[l, 0]), row(ln_b_f[l, 0]), f"res_ln_a{l}", mod=(sc2, sh2))
        a1, r2 = _mm(h2, full["ffn_w1"][l], "nn", f"ffn_up{l}", out_dtypes=(F32, MXU), rows=[row(ffn_b1[l])],
                     epi=lambda acc, b: (acc + b, jnp.square(jnp.maximum(acc + b, 0.0))))
        f = _mm(r2, full["ffn_w2"][l], "nn", f"ffn_down{l}", rows=[row(ffn_b2[l])], epi=lambda acc, b: (acc + b,))
        if l + 1 < DEPTH:
            nsh1, nsc1 = row(mod[l + 1, 0:D]), row(mod[l + 1, D:2 * D])
            x2, h1n = _res_ln_fwd(x1, f, g2, row(ln_g_f[l, 1]), row(ln_b_f[l, 1]), f"res_ln_b{l}", mod=(nsc1, nsh1))
        else:
            (x2,), h1n = _res_ln_fwd(x1, f, g2, row(ln_g_f[l, 1]), row(ln_b_f[l, 1]), f"res_ln_b{l}"), None
        saved.append(dict(x_in=xs, h1=h1, z=z, ya=ya, qf=qf, kf=kf, vtf=vtf, ob=ob, oext=oext, yc=yc, hl=hl, merged=merged,
                          pa=pa, pb=pb, pc=pc, mix=mix, x1=x1, h2=h2, a1=a1, r2=r2, f=f, x2=x2, lru_args=lru_args,
                          mods=(sh1, sc1, g1, sh2, sc2, g2)))
        xs, h1 = x2, h1n

    dy, sq = _loss_fwd(xs, tgt)
    loss = lax.psum(0.5 / D * jnp.sum(sq), ("x", "y", "c"))

    G = {}
    dmods = []
    dres, hpath = dy, None
    for l in reversed(range(DEPTH)):
        sv = saved[l]
        sh1, sc1, g1, sh2, sc2, g2 = sv["mods"]
        gl = {}
        nb = _node_bwd(f"node_b{l}", dres, hpath=hpath, upath=(sv["x1"], sv["f"], g2, row(ln_g_f[l, 1])))
        if hpath is not None:
            dmods[-1]["sc1"], dmods[-1]["sh1"] = nb["dsc"], nb["dsh"]
        dm = dict(g2=nb["dg"])
        ln_g_l1, ln_b_l1 = nb["dlng"], nb["dlnb"]
        df = nb["dbr"]
        gl["ffn_b2"] = nb["dbrsum"][0]
        gl["ffn_w2"] = _mm(sv["r2"], df, "tn", f"d_ffn_w2_{l}")
        da1, db1 = _mm(df, full["ffn_w2"][l], "nt", f"d_ffn_act{l}", out_dtypes=(MXU,), extras=[sv["a1"]], colsum=True,
                       epi=lambda acc, a: (acc * (2.0 * jnp.maximum(a, 0.0)),))
        gl["ffn_b1"] = db1[0]
        gl["ffn_w1"] = _mm(sv["h2"], da1, "tn", f"d_ffn_w1_{l}")
        dh2 = _mm(da1, full["ffn_w1"][l], "nt", f"d_ffn_in{l}")
        na = _node_bwd(f"node_a{l}", nb["du"], hpath=(dh2, sv["x1"], sc2), upath=(sv["x_in"], sv["mix"], g1, row(ln_g_f[l, 0])))
        dm.update(sc2=na["dsc"], sh2=na["dsh"], g1=na["dg"])
        gl["ln_g"] = jnp.concatenate([na["dlng"], ln_g_l1], axis=0)
        gl["ln_b"] = jnp.concatenate([na["dlnb"], ln_b_l1], axis=0)
        dmix = na["dbr"]
        gl["mix_out_w"] = _mm(sv["merged"], dmix, "tn", f"d_mix_w{l}")
        dmerged = _mm(dmix, full["mix_out_w"][l], "nt", f"d_merged{l}")
        dz, dpa, dpb, dpc, s_gate = _branch_bwd(sv["z"], dmerged, sv["pa"], sv["pb"], sv["pc"], f"branch_bwd{l}")
        wa, wc = full["branch_w"][l, 0], full["branch_w"][l, 2]
        dya = _mm(dpa, wa, "nt", f"d_ya{l}")
        dyc = _mm(dpc, wc, "nt", f"d_yc{l}")
        dwa = _mm(sv["ya"], dpa, "tn", f"d_wa{l}")
        dwc = _mm(sv["yc"], dpc, "tn", f"d_wc{l}")
        do32, dwb = _heads_bwd(dpb, sv["ob"], wb_h[l], f"heads_bwd{l}")
        gl["branch_w"] = jnp.stack([dwa, dwb.reshape(GM_W, D), dwc])
        dz, dcw, dcb, dwr, dbr_, dwi, dbi, dlam, s_lru = _lru_bwd(sv["z"], sv["hl"], dyc, *sv["lru_args"], dz, f"lru_bwd{l}")
        gl["lru_conv_w"], gl["lru_conv_b"], gl["lru_br"], gl["lru_bi"], gl["lru_lambda"] = dcw, dcb[0], dbr_[0], dbi[0], dlam[0]
        blocks = lambda m: jnp.stack([m[b * 64:(b + 1) * 64, b * 64:(b + 1) * 64] for b in range(LRU_NB)])
        gl["lru_wr"], gl["lru_wi"] = blocks(dwr), blocks(dwi)
        qa, qt, doa, dot_ = _attn_bwd_prep(sv["qf"], sv["oext"], do32, f"attn_bwd_prep{l}")
        dqf, dkf, dvf = _attn_bwd(qa, qt, doa, dot_, sv["kf"], sv["vtf"], f"attn_bwd{l}")
        dz, dgq, dgkv, dw1, dw2, dwk, dwv, s_mla = _mla_prep_bwd(sv["z"], cos_t, sin_t, row(mla_qnorm_g[l]), row(mla_kvnorm_g[l]),
                                                                 w1q[l], w2q[l], wk_p[l], wv_h[l], dqf, dkf, dvf, dz,
                                                                 f"mla_prep_bwd{l}")
        gl["mla_qnorm_g"], gl["mla_kvnorm_g"] = dgq[0], dgkv[0]
        dw1 = dw1.reshape(Q_RANK, HEADS, 128)
        dw2 = dw2.reshape(Q_RANK, HEADS, 128)
        gl["mla_wuq"] = jnp.concatenate([dw1[..., :NOPE], dw1[..., NOPE:NOPE + ROPE] + _swap16(dw2[..., NOPE:NOPE + ROPE])],
                                        axis=-1).reshape(Q_RANK, HEADS * (NOPE + ROPE))
        gl["mla_wukv"] = jnp.concatenate([dwk.reshape(KV_RANK, HEADS, 128)[..., :NOPE], dwv.transpose(1, 0, 2)],
                                         axis=-1).reshape(KV_RANK, HEADS * (NOPE + VD))
        dz, dglg, dglb, dws, dbs, s_gm = _gmlp_bwd(sv["z"], dya, row(gm_ln_g[l]), row(gm_ln_b[l]), ws_b[l], wst_b[l], bs_t[l], dz,
                                                   f"gmlp_bwd{l}")
        gl["gm_ln_g"], gl["gm_ln_b"], gl["gm_ws"], gl["gm_bs"] = dglg[0], dglb[0], dws, dbs[:, :GM_G].T
        gl["in_b"] = _unpad_in_cols(jnp.concatenate([s_gate, s_gm, s_lru, s_mla], axis=1))[0]
        gl["in_w"] = _unpad_in_cols(_mm(sv["h1"], dz, "tn", f"d_in_w{l}"))
        dh1 = _mm(dz, win_p[l], "nt", f"d_h1_{l}")
        dmods.append(dm)
        dres, hpath = na["du"], (dh1, sv["x_in"], sc1)
        G[l] = gl
    n0 = _node_bwd("node_in", dres, hpath=hpath)
    dmods[-1]["sc1"], dmods[-1]["sh1"] = n0["dsc"], n0["dsh"]
    grad_x = n0["dx"][None]
    dmods = dmods[::-1]
    dmod = jnp.stack([jnp.concatenate([dmods[l][k] for k in ("sh1", "sc1", "g1", "sh2", "sc2", "g2")], axis=1)[0]
                      for l in range(DEPTH)])
    grads = {n: jnp.stack([G[l][n] for l in range(DEPTH)]) for n in G[0]}
    grads["ada_b"] = dmod

    gsh = [_pack({n: jnp.split(grads[n], 4, axis=ax)[k] for n, ax in _BIG}) for k in range(4)]
    from_sib = _sibling_swap(gsh, "rs_pair")
    pair, pair_b = _pair_add(gsh, from_sib, ci.reshape(1).astype(jnp.int32), "rs_pair_add")
    from_chips = _chip_scatter(pair_b, "rs_chips")
    own = lax.dynamic_index_in_dim(pair, kme, axis=0, keepdims=False)
    half_sum = _chip_sum(own, from_chips, "rs_chip_sum")
    gflat = _sibling_join(half_sum, "rs_join")
    gbig = _unpack(gflat, shard_shapes)

    small_shapes = [grads[n].shape for n in _SMALL]
    gsm_all = _allgather8(_flat128([grads[n] for n in _SMALL]), "ag_small_grads")
    gsm = _unflat(_sum8(gsm_all, "sum_small_grads"), small_shapes)
    gsmall = dict(zip(_SMALL, gsm))
    dmod_all = jnp.stack([_unflat(gsm_all[d], small_shapes)[0] for d in range(8)], axis=1)
    dmod_sh = lax.dynamic_slice_in_dim(dmod_all, kme * (6 * D // 4), 6 * D // 4, axis=2)
    g_ada_w = _ada_w_grad(c_all.T, dmod_sh, "d_ada_w")
    quarter = lambda g, ax: lax.dynamic_slice_in_dim(g, kme * (g.shape[ax] // 4), g.shape[ax] // 4, axis=ax)
    gsmall["lru_conv_w"] = quarter(gsmall["lru_conv_w"], 2)
    gsmall["ln_g"] = quarter(gsmall["ln_g"], 2)
    gsmall["ln_b"] = quarter(gsmall["ln_b"], 2)

    grad = dict(gbig)
    grad.update(gsmall)
    grad["ada_w"] = g_ada_w

    delta, new_m, new_v = {}, {}, {}
    for n in ("ada_w",) + tuple(n for n, _ in _BIG):
        shp = W[n].shape
        two = lambda a: a.reshape(-1, shp[-1])
        d_, m_, v_ = _adamw(two(W[n]), two(grad[n]), two(M[n]), two(V[n]), f"adamw_{n}")
        delta[n], new_m[n], new_v[n] = d_.reshape(shp), m_.reshape(shp), v_.reshape(shp)
    sm_shapes = [W[n].shape for n in _SMALL]
    d_, m_, v_ = _adamw(_flat128([W[n] for n in _SMALL]), _flat128([grad[n] for n in _SMALL]), _flat128([M[n] for n in _SMALL]),
                        _flat128([V[n] for n in _SMALL]), "adamw_small")
    for n, a, b, c_ in zip(_SMALL, _unflat(d_, sm_shapes), _unflat(m_, sm_shapes), _unflat(v_, sm_shapes)):
        delta[n], new_m[n], new_v[n] = a, b, c_

    return (loss, grad_x, *[grad[n] for n in _NAMES], *[delta[n] for n in _NAMES], *[new_m[n] for n in _NAMES],
            *[new_v[n] for n in _NAMES])
```

```python
import functools
import math

import jax
import jax.numpy as jnp
from jax import lax
from jax.experimental import pallas as pl
from jax.experimental.pallas import tpu as pltpu
from jax.scipy.linalg import block_diag

F32 = jnp.float32
MXU = jnp.bfloat16
MESH = pl.DeviceIdType.MESH

D = 1024
DEPTH = 2
CHUNK = 64
GM_W = 512
GM_G = 4
HEADS = 8
Q_RANK = 256
KV_RANK = 128
NOPE = 64
ROPE = 32
VD = 64
LRU_W = 512
LRU_NB = 8
D_FF = 4096
ALPHA = (2.0 * DEPTH) ** 0.25
LN_EPS = 1e-5
RMS_EPS = 1e-6
ROPE_BASE = 10000.0
ATT_SCALE = (NOPE + ROPE) ** -0.5
N_IN = 5536
P_GATE, P_GM, P_LX, P_LG, P_Q, P_KV, P_KA, P_KB, NP = 0, 3072, 4096, 4608, 5120, 5376, 5504, 5632, 5760
O_GM, O_Q, O_KV, O_KR, O_LX, O_LG, O_GATE = 0, 1024, 1280, 1408, 1440, 1952, 2464

ADAM_LR, ADAM_B1, ADAM_B2, ADAM_EPS, ADAM_WD, ADAM_STEP = 0.001, 0.9, 0.999, 1e-08, 0.01, 10

ROW_TILE = 512
ATT_TILE = 512
VMEM_LIMIT = 56 * 1024 * 1024
FLAT_W = 1024
FLAT_ALIGN = 32


def _params(sem=None, vmem=VMEM_LIMIT):
    return pltpu.CompilerParams(dimension_semantics=sem, vmem_limit_bytes=vmem)


def _tile(dim, pref, mult=128):
    if dim <= pref:
        return dim
    t = (pref // mult) * mult
    while t >= mult:
        if dim % t == 0:
            return t
        t -= mult
    return dim


def _dot(a, b):
    return lax.dot_general(a, b, (((1,), (0,)), ((), ())), preferred_element_type=F32)


def _dot_nt(a, b):
    return lax.dot_general(a, b, (((1,), (1,)), ((), ())), preferred_element_type=F32)


def _dot_tn(a, b):
    return lax.dot_general(a, b, (((0,), (0,)), ((), ())), preferred_element_type=F32)


def _sigmoid(x):
    return 1.0 / (1.0 + jnp.exp(-x))


_GC = 0.7978845608028654


def _gelu(x):
    return 0.5 * x * (1.0 + jnp.tanh(_GC * (x + 0.044715 * x * x * x)))


def _gelu_grad(x):
    t = jnp.tanh(_GC * (x + 0.044715 * x * x * x))
    return 0.5 * (1.0 + t) + 0.5 * x * (1.0 - t * t) * _GC * (1.0 + 3.0 * 0.044715 * x * x)


def _ln_stats(x):
    mu = jnp.mean(x, axis=-1, keepdims=True)
    xc = x - mu
    r = lax.rsqrt(jnp.mean(xc * xc, axis=-1, keepdims=True) + LN_EPS)
    return xc * r, r


def _ln_bwd(dxh, xh, r):
    return r * (dxh - jnp.mean(dxh, axis=-1, keepdims=True) - xh * jnp.mean(dxh * xh, axis=-1, keepdims=True))


def _colsum(v):
    return jnp.sum(v, axis=0, keepdims=True)


def _acc(ref, first, val):
    @pl.when(first)
    def _():
        ref[...] = val

    @pl.when(jnp.logical_not(first))
    def _():
        ref[...] += val


def _row_spec(t, w, col=0):
    return pl.BlockSpec((t, w), lambda i, c=col: (i, c))


def _vec_spec(w):
    return pl.BlockSpec((1, w), lambda i: (0, 0))


def _full_spec(shape):
    nd = len(shape)
    return pl.BlockSpec(shape, lambda i, n=nd: (0,) * n)


def _mm(a, b, mode, name, out_dtypes=(F32,), extras=(), rows=(), epi=None, colsum=False, tm=1024, tn=1152, tk=2048):
    if mode == "nn":
        (M, K), N = a.shape, b.shape[1]
    elif mode == "nt":
        (M, K), N = a.shape, b.shape[0]
    else:
        (K, M), N = a.shape, b.shape[1]
    tm, tn, tk = _tile(M, tm), _tile(N, tn), _tile(K, tk)
    nk = K // tk
    dot = {"nn": _dot, "nt": _dot_nt, "tn": _dot_tn}[mode]
    a_spec = pl.BlockSpec((tk, tm), lambda i, j, k: (k, i)) if mode == "tn" else pl.BlockSpec((tm, tk), lambda i, j, k: (i, k))
    b_spec = pl.BlockSpec((tn, tk), lambda i, j, k: (j, k)) if mode == "nt" else pl.BlockSpec((tk, tn), lambda i, j, k: (k, j))
    o_spec = pl.BlockSpec((tm, tn), lambda i, j, k: (i, j))
    r_spec = pl.BlockSpec((1, tn), lambda i, j, k: (0, j))
    n_e, n_r, n_o = len(extras), len(rows), len(out_dtypes)
    if epi is None:
        epi = lambda acc: (acc,)

    def body(*refs):
        a_ref, b_ref = refs[0], refs[1]
        e_refs = refs[2:2 + n_e]
        r_refs = refs[2 + n_e:2 + n_e + n_r]
        o_refs = refs[2 + n_e + n_r:2 + n_e + n_r + n_o]
        p = dot(a_ref[...], b_ref[...])

        def finish(acc):
            outs = epi(acc, *[e[...] for e in e_refs], *[r[...] for r in r_refs])
            for o_ref, o in zip(o_refs, outs):
                o_ref[...] = o.astype(o_ref.dtype)
            if colsum:
                cs_ref = refs[2 + n_e + n_r + n_o]
                cols = pl.ds(pl.multiple_of(pl.program_id(1) * tn, 128), tn)
                _acc(cs_ref.at[:, cols], pl.program_id(0) == 0, _colsum(outs[0]))

        if nk == 1:
            finish(p)
        else:
            acc_ref = refs[-1]
            k = pl.program_id(2)
            _acc(acc_ref, k == 0, p)

            @pl.when(k == nk - 1)
            def _():
                finish(acc_ref[...])

    outs = pl.pallas_call(
        body, name=name, grid=(M // tm, N // tn, nk),
        in_specs=[a_spec, b_spec] + [o_spec] * n_e + [r_spec] * n_r,
        out_specs=[o_spec] * n_o + ([pl.BlockSpec((1, N), lambda i, j, k: (0, 0))] if colsum else []),
        out_shape=[jax.ShapeDtypeStruct((M, N), dt) for dt in out_dtypes] + ([jax.ShapeDtypeStruct((1, N), F32)] if colsum else []),
        scratch_shapes=[pltpu.VMEM((tm, tn), F32)] if nk > 1 else [],
        compiler_params=_params(("arbitrary",) * 3 if colsum else ("parallel", "parallel", "arbitrary")),
    )(a, b, *extras, *rows)
    return outs[0] if len(outs) == 1 else outs


def _lnmod_fwd(x, sc, sh, name):
    S = x.shape[0]
    t = min(ROW_TILE, S)

    def body(x_ref, sc_ref, sh_ref, h_ref):
        xh, _ = _ln_stats(x_ref[...])
        h_ref[...] = (xh * (1.0 + sc_ref[...]) + sh_ref[...]).astype(h_ref.dtype)

    return pl.pallas_call(
        body, name=name, grid=(S // t,),
        in_specs=[_row_spec(t, D), _vec_spec(D), _vec_spec(D)], out_specs=_row_spec(t, D),
        out_shape=jax.ShapeDtypeStruct((S, D), MXU), compiler_params=_params(("parallel",)),
    )(x, sc, sh)


def _res_ln_fwd(xprev, br, gvec, lng, lnb, name, mod=None):
    S = xprev.shape[0]
    t = min(ROW_TILE, S)
    with_h = mod is not None

    def body(*refs):
        xp_ref, br_ref, g_ref, lg_ref, lb_ref = refs[:5]
        u = ALPHA * xp_ref[...] + (1.0 + g_ref[...]) * br_ref[...]
        uh, _ = _ln_stats(u)
        xn = uh * lg_ref[...] + lb_ref[...]
        if with_h:
            sc_ref, sh_ref, xn_ref, h_ref = refs[5:]
            xh, _ = _ln_stats(xn)
            h_ref[...] = (xh * (1.0 + sc_ref[...]) + sh_ref[...]).astype(h_ref.dtype)
        else:
            xn_ref = refs[5]
        xn_ref[...] = xn

    ins = [xprev, br, gvec, lng, lnb] + (list(mod) if with_h else [])
    return pl.pallas_call(
        body, name=name, grid=(S // t,),
        in_specs=[_row_spec(t, D), _row_spec(t, D)] + [_vec_spec(D)] * (len(ins) - 2),
        out_specs=[_row_spec(t, D)] * (2 if with_h else 1),
        out_shape=[jax.ShapeDtypeStruct((S, D), F32)] + ([jax.ShapeDtypeStruct((S, D), MXU)] if with_h else []),
        compiler_params=_params(("parallel",)),
    )(*ins)


def _loss_fwd(y, tgt):
    S = y.shape[0]
    t = min(ROW_TILE, S)

    def body(y_ref, t_ref, dy_ref, sq_ref):
        d = y_ref[...] - t_ref[...]
        dy_ref[...] = d * (1.0 / D)
        _acc(sq_ref, pl.program_id(0) == 0, _colsum(d * d))

    return pl.pallas_call(
        body, name="loss_head", grid=(S // t,),
        in_specs=[_row_spec(t, D), _row_spec(t, D)], out_specs=[_row_spec(t, D), _vec_spec(D)],
        out_shape=[jax.ShapeDtypeStruct((S, D), F32), jax.ShapeDtypeStruct((1, D), F32)],
        compiler_params=_params(("arbitrary",)),
    )(y, tgt)


def _node_bwd(name, dres, *, hpath=None, upath=None):
    S = dres.shape[0]
    t = min(ROW_TILE, S)
    has_h, has_u = hpath is not None, upath is not None

    def body(*refs):
        refs = list(refs)
        first = pl.program_id(0) == 0
        dxs = refs.pop(0)[...]
        if has_h:
            dh = refs.pop(0)[...]
            xs = refs.pop(0)[...]
            sc = refs.pop(0)[...]
        if has_u:
            xp = refs.pop(0)[...]
            br = refs.pop(0)[...]
            gv = refs.pop(0)[...]
            lg = refs.pop(0)[...]
        if has_h:
            dsc_ref, dsh_ref = refs.pop(0), refs.pop(0)
            xh, r = _ln_stats(xs)
            _acc(dsc_ref, first, _colsum(dh * xh))
            _acc(dsh_ref, first, _colsum(dh))
            dxs = dxs + _ln_bwd(dh * (1.0 + sc), xh, r)
        if has_u:
            du_ref, dbr_ref, dlg_ref, dlb_ref, dg_ref, dbs_ref = refs
            uh, ru = _ln_stats(ALPHA * xp + (1.0 + gv) * br)
            _acc(dlg_ref, first, _colsum(dxs * uh))
            _acc(dlb_ref, first, _colsum(dxs))
            du = _ln_bwd(dxs * lg, uh, ru)
            _acc(dg_ref, first, _colsum(du * br))
            du_ref[...] = ALPHA * du
            dbr = (1.0 + gv) * du
            _acc(dbs_ref, first, _colsum(dbr))
            dbr_ref[...] = dbr.astype(dbr_ref.dtype)
        else:
            refs[0][...] = dxs

    ins, in_specs = [dres], [_row_spec(t, D)]
    outs, out_specs, names = [], [], []
    vec = jax.ShapeDtypeStruct((1, D), F32)
    if has_h:
        ins += list(hpath)
        in_specs += [_row_spec(t, D), _row_spec(t, D), _vec_spec(D)]
        outs += [vec, vec]
        out_specs += [_vec_spec(D), _vec_spec(D)]
        names += ["dsc", "dsh"]
    if has_u:
        ins += list(upath)
        in_specs += [_row_spec(t, D), _row_spec(t, D), _vec_spec(D), _vec_spec(D)]
        outs += [jax.ShapeDtypeStruct((S, D), F32), jax.ShapeDtypeStruct((S, D), MXU), vec, vec, vec, vec]
        out_specs += [_row_spec(t, D), _row_spec(t, D), _vec_spec(D), _vec_spec(D), _vec_spec(D), _vec_spec(D)]
        names += ["du", "dbr", "dlng", "dlnb", "dg", "dbrsum"]
    else:
        outs += [jax.ShapeDtypeStruct((S, D), F32)]
        out_specs += [_row_spec(t, D)]
        names += ["dx"]
    res = pl.pallas_call(
        body, name=name, grid=(S // t,), in_specs=in_specs, out_specs=out_specs, out_shape=outs,
        compiler_params=_params(("arbitrary",)),
    )(*ins)
    return dict(zip(names, res))


def _colsum_call(a, name):
    S, N = a.shape
    t, tn = min(ROW_TILE, S), _tile(N, 1152)

    def body(a_ref, o_ref):
        _acc(o_ref, pl.program_id(1) == 0, _colsum(a_ref[...].astype(F32)))

    return pl.pallas_call(
        body, name=name, grid=(N // tn, S // t),
        in_specs=[pl.BlockSpec((t, tn), lambda j, i: (i, j))], out_specs=pl.BlockSpec((1, tn), lambda j, i: (0, j)),
        out_shape=jax.ShapeDtypeStruct((1, N), F32), compiler_params=_params(("parallel", "arbitrary")),
    )(a)


def _gm_mask():
    i = lax.broadcasted_iota(jnp.int32, (128, 128), 0) // CHUNK
    j = lax.broadcasted_iota(jnp.int32, (128, 128), 1) // CHUNK
    return i >= j


def _gmlp_common(z, lng, lnb):
    gz = _gelu(z)
    u, v = gz[:, :GM_W], gz[:, GM_W:]
    vh, r = _ln_stats(v)
    return u, vh, r, vh * lng + lnb


def _gmlp_fwd(z, lng, lnb, ws, bst, name):
    S = z.shape[0]
    t = min(ROW_TILE, S)

    def body(z_ref, lg_ref, lb_ref, ws_ref, bs_ref, y_ref):
        u, _, _, vn = _gmlp_common(z_ref[...], lg_ref[...], lb_ref[...])
        mask = _gm_mask()
        vb = vn.astype(MXU)
        for g in range(GM_G):
            w = jnp.where(mask, ws_ref[g], jnp.zeros_like(ws_ref[g]))
            bias = bs_ref[:, g:g + 1]
            for blk in range(t // 128):
                rs, cs = slice(blk * 128, (blk + 1) * 128), slice(g * 128, (g + 1) * 128)
                f = _dot(w, vb[rs, cs]) + bias
                y_ref[rs, cs] = (u[rs, cs] * f).astype(y_ref.dtype)

    return pl.pallas_call(
        body, name=name, grid=(S // t,),
        in_specs=[_row_spec(t, 2 * GM_W, P_GM // (2 * GM_W)), _vec_spec(GM_W), _vec_spec(GM_W),
                  _full_spec((GM_G, 128, 128)), _full_spec((128, GM_G))],
        out_specs=_row_spec(t, GM_W), out_shape=jax.ShapeDtypeStruct((S, GM_W), MXU),
        compiler_params=_params(("parallel",)),
    )(z, lng, lnb, ws, bst)


def _gmlp_bwd(z, dya, lng, lnb, ws, wst, bst, dz_in, name):
    S = z.shape[0]
    t = min(ROW_TILE, S)

    def body(z_ref, dy_ref, lg_ref, lb_ref, ws_ref, wst_ref, bs_ref, _dz_in, dz_ref, dlg_ref, dlb_ref, dws_ref, dbs_ref, sum_ref):
        first = pl.program_id(0) == 0
        zz = z_ref[...]
        u, vh, r, vn = _gmlp_common(zz, lg_ref[...], lb_ref[...])
        dy = dy_ref[...]
        mask = _gm_mask()
        maskt = lax.broadcasted_iota(jnp.int32, (128, 128), 1) // CHUNK >= lax.broadcasted_iota(jnp.int32, (128, 128), 0) // CHUNK
        lane = lax.broadcasted_iota(jnp.int32, (128, 128), 1)
        vb = vn.astype(MXU)
        dfb = (dy * u).astype(MXU)
        df32 = dy * u
        dbs = jnp.zeros((128, 128), F32)
        du_cols, dvn_cols = [], []
        for g in range(GM_G):
            w = jnp.where(mask, ws_ref[g], jnp.zeros_like(ws_ref[g]))
            wt = jnp.where(maskt, wst_ref[g], jnp.zeros_like(wst_ref[g]))
            bias = bs_ref[:, g:g + 1]
            cs = slice(g * 128, (g + 1) * 128)
            dw = jnp.zeros((128, 128), F32)
            du_rows, dvn_rows = [], []
            for blk in range(t // 128):
                rs = slice(blk * 128, (blk + 1) * 128)
                f = _dot(w, vb[rs, cs]) + bias
                du_rows.append(dy[rs, cs] * f)
                dvn_rows.append(_dot(wt, dfb[rs, cs]))
                dw = dw + _dot_nt(dfb[rs, cs], vb[rs, cs])
                dbs = dbs + jnp.where(lane == g, jnp.sum(df32[rs, cs], axis=1, keepdims=True), 0.0)
            _acc(dws_ref.at[g], first, jnp.where(mask, dw, 0.0))
            du_cols.append(jnp.concatenate(du_rows, axis=0))
            dvn_cols.append(jnp.concatenate(dvn_rows, axis=0))
        _acc(dbs_ref, first, dbs)
        du = jnp.concatenate(du_cols, axis=1)
        dvn = jnp.concatenate(dvn_cols, axis=1)
        _acc(dlg_ref, first, _colsum(dvn * vh))
        _acc(dlb_ref, first, _colsum(dvn))
        dv = _ln_bwd(dvn * lg_ref[...], vh, r)
        dzz = jnp.concatenate([du, dv], axis=1) * _gelu_grad(zz)
        dz_ref[...] = dzz.astype(dz_ref.dtype)
        _acc(sum_ref, first, _colsum(dzz))

    vec = jax.ShapeDtypeStruct((1, GM_W), F32)
    return pl.pallas_call(
        body, name=name, grid=(S // t,),
        in_specs=[_row_spec(t, 2 * GM_W, P_GM // (2 * GM_W)), _row_spec(t, GM_W), _vec_spec(GM_W), _vec_spec(GM_W),
                  _full_spec((GM_G, 128, 128)), _full_spec((GM_G, 128, 128)), _full_spec((128, GM_G)), _ANY],
        out_specs=[_row_spec(t, 2 * GM_W, P_GM // (2 * GM_W)), _vec_spec(GM_W), _vec_spec(GM_W), _full_spec((GM_G, 128, 128)),
                   _full_spec((128, 128)), _vec_spec(2 * GM_W)],
        out_shape=[jax.ShapeDtypeStruct((S, NP), MXU), vec, vec,
                   jax.ShapeDtypeStruct((GM_G, 128, 128), F32), jax.ShapeDtypeStruct((128, 128), F32),
                   jax.ShapeDtypeStruct((1, 2 * GM_W), F32)],
        input_output_aliases=_dz_alias(8), compiler_params=_params(("arbitrary",)),
    )(z, dya, lng, lnb, ws, wst, bst, dz_in)


def _rms(x, g):
    r = lax.rsqrt(jnp.mean(x * x, axis=-1, keepdims=True) + RMS_EPS)
    xh = x * r
    return xh, r, xh * g


def _mla_specs(t):
    return [_row_spec(t, Q_RANK, P_Q // Q_RANK), _row_spec(t, KV_RANK, P_KV // 128), _row_spec(t, 128, P_KA // 128),
            _row_spec(t, 128, P_KB // 128), _row_spec(t, 128), _row_spec(t, 128), _vec_spec(Q_RANK), _vec_spec(KV_RANK),
            _full_spec((Q_RANK, D)), _full_spec((Q_RANK, D)), _full_spec((KV_RANK, D)), _full_spec((HEADS, KV_RANK, VD))]


def _mla_prep_fwd(z, cos, sin, gq, gkv, w1, w2, wk, wv, wvt, name):
    S = z.shape[0]
    t = min(ROW_TILE, S)

    def body(zq_ref, zkv_ref, zka_ref, zkb_ref, cos_ref, sin_ref, gq_ref, gkv_ref, w1_ref, w2_ref, wk_ref, wv_ref, wvt_ref,
             q_ref, k_ref, v_ref, vt_ref):
        cos_, sin_ = cos_ref[...], sin_ref[...]
        cq = cos_ + jnp.where(lax.broadcasted_iota(jnp.int32, cos_.shape, 1) < NOPE, 1.0, 0.0)
        qn = _rms(zq_ref[...], gq_ref[...])[2].astype(MXU)
        q1, q2 = _dot(qn, w1_ref[...]), _dot(qn, w2_ref[...])
        kvn = _rms(zkv_ref[...], gkv_ref[...])[2].astype(MXU)
        kn = _dot(kvn, wk_ref[...])
        krot = zka_ref[...] * cos_ + zkb_ref[...] * sin_
        for h in range(HEADS):
            hs = slice(h * 128, (h + 1) * 128)
            q_ref[h] = (q1[:, hs] * cq + q2[:, hs] * sin_).astype(q_ref.dtype)
            k_ref[h] = (kn[:, hs] + krot).astype(k_ref.dtype)
            v_ref[h] = _dot(kvn, wv_ref[h]).astype(v_ref.dtype)
            vt_ref[h] = _dot_nt(wvt_ref[h], kvn).astype(vt_ref.dtype)

    hspec = lambda w: pl.BlockSpec((HEADS, t, w), lambda i: (0, i, 0))
    return pl.pallas_call(
        body, name=name, grid=(S // t,), in_specs=_mla_specs(t) + [_full_spec((HEADS, VD, KV_RANK))],
        out_specs=[hspec(128), hspec(128), hspec(VD), pl.BlockSpec((HEADS, VD, t), lambda i: (0, 0, i))],
        out_shape=[jax.ShapeDtypeStruct((HEADS, S, 128), MXU), jax.ShapeDtypeStruct((HEADS, S, 128), MXU),
                   jax.ShapeDtypeStruct((HEADS, S, VD), MXU), jax.ShapeDtypeStruct((HEADS, VD, S), MXU)],
        compiler_params=_params(("parallel",)),
    )(z, z, z, z, cos, sin, gq, gkv, w1, w2, wk, wv, wvt)


def _mla_prep_bwd(z, cos, sin, gq, gkv, w1, w2, wk, wv, dq, dk, dv, dz_in, name):
    S = z.shape[0]
    t = min(ROW_TILE, S)

    def body(zq_ref, zkv_ref, zka_ref, zkb_ref, cos_ref, sin_ref, gq_ref, gkv_ref, w1_ref, w2_ref, wk_ref, wv_ref,
             dq_ref, dk_ref, dv_ref, _dz_in, dz_ref, dgq_ref, dgkv_ref, dw1_ref, dw2_ref, dwk_ref, dwv_ref, sum_ref):
        first = pl.program_id(0) == 0
        cos_, sin_ = cos_ref[...], sin_ref[...]
        cq = cos_ + jnp.where(lax.broadcasted_iota(jnp.int32, cos_.shape, 1) < NOPE, 1.0, 0.0)
        gq_, gkv_ = gq_ref[...], gkv_ref[...]
        xhq, rq, qn32 = _rms(zq_ref[...], gq_)
        qn = qn32.astype(MXU)
        dq1 = jnp.concatenate([dq_ref[h] * cq for h in range(HEADS)], axis=1).astype(MXU)
        dq2 = jnp.concatenate([dq_ref[h] * sin_ for h in range(HEADS)], axis=1).astype(MXU)
        dqn = _dot_nt(dq1, w1_ref[...]) + _dot_nt(dq2, w2_ref[...])
        _acc(dw1_ref, first, _dot_tn(qn, dq1))
        _acc(dw2_ref, first, _dot_tn(qn, dq2))
        _acc(dgq_ref, first, _colsum(dqn * xhq))
        dxn = dqn * gq_
        dzq = rq * (dxn - xhq * jnp.mean(dxn * xhq, axis=-1, keepdims=True))

        xhk, rk, kvn32 = _rms(zkv_ref[...], gkv_)
        kvn = kvn32.astype(MXU)
        dks = [dk_ref[h] for h in range(HEADS)]
        dkall = jnp.concatenate(dks, axis=1).astype(MXU)
        dkrot = functools.reduce(lambda p, q_: p + q_, dks)
        dkvn = _dot_nt(dkall, wk_ref[...])
        _acc(dwk_ref, first, _dot_tn(kvn, dkall))
        for h in range(HEADS):
            dvb = dv_ref[h].astype(MXU)
            dkvn = dkvn + _dot_nt(dvb, wv_ref[h])
            _acc(dwv_ref.at[h], first, _dot_tn(kvn, dvb))
        _acc(dgkv_ref, first, _colsum(dkvn * xhk))
        dxk = dkvn * gkv_
        dzkv = rk * (dxk - xhk * jnp.mean(dxk * xhk, axis=-1, keepdims=True))
        dzm = jnp.concatenate([dzq, dzkv, dkrot * cos_, dkrot * sin_], axis=1)
        dz_ref[...] = dzm.astype(dz_ref.dtype)
        _acc(sum_ref, first, _colsum(dzm))

    hspec = lambda w: pl.BlockSpec((HEADS, t, w), lambda i: (0, i, 0))
    sds = jax.ShapeDtypeStruct
    return pl.pallas_call(
        body, name=name, grid=(S // t,),
        in_specs=_mla_specs(t) + [hspec(128), hspec(128), hspec(VD), _ANY],
        out_specs=[_row_spec(t, 640, P_Q // 640), _vec_spec(Q_RANK), _vec_spec(KV_RANK), _full_spec((Q_RANK, D)),
                   _full_spec((Q_RANK, D)), _full_spec((KV_RANK, D)), _full_spec((HEADS, KV_RANK, VD)), _vec_spec(640)],
        out_shape=[sds((S, NP), MXU), sds((1, Q_RANK), F32), sds((1, KV_RANK), F32), sds((Q_RANK, D), F32),
                   sds((Q_RANK, D), F32), sds((KV_RANK, D), F32), sds((HEADS, KV_RANK, VD), F32), sds((1, 640), F32)],
        input_output_aliases=_dz_alias(16), compiler_params=_params(("arbitrary",)),
    )(z, z, z, z, cos, sin, gq, gkv, w1, w2, wk, wv, dq, dk, dv, dz_in)


def _chunk_mask(t, transposed):
    r = lax.broadcasted_iota(jnp.int32, (t, t), 0) // CHUNK
    c = lax.broadcasted_iota(jnp.int32, (t, t), 1) // CHUNK
    return (r <= c) if transposed else (c <= r)


def _attn_fwd(q, k, vt, name):
    S = q.shape[1]
    t = min(ATT_TILE, S)
    n = S // t

    hb = 4

    def body(q_ref, k_ref, vt_ref, ob_ref, o_ref, lse_ref):
        qi = pl.program_id(1)
        qbs = [q_ref[g] for g in range(hb)]

        def block(j, carries, masked):
            cols = pl.ds(pl.multiple_of(j * t, t), t)
            out = []
            for g in range(hb):
                m, l, acc = carries[g]
                st = _dot_nt(k_ref[g, cols, :], qbs[g]) * ATT_SCALE
                if masked:
                    st = jnp.where(_chunk_mask(t, True), st, -jnp.inf)
                m_new = jnp.maximum(m, jnp.max(st, axis=0, keepdims=True))
                p = jnp.exp(st - m_new)
                alpha = jnp.exp(m - m_new)
                l = alpha * l + jnp.sum(p, axis=0, keepdims=True)
                acc = alpha * acc + _dot(vt_ref[g, :, cols], p.astype(MXU))
                out.append((m_new, l, acc))
            return tuple(out)

        init = tuple((jnp.full((1, t), -jnp.inf, F32), jnp.zeros((1, t), F32), jnp.zeros((VD, t), F32)) for _ in range(hb))
        carries = lax.fori_loop(0, qi, lambda j, c: block(j, c, False), init)
        for g, (m, l, acc) in enumerate(block(qi, carries, True)):
            o = (acc / l).T
            o_ref[g] = o
            ob_ref[g] = o.astype(ob_ref.dtype)
            lse_ref[g] = m + jnp.log(l)

    qspec = lambda w: pl.BlockSpec((hb, t, w), lambda h, i: (h, i, 0))
    sds = jax.ShapeDtypeStruct
    return pl.pallas_call(
        body, name=name, grid=(HEADS // hb, n),
        in_specs=[qspec(128), pl.BlockSpec((hb, S, 128), lambda h, i: (h, 0, 0)), pl.BlockSpec((hb, VD, S), lambda h, i: (h, 0, 0))],
        out_specs=[qspec(VD), qspec(VD), pl.BlockSpec((hb, 1, t), lambda h, i: (h, 0, i))],
        out_shape=[sds((HEADS, S, VD), MXU), sds((HEADS, S, VD), F32), sds((HEADS, 1, S), F32)],
        compiler_params=_params(("parallel", "arbitrary")),
    )(q, k, vt)


def _attn_delta(o, do, name):
    S = o.shape[1]
    t = min(ROW_TILE, S)

    def body(o_ref, do_ref, d_ref):
        for h in range(HEADS):
            d_ref[h] = jnp.sum(o_ref[h] * do_ref[h], axis=1, keepdims=True)

    hspec = lambda w: pl.BlockSpec((HEADS, t, w), lambda i: (0, i, 0))
    return pl.pallas_call(
        body, name=name, grid=(S // t,), in_specs=[hspec(VD), hspec(VD)], out_specs=hspec(1),
        out_shape=jax.ShapeDtypeStruct((HEADS, S, 1), F32), compiler_params=_params(("parallel",)),
    )(o, do)


def _attn_bwd(q, k, v, do, lse_row, d_row, name):
    S = q.shape[1]
    t = min(ATT_TILE, S)
    n = S // t

    def body(q_ref, do_ref, lse_ref, d_ref, k_ref, v_ref, dq_ref, dk_ref, dv_ref, dk_s, dv_s):
        ki = pl.program_id(1)

        @pl.when(ki == 0)
        def _():
            dq_ref[...] = jnp.zeros(dq_ref.shape, F32)

        dk_s[...] = jnp.zeros(dk_s.shape, F32)
        dv_s[...] = jnp.zeros(dv_s.shape, F32)
        kb, vb = k_ref[0], v_ref[0]

        def step(qi, masked):
            rows = pl.ds(pl.multiple_of(qi * t, t), t)
            qb, dob = q_ref[0, rows, :], do_ref[0, rows, :]
            st = _dot_nt(kb, qb) * ATT_SCALE
            if masked:
                st = jnp.where(_chunk_mask(t, True), st, -jnp.inf)
            pt = jnp.exp(st - lse_ref[0, :, rows])
            dv_s[...] += _dot(pt.astype(MXU), dob)
            dpt = _dot_nt(vb, dob)
            dst = (pt * (dpt - d_ref[0, :, rows]) * ATT_SCALE).astype(MXU)
            dk_s[...] += _dot(dst, qb)
            dq_ref[0, rows, :] += _dot_tn(dst, kb)

        step(ki, True)

        def loop(qi, c):
            step(qi, False)
            return c

        lax.fori_loop(ki + 1, n, loop, 0)
        dk_ref[0] = dk_s[...]
        dv_ref[0] = dv_s[...]

    head = lambda *shape: pl.BlockSpec((1,) + shape, lambda h, j: (h, 0, 0))
    kmap = lambda h, j: (h, j, 0)
    sds = jax.ShapeDtypeStruct
    return pl.pallas_call(
        body, name=name, grid=(HEADS, n),
        in_specs=[head(S, 128), head(S, VD), head(1, S), head(1, S), pl.BlockSpec((1, t, 128), kmap), pl.BlockSpec((1, t, VD), kmap)],
        out_specs=[head(S, 128), pl.BlockSpec((1, t, 128), kmap), pl.BlockSpec((1, t, VD), kmap)],
        out_shape=[sds((HEADS, S, 128), F32), sds((HEADS, S, 128), F32), sds((HEADS, S, VD), F32)],
        scratch_shapes=[pltpu.VMEM((t, 128), F32), pltpu.VMEM((t, VD), F32)],
        compiler_params=_params(("parallel", "arbitrary")),
    )(q, do, lse_row, d_row, k, v)


def _shift_down(x, prev8, d):
    xr = pltpu.roll(x, d, 0)
    r8 = lax.broadcasted_iota(jnp.int32, prev8.shape, 0)
    top = jnp.where(r8 < d, pltpu.roll(prev8, d, 0), xr[0:8])
    return jnp.concatenate([top, xr[8:]], axis=0)


def _shift_up(x, next8, d):
    n = x.shape[0]
    xr = pltpu.roll(x, n - d, 0)
    r8 = lax.broadcasted_iota(jnp.int32, next8.shape, 0)
    bot = jnp.where(r8 >= 8 - d, pltpu.roll(next8, 8 - d, 0), xr[n - 8:])
    return jnp.concatenate([xr[:n - 8], bot], axis=0)


def _log1p(u):
    return jnp.where(u < 0.01, u * (1.0 - u * (0.5 - u * (1.0 / 3.0 - 0.25 * u))), jnp.log(1.0 + u))


def _neg_expm1(y):
    series = -y * (1.0 + 0.5 * y * (1.0 + (1.0 / 3.0) * y * (1.0 + 0.25 * y)))
    return jnp.where(y > -0.05, series, 1.0 - jnp.exp(y))


def _softplus_neg(lam):
    return jnp.maximum(-lam, 0.0) + _log1p(jnp.exp(-jnp.abs(lam)))


def _lru_gates(x, prev8, cw, cb, wr, br, wi, bi, lam):
    xs1, xs2, xs3 = _shift_down(x, prev8, 1), _shift_down(x, prev8, 2), _shift_down(x, prev8, 3)
    xc = cb + cw[0:1] * xs3 + cw[1:2] * xs2 + cw[2:3] * xs1 + cw[3:4] * x
    xcb = xc.astype(MXU)
    r = _sigmoid(_dot(xcb, wr) + br)
    ig = _sigmoid(_dot(xcb, wi) + bi)
    sp = _softplus_neg(lam)
    log_a = -8.0 * r * sp
    a = jnp.exp(log_a)
    gb = jnp.sqrt(_neg_expm1(2.0 * log_a))
    return (xs1, xs2, xs3), xc, xcb, r, ig, sp, a, gb


def _lru_fwd(z, cw, cb, wr, br, wi, bi, lam, name):
    S = z.shape[0]
    t = min(ROW_TILE, S)

    def body(zx_ref, zg_ref, cw_ref, cb_ref, wr_ref, br_ref, wi_ref, bi_ref, lam_ref, y_ref, h_ref, xp_s, hc_s):
        @pl.when(pl.program_id(0) == 0)
        def _():
            xp_s[...] = jnp.zeros(xp_s.shape, F32)
            hc_s[...] = jnp.zeros(hc_s.shape, F32)

        x = zx_ref[...]
        _, xc, _, _, ig, _, a, gb = _lru_gates(x, xp_s[...], cw_ref[...], cb_ref[...], wr_ref[...], br_ref[...],
                                               wi_ref[...], bi_ref[...], lam_ref[...])
        b = gb * (ig * xc)
        rows = lax.broadcasted_iota(jnp.int32, a.shape, 0)
        d = 1
        while d < t:
            ar, brr = pltpu.roll(a, d, 0), pltpu.roll(b, d, 0)
            ok = rows >= d
            b = jnp.where(ok, a * brr, 0.0) + b
            a = jnp.where(ok, a * ar, a)
            d *= 2
        h = a * hc_s[7:8, :] + b
        h_ref[...] = h
        y_ref[...] = (h * _gelu(zg_ref[...])).astype(y_ref.dtype)
        hc_s[...] = h[t - 8:, :]
        xp_s[...] = x[t - 8:, :]

    w = LRU_W
    return pl.pallas_call(
        body, name=name, grid=(S // t,),
        in_specs=[_row_spec(t, w, P_LX // w), _row_spec(t, w, P_LG // w), _full_spec((4, w)), _vec_spec(w), _full_spec((w, w)),
                  _vec_spec(w), _full_spec((w, w)), _vec_spec(w), _vec_spec(w)],
        out_specs=[_row_spec(t, w), _row_spec(t, w)],
        out_shape=[jax.ShapeDtypeStruct((S, w), MXU), jax.ShapeDtypeStruct((S, w), F32)],
        scratch_shapes=[pltpu.VMEM((8, w), F32), pltpu.VMEM((8, w), F32)],
        compiler_params=_params(("arbitrary",)),
    )(z, z, cw, cb, wr, br, wi, bi, lam)


def _lru_bwd(z, h, dy, cw, cb, wr, br, wi, bi, lam, dz_in, name):
    S = z.shape[0]
    t = min(ROW_TILE, S)
    n = S // t
    w = LRU_W

    def body(zx_ref, zxp_ref, zg_ref, h_ref, hp_ref, dy_ref, cw_ref, cb_ref, wr_ref, br_ref, wi_ref, bi_ref, lam_ref, _dz_in,
             dz_ref, dcw_ref, dcb_ref, dwr_ref, dbr_ref, dwi_ref, dbi_ref, dlam_ref, sum_ref, gc_s, dn_s):
        i = pl.program_id(0)
        first = i == 0
        j = n - 1 - i

        @pl.when(first)
        def _():
            gc_s[...] = jnp.zeros(gc_s.shape, F32)
            dn_s[...] = jnp.zeros(dn_s.shape, F32)

        live = (j > 0).astype(F32)
        xprev8, hprev8 = zxp_ref[...] * live, hp_ref[...] * live
        x, zg, hh, dy_ = zx_ref[...], zg_ref[...], h_ref[...], dy_ref[...]
        cw_, lam_ = cw_ref[...], lam_ref[...]
        (xs1, xs2, xs3), xc, xcb, r, ig, sp, a, gb = _lru_gates(x, xprev8, cw_, cb_ref[...], wr_ref[...], br_ref[...],
                                                                wi_ref[...], bi_ref[...], lam_)
        hm1 = _shift_down(hh, hprev8, 1)
        dh = dy_ * _gelu(zg)
        dzg = dy_ * hh * _gelu_grad(zg)
        rows = lax.broadcasted_iota(jnp.int32, a.shape, 0)
        last = rows == t - 1
        ca = jnp.where(last, 0.0, pltpu.roll(a, t - 1, 0))
        g = dh + jnp.where(last, gc_s[0:1, :], 0.0)
        d = 1
        while d < t:
            ok = rows < t - d
            g = g + jnp.where(ok, ca * pltpu.roll(g, t - d, 0), 0.0)
            ca = jnp.where(ok, ca * pltpu.roll(ca, t - d, 0), 0.0)
            d *= 2
        gc_s[...] = a[0:8, :] * g[0:8, :]
        da = g * hm1
        dgb = g * (ig * xc)
        dub = g * gb
        di = dub * xc
        dxc = dub * ig
        dlog_a = da * a - dgb * (a * a) / gb
        dr = dlog_a * (-8.0 * sp)
        _acc(dlam_ref, first, _colsum(dlog_a * (-8.0 * r)) * (-_sigmoid(-lam_)))
        dpr = dr * r * (1.0 - r)
        dpi = di * ig * (1.0 - ig)
        _acc(dbr_ref, first, _colsum(dpr))
        _acc(dbi_ref, first, _colsum(dpi))
        dprb, dpib = dpr.astype(MXU), dpi.astype(MXU)
        _acc(dwr_ref, first, _dot_tn(xcb, dprb))
        _acc(dwi_ref, first, _dot_tn(xcb, dpib))
        dxc = dxc + _dot_nt(dprb, wr_ref[...]) + _dot_nt(dpib, wi_ref[...])
        _acc(dcb_ref, first, _colsum(dxc))
        _acc(dcw_ref, first, jnp.concatenate([_colsum(dxc * xs3), _colsum(dxc * xs2), _colsum(dxc * xs1), _colsum(dxc * x)], axis=0))
        nxt = dn_s[...]
        dx = cw_[3:4] * dxc + cw_[2:3] * _shift_up(dxc, nxt, 1) + cw_[1:2] * _shift_up(dxc, nxt, 2) + cw_[0:1] * _shift_up(dxc, nxt, 3)
        dn_s[...] = dxc[0:8, :]
        dz_ref[...] = jnp.concatenate([dx, dzg], axis=1).astype(dz_ref.dtype)
        _acc(sum_ref, first, jnp.concatenate([_colsum(dx), _colsum(dzg)], axis=1))

    rev = lambda col: pl.BlockSpec((t, w), lambda i, c=col: (n - 1 - i, c))
    prev8 = lambda col: pl.BlockSpec((8, w), lambda i, c=col: (jnp.maximum((n - 1 - i) * (t // 8) - 1, 0), c))
    vec = jax.ShapeDtypeStruct((1, w), F32)
    sds = jax.ShapeDtypeStruct
    return pl.pallas_call(
        body, name=name, grid=(n,),
        in_specs=[rev(P_LX // w), prev8(P_LX // w), rev(P_LG // w), rev(0), prev8(0), rev(0), _full_spec((4, w)), _vec_spec(w),
                  _full_spec((w, w)), _vec_spec(w), _full_spec((w, w)), _vec_spec(w), _vec_spec(w), _ANY],
        out_specs=[pl.BlockSpec((t, 2 * w), lambda i: (n - 1 - i, P_LX // (2 * w))), _full_spec((4, w)), _vec_spec(w),
                   _full_spec((w, w)), _vec_spec(w), _full_spec((w, w)), _vec_spec(w), _vec_spec(w), _vec_spec(2 * w)],
        out_shape=[sds((S, NP), MXU), sds((4, w), F32), vec, sds((w, w), F32), vec, sds((w, w), F32), vec, vec, sds((1, 2 * w), F32)],
        scratch_shapes=[pltpu.VMEM((8, w), F32), pltpu.VMEM((8, w), F32)],
        input_output_aliases=_dz_alias(14), compiler_params=_params(("arbitrary",)),
    )(z, z, z, h, h, dy, cw, cb, wr, br, wi, bi, lam, dz_in)


def _branch_fwd(z, ya, o, yc, wa, wb, wc, name):
    S = z.shape[0]
    t = min(ROW_TILE, S)

    def body(ga_ref, gb_ref, gc_ref, ya_ref, o_ref, yc_ref, wa_ref, wb_ref, wc_ref, m_ref, pa_ref, pb_ref, pc_ref):
        pa = _dot(ya_ref[...], wa_ref[...])
        pc = _dot(yc_ref[...], wc_ref[...])
        pb = _dot(o_ref[0], wb_ref[0])
        for h in range(1, HEADS):
            pb = pb + _dot(o_ref[h], wb_ref[h])
        pa_ref[...], pb_ref[...], pc_ref[...] = pa, pb, pc
        m = _sigmoid(ga_ref[...]) * pa + _sigmoid(gb_ref[...]) * pb + _sigmoid(gc_ref[...]) * pc
        m_ref[...] = m.astype(m_ref.dtype)

    g0 = P_GATE // D
    sds = jax.ShapeDtypeStruct
    return pl.pallas_call(
        body, name=name, grid=(S // t,),
        in_specs=[_row_spec(t, D, g0), _row_spec(t, D, g0 + 1), _row_spec(t, D, g0 + 2), _row_spec(t, GM_W),
                  pl.BlockSpec((HEADS, t, VD), lambda i: (0, i, 0)), _row_spec(t, LRU_W),
                  _full_spec((GM_W, D)), _full_spec((HEADS, VD, D)), _full_spec((LRU_W, D))],
        out_specs=[_row_spec(t, D)] * 4,
        out_shape=[sds((S, D), MXU), sds((S, D), F32), sds((S, D), F32), sds((S, D), F32)],
        compiler_params=_params(("parallel",)),
    )(z, z, z, ya, o, yc, wa, wb, wc)


def _dz_alias(n_inputs):
    return {n_inputs - 1: 0}


def _branch_bwd(z, dm, pa, pb, pc, name):
    S = z.shape[0]
    t = min(ROW_TILE, S)

    def body(ga_ref, gb_ref, gc_ref, dm_ref, pa_ref, pb_ref, pc_ref, dz_ref, da_ref, db_ref, dc_ref, sum_ref):
        first = pl.program_id(0) == 0
        dm_ = dm_ref[...]
        for n_, (g_ref, p_ref, d_ref) in enumerate(((ga_ref, pa_ref, da_ref), (gb_ref, pb_ref, db_ref), (gc_ref, pc_ref, dc_ref))):
            gt = _sigmoid(g_ref[...])
            d_ref[...] = (dm_ * gt).astype(d_ref.dtype)
            dzg = dm_ * p_ref[...] * gt * (1.0 - gt)
            dz_ref[:, n_ * D:(n_ + 1) * D] = dzg.astype(dz_ref.dtype)
            _acc(sum_ref.at[:, n_ * D:(n_ + 1) * D], first, _colsum(dzg))

    g0 = P_GATE // D
    sds = jax.ShapeDtypeStruct
    return pl.pallas_call(
        body, name=name, grid=(S // t,),
        in_specs=[_row_spec(t, D, g0), _row_spec(t, D, g0 + 1), _row_spec(t, D, g0 + 2)] + [_row_spec(t, D)] * 4,
        out_specs=[_row_spec(t, 3 * D, P_GATE // (3 * D))] + [_row_spec(t, D)] * 3 + [_vec_spec(3 * D)],
        out_shape=[sds((S, NP), MXU)] + [sds((S, D), MXU)] * 3 + [sds((1, 3 * D), F32)],
        compiler_params=_params(("arbitrary",)),
    )(z, z, z, dm, pa, pb, pc)


def _heads_bwd(dpb, o, wb, name):
    S = dpb.shape[0]
    t = min(ROW_TILE, S)

    def body(dp_ref, o_ref, wb_ref, dob_ref, do_ref, dwb_ref):
        first = pl.program_id(0) == 0
        dp = dp_ref[...]
        for h in range(HEADS):
            do = _dot_nt(dp, wb_ref[h])
            do_ref[h] = do
            dob_ref[h] = do.astype(dob_ref.dtype)
            _acc(dwb_ref.at[h], first, _dot_tn(o_ref[h], dp))

    hspec = pl.BlockSpec((HEADS, t, VD), lambda i: (0, i, 0))
    sds = jax.ShapeDtypeStruct
    return pl.pallas_call(
        body, name=name, grid=(S // t,),
        in_specs=[_row_spec(t, D), hspec, _full_spec((HEADS, VD, D))],
        out_specs=[hspec, hspec, _full_spec((HEADS, VD, D))],
        out_shape=[sds((HEADS, S, VD), MXU), sds((HEADS, S, VD), F32), sds((HEADS, VD, D), F32)],
        compiler_params=_params(("arbitrary",)),
    )(dpb, o, wb)


def _mod_fwd(c_all, ada_w, name):
    n = ada_w.shape[2]

    def body(c_ref, w_ref, o_ref):
        c = c_ref[...]
        ca = (c * _sigmoid(c)).astype(MXU)
        for l in range(DEPTH):
            o_ref[l] = _dot(ca, w_ref[l].astype(MXU))

    return pl.pallas_call(body, name=name, out_shape=jax.ShapeDtypeStruct((DEPTH, 8, n), F32), compiler_params=_params())(c_all, ada_w)


def _ada_w_grad(c_all_t, dmod, name):
    n = dmod.shape[2]

    def body(c_ref, d_ref, o_ref):
        c = c_ref[...]
        ca = c * _sigmoid(c)
        for l in range(DEPTH):
            dm = d_ref[l]
            acc = ca[:, 0:1] * dm[0:1, :]
            for b in range(1, 8):
                acc = acc + ca[:, b:b + 1] * dm[b:b + 1, :]
            o_ref[l] = acc

    return pl.pallas_call(body, name=name, out_shape=jax.ShapeDtypeStruct((DEPTH, D, n), F32), compiler_params=_params())(c_all_t, dmod)


def _adamw(w, g, m, v, name):
    R, C = w.shape
    t = _tile(R, 256) if R % 8 == 0 else R
    c1, c2 = 1.0 - ADAM_B1 ** ADAM_STEP, 1.0 - ADAM_B2 ** ADAM_STEP

    def body(w_ref, g_ref, m_ref, v_ref, d_ref, nm_ref, nv_ref):
        g_ = g_ref[...]
        m_ = ADAM_B1 * m_ref[...] + (1.0 - ADAM_B1) * g_
        v_ = ADAM_B2 * v_ref[...] + (1.0 - ADAM_B2) * (g_ * g_)
        nm_ref[...] = m_
        nv_ref[...] = v_
        d_ref[...] = -ADAM_LR * ((m_ / c1) / (jnp.sqrt(v_ / c2) + ADAM_EPS) + ADAM_WD * w_ref[...])

    spec = pl.BlockSpec((t, C), lambda i: (i, 0))
    return pl.pallas_call(
        body, name=name, grid=(R // t,), in_specs=[spec] * 4, out_specs=[spec] * 3,
        out_shape=[jax.ShapeDtypeStruct((R, C), F32)] * 3, compiler_params=_params(("parallel",)),
    )(w, g, m, v)


def _sum8(a, name):
    _, R, C = a.shape
    t = _tile(R, 512) if R % 8 == 0 else R

    def body(a_ref, o_ref):
        s = a_ref[0]
        for k in range(1, 8):
            s = s + a_ref[k]
        o_ref[...] = s

    return pl.pallas_call(
        body, name=name, grid=(R // t,), in_specs=[pl.BlockSpec((8, t, C), lambda i: (0, i, 0))],
        out_specs=pl.BlockSpec((t, C), lambda i: (i, 0)), out_shape=jax.ShapeDtypeStruct((R, C), F32),
        compiler_params=_params(("parallel",)),
    )(a)


def _pair_add(shards, from_sib, half, name):
    _, rh, C = from_sib.shape
    t = _tile(rh, 384, 16)
    nb = rh // t

    def body(half_ref, a0, a1, a2, a3, b_ref, o_ref, ob_ref):
        for k, a_ref in enumerate((a0, a1, a2, a3)):
            s = a_ref[...] + b_ref[k]
            o_ref[k] = s
            ob_ref[k] = s.astype(ob_ref.dtype)

    a_spec = pl.BlockSpec((t, C), lambda i, half_ref: (half_ref[0] * nb + i, 0))
    s_spec = pl.BlockSpec((4, t, C), lambda i, half_ref: (0, i, 0))
    return pl.pallas_call(
        body, name=name,
        grid_spec=pltpu.PrefetchScalarGridSpec(num_scalar_prefetch=1, grid=(nb,), in_specs=[a_spec] * 4 + [s_spec],
                                               out_specs=[s_spec, s_spec]),
        out_shape=[jax.ShapeDtypeStruct((4, rh, C), F32), jax.ShapeDtypeStruct((4, rh, C), jnp.bfloat16)],
        compiler_params=_params(("parallel",)),
    )(half, *shards, from_sib)


def _chip_sum(own, recv, half, name):
    rh, C = own.shape
    t = _tile(rh, 512, 16)
    nb = rh // t

    def body(half_ref, a_ref, r_ref, o_ref):
        s = a_ref[...]
        for k in range(3):
            s = s + r_ref[k].astype(F32)
        o_ref[...] = s

    return pl.pallas_call(
        body, name=name,
        grid_spec=pltpu.PrefetchScalarGridSpec(
            num_scalar_prefetch=1, grid=(nb,),
            in_specs=[pl.BlockSpec((t, C), lambda i, h: (i, 0)), pl.BlockSpec((3, t, C), lambda i, h: (0, i, 0))],
            out_specs=pl.BlockSpec((t, C), lambda i, h: (h[0] * nb + i, 0))),
        out_shape=jax.ShapeDtypeStruct((2 * rh, C), F32), compiler_params=_params(("parallel",)),
    )(half, own, recv)


_ANY = pl.BlockSpec(memory_space=pl.ANY)


def _coords():
    return lax.axis_index("x"), lax.axis_index("y"), lax.axis_index("c")


def _allgather8(v, name):
    R, C = v.shape

    def body(v_ref, out_ref, send_sems, recv_sems):
        x, y, c = _coords()
        me = 4 * x + 2 * y + c
        sends = []
        for k in range(1, 8):
            bx, by, bc = (k >> 2) & 1, (k >> 1) & 1, k & 1
            peer = (x ^ bx, y ^ by, c ^ bc)
            cp = pltpu.make_async_remote_copy(src_ref=v_ref, dst_ref=out_ref.at[me], send_sem=send_sems.at[k - 1],
                                              recv_sem=recv_sems.at[k - 1], device_id=peer, device_id_type=MESH)
            cp.start()
            sends.append(cp)
        for k in range(1, 8):
            bx, by, bc = (k >> 2) & 1, (k >> 1) & 1, k & 1
            peer = (x ^ bx, y ^ by, c ^ bc)
            src = 4 * peer[0] + 2 * peer[1] + peer[2]
            pltpu.make_async_remote_copy(src_ref=v_ref, dst_ref=out_ref.at[src], send_sem=send_sems.at[k - 1],
                                         recv_sem=recv_sems.at[k - 1], device_id=peer, device_id_type=MESH).wait_recv()
        for cp in sends:
            cp.wait_send()

    out = pl.pallas_call(
        body, name=name, in_specs=[_ANY], out_specs=_ANY, out_shape=jax.ShapeDtypeStruct((8, R, C), v.dtype),
        scratch_shapes=[pltpu.SemaphoreType.DMA((7,)), pltpu.SemaphoreType.DMA((7,))],
    )(v)
    x, y, c = _coords()
    return lax.dynamic_update_slice(out, v[None], (4 * x + 2 * y + c, 0, 0))


def _chip_patterns(x, y):
    return [(1 - x, y), (x, 1 - y), (1 - x, 1 - y)]


def _allgather_weights(w, name):
    R, C = w.shape
    rh = R // 2

    def body(w_ref, out_ref, send_sems, recv_sems):
        x, y, c = _coords()
        kme = 2 * x + y
        sibling = (x, y, 1 - c)
        mine_half = pl.ds(pl.multiple_of(c * rh, FLAT_ALIGN // 2), rh)
        other_half = pl.ds(pl.multiple_of((1 - c) * rh, FLAT_ALIGN // 2), rh)
        chips = _chip_patterns(x, y)

        def copy(k, chip_idx, half, to, src=None):
            dst = out_ref.at[chip_idx, half, :]
            return pltpu.make_async_remote_copy(src_ref=dst if src is None else src, dst_ref=dst, send_sem=send_sems.at[k],
                                                recv_sem=recv_sems.at[k], device_id=to, device_id_type=MESH)

        first = [copy(j, kme, mine_half, (cx, cy, c), src=w_ref.at[mine_half, :]) for j, (cx, cy) in enumerate(chips)]
        own = pltpu.make_async_remote_copy(src_ref=w_ref, dst_ref=out_ref.at[kme], send_sem=send_sems.at[6], recv_sem=recv_sems.at[6],
                                           device_id=sibling, device_id_type=MESH)
        first.append(own)
        for cp in first:
            cp.start()
        passed = []
        for j, (cx, cy) in enumerate(chips):
            kj = 2 * cx + cy
            copy(j, kj, mine_half, (x, y, c)).wait_recv()
            fw = copy(3 + j, kj, mine_half, sibling)
            fw.start()
            passed.append(fw)
        for j, (cx, cy) in enumerate(chips):
            copy(3 + j, 2 * cx + cy, other_half, (x, y, c)).wait_recv()
        own.wait_recv()
        for cp in first + passed:
            cp.wait_send()

    return pl.pallas_call(
        body, name=name, in_specs=[_ANY], out_specs=_ANY, out_shape=jax.ShapeDtypeStruct((4, R, C), w.dtype),
        scratch_shapes=[pltpu.SemaphoreType.DMA((7,)), pltpu.SemaphoreType.DMA((7,))],
    )(w)


def _sibling_swap(shards, name):
    R, C = shards[0].shape
    rh = R // 2

    def body(g0, g1, g2, g3, out_ref, send_sems, recv_sems):
        x, y, c = _coords()
        other_half = pl.ds(pl.multiple_of((1 - c) * rh, 8), rh)
        cps = []
        for k, g_ref in enumerate((g0, g1, g2, g3)):
            cp = pltpu.make_async_remote_copy(src_ref=g_ref.at[other_half, :], dst_ref=out_ref.at[k], send_sem=send_sems.at[k],
                                              recv_sem=recv_sems.at[k], device_id=(x, y, 1 - c), device_id_type=MESH)
            cp.start()
            cps.append(cp)
        for cp in cps:
            cp.wait()

    return pl.pallas_call(
        body, name=name, in_specs=[_ANY] * 4, out_specs=_ANY, out_shape=jax.ShapeDtypeStruct((4, rh, C), shards[0].dtype),
        scratch_shapes=[pltpu.SemaphoreType.DMA((4,)), pltpu.SemaphoreType.DMA((4,))],
    )(*shards)


def _chip_scatter(a, name):
    _, R, C = a.shape

    def body(a_ref, out_ref, send_sems, recv_sems):
        x, y, c = _coords()
        cps = []
        for j, (cx, cy) in enumerate(_chip_patterns(x, y)):
            cp = pltpu.make_async_remote_copy(src_ref=a_ref.at[2 * cx + cy], dst_ref=out_ref.at[j], send_sem=send_sems.at[j],
                                              recv_sem=recv_sems.at[j], device_id=(cx, cy, c), device_id_type=MESH)
            cp.start()
            cps.append(cp)
        for cp in cps:
            cp.wait()

    return pl.pallas_call(
        body, name=name, in_specs=[_ANY], out_specs=_ANY, out_shape=jax.ShapeDtypeStruct((3, R, C), a.dtype),
        scratch_shapes=[pltpu.SemaphoreType.DMA((3,)), pltpu.SemaphoreType.DMA((3,))],
    )(a)


def _sibling_join(f, name):
    R, C = f.shape
    rh = R // 2

    def body(f_ref, out_ref, send_sem, recv_sem):
        x, y, c = _coords()
        mine_half = pl.ds(pl.multiple_of(c * rh, 8), rh)
        cp = pltpu.make_async_remote_copy(src_ref=f_ref.at[mine_half, :], dst_ref=out_ref.at[mine_half, :], send_sem=send_sem,
                                          recv_sem=recv_sem, device_id=(x, y, 1 - c), device_id_type=MESH)
        cp.start()
        cp.wait()

    return pl.pallas_call(
        body, name=name, in_specs=[_ANY], out_specs=_ANY, out_shape=jax.ShapeDtypeStruct((R, C), f.dtype),
        scratch_shapes=[pltpu.SemaphoreType.DMA(()), pltpu.SemaphoreType.DMA(())], input_output_aliases={0: 0},
    )(f)


_BIG = (("in_w", 2), ("mla_wuq", 2), ("mla_wukv", 2), ("branch_w", 3), ("mix_out_w", 1), ("ffn_w1", 2), ("ffn_w2", 1))


def _pack(shards):
    flat = jnp.concatenate([shards[n].reshape(-1, FLAT_W) for n, _ in _BIG], axis=0)
    pad = (-flat.shape[0]) % FLAT_ALIGN
    return jnp.pad(flat, ((0, pad), (0, 0)))


def _unpack(flat, shapes):
    out, r = {}, 0
    for n, _ in _BIG:
        rows = math.prod(shapes[n]) // FLAT_W
        out[n] = flat[r:r + rows].reshape(shapes[n])
        r += rows
    return out


def _swap16(a):
    return jnp.concatenate([a[..., 16:32], a[..., 0:16]], axis=-1)


def _shard_cols(parts, a, b):
    w = parts[0].shape[-1]
    pieces = []
    for k, p in enumerate(parts):
        s, e = max(a, k * w), min(b, (k + 1) * w)
        if s < e:
            pieces.append(p[..., s - k * w:e - k * w])
    return pieces


def _pad_in_cols(parts):
    lead, dt = parts[0].shape[:-1], parts[0].dtype
    z = lambda n: [jnp.zeros(lead + (n,), dt)]
    seg = lambda a, b: _shard_cols(parts, a, b)
    kr = jnp.concatenate(seg(O_KR, O_LX), axis=-1)
    return jnp.concatenate(seg(O_GATE, N_IN) + seg(O_GM, O_Q) + seg(O_LX, O_LG) + seg(O_LG, O_GATE) + seg(O_Q, O_KV) + seg(O_KV, O_KR)
                           + z(64) + [kr] + z(32) + z(64) + [_swap16(kr)] + z(32), axis=-1)


_SEGS = ((O_GM, O_Q, P_GM), (O_Q, O_KV, P_Q), (O_KV, O_KR, P_KV), (O_KR, O_LX, None), (O_LX, O_LG, P_LX), (O_LG, O_GATE, P_LG),
         (O_GATE, N_IN, P_GATE))


def _unpad_in_cols(g, a=0, b=N_IN):
    pieces = []
    for lo, hi, p in _SEGS:
        s, e = max(a, lo), min(b, hi)
        if s >= e:
            continue
        if p is None:
            kr = g[..., P_KA + 64:P_KA + 96] + _swap16(g[..., P_KB + 64:P_KB + 96])
            pieces.append(kr[..., s - lo:e - lo])
        else:
            pieces.append(g[..., p + s - lo:p + e - lo])
    return jnp.concatenate(pieces, axis=-1)


def _flat128(vecs):
    flat = jnp.concatenate([v.reshape(-1) for v in vecs])
    pad = (-flat.shape[0]) % 1024
    return jnp.pad(flat, (0, pad)).reshape(-1, 128)


def _unflat(flat, shapes):
    flat = flat.reshape(-1)
    out, r = [], 0
    for s in shapes:
        n = math.prod(s)
        out.append(flat[r:r + n].reshape(s))
        r += n
    return out


_SMALL = ("ada_b", "in_b", "gm_ln_g", "gm_ln_b", "gm_ws", "gm_bs", "mla_qnorm_g", "mla_kvnorm_g", "lru_conv_w", "lru_conv_b",
          "lru_wr", "lru_br", "lru_wi", "lru_bi", "lru_lambda", "ffn_b1", "ffn_b2", "ln_g", "ln_b")
_NAMES = ("ada_w", "ada_b", "in_w", "in_b", "gm_ln_g", "gm_ln_b", "gm_ws", "gm_bs", "mla_qnorm_g", "mla_wuq", "mla_kvnorm_g",
          "mla_wukv", "lru_conv_w", "lru_conv_b", "lru_wr", "lru_br", "lru_wi", "lru_bi", "lru_lambda", "branch_w", "mix_out_w",
          "ffn_w1", "ffn_b1", "ffn_w2", "ffn_b2", "ln_g", "ln_b")


def kernel(x, c, ada_w, ada_b, in_w, in_b, gm_ln_g, gm_ln_b, gm_ws, gm_bs, mla_qnorm_g, mla_wuq, mla_kvnorm_g, mla_wukv, lru_conv_w, lru_conv_b, lru_wr, lru_br, lru_wi, lru_bi, lru_lambda, branch_w, mix_out_w, ffn_w1, ffn_b1, ffn_w2, ffn_b2, ln_g, ln_b, loss_target, m_ada_w, m_ada_b, m_in_w, m_in_b, m_gm_ln_g, m_gm_ln_b, m_gm_ws, m_gm_bs, m_mla_qnorm_g, m_mla_wuq, m_mla_kvnorm_g, m_mla_wukv, m_lru_conv_w, m_lru_conv_b, m_lru_wr, m_lru_br, m_lru_wi, m_lru_bi, m_lru_lambda, m_branch_w, m_mix_out_w, m_ffn_w1, m_ffn_b1, m_ffn_w2, m_ffn_b2, m_ln_g, m_ln_b, v_ada_w, v_ada_b, v_in_w, v_in_b, v_gm_ln_g, v_gm_ln_b, v_gm_ws, v_gm_bs, v_mla_qnorm_g, v_mla_wuq, v_mla_kvnorm_g, v_mla_wukv, v_lru_conv_w, v_lru_conv_b, v_lru_wr, v_lru_br, v_lru_wi, v_lru_bi, v_lru_lambda, v_branch_w, v_mix_out_w, v_ffn_w1, v_ffn_b1, v_ffn_w2, v_ffn_b2, v_ln_g, v_ln_b):
    W = dict(ada_w=ada_w, ada_b=ada_b, in_w=in_w, in_b=in_b, gm_ln_g=gm_ln_g, gm_ln_b=gm_ln_b, gm_ws=gm_ws, gm_bs=gm_bs,
             mla_qnorm_g=mla_qnorm_g, mla_wuq=mla_wuq, mla_kvnorm_g=mla_kvnorm_g, mla_wukv=mla_wukv, lru_conv_w=lru_conv_w,
             lru_conv_b=lru_conv_b, lru_wr=lru_wr, lru_br=lru_br, lru_wi=lru_wi, lru_bi=lru_bi, lru_lambda=lru_lambda,
             branch_w=branch_w, mix_out_w=mix_out_w, ffn_w1=ffn_w1, ffn_b1=ffn_b1, ffn_w2=ffn_w2, ffn_b2=ffn_b2, ln_g=ln_g, ln_b=ln_b)
    M = dict(ada_w=m_ada_w, ada_b=m_ada_b, in_w=m_in_w, in_b=m_in_b, gm_ln_g=m_gm_ln_g, gm_ln_b=m_gm_ln_b, gm_ws=m_gm_ws,
             gm_bs=m_gm_bs, mla_qnorm_g=m_mla_qnorm_g, mla_wuq=m_mla_wuq, mla_kvnorm_g=m_mla_kvnorm_g, mla_wukv=m_mla_wukv,
             lru_conv_w=m_lru_conv_w, lru_conv_b=m_lru_conv_b, lru_wr=m_lru_wr, lru_br=m_lru_br, lru_wi=m_lru_wi, lru_bi=m_lru_bi,
             lru_lambda=m_lru_lambda, branch_w=m_branch_w, mix_out_w=m_mix_out_w, ffn_w1=m_ffn_w1, ffn_b1=m_ffn_b1, ffn_w2=m_ffn_w2,
             ffn_b2=m_ffn_b2, ln_g=m_ln_g, ln_b=m_ln_b)
    V = dict(ada_w=v_ada_w, ada_b=v_ada_b, in_w=v_in_w, in_b=v_in_b, gm_ln_g=v_gm_ln_g, gm_ln_b=v_gm_ln_b, gm_ws=v_gm_ws,
             gm_bs=v_gm_bs, mla_qnorm_g=v_mla_qnorm_g, mla_wuq=v_mla_wuq, mla_kvnorm_g=v_mla_kvnorm_g, mla_wukv=v_mla_wukv,
             lru_conv_w=v_lru_conv_w, lru_conv_b=v_lru_conv_b, lru_wr=v_lru_wr, lru_br=v_lru_br, lru_wi=v_lru_wi, lru_bi=v_lru_bi,
             lru_lambda=v_lru_lambda, branch_w=v_branch_w, mix_out_w=v_mix_out_w, ffn_w1=v_ffn_w1, ffn_b1=v_ffn_b1, ffn_w2=v_ffn_w2,
             ffn_b2=v_ffn_b2, ln_g=v_ln_g, ln_b=v_ln_b)

    S = x.shape[1]
    xi, yi, ci = _coords()
    kme = 2 * xi + yi
    me = 4 * xi + 2 * yi + ci
    x0 = x[0]
    tgt = loss_target[0]

    shard_shapes = {n: W[n].shape for n, _ in _BIG}
    gathered = _allgather_weights(_pack({n: W[n].astype(MXU) for n, _ in _BIG}), "ag_weights")
    parts = [_unpack(gathered[k], shard_shapes) for k in range(4)]
    full = {n: jnp.concatenate([p[n] for p in parts], axis=ax) for n, ax in _BIG if n != "in_w"}

    small1 = _allgather8(_flat128([c, ln_g, ln_b, lru_conv_w]), "ag_small")
    per_dev = [_unflat(small1[d], [c.shape, ln_g.shape, ln_b.shape, lru_conv_w.shape]) for d in range(8)]
    c_all = jnp.concatenate([p[0] for p in per_dev], axis=0)
    chip = lambda i: [per_dev[2 * k][i] for k in range(4)]
    ln_g_f, ln_b_f, conv_w_f = (jnp.concatenate(chip(1), axis=2), jnp.concatenate(chip(2), axis=2), jnp.concatenate(chip(3), axis=2))

    mod_sh = _mod_fwd(c_all, ada_w, "mod_fwd")
    mod_all = _allgather8(_flat128([mod_sh]), "ag_mod")
    mod_parts = [_unflat(mod_all[2 * k], [mod_sh.shape])[0] for k in range(4)]
    mod = jnp.concatenate(mod_parts, axis=2)
    mod = lax.dynamic_index_in_dim(mod, me, axis=1, keepdims=False) + ada_b

    pos = jnp.arange(S, dtype=F32)
    inv = ROPE_BASE ** (-jnp.arange(0, ROPE, 2, dtype=F32) / ROPE)
    ang = pos[:, None] * inv[None, :]
    cs, sn = jnp.cos(ang), jnp.sin(ang)
    zc = lambda n: jnp.zeros((S, n), F32)
    cos_t = jnp.concatenate([zc(64), cs, cs, zc(32)], axis=1)
    sin_t = jnp.concatenate([zc(64), -sn, sn, zc(32)], axis=1)

    win_p = _pad_in_cols([p["in_w"] for p in parts])
    bin_p = _pad_in_cols([in_b])
    wuq = full["mla_wuq"].reshape(DEPTH, Q_RANK, HEADS, NOPE + ROPE)
    zq = lambda n: jnp.zeros((DEPTH, Q_RANK, HEADS, n), MXU)
    w1q = jnp.concatenate([wuq, zq(32)], axis=-1).reshape(DEPTH, Q_RANK, D)
    w2q = jnp.concatenate([zq(64), _swap16(wuq[..., NOPE:]), zq(32)], axis=-1).reshape(DEPTH, Q_RANK, D)
    wukv = full["mla_wukv"].reshape(DEPTH, KV_RANK, HEADS, NOPE + VD)
    wk_p = jnp.concatenate([wukv[..., :NOPE], jnp.zeros((DEPTH, KV_RANK, HEADS, 64), MXU)], axis=-1).reshape(DEPTH, KV_RANK, D)
    wv_h = wukv[..., NOPE:].transpose(0, 2, 1, 3)
    wvt_h = wukv[..., NOPE:].transpose(0, 2, 3, 1)
    wb_h = full["branch_w"][:, 1].reshape(DEPTH, HEADS, VD, D)
    wr_f = jnp.stack([block_diag(*[lru_wr[l, b] for b in range(LRU_NB)]) for l in range(DEPTH)]).astype(MXU)
    wi_f = jnp.stack([block_diag(*[lru_wi[l, b] for b in range(LRU_NB)]) for l in range(DEPTH)]).astype(MXU)
    ws_b = gm_ws.astype(MXU)
    wst_b = gm_ws.transpose(0, 1, 3, 2).astype(MXU)
    bs_t = gm_bs.transpose(0, 2, 1)
    row = lambda v: v.reshape(1, -1)

    saved = []
    xs = x0
    h1 = None
    for l in range(DEPTH):
        sh1, sc1, g1, sh2, sc2, g2 = [row(mod[l, i * D:(i + 1) * D]) for i in range(6)]
        if l == 0:
            h1 = _lnmod_fwd(xs, sc1, sh1, f"lnmod_fwd{l}")
        z = _mm(h1, win_p[l], "nn", f"in_proj{l}", rows=[row(bin_p[l])], epi=lambda acc, b: (acc + b,))
        ya = _gmlp_fwd(z, row(gm_ln_g[l]), row(gm_ln_b[l]), ws_b[l], bs_t[l], f"gmlp_fwd{l}")
        qf, kf, vf, vtf = _mla_prep_fwd(z, cos_t, sin_t, row(mla_qnorm_g[l]), row(mla_kvnorm_g[l]), w1q[l], w2q[l], wk_p[l], wv_h[l],
                                        wvt_h[l], f"mla_prep_fwd{l}")
        ob, o32, lse = _attn_fwd(qf, kf, vtf, f"attn_fwd{l}")
        lru_args = (conv_w_f[l], row(lru_conv_b[l]), wr_f[l], row(lru_br[l]), wi_f[l], row(lru_bi[l]), row(lru_lambda[l]))
        yc, hl = _lru_fwd(z, *lru_args, f"lru_fwd{l}")
        merged, pa, pb, pc = _branch_fwd(z, ya, ob, yc, full["branch_w"][l, 0], wb_h[l], full["branch_w"][l, 2], f"branch_fwd{l}")
        mix = _mm(merged, full["mix_out_w"][l], "nn", f"mix_out{l}")
        x1, h2 = _res_ln_fwd(xs, mix, g1, row(ln_g_f[l, 0]), row(ln_b_f[l, 0]), f"res_ln_a{l}", mod=(sc2, sh2))
        a1, r2 = _mm(h2, full["ffn_w1"][l], "nn", f"ffn_up{l}", out_dtypes=(F32, MXU), rows=[row(ffn_b1[l])],
                     epi=lambda acc, b: (acc + b, jnp.square(jnp.maximum(acc + b, 0.0))))
        f = _mm(r2, full["ffn_w2"][l], "nn", f"ffn_down{l}", rows=[row(ffn_b2[l])], epi=lambda acc, b: (acc + b,))
        if l + 1 < DEPTH:
            nsh1, nsc1 = row(mod[l + 1, 0:D]), row(mod[l + 1, D:2 * D])
            x2, h1n = _res_ln_fwd(x1, f, g2, row(ln_g_f[l, 1]), row(ln_b_f[l, 1]), f"res_ln_b{l}", mod=(nsc1, nsh1))
        else:
            (x2,), h1n = _res_ln_fwd(x1, f, g2, row(ln_g_f[l, 1]), row(ln_b_f[l, 1]), f"res_ln_b{l}"), None
        saved.append(dict(x_in=xs, h1=h1, z=z, ya=ya, qf=qf, kf=kf, vf=vf, ob=ob, o32=o32, lse=lse, yc=yc, hl=hl, merged=merged,
                          pa=pa, pb=pb, pc=pc, mix=mix, x1=x1, h2=h2, a1=a1, r2=r2, f=f, x2=x2, lru_args=lru_args,
                          mods=(sh1, sc1, g1, sh2, sc2, g2)))
        xs, h1 = x2, h1n

    dy, sq = _loss_fwd(xs, tgt)
    loss = lax.psum(0.5 / D * jnp.sum(sq), ("x", "y", "c"))

    G = {}
    dmods = []
    dres, hpath = dy, None
    for l in reversed(range(DEPTH)):
        sv = saved[l]
        sh1, sc1, g1, sh2, sc2, g2 = sv["mods"]
        gl = {}
        nb = _node_bwd(f"node_b{l}", dres, hpath=hpath, upath=(sv["x1"], sv["f"], g2, row(ln_g_f[l, 1])))
        if hpath is not None:
            dmods[-1]["sc1"], dmods[-1]["sh1"] = nb["dsc"], nb["dsh"]
        dm = dict(g2=nb["dg"])
        ln_g_l1, ln_b_l1 = nb["dlng"], nb["dlnb"]
        df = nb["dbr"]
        gl["ffn_b2"] = nb["dbrsum"][0]
        gl["ffn_w2"] = _mm(sv["r2"], df, "tn", f"d_ffn_w2_{l}")
        da1, db1 = _mm(df, full["ffn_w2"][l], "nt", f"d_ffn_act{l}", out_dtypes=(MXU,), extras=[sv["a1"]], colsum=True,
                       epi=lambda acc, a: (acc * (2.0 * jnp.maximum(a, 0.0)),))
        gl["ffn_b1"] = db1[0]
        gl["ffn_w1"] = _mm(sv["h2"], da1, "tn", f"d_ffn_w1_{l}")
        dh2 = _mm(da1, full["ffn_w1"][l], "nt", f"d_ffn_in{l}")
        na = _node_bwd(f"node_a{l}", nb["du"], hpath=(dh2, sv["x1"], sc2), upath=(sv["x_in"], sv["mix"], g1, row(ln_g_f[l, 0])))
        dm.update(sc2=na["dsc"], sh2=na["dsh"], g1=na["dg"])
        gl["ln_g"] = jnp.concatenate([na["dlng"], ln_g_l1], axis=0)
        gl["ln_b"] = jnp.concatenate([na["dlnb"], ln_b_l1], axis=0)
        dmix = na["dbr"]
        gl["mix_out_w"] = _mm(sv["merged"], dmix, "tn", f"d_mix_w{l}")
        dmerged = _mm(dmix, full["mix_out_w"][l], "nt", f"d_merged{l}")
        dz, dpa, dpb, dpc, s_gate = _branch_bwd(sv["z"], dmerged, sv["pa"], sv["pb"], sv["pc"], f"branch_bwd{l}")
        wa, wc = full["branch_w"][l, 0], full["branch_w"][l, 2]
        dya = _mm(dpa, wa, "nt", f"d_ya{l}")
        dyc = _mm(dpc, wc, "nt", f"d_yc{l}")
        dwa = _mm(sv["ya"], dpa, "tn", f"d_wa{l}")
        dwc = _mm(sv["yc"], dpc, "tn", f"d_wc{l}")
        dob, do32, dwb = _heads_bwd(dpb, sv["ob"], wb_h[l], f"heads_bwd{l}")
        gl["branch_w"] = jnp.stack([dwa, dwb.reshape(GM_W, D), dwc])
        dz, dcw, dcb, dwr, dbr_, dwi, dbi, dlam, s_lru = _lru_bwd(sv["z"], sv["hl"], dyc, *sv["lru_args"], dz, f"lru_bwd{l}")
        gl["lru_conv_w"], gl["lru_conv_b"], gl["lru_br"], gl["lru_bi"], gl["lru_lambda"] = dcw, dcb[0], dbr_[0], dbi[0], dlam[0]
        blocks = lambda m: jnp.stack([m[b * 64:(b + 1) * 64, b * 64:(b + 1) * 64] for b in range(LRU_NB)])
        gl["lru_wr"], gl["lru_wi"] = blocks(dwr), blocks(dwi)
        delta = _attn_delta(sv["o32"], do32, f"attn_delta{l}")
        dqf, dkf, dvf = _attn_bwd(sv["qf"], sv["kf"], sv["vf"], dob, sv["lse"], delta.reshape(HEADS, 1, S), f"attn_bwd{l}")
        dz, dgq, dgkv, dw1, dw2, dwk, dwv, s_mla = _mla_prep_bwd(sv["z"], cos_t, sin_t, row(mla_qnorm_g[l]), row(mla_kvnorm_g[l]),
                                                                 w1q[l], w2q[l], wk_p[l], wv_h[l], dqf, dkf, dvf, dz,
                                                                 f"mla_prep_bwd{l}")
        gl["mla_qnorm_g"], gl["mla_kvnorm_g"] = dgq[0], dgkv[0]
        dw1 = dw1.reshape(Q_RANK, HEADS, 128)
        dw2 = dw2.reshape(Q_RANK, HEADS, 128)
        gl["mla_wuq"] = jnp.concatenate([dw1[..., :NOPE], dw1[..., NOPE:NOPE + ROPE] + _swap16(dw2[..., NOPE:NOPE + ROPE])],
                                        axis=-1).reshape(Q_RANK, HEADS * (NOPE + ROPE))
        gl["mla_wukv"] = jnp.concatenate([dwk.reshape(KV_RANK, HEADS, 128)[..., :NOPE], dwv.transpose(1, 0, 2)],
                                         axis=-1).reshape(KV_RANK, HEADS * (NOPE + VD))
        dz, dglg, dglb, dws, dbs, s_gm = _gmlp_bwd(sv["z"], dya, row(gm_ln_g[l]), row(gm_ln_b[l]), ws_b[l], wst_b[l], bs_t[l], dz,
                                                   f"gmlp_bwd{l}")
        gl["gm_ln_g"], gl["gm_ln_b"], gl["gm_ws"], gl["gm_bs"] = dglg[0], dglb[0], dws, dbs[:, :GM_G].T
        gl["in_b"] = _unpad_in_cols(jnp.concatenate([s_gate, s_gm, s_lru, s_mla], axis=1))[0]
        gl["in_w"] = _mm(sv["h1"], dz, "tn", f"d_in_w{l}")
        dh1 = _mm(dz, win_p[l], "nt", f"d_h1_{l}")
        dmods.append(dm)
        dres, hpath = na["du"], (dh1, sv["x_in"], sc1)
        G[l] = gl
    n0 = _node_bwd("node_in", dres, hpath=hpath)
    dmods[-1]["sc1"], dmods[-1]["sh1"] = n0["dsc"], n0["dsh"]
    grad_x = n0["dx"][None]
    dmods = dmods[::-1]
    dmod = jnp.stack([jnp.concatenate([dmods[l][k] for k in ("sh1", "sc1", "g1", "sh2", "sc2", "g2")], axis=1)[0]
                      for l in range(DEPTH)])
    grads = {n: jnp.stack([G[l][n] for l in range(DEPTH)]) for n in G[0]}
    grads["ada_b"] = dmod

    wcols = in_w.shape[2]
    shard_of = lambda n, ax, k: (_unpad_in_cols(grads[n], k * wcols, (k + 1) * wcols) if n == "in_w"
                                 else jnp.split(grads[n], 4, axis=ax)[k])
    gsh = [_pack({n: shard_of(n, ax, k) for n, ax in _BIG}) for k in range(4)]
    half = ci.reshape(1).astype(jnp.int32)
    from_sib = _sibling_swap(gsh, "rs_pair")
    pair, pair_b = _pair_add(gsh, from_sib, half, "rs_pair_add")
    from_chips = _chip_scatter(pair_b, "rs_chips")
    own = lax.dynamic_index_in_dim(pair, kme, axis=0, keepdims=False)
    half_sum = _chip_sum(own, from_chips, half, "rs_chip_sum")
    gflat = _sibling_join(half_sum, "rs_join")
    gbig = _unpack(gflat, shard_shapes)

    small_shapes = [grads[n].shape for n in _SMALL]
    gsm_all = _allgather8(_flat128([grads[n] for n in _SMALL]), "ag_small_grads")
    gsm = _unflat(_sum8(gsm_all, "sum_small_grads"), small_shapes)
    gsmall = dict(zip(_SMALL, gsm))
    n_mod = DEPTH * 6 * D
    dmod_all = gsm_all[:, :n_mod // 128].reshape(8, DEPTH, 6 * D).transpose(1, 0, 2)
    dmod_sh = lax.dynamic_slice_in_dim(dmod_all, kme * (6 * D // 4), 6 * D // 4, axis=2)
    g_ada_w = _ada_w_grad(c_all.T, dmod_sh, "d_ada_w")
    quarter = lambda g, ax: lax.dynamic_slice_in_dim(g, kme * (g.shape[ax] // 4), g.shape[ax] // 4, axis=ax)
    gsmall["lru_conv_w"] = quarter(gsmall["lru_conv_w"], 2)
    gsmall["ln_g"] = quarter(gsmall["ln_g"], 2)
    gsmall["ln_b"] = quarter(gsmall["ln_b"], 2)

    grad = dict(gbig)
    grad.update(gsmall)
    grad["ada_w"] = g_ada_w

    delta, new_m, new_v = {}, {}, {}
    for n in _NAMES:
        shp = W[n].shape
        two = lambda a: a.reshape(-1, shp[-1])
        d_, m_, v_ = _adamw(two(W[n]), two(grad[n]), two(M[n]), two(V[n]), f"adamw_{n}")
        delta[n], new_m[n], new_v[n] = d_.reshape(shp), m_.reshape(shp), v_.reshape(shp)

    return (loss, grad_x, *[grad[n] for n in _NAMES], *[delta[n] for n in _NAMES], *[new_m[n] for n in _NAMES],
            *[new_v[n] for n in _NAMES])
```

```python
import functools
import math

import jax
import jax.numpy as jnp
from jax import lax
from jax.experimental import pallas as pl
from jax.experimental.pallas import tpu as pltpu
from jax.scipy.linalg import block_diag

F32 = jnp.float32
MXU = jnp.bfloat16
MESH = pl.DeviceIdType.MESH

D = 1024
DEPTH = 2
CHUNK = 64
GM_W = 512
GM_G = 4
HEADS = 8
Q_RANK = 256
KV_RANK = 128
NOPE = 64
ROPE = 32
VD = 64
LRU_W = 512
LRU_NB = 8
D_FF = 4096
ALPHA = (2.0 * DEPTH) ** 0.25
LN_EPS = 1e-5
RMS_EPS = 1e-6
ROPE_BASE = 10000.0
ATT_SCALE = (NOPE + ROPE) ** -0.5
N_IN = 5536
P_GATE, P_GM, P_LX, P_LG, P_Q, P_KV, P_KA, P_KB, NP = 0, 3072, 4096, 4608, 5120, 5376, 5504, 5632, 5760
O_GM, O_Q, O_KV, O_KR, O_LX, O_LG, O_GATE = 0, 1024, 1280, 1408, 1440, 1952, 2464

ADAM_LR, ADAM_B1, ADAM_B2, ADAM_EPS, ADAM_WD, ADAM_STEP = 0.001, 0.9, 0.999, 1e-08, 0.01, 10

ROW_TILE = 512
ATT_TILE = 512
VMEM_LIMIT = 56 * 1024 * 1024
FLAT_W = 1024
FLAT_ALIGN = 32


def _params(sem=None, vmem=VMEM_LIMIT):
    return pltpu.CompilerParams(dimension_semantics=sem, vmem_limit_bytes=vmem)


def _tile(dim, pref, mult=128):
    if dim <= pref:
        return dim
    t = (pref // mult) * mult
    while t >= mult:
        if dim % t == 0:
            return t
        t -= mult
    return dim


def _dot(a, b):
    return lax.dot_general(a, b, (((1,), (0,)), ((), ())), preferred_element_type=F32)


def _dot_nt(a, b):
    return lax.dot_general(a, b, (((1,), (1,)), ((), ())), preferred_element_type=F32)


def _dot_tn(a, b):
    return lax.dot_general(a, b, (((0,), (0,)), ((), ())), preferred_element_type=F32)


def _sigmoid(x):
    return 1.0 / (1.0 + jnp.exp(-x))


_GC = 0.7978845608028654


def _gelu(x):
    return 0.5 * x * (1.0 + jnp.tanh(_GC * (x + 0.044715 * x * x * x)))


def _gelu_grad(x):
    t = jnp.tanh(_GC * (x + 0.044715 * x * x * x))
    return 0.5 * (1.0 + t) + 0.5 * x * (1.0 - t * t) * _GC * (1.0 + 3.0 * 0.044715 * x * x)


def _ln_stats(x):
    mu = jnp.mean(x, axis=-1, keepdims=True)
    xc = x - mu
    r = lax.rsqrt(jnp.mean(xc * xc, axis=-1, keepdims=True) + LN_EPS)
    return xc * r, r


def _ln_bwd(dxh, xh, r):
    return r * (dxh - jnp.mean(dxh, axis=-1, keepdims=True) - xh * jnp.mean(dxh * xh, axis=-1, keepdims=True))


def _colsum(v):
    return jnp.sum(v, axis=0, keepdims=True)


def _acc(ref, first, val):
    @pl.when(first)
    def _():
        ref[...] = val

    @pl.when(jnp.logical_not(first))
    def _():
        ref[...] += val


def _row_spec(t, w, col=0):
    return pl.BlockSpec((t, w), lambda i, c=col: (i, c))


def _vec_spec(w):
    return pl.BlockSpec((1, w), lambda i: (0, 0))


def _full_spec(shape):
    nd = len(shape)
    return pl.BlockSpec(shape, lambda i, n=nd: (0,) * n)


def _mm(a, b, mode, name, out_dtypes=(F32,), extras=(), rows=(), epi=None, colsum=False, tm=1024, tn=1152, tk=2048):
    if mode == "nn":
        (M, K), N = a.shape, b.shape[1]
    elif mode == "nt":
        (M, K), N = a.shape, b.shape[0]
    else:
        (K, M), N = a.shape, b.shape[1]
    tm, tn, tk = _tile(M, tm), _tile(N, tn), _tile(K, tk)
    nk = K // tk
    dot = {"nn": _dot, "nt": _dot_nt, "tn": _dot_tn}[mode]
    a_spec = pl.BlockSpec((tk, tm), lambda i, j, k: (k, i)) if mode == "tn" else pl.BlockSpec((tm, tk), lambda i, j, k: (i, k))
    b_spec = pl.BlockSpec((tn, tk), lambda i, j, k: (j, k)) if mode == "nt" else pl.BlockSpec((tk, tn), lambda i, j, k: (k, j))
    o_spec = pl.BlockSpec((tm, tn), lambda i, j, k: (i, j))
    r_spec = pl.BlockSpec((1, tn), lambda i, j, k: (0, j))
    n_e, n_r, n_o = len(extras), len(rows), len(out_dtypes)
    if epi is None:
        epi = lambda acc: (acc,)

    def body(*refs):
        a_ref, b_ref = refs[0], refs[1]
        e_refs = refs[2:2 + n_e]
        r_refs = refs[2 + n_e:2 + n_e + n_r]
        o_refs = refs[2 + n_e + n_r:2 + n_e + n_r + n_o]
        p = dot(a_ref[...], b_ref[...])

        def finish(acc):
            outs = epi(acc, *[e[...] for e in e_refs], *[r[...] for r in r_refs])
            for o_ref, o in zip(o_refs, outs):
                o_ref[...] = o.astype(o_ref.dtype)
            if colsum:
                cs_ref = refs[2 + n_e + n_r + n_o]
                cols = pl.ds(pl.multiple_of(pl.program_id(1) * tn, 128), tn)
                _acc(cs_ref.at[:, cols], pl.program_id(0) == 0, _colsum(outs[0]))

        if nk == 1:
            finish(p)
        else:
            acc_ref = refs[-1]
            k = pl.program_id(2)
            _acc(acc_ref, k == 0, p)

            @pl.when(k == nk - 1)
            def _():
                finish(acc_ref[...])

    outs = pl.pallas_call(
        body, name=name, grid=(M // tm, N // tn, nk),
        in_specs=[a_spec, b_spec] + [o_spec] * n_e + [r_spec] * n_r,
        out_specs=[o_spec] * n_o + ([pl.BlockSpec((1, N), lambda i, j, k: (0, 0))] if colsum else []),
        out_shape=[jax.ShapeDtypeStruct((M, N), dt) for dt in out_dtypes] + ([jax.ShapeDtypeStruct((1, N), F32)] if colsum else []),
        scratch_shapes=[pltpu.VMEM((tm, tn), F32)] if nk > 1 else [],
        compiler_params=_params(("arbitrary",) * 3 if colsum else ("parallel", "parallel", "arbitrary")),
    )(a, b, *extras, *rows)
    return outs[0] if len(outs) == 1 else outs


def _lnmod_fwd(x, sc, sh, name):
    S = x.shape[0]
    t = min(ROW_TILE, S)

    def body(x_ref, sc_ref, sh_ref, h_ref):
        xh, _ = _ln_stats(x_ref[...])
        h_ref[...] = (xh * (1.0 + sc_ref[...]) + sh_ref[...]).astype(h_ref.dtype)

    return pl.pallas_call(
        body, name=name, grid=(S // t,),
        in_specs=[_row_spec(t, D), _vec_spec(D), _vec_spec(D)], out_specs=_row_spec(t, D),
        out_shape=jax.ShapeDtypeStruct((S, D), MXU), compiler_params=_params(("parallel",)),
    )(x, sc, sh)


def _res_ln_fwd(xprev, br, gvec, lng, lnb, name, mod=None):
    S = xprev.shape[0]
    t = min(ROW_TILE, S)
    with_h = mod is not None

    def body(*refs):
        xp_ref, br_ref, g_ref, lg_ref, lb_ref = refs[:5]
        u = ALPHA * xp_ref[...] + (1.0 + g_ref[...]) * br_ref[...]
        uh, _ = _ln_stats(u)
        xn = uh * lg_ref[...] + lb_ref[...]
        if with_h:
            sc_ref, sh_ref, xn_ref, h_ref = refs[5:]
            xh, _ = _ln_stats(xn)
            h_ref[...] = (xh * (1.0 + sc_ref[...]) + sh_ref[...]).astype(h_ref.dtype)
        else:
            xn_ref = refs[5]
        xn_ref[...] = xn

    ins = [xprev, br, gvec, lng, lnb] + (list(mod) if with_h else [])
    return pl.pallas_call(
        body, name=name, grid=(S // t,),
        in_specs=[_row_spec(t, D), _row_spec(t, D)] + [_vec_spec(D)] * (len(ins) - 2),
        out_specs=[_row_spec(t, D)] * (2 if with_h else 1),
        out_shape=[jax.ShapeDtypeStruct((S, D), F32)] + ([jax.ShapeDtypeStruct((S, D), MXU)] if with_h else []),
        compiler_params=_params(("parallel",)),
    )(*ins)


def _loss_fwd(y, tgt):
    S = y.shape[0]
    t = min(ROW_TILE, S)

    def body(y_ref, t_ref, dy_ref, sq_ref):
        d = y_ref[...] - t_ref[...]
        dy_ref[...] = d * (1.0 / D)
        _acc(sq_ref, pl.program_id(0) == 0, _colsum(d * d))

    return pl.pallas_call(
        body, name="loss_head", grid=(S // t,),
        in_specs=[_row_spec(t, D), _row_spec(t, D)], out_specs=[_row_spec(t, D), _vec_spec(D)],
        out_shape=[jax.ShapeDtypeStruct((S, D), F32), jax.ShapeDtypeStruct((1, D), F32)],
        compiler_params=_params(("arbitrary",)),
    )(y, tgt)


def _node_bwd(name, dres, *, hpath=None, upath=None):
    S = dres.shape[0]
    t = min(ROW_TILE, S)
    has_h, has_u = hpath is not None, upath is not None

    def body(*refs):
        refs = list(refs)
        first = pl.program_id(0) == 0
        dxs = refs.pop(0)[...]
        if has_h:
            dh = refs.pop(0)[...]
            xs = refs.pop(0)[...]
            sc = refs.pop(0)[...]
        if has_u:
            xp = refs.pop(0)[...]
            br = refs.pop(0)[...]
            gv = refs.pop(0)[...]
            lg = refs.pop(0)[...]
        if has_h:
            dsc_ref, dsh_ref = refs.pop(0), refs.pop(0)
            xh, r = _ln_stats(xs)
            _acc(dsc_ref, first, _colsum(dh * xh))
            _acc(dsh_ref, first, _colsum(dh))
            dxs = dxs + _ln_bwd(dh * (1.0 + sc), xh, r)
        if has_u:
            du_ref, dbr_ref, dlg_ref, dlb_ref, dg_ref, dbs_ref = refs
            uh, ru = _ln_stats(ALPHA * xp + (1.0 + gv) * br)
            _acc(dlg_ref, first, _colsum(dxs * uh))
            _acc(dlb_ref, first, _colsum(dxs))
            du = _ln_bwd(dxs * lg, uh, ru)
            _acc(dg_ref, first, _colsum(du * br))
            du_ref[...] = ALPHA * du
            dbr = (1.0 + gv) * du
            _acc(dbs_ref, first, _colsum(dbr))
            dbr_ref[...] = dbr.astype(dbr_ref.dtype)
        else:
            refs[0][...] = dxs

    ins, in_specs = [dres], [_row_spec(t, D)]
    outs, out_specs, names = [], [], []
    vec = jax.ShapeDtypeStruct((1, D), F32)
    if has_h:
        ins += list(hpath)
        in_specs += [_row_spec(t, D), _row_spec(t, D), _vec_spec(D)]
        outs += [vec, vec]
        out_specs += [_vec_spec(D), _vec_spec(D)]
        names += ["dsc", "dsh"]
    if has_u:
        ins += list(upath)
        in_specs += [_row_spec(t, D), _row_spec(t, D), _vec_spec(D), _vec_spec(D)]
        outs += [jax.ShapeDtypeStruct((S, D), F32), jax.ShapeDtypeStruct((S, D), MXU), vec, vec, vec, vec]
        out_specs += [_row_spec(t, D), _row_spec(t, D), _vec_spec(D), _vec_spec(D), _vec_spec(D), _vec_spec(D)]
        names += ["du", "dbr", "dlng", "dlnb", "dg", "dbrsum"]
    else:
        outs += [jax.ShapeDtypeStruct((S, D), F32)]
        out_specs += [_row_spec(t, D)]
        names += ["dx"]
    res = pl.pallas_call(
        body, name=name, grid=(S // t,), in_specs=in_specs, out_specs=out_specs, out_shape=outs,
        compiler_params=_params(("arbitrary",)),
    )(*ins)
    return dict(zip(names, res))


def _colsum_call(a, name):
    S, N = a.shape
    t, tn = min(ROW_TILE, S), _tile(N, 1152)

    def body(a_ref, o_ref):
        _acc(o_ref, pl.program_id(1) == 0, _colsum(a_ref[...].astype(F32)))

    return pl.pallas_call(
        body, name=name, grid=(N // tn, S // t),
        in_specs=[pl.BlockSpec((t, tn), lambda j, i: (i, j))], out_specs=pl.BlockSpec((1, tn), lambda j, i: (0, j)),
        out_shape=jax.ShapeDtypeStruct((1, N), F32), compiler_params=_params(("parallel", "arbitrary")),
    )(a)


def _gm_mask():
    i = lax.broadcasted_iota(jnp.int32, (128, 128), 0) // CHUNK
    j = lax.broadcasted_iota(jnp.int32, (128, 128), 1) // CHUNK
    return i >= j


def _gmlp_common(z, lng, lnb):
    gz = _gelu(z)
    u, v = gz[:, :GM_W], gz[:, GM_W:]
    vh, r = _ln_stats(v)
    return u, vh, r, vh * lng + lnb


def _gmlp_fwd(z, lng, lnb, ws, bst, name):
    S = z.shape[0]
    t = min(ROW_TILE, S)

    def body(z_ref, lg_ref, lb_ref, ws_ref, bs_ref, y_ref):
        u, _, _, vn = _gmlp_common(z_ref[...], lg_ref[...], lb_ref[...])
        mask = _gm_mask()
        vb = vn.astype(MXU)
        for g in range(GM_G):
            w = jnp.where(mask, ws_ref[g], jnp.zeros_like(ws_ref[g]))
            bias = bs_ref[:, g:g + 1]
            for blk in range(t // 128):
                rs, cs = slice(blk * 128, (blk + 1) * 128), slice(g * 128, (g + 1) * 128)
                f = _dot(w, vb[rs, cs]) + bias
                y_ref[rs, cs] = (u[rs, cs] * f).astype(y_ref.dtype)

    return pl.pallas_call(
        body, name=name, grid=(S // t,),
        in_specs=[_row_spec(t, 2 * GM_W, P_GM // (2 * GM_W)), _vec_spec(GM_W), _vec_spec(GM_W),
                  _full_spec((GM_G, 128, 128)), _full_spec((128, GM_G))],
        out_specs=_row_spec(t, GM_W), out_shape=jax.ShapeDtypeStruct((S, GM_W), MXU),
        compiler_params=_params(("parallel",)),
    )(z, lng, lnb, ws, bst)


def _gmlp_bwd(z, dya, lng, lnb, ws, wst, bst, dz_in, name):
    S = z.shape[0]
    t = min(ROW_TILE, S)

    def body(z_ref, dy_ref, lg_ref, lb_ref, ws_ref, wst_ref, bs_ref, _dz_in, dz_ref, dlg_ref, dlb_ref, dws_ref, dbs_ref, sum_ref):
        first = pl.program_id(0) == 0
        zz = z_ref[...]
        u, vh, r, vn = _gmlp_common(zz, lg_ref[...], lb_ref[...])
        dy = dy_ref[...]
        mask = _gm_mask()
        maskt = lax.broadcasted_iota(jnp.int32, (128, 128), 1) // CHUNK >= lax.broadcasted_iota(jnp.int32, (128, 128), 0) // CHUNK
        lane = lax.broadcasted_iota(jnp.int32, (128, 128), 1)
        vb = vn.astype(MXU)
        dfb = (dy * u).astype(MXU)
        df32 = dy * u
        dbs = jnp.zeros((128, 128), F32)
        du_cols, dvn_cols = [], []
        for g in range(GM_G):
            w = jnp.where(mask, ws_ref[g], jnp.zeros_like(ws_ref[g]))
            wt = jnp.where(maskt, wst_ref[g], jnp.zeros_like(wst_ref[g]))
            bias = bs_ref[:, g:g + 1]
            cs = slice(g * 128, (g + 1) * 128)
            dw = jnp.zeros((128, 128), F32)
            du_rows, dvn_rows = [], []
            for blk in range(t // 128):
                rs = slice(blk * 128, (blk + 1) * 128)
                f = _dot(w, vb[rs, cs]) + bias
                du_rows.append(dy[rs, cs] * f)
                dvn_rows.append(_dot(wt, dfb[rs, cs]))
                dw = dw + _dot_nt(dfb[rs, cs], vb[rs, cs])
                dbs = dbs + jnp.where(lane == g, jnp.sum(df32[rs, cs], axis=1, keepdims=True), 0.0)
            _acc(dws_ref.at[g], first, jnp.where(mask, dw, 0.0))
            du_cols.append(jnp.concatenate(du_rows, axis=0))
            dvn_cols.append(jnp.concatenate(dvn_rows, axis=0))
        _acc(dbs_ref, first, dbs)
        du = jnp.concatenate(du_cols, axis=1)
        dvn = jnp.concatenate(dvn_cols, axis=1)
        _acc(dlg_ref, first, _colsum(dvn * vh))
        _acc(dlb_ref, first, _colsum(dvn))
        dv = _ln_bwd(dvn * lg_ref[...], vh, r)
        dzz = jnp.concatenate([du, dv], axis=1) * _gelu_grad(zz)
        dz_ref[...] = dzz.astype(dz_ref.dtype)
        _acc(sum_ref, first, _colsum(dzz))

    vec = jax.ShapeDtypeStruct((1, GM_W), F32)
    return pl.pallas_call(
        body, name=name, grid=(S // t,),
        in_specs=[_row_spec(t, 2 * GM_W, P_GM // (2 * GM_W)), _row_spec(t, GM_W), _vec_spec(GM_W), _vec_spec(GM_W),
                  _full_spec((GM_G, 128, 128)), _full_spec((GM_G, 128, 128)), _full_spec((128, GM_G)), _ANY],
        out_specs=[_row_spec(t, 2 * GM_W, P_GM // (2 * GM_W)), _vec_spec(GM_W), _vec_spec(GM_W), _full_spec((GM_G, 128, 128)),
                   _full_spec((128, 128)), _vec_spec(2 * GM_W)],
        out_shape=[jax.ShapeDtypeStruct((S, NP), MXU), vec, vec,
                   jax.ShapeDtypeStruct((GM_G, 128, 128), F32), jax.ShapeDtypeStruct((128, 128), F32),
                   jax.ShapeDtypeStruct((1, 2 * GM_W), F32)],
        input_output_aliases=_dz_alias(8), compiler_params=_params(("arbitrary",)),
    )(z, dya, lng, lnb, ws, wst, bst, dz_in)


def _rms(x, g):
    r = lax.rsqrt(jnp.mean(x * x, axis=-1, keepdims=True) + RMS_EPS)
    xh = x * r
    return xh, r, xh * g


def _mla_specs(t):
    return [_row_spec(t, Q_RANK, P_Q // Q_RANK), _row_spec(t, KV_RANK, P_KV // 128), _row_spec(t, 128, P_KA // 128),
            _row_spec(t, 128, P_KB // 128), _row_spec(t, 128), _row_spec(t, 128), _vec_spec(Q_RANK), _vec_spec(KV_RANK),
            _full_spec((Q_RANK, D)), _full_spec((Q_RANK, D)), _full_spec((KV_RANK, D)), _full_spec((HEADS, KV_RANK, VD))]


def _mla_prep_fwd(z, cos, sin, gq, gkv, w1, w2, wk, wv, wvt, name):
    S = z.shape[0]
    t = min(ROW_TILE, S)

    def body(zq_ref, zkv_ref, zka_ref, zkb_ref, cos_ref, sin_ref, gq_ref, gkv_ref, w1_ref, w2_ref, wk_ref, wv_ref, wvt_ref,
             q_ref, k_ref, v_ref, vt_ref):
        cos_, sin_ = cos_ref[...], sin_ref[...]
        cq = cos_ + jnp.where(lax.broadcasted_iota(jnp.int32, cos_.shape, 1) < NOPE, 1.0, 0.0)
        qn = _rms(zq_ref[...], gq_ref[...])[2].astype(MXU)
        q1, q2 = _dot(qn, w1_ref[...]), _dot(qn, w2_ref[...])
        kvn = _rms(zkv_ref[...], gkv_ref[...])[2].astype(MXU)
        kn = _dot(kvn, wk_ref[...])
        krot = zka_ref[...] * cos_ + zkb_ref[...] * sin_
        for h in range(HEADS):
            hs = slice(h * 128, (h + 1) * 128)
            q_ref[h] = (q1[:, hs] * cq + q2[:, hs] * sin_).astype(q_ref.dtype)
            k_ref[h] = (kn[:, hs] + krot).astype(k_ref.dtype)
            v_ref[h] = _dot(kvn, wv_ref[h]).astype(v_ref.dtype)
            vt_ref[h] = _dot_nt(wvt_ref[h], kvn).astype(vt_ref.dtype)

    hspec = lambda w: pl.BlockSpec((HEADS, t, w), lambda i: (0, i, 0))
    return pl.pallas_call(
        body, name=name, grid=(S // t,), in_specs=_mla_specs(t) + [_full_spec((HEADS, VD, KV_RANK))],
        out_specs=[hspec(128), hspec(128), hspec(VD), pl.BlockSpec((HEADS, VD, t), lambda i: (0, 0, i))],
        out_shape=[jax.ShapeDtypeStruct((HEADS, S, 128), MXU), jax.ShapeDtypeStruct((HEADS, S, 128), MXU),
                   jax.ShapeDtypeStruct((HEADS, S, VD), MXU), jax.ShapeDtypeStruct((HEADS, VD, S), MXU)],
        compiler_params=_params(("parallel",)),
    )(z, z, z, z, cos, sin, gq, gkv, w1, w2, wk, wv, wvt)


def _mla_prep_bwd(z, cos, sin, gq, gkv, w1, w2, wk, wv, dq, dk, dv, dz_in, name):
    S = z.shape[0]
    t = min(ROW_TILE, S)

    def body(zq_ref, zkv_ref, zka_ref, zkb_ref, cos_ref, sin_ref, gq_ref, gkv_ref, w1_ref, w2_ref, wk_ref, wv_ref,
             dq_ref, dk_ref, dv_ref, _dz_in, dz_ref, dgq_ref, dgkv_ref, dw1_ref, dw2_ref, dwk_ref, dwv_ref, sum_ref):
        first = pl.program_id(0) == 0
        cos_, sin_ = cos_ref[...], sin_ref[...]
        cq = cos_ + jnp.where(lax.broadcasted_iota(jnp.int32, cos_.shape, 1) < NOPE, 1.0, 0.0)
        gq_, gkv_ = gq_ref[...], gkv_ref[...]
        xhq, rq, qn32 = _rms(zq_ref[...], gq_)
        qn = qn32.astype(MXU)
        dq1 = jnp.concatenate([dq_ref[h] * cq for h in range(HEADS)], axis=1).astype(MXU)
        dq2 = jnp.concatenate([dq_ref[h] * sin_ for h in range(HEADS)], axis=1).astype(MXU)
        dqn = _dot_nt(dq1, w1_ref[...]) + _dot_nt(dq2, w2_ref[...])
        _acc(dw1_ref, first, _dot_tn(qn, dq1))
        _acc(dw2_ref, first, _dot_tn(qn, dq2))
        _acc(dgq_ref, first, _colsum(dqn * xhq))
        dxn = dqn * gq_
        dzq = rq * (dxn - xhq * jnp.mean(dxn * xhq, axis=-1, keepdims=True))

        xhk, rk, kvn32 = _rms(zkv_ref[...], gkv_)
        kvn = kvn32.astype(MXU)
        dks = [dk_ref[h] for h in range(HEADS)]
        dkall = jnp.concatenate(dks, axis=1).astype(MXU)
        dkrot = functools.reduce(lambda p, q_: p + q_, dks)
        dkvn = _dot_nt(dkall, wk_ref[...])
        _acc(dwk_ref, first, _dot_tn(kvn, dkall))
        for h in range(HEADS):
            dvb = dv_ref[h].astype(MXU)
            dkvn = dkvn + _dot_nt(dvb, wv_ref[h])
            _acc(dwv_ref.at[h], first, _dot_tn(kvn, dvb))
        _acc(dgkv_ref, first, _colsum(dkvn * xhk))
        dxk = dkvn * gkv_
        dzkv = rk * (dxk - xhk * jnp.mean(dxk * xhk, axis=-1, keepdims=True))
        dzm = jnp.concatenate([dzq, dzkv, dkrot * cos_, dkrot * sin_], axis=1)
        dz_ref[...] = dzm.astype(dz_ref.dtype)
        _acc(sum_ref, first, _colsum(dzm))

    hspec = lambda w: pl.BlockSpec((HEADS, t, w), lambda i: (0, i, 0))
    sds = jax.ShapeDtypeStruct
    return pl.pallas_call(
        body, name=name, grid=(S // t,),
        in_specs=_mla_specs(t) + [hspec(128), hspec(128), hspec(VD), _ANY],
        out_specs=[_row_spec(t, 640, P_Q // 640), _vec_spec(Q_RANK), _vec_spec(KV_RANK), _full_spec((Q_RANK, D)),
                   _full_spec((Q_RANK, D)), _full_spec((KV_RANK, D)), _full_spec((HEADS, KV_RANK, VD)), _vec_spec(640)],
        out_shape=[sds((S, NP), MXU), sds((1, Q_RANK), F32), sds((1, KV_RANK), F32), sds((Q_RANK, D), F32),
                   sds((Q_RANK, D), F32), sds((KV_RANK, D), F32), sds((HEADS, KV_RANK, VD), F32), sds((1, 640), F32)],
        input_output_aliases=_dz_alias(16), compiler_params=_params(("arbitrary",)),
    )(z, z, z, z, cos, sin, gq, gkv, w1, w2, wk, wv, dq, dk, dv, dz_in)


def _chunk_mask(t, transposed):
    r = lax.broadcasted_iota(jnp.int32, (t, t), 0) // CHUNK
    c = lax.broadcasted_iota(jnp.int32, (t, t), 1) // CHUNK
    return (r <= c) if transposed else (c <= r)


def _attn_fwd(q, k, vt, name, gather=None):
    S = q.shape[1]
    t = min(ATT_TILE, S)
    n = S // t

    hb = 4
    nh = HEADS // hb

    def body(*refs):
        if gather is None:
            q_ref, k_ref, vt_ref, ob_ref, o_ref, lse_ref = refs
        else:
            q_ref, k_ref, vt_ref, w_ref, ob_ref, o_ref, lse_ref, g_ref, send_sems, recv_sems = refs
            hi, step = pl.program_id(0), pl.program_id(1)
            for stage, at in (("start", (0, 0)), ("forward", (nh // 2, 0))):
                pl.when((hi == at[0]) & (step == at[1]))(functools.partial(_ag_stage, stage, w_ref, g_ref, send_sems, recv_sems))
        qi = pl.program_id(1)
        qbs = [q_ref[g] for g in range(hb)]

        def block(j, carries, masked):
            cols = pl.ds(pl.multiple_of(j * t, t), t)
            out = []
            for g in range(hb):
                m, l, acc = carries[g]
                st = _dot_nt(k_ref[g, cols, :], qbs[g]) * ATT_SCALE
                if masked:
                    st = jnp.where(_chunk_mask(t, True), st, -jnp.inf)
                m_new = jnp.maximum(m, jnp.max(st, axis=0, keepdims=True))
                p = jnp.exp(st - m_new)
                alpha = jnp.exp(m - m_new)
                l = alpha * l + jnp.sum(p, axis=0, keepdims=True)
                acc = alpha * acc + _dot(vt_ref[g, :, cols], p.astype(MXU))
                out.append((m_new, l, acc))
            return tuple(out)

        init = tuple((jnp.full((1, t), -jnp.inf, F32), jnp.zeros((1, t), F32), jnp.zeros((VD, t), F32)) for _ in range(hb))
        carries = lax.fori_loop(0, qi, lambda j, c: block(j, c, False), init)
        for g, (m, l, acc) in enumerate(block(qi, carries, True)):
            o = (acc / l).T
            o_ref[g] = o
            ob_ref[g] = o.astype(ob_ref.dtype)
            lse_ref[g] = m + jnp.log(l)
        if gather is not None:
            pl.when((hi == nh - 1) & (step == n - 1))(functools.partial(_ag_stage, "finish", w_ref, g_ref, send_sems, recv_sems))

    qspec = lambda w: pl.BlockSpec((hb, t, w), lambda h, i: (h, i, 0))
    sds = jax.ShapeDtypeStruct
    comm = gather is not None
    return pl.pallas_call(
        body, name=name, grid=(nh, n),
        in_specs=[qspec(128), pl.BlockSpec((hb, S, 128), lambda h, i: (h, 0, 0)), pl.BlockSpec((hb, VD, S), lambda h, i: (h, 0, 0))]
        + ([_ANY] if comm else []),
        out_specs=[qspec(VD), qspec(VD), pl.BlockSpec((hb, 1, t), lambda h, i: (h, 0, i))] + ([_ANY] if comm else []),
        out_shape=[sds((HEADS, S, VD), MXU), sds((HEADS, S, VD), F32), sds((HEADS, 1, S), F32)]
        + ([sds((4,) + gather.shape, gather.dtype)] if comm else []),
        scratch_shapes=list(_AG_SEMS) if comm else [],
        compiler_params=_params(("arbitrary", "arbitrary") if comm else ("parallel", "arbitrary")),
    )(*((q, k, vt) + ((gather,) if comm else ())))


def _attn_delta(o, do, name):
    S = o.shape[1]
    t = min(ROW_TILE, S)

    def body(o_ref, do_ref, d_ref):
        for h in range(HEADS):
            d_ref[h] = jnp.sum(o_ref[h] * do_ref[h], axis=1, keepdims=True)

    hspec = lambda w: pl.BlockSpec((HEADS, t, w), lambda i: (0, i, 0))
    return pl.pallas_call(
        body, name=name, grid=(S // t,), in_specs=[hspec(VD), hspec(VD)], out_specs=hspec(1),
        out_shape=jax.ShapeDtypeStruct((HEADS, S, 1), F32), compiler_params=_params(("parallel",)),
    )(o, do)


def _scatter_stage(stage, g_ref, land_ref, send_sems, recv_sems):
    rh = g_ref.shape[1] // 2
    x, y, c = _coords()
    cps = []
    for p in range(1, 8):
        tx, ty, tc = x ^ ((p >> 2) & 1), y ^ ((p >> 1) & 1), c ^ (p & 1)
        rows = pl.ds(pl.multiple_of(tc * rh, FLAT_ALIGN // 2), rh)
        cps.append(pltpu.make_async_remote_copy(src_ref=g_ref.at[2 * tx + ty, rows, :], dst_ref=land_ref.at[p - 1],
                                                send_sem=send_sems.at[p - 1], recv_sem=recv_sems.at[p - 1], device_id=(tx, ty, tc),
                                                device_id_type=MESH))
    for cp in cps:
        if stage == "start":
            cp.start()
        else:
            cp.wait()


def _attn_bwd(q, k, v, do, lse_row, d_row, name, scatter=None):
    S = q.shape[1]
    t = min(ATT_TILE, S)
    n = S // t

    def body(*refs):
        if scatter is None:
            q_ref, do_ref, lse_ref, d_ref, k_ref, v_ref, dq_ref, dk_ref, dv_ref, dk_s, dv_s = refs
        else:
            q_ref, do_ref, lse_ref, d_ref, k_ref, v_ref, g_ref, dq_ref, dk_ref, dv_ref, land_ref, dk_s, dv_s, send_sems, recv_sems = refs
            pl.when((pl.program_id(0) == 0) & (pl.program_id(1) == 0))(
                functools.partial(_scatter_stage, "start", g_ref, land_ref, send_sems, recv_sems))
        ki = pl.program_id(1)

        @pl.when(ki == 0)
        def _():
            dq_ref[...] = jnp.zeros(dq_ref.shape, F32)

        dk_s[...] = jnp.zeros(dk_s.shape, F32)
        dv_s[...] = jnp.zeros(dv_s.shape, F32)
        kb, vb = k_ref[0], v_ref[0]

        def step(qi, masked):
            rows = pl.ds(pl.multiple_of(qi * t, t), t)
            qb, dob = q_ref[0, rows, :], do_ref[0, rows, :]
            st = _dot_nt(kb, qb) * ATT_SCALE
            if masked:
                st = jnp.where(_chunk_mask(t, True), st, -jnp.inf)
            pt = jnp.exp(st - lse_ref[0, :, rows])
            dv_s[...] += _dot(pt.astype(MXU), dob)
            dpt = _dot_nt(vb, dob)
            dst = (pt * (dpt - d_ref[0, :, rows]) * ATT_SCALE).astype(MXU)
            dk_s[...] += _dot(dst, qb)
            dq_ref[0, rows, :] += _dot_tn(dst, kb)

        step(ki, True)

        def loop(qi, c):
            step(qi, False)
            return c

        lax.fori_loop(ki + 1, n, loop, 0)
        dk_ref[0] = dk_s[...]
        dv_ref[0] = dv_s[...]
        if scatter is not None:
            pl.when((pl.program_id(0) == HEADS - 1) & (ki == n - 1))(
                functools.partial(_scatter_stage, "finish", g_ref, land_ref, send_sems, recv_sems))

    head = lambda *shape: pl.BlockSpec((1,) + shape, lambda h, j: (h, 0, 0))
    kmap = lambda h, j: (h, j, 0)
    sds = jax.ShapeDtypeStruct
    comm = scatter is not None
    return pl.pallas_call(
        body, name=name, grid=(HEADS, n),
        in_specs=[head(S, 128), head(S, VD), head(1, S), head(1, S), pl.BlockSpec((1, t, 128), kmap), pl.BlockSpec((1, t, VD), kmap)]
        + ([_ANY] if comm else []),
        out_specs=[head(S, 128), pl.BlockSpec((1, t, 128), kmap), pl.BlockSpec((1, t, VD), kmap)] + ([_ANY] if comm else []),
        out_shape=[sds((HEADS, S, 128), F32), sds((HEADS, S, 128), F32), sds((HEADS, S, VD), F32)]
        + ([sds((7, scatter.shape[1] // 2, scatter.shape[2]), scatter.dtype)] if comm else []),
        scratch_shapes=[pltpu.VMEM((t, 128), F32), pltpu.VMEM((t, VD), F32)]
        + ([pltpu.SemaphoreType.DMA((7,)), pltpu.SemaphoreType.DMA((7,))] if comm else []),
        compiler_params=_params(("arbitrary", "arbitrary") if comm else ("parallel", "arbitrary")),
    )(*((q, do, lse_row, d_row, k, v) + ((scatter,) if comm else ())))


def _shift_down(x, prev8, d):
    xr = pltpu.roll(x, d, 0)
    r8 = lax.broadcasted_iota(jnp.int32, prev8.shape, 0)
    top = jnp.where(r8 < d, pltpu.roll(prev8, d, 0), xr[0:8])
    return jnp.concatenate([top, xr[8:]], axis=0)


def _shift_up(x, next8, d):
    n = x.shape[0]
    xr = pltpu.roll(x, n - d, 0)
    r8 = lax.broadcasted_iota(jnp.int32, next8.shape, 0)
    bot = jnp.where(r8 >= 8 - d, pltpu.roll(next8, 8 - d, 0), xr[n - 8:])
    return jnp.concatenate([xr[:n - 8], bot], axis=0)


def _log1p(u):
    return jnp.where(u < 0.01, u * (1.0 - u * (0.5 - u * (1.0 / 3.0 - 0.25 * u))), jnp.log(1.0 + u))


def _neg_expm1(y):
    series = -y * (1.0 + 0.5 * y * (1.0 + (1.0 / 3.0) * y * (1.0 + 0.25 * y)))
    return jnp.where(y > -0.05, series, 1.0 - jnp.exp(y))


def _softplus_neg(lam):
    return jnp.maximum(-lam, 0.0) + _log1p(jnp.exp(-jnp.abs(lam)))


def _lru_gates(x, prev8, cw, cb, wr, br, wi, bi, lam):
    xs1, xs2, xs3 = _shift_down(x, prev8, 1), _shift_down(x, prev8, 2), _shift_down(x, prev8, 3)
    xc = cb + cw[0:1] * xs3 + cw[1:2] * xs2 + cw[2:3] * xs1 + cw[3:4] * x
    xcb = xc.astype(MXU)
    r = _sigmoid(_dot(xcb, wr) + br)
    ig = _sigmoid(_dot(xcb, wi) + bi)
    sp = _softplus_neg(lam)
    log_a = -8.0 * r * sp
    a = jnp.exp(log_a)
    gb = jnp.sqrt(_neg_expm1(2.0 * log_a))
    return (xs1, xs2, xs3), xc, xcb, r, ig, sp, a, gb


def _lru_fwd(z, cw, cb, wr, br, wi, bi, lam, name):
    S = z.shape[0]
    t = min(ROW_TILE, S)

    def body(zx_ref, zg_ref, cw_ref, cb_ref, wr_ref, br_ref, wi_ref, bi_ref, lam_ref, y_ref, h_ref, xp_s, hc_s):
        @pl.when(pl.program_id(0) == 0)
        def _():
            xp_s[...] = jnp.zeros(xp_s.shape, F32)
            hc_s[...] = jnp.zeros(hc_s.shape, F32)

        x = zx_ref[...]
        _, xc, _, _, ig, _, a, gb = _lru_gates(x, xp_s[...], cw_ref[...], cb_ref[...], wr_ref[...], br_ref[...],
                                               wi_ref[...], bi_ref[...], lam_ref[...])
        b = gb * (ig * xc)
        rows = lax.broadcasted_iota(jnp.int32, a.shape, 0)
        d = 1
        while d < t:
            ar, brr = pltpu.roll(a, d, 0), pltpu.roll(b, d, 0)
            ok = rows >= d
            b = jnp.where(ok, a * brr, 0.0) + b
            a = jnp.where(ok, a * ar, a)
            d *= 2
        h = a * hc_s[7:8, :] + b
        h_ref[...] = h
        y_ref[...] = (h * _gelu(zg_ref[...])).astype(y_ref.dtype)
        hc_s[...] = h[t - 8:, :]
        xp_s[...] = x[t - 8:, :]

    w = LRU_W
    return pl.pallas_call(
        body, name=name, grid=(S // t,),
        in_specs=[_row_spec(t, w, P_LX // w), _row_spec(t, w, P_LG // w), _full_spec((4, w)), _vec_spec(w), _full_spec((w, w)),
                  _vec_spec(w), _full_spec((w, w)), _vec_spec(w), _vec_spec(w)],
        out_specs=[_row_spec(t, w), _row_spec(t, w)],
        out_shape=[jax.ShapeDtypeStruct((S, w), MXU), jax.ShapeDtypeStruct((S, w), F32)],
        scratch_shapes=[pltpu.VMEM((8, w), F32), pltpu.VMEM((8, w), F32)],
        compiler_params=_params(("arbitrary",)),
    )(z, z, cw, cb, wr, br, wi, bi, lam)


def _lru_bwd(z, h, dy, cw, cb, wr, br, wi, bi, lam, dz_in, name):
    S = z.shape[0]
    t = min(ROW_TILE, S)
    n = S // t
    w = LRU_W

    def body(zx_ref, zxp_ref, zg_ref, h_ref, hp_ref, dy_ref, cw_ref, cb_ref, wr_ref, br_ref, wi_ref, bi_ref, lam_ref, _dz_in,
             dz_ref, dcw_ref, dcb_ref, dwr_ref, dbr_ref, dwi_ref, dbi_ref, dlam_ref, sum_ref, gc_s, dn_s):
        i = pl.program_id(0)
        first = i == 0
        j = n - 1 - i

        @pl.when(first)
        def _():
            gc_s[...] = jnp.zeros(gc_s.shape, F32)
            dn_s[...] = jnp.zeros(dn_s.shape, F32)

        live = (j > 0).astype(F32)
        xprev8, hprev8 = zxp_ref[...] * live, hp_ref[...] * live
        x, zg, hh, dy_ = zx_ref[...], zg_ref[...], h_ref[...], dy_ref[...]
        cw_, lam_ = cw_ref[...], lam_ref[...]
        (xs1, xs2, xs3), xc, xcb, r, ig, sp, a, gb = _lru_gates(x, xprev8, cw_, cb_ref[...], wr_ref[...], br_ref[...],
                                                                wi_ref[...], bi_ref[...], lam_)
        hm1 = _shift_down(hh, hprev8, 1)
        dh = dy_ * _gelu(zg)
        dzg = dy_ * hh * _gelu_grad(zg)
        rows = lax.broadcasted_iota(jnp.int32, a.shape, 0)
        last = rows == t - 1
        ca = jnp.where(last, 0.0, pltpu.roll(a, t - 1, 0))
        g = dh + jnp.where(last, gc_s[0:1, :], 0.0)
        d = 1
        while d < t:
            ok = rows < t - d
            g = g + jnp.where(ok, ca * pltpu.roll(g, t - d, 0), 0.0)
            ca = jnp.where(ok, ca * pltpu.roll(ca, t - d, 0), 0.0)
            d *= 2
        gc_s[...] = a[0:8, :] * g[0:8, :]
        da = g * hm1
        dgb = g * (ig * xc)
        dub = g * gb
        di = dub * xc
        dxc = dub * ig
        dlog_a = da * a - dgb * (a * a) / gb
        dr = dlog_a * (-8.0 * sp)
        _acc(dlam_ref, first, _colsum(dlog_a * (-8.0 * r)) * (-_sigmoid(-lam_)))
        dpr = dr * r * (1.0 - r)
        dpi = di * ig * (1.0 - ig)
        _acc(dbr_ref, first, _colsum(dpr))
        _acc(dbi_ref, first, _colsum(dpi))
        dprb, dpib = dpr.astype(MXU), dpi.astype(MXU)
        _acc(dwr_ref, first, _dot_tn(xcb, dprb))
        _acc(dwi_ref, first, _dot_tn(xcb, dpib))
        dxc = dxc + _dot_nt(dprb, wr_ref[...]) + _dot_nt(dpib, wi_ref[...])
        _acc(dcb_ref, first, _colsum(dxc))
        _acc(dcw_ref, first, jnp.concatenate([_colsum(dxc * xs3), _colsum(dxc * xs2), _colsum(dxc * xs1), _colsum(dxc * x)], axis=0))
        nxt = dn_s[...]
        dx = cw_[3:4] * dxc + cw_[2:3] * _shift_up(dxc, nxt, 1) + cw_[1:2] * _shift_up(dxc, nxt, 2) + cw_[0:1] * _shift_up(dxc, nxt, 3)
        dn_s[...] = dxc[0:8, :]
        dz_ref[...] = jnp.concatenate([dx, dzg], axis=1).astype(dz_ref.dtype)
        _acc(sum_ref, first, jnp.concatenate([_colsum(dx), _colsum(dzg)], axis=1))

    rev = lambda col: pl.BlockSpec((t, w), lambda i, c=col: (n - 1 - i, c))
    prev8 = lambda col: pl.BlockSpec((8, w), lambda i, c=col: (jnp.maximum((n - 1 - i) * (t // 8) - 1, 0), c))
    vec = jax.ShapeDtypeStruct((1, w), F32)
    sds = jax.ShapeDtypeStruct
    return pl.pallas_call(
        body, name=name, grid=(n,),
        in_specs=[rev(P_LX // w), prev8(P_LX // w), rev(P_LG // w), rev(0), prev8(0), rev(0), _full_spec((4, w)), _vec_spec(w),
                  _full_spec((w, w)), _vec_spec(w), _full_spec((w, w)), _vec_spec(w), _vec_spec(w), _ANY],
        out_specs=[pl.BlockSpec((t, 2 * w), lambda i: (n - 1 - i, P_LX // (2 * w))), _full_spec((4, w)), _vec_spec(w),
                   _full_spec((w, w)), _vec_spec(w), _full_spec((w, w)), _vec_spec(w), _vec_spec(w), _vec_spec(2 * w)],
        out_shape=[sds((S, NP), MXU), sds((4, w), F32), vec, sds((w, w), F32), vec, sds((w, w), F32), vec, vec, sds((1, 2 * w), F32)],
        scratch_shapes=[pltpu.VMEM((8, w), F32), pltpu.VMEM((8, w), F32)],
        input_output_aliases=_dz_alias(14), compiler_params=_params(("arbitrary",)),
    )(z, z, z, h, h, dy, cw, cb, wr, br, wi, bi, lam, dz_in)


def _branch_fwd(z, ya, o, yc, wa, wb, wc, name):
    S = z.shape[0]
    t = min(ROW_TILE, S)

    def body(ga_ref, gb_ref, gc_ref, ya_ref, o_ref, yc_ref, wa_ref, wb_ref, wc_ref, m_ref, pa_ref, pb_ref, pc_ref):
        pa = _dot(ya_ref[...], wa_ref[...])
        pc = _dot(yc_ref[...], wc_ref[...])
        pb = _dot(o_ref[0], wb_ref[0])
        for h in range(1, HEADS):
            pb = pb + _dot(o_ref[h], wb_ref[h])
        pa_ref[...], pb_ref[...], pc_ref[...] = pa, pb, pc
        m = _sigmoid(ga_ref[...]) * pa + _sigmoid(gb_ref[...]) * pb + _sigmoid(gc_ref[...]) * pc
        m_ref[...] = m.astype(m_ref.dtype)

    g0 = P_GATE // D
    sds = jax.ShapeDtypeStruct
    return pl.pallas_call(
        body, name=name, grid=(S // t,),
        in_specs=[_row_spec(t, D, g0), _row_spec(t, D, g0 + 1), _row_spec(t, D, g0 + 2), _row_spec(t, GM_W),
                  pl.BlockSpec((HEADS, t, VD), lambda i: (0, i, 0)), _row_spec(t, LRU_W),
                  _full_spec((GM_W, D)), _full_spec((HEADS, VD, D)), _full_spec((LRU_W, D))],
        out_specs=[_row_spec(t, D)] * 4,
        out_shape=[sds((S, D), MXU), sds((S, D), F32), sds((S, D), F32), sds((S, D), F32)],
        compiler_params=_params(("parallel",)),
    )(z, z, z, ya, o, yc, wa, wb, wc)


def _dz_alias(n_inputs):
    return {n_inputs - 1: 0}


def _branch_bwd(z, dm, pa, pb, pc, name):
    S = z.shape[0]
    t = min(ROW_TILE, S)

    def body(ga_ref, gb_ref, gc_ref, dm_ref, pa_ref, pb_ref, pc_ref, dz_ref, da_ref, db_ref, dc_ref, sum_ref):
        first = pl.program_id(0) == 0
        dm_ = dm_ref[...]
        for n_, (g_ref, p_ref, d_ref) in enumerate(((ga_ref, pa_ref, da_ref), (gb_ref, pb_ref, db_ref), (gc_ref, pc_ref, dc_ref))):
            gt = _sigmoid(g_ref[...])
            d_ref[...] = (dm_ * gt).astype(d_ref.dtype)
            dzg = dm_ * p_ref[...] * gt * (1.0 - gt)
            dz_ref[:, n_ * D:(n_ + 1) * D] = dzg.astype(dz_ref.dtype)
            _acc(sum_ref.at[:, n_ * D:(n_ + 1) * D], first, _colsum(dzg))

    g0 = P_GATE // D
    sds = jax.ShapeDtypeStruct
    return pl.pallas_call(
        body, name=name, grid=(S // t,),
        in_specs=[_row_spec(t, D, g0), _row_spec(t, D, g0 + 1), _row_spec(t, D, g0 + 2)] + [_row_spec(t, D)] * 4,
        out_specs=[_row_spec(t, 3 * D, P_GATE // (3 * D))] + [_row_spec(t, D)] * 3 + [_vec_spec(3 * D)],
        out_shape=[sds((S, NP), MXU)] + [sds((S, D), MXU)] * 3 + [sds((1, 3 * D), F32)],
        compiler_params=_params(("arbitrary",)),
    )(z, z, z, dm, pa, pb, pc)


def _heads_bwd(dpb, o, wb, name):
    S = dpb.shape[0]
    t = min(ROW_TILE, S)

    def body(dp_ref, o_ref, wb_ref, dob_ref, do_ref, dwb_ref):
        first = pl.program_id(0) == 0
        dp = dp_ref[...]
        for h in range(HEADS):
            do = _dot_nt(dp, wb_ref[h])
            do_ref[h] = do
            dob_ref[h] = do.astype(dob_ref.dtype)
            _acc(dwb_ref.at[h], first, _dot_tn(o_ref[h], dp))

    hspec = pl.BlockSpec((HEADS, t, VD), lambda i: (0, i, 0))
    sds = jax.ShapeDtypeStruct
    return pl.pallas_call(
        body, name=name, grid=(S // t,),
        in_specs=[_row_spec(t, D), hspec, _full_spec((HEADS, VD, D))],
        out_specs=[hspec, hspec, _full_spec((HEADS, VD, D))],
        out_shape=[sds((HEADS, S, VD), MXU), sds((HEADS, S, VD), F32), sds((HEADS, VD, D), F32)],
        compiler_params=_params(("arbitrary",)),
    )(dpb, o, wb)


def _mod_fwd(c_all, ada_w, name):
    n = ada_w.shape[2]

    def body(c_ref, w_ref, o_ref):
        c = c_ref[...]
        ca = (c * _sigmoid(c)).astype(MXU)
        for l in range(DEPTH):
            o_ref[l] = _dot(ca, w_ref[l].astype(MXU))

    return pl.pallas_call(body, name=name, out_shape=jax.ShapeDtypeStruct((DEPTH, 8, n), F32), compiler_params=_params())(c_all, ada_w)


def _ada_w_grad(c_all_t, dmod, name):
    n = dmod.shape[2]

    def body(c_ref, d_ref, o_ref):
        c = c_ref[...]
        ca = c * _sigmoid(c)
        for l in range(DEPTH):
            dm = d_ref[l]
            acc = ca[:, 0:1] * dm[0:1, :]
            for b in range(1, 8):
                acc = acc + ca[:, b:b + 1] * dm[b:b + 1, :]
            o_ref[l] = acc

    return pl.pallas_call(body, name=name, out_shape=jax.ShapeDtypeStruct((DEPTH, D, n), F32), compiler_params=_params())(c_all_t, dmod)


def _adamw(w, g, m, v, name):
    R, C = w.shape
    t = _tile(R, 256) if R % 8 == 0 else R
    c1, c2 = 1.0 - ADAM_B1 ** ADAM_STEP, 1.0 - ADAM_B2 ** ADAM_STEP

    def body(w_ref, g_ref, m_ref, v_ref, d_ref, nm_ref, nv_ref):
        g_ = g_ref[...]
        m_ = ADAM_B1 * m_ref[...] + (1.0 - ADAM_B1) * g_
        v_ = ADAM_B2 * v_ref[...] + (1.0 - ADAM_B2) * (g_ * g_)
        nm_ref[...] = m_
        nv_ref[...] = v_
        d_ref[...] = -ADAM_LR * ((m_ / c1) / (jnp.sqrt(v_ / c2) + ADAM_EPS) + ADAM_WD * w_ref[...])

    spec = pl.BlockSpec((t, C), lambda i: (i, 0))
    return pl.pallas_call(
        body, name=name, grid=(R // t,), in_specs=[spec] * 4, out_specs=[spec] * 3,
        out_shape=[jax.ShapeDtypeStruct((R, C), F32)] * 3, compiler_params=_params(("parallel",)),
    )(w, g, m, v)


def _sum8(a, name):
    _, R, C = a.shape
    t = _tile(R, 512) if R % 8 == 0 else R

    def body(a_ref, o_ref):
        s = a_ref[0]
        for k in range(1, 8):
            s = s + a_ref[k]
        o_ref[...] = s

    return pl.pallas_call(
        body, name=name, grid=(R // t,), in_specs=[pl.BlockSpec((8, t, C), lambda i: (0, i, 0))],
        out_specs=pl.BlockSpec((t, C), lambda i: (i, 0)), out_shape=jax.ShapeDtypeStruct((R, C), F32),
        compiler_params=_params(("parallel",)),
    )(a)


def _pair_add(shards, from_sib, half, name):
    _, rh, C = from_sib.shape
    t = _tile(rh, 384, 16)
    nb = rh // t

    def body(half_ref, a0, a1, a2, a3, b_ref, o_ref, ob_ref):
        for k, a_ref in enumerate((a0, a1, a2, a3)):
            s = a_ref[...] + b_ref[k]
            o_ref[k] = s
            ob_ref[k] = s.astype(ob_ref.dtype)

    a_spec = pl.BlockSpec((t, C), lambda i, half_ref: (half_ref[0] * nb + i, 0))
    s_spec = pl.BlockSpec((4, t, C), lambda i, half_ref: (0, i, 0))
    return pl.pallas_call(
        body, name=name,
        grid_spec=pltpu.PrefetchScalarGridSpec(num_scalar_prefetch=1, grid=(nb,), in_specs=[a_spec] * 4 + [s_spec],
                                               out_specs=[s_spec, s_spec]),
        out_shape=[jax.ShapeDtypeStruct((4, rh, C), F32), jax.ShapeDtypeStruct((4, rh, C), jnp.bfloat16)],
        compiler_params=_params(("parallel",)),
    )(half, *shards, from_sib)


def _land_sum(g, land, half, name):
    _, rh, C = land.shape
    t = _tile(rh, 256, 16)
    nb = rh // t

    def body(half_ref, g_ref, l_ref, o_ref):
        s = g_ref[...].astype(F32)
        for k in range(7):
            s = s + l_ref[k].astype(F32)
        o_ref[...] = s

    return pl.pallas_call(
        body, name=name,
        grid_spec=pltpu.PrefetchScalarGridSpec(
            num_scalar_prefetch=1, grid=(nb,),
            in_specs=[pl.BlockSpec((t, C), lambda i, h: (h[0] * nb + i, 0)), pl.BlockSpec((7, t, C), lambda i, h: (0, i, 0))],
            out_specs=pl.BlockSpec((t, C), lambda i, h: (h[0] * nb + i, 0))),
        out_shape=jax.ShapeDtypeStruct((2 * rh, C), F32), compiler_params=_params(("parallel",)),
    )(half, g, land)


def _chip_sum(own, recv, half, name):
    rh, C = own.shape
    t = _tile(rh, 512, 16)
    nb = rh // t

    def body(half_ref, a_ref, r_ref, o_ref):
        s = a_ref[...]
        for k in range(3):
            s = s + r_ref[k].astype(F32)
        o_ref[...] = s

    return pl.pallas_call(
        body, name=name,
        grid_spec=pltpu.PrefetchScalarGridSpec(
            num_scalar_prefetch=1, grid=(nb,),
            in_specs=[pl.BlockSpec((t, C), lambda i, h: (i, 0)), pl.BlockSpec((3, t, C), lambda i, h: (0, i, 0))],
            out_specs=pl.BlockSpec((t, C), lambda i, h: (h[0] * nb + i, 0))),
        out_shape=jax.ShapeDtypeStruct((2 * rh, C), F32), compiler_params=_params(("parallel",)),
    )(half, own, recv)


_ANY = pl.BlockSpec(memory_space=pl.ANY)


def _coords():
    return lax.axis_index("x"), lax.axis_index("y"), lax.axis_index("c")


def _allgather8(v, name):
    R, C = v.shape

    def body(v_ref, out_ref, send_sems, recv_sems):
        x, y, c = _coords()
        me = 4 * x + 2 * y + c
        sends = []
        for k in range(1, 8):
            bx, by, bc = (k >> 2) & 1, (k >> 1) & 1, k & 1
            peer = (x ^ bx, y ^ by, c ^ bc)
            cp = pltpu.make_async_remote_copy(src_ref=v_ref, dst_ref=out_ref.at[me], send_sem=send_sems.at[k - 1],
                                              recv_sem=recv_sems.at[k - 1], device_id=peer, device_id_type=MESH)
            cp.start()
            sends.append(cp)
        for k in range(1, 8):
            bx, by, bc = (k >> 2) & 1, (k >> 1) & 1, k & 1
            peer = (x ^ bx, y ^ by, c ^ bc)
            src = 4 * peer[0] + 2 * peer[1] + peer[2]
            pltpu.make_async_remote_copy(src_ref=v_ref, dst_ref=out_ref.at[src], send_sem=send_sems.at[k - 1],
                                         recv_sem=recv_sems.at[k - 1], device_id=peer, device_id_type=MESH).wait_recv()
        for cp in sends:
            cp.wait_send()

    out = pl.pallas_call(
        body, name=name, in_specs=[_ANY], out_specs=_ANY, out_shape=jax.ShapeDtypeStruct((8, R, C), v.dtype),
        scratch_shapes=[pltpu.SemaphoreType.DMA((7,)), pltpu.SemaphoreType.DMA((7,))],
    )(v)
    x, y, c = _coords()
    return lax.dynamic_update_slice(out, v[None], (4 * x + 2 * y + c, 0, 0))


def _chip_patterns(x, y):
    return [(1 - x, y), (x, 1 - y), (1 - x, 1 - y)]


def _ag_stage(stage, w_ref, out_ref, send_sems, recv_sems):
    rh = w_ref.shape[0] // 2
    x, y, c = _coords()
    kme = 2 * x + y
    sibling = (x, y, 1 - c)
    mine_half = pl.ds(pl.multiple_of(c * rh, FLAT_ALIGN // 2), rh)
    other_half = pl.ds(pl.multiple_of((1 - c) * rh, FLAT_ALIGN // 2), rh)
    chips = _chip_patterns(x, y)

    def copy(k, chip_idx, half, to, src=None):
        dst = out_ref.at[chip_idx, half, :]
        return pltpu.make_async_remote_copy(src_ref=dst if src is None else src, dst_ref=dst, send_sem=send_sems.at[k],
                                            recv_sem=recv_sems.at[k], device_id=to, device_id_type=MESH)

    first = [copy(j, kme, mine_half, (cx, cy, c), src=w_ref.at[mine_half, :]) for j, (cx, cy) in enumerate(chips)]
    own = pltpu.make_async_remote_copy(src_ref=w_ref, dst_ref=out_ref.at[kme], send_sem=send_sems.at[6], recv_sem=recv_sems.at[6],
                                       device_id=sibling, device_id_type=MESH)
    passed = [copy(3 + j, 2 * cx + cy, mine_half, sibling) for j, (cx, cy) in enumerate(chips)]
    if stage == "start":
        for cp in first + [own]:
            cp.start()
    elif stage == "forward":
        for j, (cx, cy) in enumerate(chips):
            copy(j, 2 * cx + cy, mine_half, (x, y, c)).wait_recv()
            passed[j].start()
    else:
        for j, (cx, cy) in enumerate(chips):
            copy(3 + j, 2 * cx + cy, other_half, (x, y, c)).wait_recv()
        own.wait_recv()
        for cp in first + [own] + passed:
            cp.wait_send()


_AG_SEMS = [pltpu.SemaphoreType.DMA((7,)), pltpu.SemaphoreType.DMA((7,))]


def _allgather_weights(w, name):
    R, C = w.shape

    def body(w_ref, out_ref, send_sems, recv_sems):
        for stage in ("start", "forward", "finish"):
            _ag_stage(stage, w_ref, out_ref, send_sems, recv_sems)

    return pl.pallas_call(
        body, name=name, in_specs=[_ANY], out_specs=_ANY, out_shape=jax.ShapeDtypeStruct((4, R, C), w.dtype),
        scratch_shapes=list(_AG_SEMS),
    )(w)


def _sibling_swap(shards, name):
    R, C = shards[0].shape
    rh = R // 2

    def body(g0, g1, g2, g3, out_ref, send_sems, recv_sems):
        x, y, c = _coords()
        other_half = pl.ds(pl.multiple_of((1 - c) * rh, 8), rh)
        cps = []
        for k, g_ref in enumerate((g0, g1, g2, g3)):
            cp = pltpu.make_async_remote_copy(src_ref=g_ref.at[other_half, :], dst_ref=out_ref.at[k], send_sem=send_sems.at[k],
                                              recv_sem=recv_sems.at[k], device_id=(x, y, 1 - c), device_id_type=MESH)
            cp.start()
            cps.append(cp)
        for cp in cps:
            cp.wait()

    return pl.pallas_call(
        body, name=name, in_specs=[_ANY] * 4, out_specs=_ANY, out_shape=jax.ShapeDtypeStruct((4, rh, C), shards[0].dtype),
        scratch_shapes=[pltpu.SemaphoreType.DMA((4,)), pltpu.SemaphoreType.DMA((4,))],
    )(*shards)


def _chip_scatter(a, name):
    _, R, C = a.shape

    def body(a_ref, out_ref, send_sems, recv_sems):
        x, y, c = _coords()
        cps = []
        for j, (cx, cy) in enumerate(_chip_patterns(x, y)):
            cp = pltpu.make_async_remote_copy(src_ref=a_ref.at[2 * cx + cy], dst_ref=out_ref.at[j], send_sem=send_sems.at[j],
                                              recv_sem=recv_sems.at[j], device_id=(cx, cy, c), device_id_type=MESH)
            cp.start()
            cps.append(cp)
        for cp in cps:
            cp.wait()

    return pl.pallas_call(
        body, name=name, in_specs=[_ANY], out_specs=_ANY, out_shape=jax.ShapeDtypeStruct((3, R, C), a.dtype),
        scratch_shapes=[pltpu.SemaphoreType.DMA((3,)), pltpu.SemaphoreType.DMA((3,))],
    )(a)


def _sibling_join(f, name):
    R, C = f.shape
    rh = R // 2

    def body(f_ref, out_ref, send_sem, recv_sem):
        x, y, c = _coords()
        mine_half = pl.ds(pl.multiple_of(c * rh, 8), rh)
        cp = pltpu.make_async_remote_copy(src_ref=f_ref.at[mine_half, :], dst_ref=out_ref.at[mine_half, :], send_sem=send_sem,
                                          recv_sem=recv_sem, device_id=(x, y, 1 - c), device_id_type=MESH)
        cp.start()
        cp.wait()

    return pl.pallas_call(
        body, name=name, in_specs=[_ANY], out_specs=_ANY, out_shape=jax.ShapeDtypeStruct((R, C), f.dtype),
        scratch_shapes=[pltpu.SemaphoreType.DMA(()), pltpu.SemaphoreType.DMA(())], input_output_aliases={0: 0},
    )(f)


_BIG = (("in_w", 1), ("mla_wuq", 1), ("mla_wukv", 1), ("branch_w", 2), ("mix_out_w", 0), ("ffn_w1", 1), ("ffn_w2", 0))


def _pack(shards):
    flat = jnp.concatenate([shards[n].reshape(-1, FLAT_W) for n, _ in _BIG], axis=0)
    pad = (-flat.shape[0]) % FLAT_ALIGN
    return jnp.pad(flat, ((0, pad), (0, 0)))


def _unpack(flat, shapes):
    out, r = {}, 0
    for n, _ in _BIG:
        rows = math.prod(shapes[n]) // FLAT_W
        out[n] = flat[r:r + rows].reshape(shapes[n])
        r += rows
    return out


def _swap16(a):
    return jnp.concatenate([a[..., 16:32], a[..., 0:16]], axis=-1)


def _shard_cols(parts, a, b):
    w = parts[0].shape[-1]
    pieces = []
    for k, p in enumerate(parts):
        s, e = max(a, k * w), min(b, (k + 1) * w)
        if s < e:
            pieces.append(p[..., s - k * w:e - k * w])
    return pieces


def _pad_in_cols(parts):
    lead, dt = parts[0].shape[:-1], parts[0].dtype
    z = lambda n: [jnp.zeros(lead + (n,), dt)]
    seg = lambda a, b: _shard_cols(parts, a, b)
    kr = jnp.concatenate(seg(O_KR, O_LX), axis=-1)
    return jnp.concatenate(seg(O_GATE, N_IN) + seg(O_GM, O_Q) + seg(O_LX, O_LG) + seg(O_LG, O_GATE) + seg(O_Q, O_KV) + seg(O_KV, O_KR)
                           + z(64) + [kr] + z(32) + z(64) + [_swap16(kr)] + z(32), axis=-1)


_SEGS = ((O_GM, O_Q, P_GM), (O_Q, O_KV, P_Q), (O_KV, O_KR, P_KV), (O_KR, O_LX, None), (O_LX, O_LG, P_LX), (O_LG, O_GATE, P_LG),
         (O_GATE, N_IN, P_GATE))


def _unpad_in_cols(g, a=0, b=N_IN):
    pieces = []
    for lo, hi, p in _SEGS:
        s, e = max(a, lo), min(b, hi)
        if s >= e:
            continue
        if p is None:
            kr = g[..., P_KA + 64:P_KA + 96] + _swap16(g[..., P_KB + 64:P_KB + 96])
            pieces.append(kr[..., s - lo:e - lo])
        else:
            pieces.append(g[..., p + s - lo:p + e - lo])
    return jnp.concatenate(pieces, axis=-1)


def _flat128(vecs):
    flat = jnp.concatenate([v.reshape(-1) for v in vecs])
    pad = (-flat.shape[0]) % 1024
    return jnp.pad(flat, (0, pad)).reshape(-1, 128)


def _unflat(flat, shapes):
    flat = flat.reshape(-1)
    out, r = [], 0
    for s in shapes:
        n = math.prod(s)
        out.append(flat[r:r + n].reshape(s))
        r += n
    return out


_SMALL = ("ada_b", "in_b", "gm_ln_g", "gm_ln_b", "gm_ws", "gm_bs", "mla_qnorm_g", "mla_kvnorm_g", "lru_conv_w", "lru_conv_b",
          "lru_wr", "lru_br", "lru_wi", "lru_bi", "lru_lambda", "ffn_b1", "ffn_b2", "ln_g", "ln_b")
_NAMES = ("ada_w", "ada_b", "in_w", "in_b", "gm_ln_g", "gm_ln_b", "gm_ws", "gm_bs", "mla_qnorm_g", "mla_wuq", "mla_kvnorm_g",
          "mla_wukv", "lru_conv_w", "lru_conv_b", "lru_wr", "lru_br", "lru_wi", "lru_bi", "lru_lambda", "branch_w", "mix_out_w",
          "ffn_w1", "ffn_b1", "ffn_w2", "ffn_b2", "ln_g", "ln_b")


def kernel(x, c, ada_w, ada_b, in_w, in_b, gm_ln_g, gm_ln_b, gm_ws, gm_bs, mla_qnorm_g, mla_wuq, mla_kvnorm_g, mla_wukv, lru_conv_w, lru_conv_b, lru_wr, lru_br, lru_wi, lru_bi, lru_lambda, branch_w, mix_out_w, ffn_w1, ffn_b1, ffn_w2, ffn_b2, ln_g, ln_b, loss_target, m_ada_w, m_ada_b, m_in_w, m_in_b, m_gm_ln_g, m_gm_ln_b, m_gm_ws, m_gm_bs, m_mla_qnorm_g, m_mla_wuq, m_mla_kvnorm_g, m_mla_wukv, m_lru_conv_w, m_lru_conv_b, m_lru_wr, m_lru_br, m_lru_wi, m_lru_bi, m_lru_lambda, m_branch_w, m_mix_out_w, m_ffn_w1, m_ffn_b1, m_ffn_w2, m_ffn_b2, m_ln_g, m_ln_b, v_ada_w, v_ada_b, v_in_w, v_in_b, v_gm_ln_g, v_gm_ln_b, v_gm_ws, v_gm_bs, v_mla_qnorm_g, v_mla_wuq, v_mla_kvnorm_g, v_mla_wukv, v_lru_conv_w, v_lru_conv_b, v_lru_wr, v_lru_br, v_lru_wi, v_lru_bi, v_lru_lambda, v_branch_w, v_mix_out_w, v_ffn_w1, v_ffn_b1, v_ffn_w2, v_ffn_b2, v_ln_g, v_ln_b):
    W = dict(ada_w=ada_w, ada_b=ada_b, in_w=in_w, in_b=in_b, gm_ln_g=gm_ln_g, gm_ln_b=gm_ln_b, gm_ws=gm_ws, gm_bs=gm_bs,
             mla_qnorm_g=mla_qnorm_g, mla_wuq=mla_wuq, mla_kvnorm_g=mla_kvnorm_g, mla_wukv=mla_wukv, lru_conv_w=lru_conv_w,
             lru_conv_b=lru_conv_b, lru_wr=lru_wr, lru_br=lru_br, lru_wi=lru_wi, lru_bi=lru_bi, lru_lambda=lru_lambda,
             branch_w=branch_w, mix_out_w=mix_out_w, ffn_w1=ffn_w1, ffn_b1=ffn_b1, ffn_w2=ffn_w2, ffn_b2=ffn_b2, ln_g=ln_g, ln_b=ln_b)
    M = dict(ada_w=m_ada_w, ada_b=m_ada_b, in_w=m_in_w, in_b=m_in_b, gm_ln_g=m_gm_ln_g, gm_ln_b=m_gm_ln_b, gm_ws=m_gm_ws,
             gm_bs=m_gm_bs, mla_qnorm_g=m_mla_qnorm_g, mla_wuq=m_mla_wuq, mla_kvnorm_g=m_mla_kvnorm_g, mla_wukv=m_mla_wukv,
             lru_conv_w=m_lru_conv_w, lru_conv_b=m_lru_conv_b, lru_wr=m_lru_wr, lru_br=m_lru_br, lru_wi=m_lru_wi, lru_bi=m_lru_bi,
             lru_lambda=m_lru_lambda, branch_w=m_branch_w, mix_out_w=m_mix_out_w, ffn_w1=m_ffn_w1, ffn_b1=m_ffn_b1, ffn_w2=m_ffn_w2,
             ffn_b2=m_ffn_b2, ln_g=m_ln_g, ln_b=m_ln_b)
    V = dict(ada_w=v_ada_w, ada_b=v_ada_b, in_w=v_in_w, in_b=v_in_b, gm_ln_g=v_gm_ln_g, gm_ln_b=v_gm_ln_b, gm_ws=v_gm_ws,
             gm_bs=v_gm_bs, mla_qnorm_g=v_mla_qnorm_g, mla_wuq=v_mla_wuq, mla_kvnorm_g=v_mla_kvnorm_g, mla_wukv=v_mla_wukv,
             lru_conv_w=v_lru_conv_w, lru_conv_b=v_lru_conv_b, lru_wr=v_lru_wr, lru_br=v_lru_br, lru_wi=v_lru_wi, lru_bi=v_lru_bi,
             lru_lambda=v_lru_lambda, branch_w=v_branch_w, mix_out_w=v_mix_out_w, ffn_w1=v_ffn_w1, ffn_b1=v_ffn_b1, ffn_w2=v_ffn_w2,
             ffn_b2=v_ffn_b2, ln_g=v_ln_g, ln_b=v_ln_b)

    S = x.shape[1]
    xi, yi, ci = _coords()
    kme = 2 * xi + yi
    me = 4 * xi + 2 * yi + ci
    x0 = x[0]
    tgt = loss_target[0]

    shard_shapes = {n: W[n].shape[1:] for n, _ in _BIG}
    wflat = [_pack({n: W[n][l].astype(MXU) for n, _ in _BIG}) for l in range(DEPTH)]

    def layer_weights(gathered):
        parts = [_unpack(gathered[k], shard_shapes) for k in range(4)]
        full = {n: jnp.concatenate([p[n] for p in parts], axis=ax) for n, ax in _BIG if n != "in_w"}
        wuq = full["mla_wuq"].reshape(Q_RANK, HEADS, NOPE + ROPE)
        zq = lambda n: jnp.zeros((Q_RANK, HEADS, n), MXU)
        wukv = full["mla_wukv"].reshape(KV_RANK, HEADS, NOPE + VD)
        return dict(
            win=_pad_in_cols([p["in_w"] for p in parts]),
            w1q=jnp.concatenate([wuq, zq(32)], axis=-1).reshape(Q_RANK, D),
            w2q=jnp.concatenate([zq(64), _swap16(wuq[..., NOPE:]), zq(32)], axis=-1).reshape(Q_RANK, D),
            wk=jnp.concatenate([wukv[..., :NOPE], jnp.zeros((KV_RANK, HEADS, 64), MXU)], axis=-1).reshape(KV_RANK, D),
            wv=wukv[..., NOPE:].transpose(1, 0, 2), wvt=wukv[..., NOPE:].transpose(1, 2, 0),
            wa=full["branch_w"][0], wb=full["branch_w"][1].reshape(HEADS, VD, D), wc=full["branch_w"][2],
            mix=full["mix_out_w"], w1=full["ffn_w1"], w2=full["ffn_w2"])

    LW = [layer_weights(_allgather_weights(wflat[0], "ag_weights0"))]

    small1 = _allgather8(_flat128([c, ln_g, ln_b, lru_conv_w]), "ag_small")
    per_dev = [_unflat(small1[d], [c.shape, ln_g.shape, ln_b.shape, lru_conv_w.shape]) for d in range(8)]
    c_all = jnp.concatenate([p[0] for p in per_dev], axis=0)
    chip = lambda i: [per_dev[2 * k][i] for k in range(4)]
    ln_g_f, ln_b_f, conv_w_f = (jnp.concatenate(chip(1), axis=2), jnp.concatenate(chip(2), axis=2), jnp.concatenate(chip(3), axis=2))

    mod_sh = _mod_fwd(c_all, ada_w, "mod_fwd")
    mod_all = _allgather8(_flat128([mod_sh]), "ag_mod")
    mod_parts = [_unflat(mod_all[2 * k], [mod_sh.shape])[0] for k in range(4)]
    mod = jnp.concatenate(mod_parts, axis=2)
    mod = lax.dynamic_index_in_dim(mod, me, axis=1, keepdims=False) + ada_b

    pos = jnp.arange(S, dtype=F32)
    inv = ROPE_BASE ** (-jnp.arange(0, ROPE, 2, dtype=F32) / ROPE)
    ang = pos[:, None] * inv[None, :]
    cs, sn = jnp.cos(ang), jnp.sin(ang)
    zc = lambda n: jnp.zeros((S, n), F32)
    cos_t = jnp.concatenate([zc(64), cs, cs, zc(32)], axis=1)
    sin_t = jnp.concatenate([zc(64), -sn, sn, zc(32)], axis=1)

    bin_p = _pad_in_cols([in_b])
    wr_f =jnp.stack([block_diag(*[lru_wr[l, b] for b in range(LRU_NB)]) for l in range(DEPTH)]).astype(MXU)
    wi_f = jnp.stack([block_diag(*[lru_wi[l, b] for b in range(LRU_NB)]) for l in range(DEPTH)]).astype(MXU)
    ws_b = gm_ws.astype(MXU)
    wst_b = gm_ws.transpose(0, 1, 3, 2).astype(MXU)
    bs_t = gm_bs.transpose(0, 2, 1)
    row = lambda v: v.reshape(1, -1)

    saved = []
    xs = x0
    h1 = None
    for l in range(DEPTH):
        sh1, sc1, g1, sh2, sc2, g2 = [row(mod[l, i * D:(i + 1) * D]) for i in range(6)]
        if l == 0:
            h1 = _lnmod_fwd(xs, sc1, sh1, f"lnmod_fwd{l}")
        lw = LW[l]
        z = _mm(h1, lw["win"], "nn", f"in_proj{l}", rows=[row(bin_p[l])], epi=lambda acc, b: (acc + b,))
        ya = _gmlp_fwd(z, row(gm_ln_g[l]), row(gm_ln_b[l]), ws_b[l], bs_t[l], f"gmlp_fwd{l}")
        qf, kf, vf, vtf = _mla_prep_fwd(z, cos_t, sin_t, row(mla_qnorm_g[l]), row(mla_kvnorm_g[l]), lw["w1q"], lw["w2q"], lw["wk"],
                                        lw["wv"], lw["wvt"], f"mla_prep_fwd{l}")
        if l + 1 < DEPTH:
            ob, o32, lse, gathered_next = _attn_fwd(qf, kf, vtf, f"attn_fwd{l}", gather=wflat[l + 1])
            LW.append(layer_weights(gathered_next))
        else:
            ob, o32, lse = _attn_fwd(qf, kf, vtf, f"attn_fwd{l}")
        lru_args = (conv_w_f[l], row(lru_conv_b[l]), wr_f[l], row(lru_br[l]), wi_f[l], row(lru_bi[l]), row(lru_lambda[l]))
        yc, hl = _lru_fwd(z, *lru_args, f"lru_fwd{l}")
        merged, pa, pb, pc = _branch_fwd(z, ya, ob, yc, lw["wa"], lw["wb"], lw["wc"], f"branch_fwd{l}")
        mix = _mm(merged, lw["mix"], "nn", f"mix_out{l}")
        x1, h2 = _res_ln_fwd(xs, mix, g1, row(ln_g_f[l, 0]), row(ln_b_f[l, 0]), f"res_ln_a{l}", mod=(sc2, sh2))
        a1, r2 = _mm(h2, lw["w1"], "nn", f"ffn_up{l}", out_dtypes=(F32, MXU), rows=[row(ffn_b1[l])],
                     epi=lambda acc, b: (acc + b, jnp.square(jnp.maximum(acc + b, 0.0))))
        f = _mm(r2, lw["w2"], "nn", f"ffn_down{l}", rows=[row(ffn_b2[l])], epi=lambda acc, b: (acc + b,))
        if l + 1 < DEPTH:
            nsh1, nsc1 = row(mod[l + 1, 0:D]), row(mod[l + 1, D:2 * D])
            x2, h1n = _res_ln_fwd(x1, f, g2, row(ln_g_f[l, 1]), row(ln_b_f[l, 1]), f"res_ln_b{l}", mod=(nsc1, nsh1))
        else:
            (x2,), h1n = _res_ln_fwd(x1, f, g2, row(ln_g_f[l, 1]), row(ln_b_f[l, 1]), f"res_ln_b{l}"), None
        saved.append(dict(x_in=xs, h1=h1, z=z, ya=ya, qf=qf, kf=kf, vf=vf, ob=ob, o32=o32, lse=lse, yc=yc, hl=hl, merged=merged,
                          pa=pa, pb=pb, pc=pc, mix=mix, x1=x1, h2=h2, a1=a1, r2=r2, f=f, x2=x2, lru_args=lru_args,
                          mods=(sh1, sc1, g1, sh2, sc2, g2)))
        xs, h1 = x2, h1n

    dy, sq = _loss_fwd(xs, tgt)
    loss = lax.psum(0.5 / D * jnp.sum(sq), ("x", "y", "c"))

    G = {}
    dmods = []
    dres, hpath = dy, None
    wcols = in_w.shape[2]

    def grad_shards(gl, dtype):
        shard_of = lambda n, ax, k: (_unpad_in_cols(gl[n], k * wcols, (k + 1) * wcols) if n == "in_w" else jnp.split(gl[n], 4, axis=ax)[k])
        return [_pack({n: shard_of(n, ax, k).astype(dtype) for n, ax in _BIG}) for k in range(4)]

    landed = None
    for l in reversed(range(DEPTH)):
        sv = saved[l]
        lw = LW[l]
        sh1, sc1, g1, sh2, sc2, g2 = sv["mods"]
        gl = {}
        nb = _node_bwd(f"node_b{l}", dres, hpath=hpath, upath=(sv["x1"], sv["f"], g2, row(ln_g_f[l, 1])))
        if hpath is not None:
            dmods[-1]["sc1"], dmods[-1]["sh1"] = nb["dsc"], nb["dsh"]
        dm = dict(g2=nb["dg"])
        ln_g_l1, ln_b_l1 = nb["dlng"], nb["dlnb"]
        df = nb["dbr"]
        gl["ffn_b2"] = nb["dbrsum"][0]
        gl["ffn_w2"] = _mm(sv["r2"], df, "tn", f"d_ffn_w2_{l}")
        da1, db1 = _mm(df, lw["w2"], "nt", f"d_ffn_act{l}", out_dtypes=(MXU,), extras=[sv["a1"]], colsum=True,
                       epi=lambda acc, a: (acc * (2.0 * jnp.maximum(a, 0.0)),))
        gl["ffn_b1"] = db1[0]
        gl["ffn_w1"] = _mm(sv["h2"], da1, "tn", f"d_ffn_w1_{l}")
        dh2 = _mm(da1, lw["w1"], "nt", f"d_ffn_in{l}")
        na = _node_bwd(f"node_a{l}", nb["du"], hpath=(dh2, sv["x1"], sc2), upath=(sv["x_in"], sv["mix"], g1, row(ln_g_f[l, 0])))
        dm.update(sc2=na["dsc"], sh2=na["dsh"], g1=na["dg"])
        gl["ln_g"] = jnp.concatenate([na["dlng"], ln_g_l1], axis=0)
        gl["ln_b"] = jnp.concatenate([na["dlnb"], ln_b_l1], axis=0)
        dmix = na["dbr"]
        gl["mix_out_w"] = _mm(sv["merged"], dmix, "tn", f"d_mix_w{l}")
        dmerged = _mm(dmix, lw["mix"], "nt", f"d_merged{l}")
        dz, dpa, dpb, dpc, s_gate = _branch_bwd(sv["z"], dmerged, sv["pa"], sv["pb"], sv["pc"], f"branch_bwd{l}")
        dya = _mm(dpa, lw["wa"], "nt", f"d_ya{l}")
        dyc = _mm(dpc, lw["wc"], "nt", f"d_yc{l}")
        dwa = _mm(sv["ya"], dpa, "tn", f"d_wa{l}")
        dwc = _mm(sv["yc"], dpc, "tn", f"d_wc{l}")
        dob, do32, dwb = _heads_bwd(dpb, sv["ob"], lw["wb"], f"heads_bwd{l}")
        gl["branch_w"] = jnp.stack([dwa, dwb.reshape(GM_W, D), dwc])
        dz, dcw, dcb, dwr, dbr_, dwi, dbi, dlam, s_lru = _lru_bwd(sv["z"], sv["hl"], dyc, *sv["lru_args"], dz, f"lru_bwd{l}")
        gl["lru_conv_w"], gl["lru_conv_b"], gl["lru_br"], gl["lru_bi"], gl["lru_lambda"] = dcw, dcb[0], dbr_[0], dbi[0], dlam[0]
        blocks = lambda m: jnp.stack([m[b * 64:(b + 1) * 64, b * 64:(b + 1) * 64] for b in range(LRU_NB)])
        gl["lru_wr"], gl["lru_wi"] = blocks(dwr), blocks(dwi)
        delta = _attn_delta(sv["o32"], do32, f"attn_delta{l}")
        attn_args = (sv["qf"], sv["kf"], sv["vf"], dob, sv["lse"], delta.reshape(HEADS, 1, S), f"attn_bwd{l}")
        if l + 1 < DEPTH:
            sent = jnp.stack(grad_shards(G[l + 1], jnp.bfloat16))
            dqf, dkf, dvf, landed = _attn_bwd(*attn_args, scatter=sent)
        else:
            dqf, dkf, dvf = _attn_bwd(*attn_args)
        dz, dgq, dgkv, dw1, dw2, dwk, dwv, s_mla = _mla_prep_bwd(sv["z"], cos_t, sin_t, row(mla_qnorm_g[l]), row(mla_kvnorm_g[l]),
                                                                 lw["w1q"], lw["w2q"], lw["wk"], lw["wv"], dqf, dkf, dvf, dz,
                                                                 f"mla_prep_bwd{l}")
        gl["mla_qnorm_g"], gl["mla_kvnorm_g"] = dgq[0], dgkv[0]
        dw1 = dw1.reshape(Q_RANK, HEADS, 128)
        dw2 = dw2.reshape(Q_RANK, HEADS, 128)
        gl["mla_wuq"] = jnp.concatenate([dw1[..., :NOPE], dw1[..., NOPE:NOPE + ROPE] + _swap16(dw2[..., NOPE:NOPE + ROPE])],
                                        axis=-1).reshape(Q_RANK, HEADS * (NOPE + ROPE))
        gl["mla_wukv"] = jnp.concatenate([dwk.reshape(KV_RANK, HEADS, 128)[..., :NOPE], dwv.transpose(1, 0, 2)],
                                         axis=-1).reshape(KV_RANK, HEADS * (NOPE + VD))
        dz, dglg, dglb, dws, dbs, s_gm = _gmlp_bwd(sv["z"], dya, row(gm_ln_g[l]), row(gm_ln_b[l]), ws_b[l], wst_b[l], bs_t[l], dz,
                                                   f"gmlp_bwd{l}")
        gl["gm_ln_g"], gl["gm_ln_b"], gl["gm_ws"], gl["gm_bs"] = dglg[0], dglb[0], dws, dbs[:, :GM_G].T
        gl["in_b"] = _unpad_in_cols(jnp.concatenate([s_gate, s_gm, s_lru, s_mla], axis=1))[0]
        gl["in_w"] = _mm(sv["h1"], dz, "tn", f"d_in_w{l}")
        dh1 = _mm(dz, lw["win"], "nt", f"d_h1_{l}")
        dmods.append(dm)
        dres, hpath = na["du"], (dh1, sv["x_in"], sc1)
        G[l] = gl
    n0 = _node_bwd("node_in", dres, hpath=hpath)
    dmods[-1]["sc1"], dmods[-1]["sh1"] = n0["dsc"], n0["dsh"]
    grad_x = n0["dx"][None]
    dmods = dmods[::-1]
    dmod = jnp.stack([jnp.concatenate([dmods[l][k] for k in ("sh1", "sc1", "g1", "sh2", "sc2", "g2")], axis=1)[0]
                      for l in range(DEPTH)])
    grads = {n: jnp.stack([G[l][n] for l in range(DEPTH)]) for n in _SMALL if n != "ada_b"}
    grads["ada_b"] = dmod

    half = ci.reshape(1).astype(jnp.int32)
    sum1 = _land_sum(lax.dynamic_index_in_dim(sent, kme, axis=0, keepdims=False), landed, half, "rs1_sum")
    gsh = grad_shards(G[0], F32)
    from_sib = _sibling_swap(gsh, "rs_pair")
    pair, pair_b = _pair_add(gsh, from_sib, half, "rs_pair_add")
    from_chips = _chip_scatter(pair_b, "rs_chips")
    own = lax.dynamic_index_in_dim(pair, kme, axis=0, keepdims=False)
    sum0 = _chip_sum(own, from_chips, half, "rs_chip_sum")
    gl0, gl1 = _unpack(_sibling_join(sum0, "rs_join0"), shard_shapes), _unpack(_sibling_join(sum1, "rs_join1"), shard_shapes)
    gbig = {n: jnp.stack([gl0[n], gl1[n]]) for n, _ in _BIG}

    small_shapes = [grads[n].shape for n in _SMALL]
    gsm_all = _allgather8(_flat128([grads[n] for n in _SMALL]), "ag_small_grads")
    gsm = _unflat(_sum8(gsm_all, "sum_small_grads"), small_shapes)
    gsmall = dict(zip(_SMALL, gsm))
    n_mod = DEPTH * 6 * D
    dmod_all = gsm_all[:, :n_mod // 128].reshape(8, DEPTH, 6 * D).transpose(1, 0, 2)
    dmod_sh = lax.dynamic_slice_in_dim(dmod_all, kme * (6 * D // 4), 6 * D // 4, axis=2)
    g_ada_w = _ada_w_grad(c_all.T, dmod_sh, "d_ada_w")
    quarter = lambda g, ax: lax.dynamic_slice_in_dim(g, kme * (g.shape[ax] // 4), g.shape[ax] // 4, axis=ax)
    gsmall["lru_conv_w"] = quarter(gsmall["lru_conv_w"], 2)
    gsmall["ln_g"] = quarter(gsmall["ln_g"], 2)
    gsmall["ln_b"] = quarter(gsmall["ln_b"], 2)

    grad = dict(gbig)
    grad.update(gsmall)
    grad["ada_w"] = g_ada_w

    delta, new_m, new_v = {}, {}, {}
    for n in _NAMES:
        shp = W[n].shape
        two = lambda a: a.reshape(-1, shp[-1])
        d_, m_, v_ = _adamw(two(W[n]), two(grad[n]), two(M[n]), two(V[n]), f"adamw_{n}")
        delta[n], new_m[n], new_v[n] = d_.reshape(shp), m_.reshape(shp), v_.reshape(shp)

    return (loss, grad_x, *[grad[n] for n in _NAMES], *[delta[n] for n in _NAMES], *[new_m[n] for n in _NAMES],
            *[new_v[n] for n in _NAMES])
```

```python
import functools
import math

import jax
import jax.numpy as jnp
from jax import lax
from jax.experimental import pallas as pl
from jax.experimental.pallas import tpu as pltpu
from jax.scipy.linalg import block_diag

F32 = jnp.float32
MXU = jnp.bfloat16
MESH = pl.DeviceIdType.MESH

D = 1024
DEPTH = 2
CHUNK = 64
GM_W = 512
GM_G = 4
HEADS = 8
Q_RANK = 256
KV_RANK = 128
NOPE = 64
ROPE = 32
VD = 64
LRU_W = 512
LRU_NB = 8
D_FF = 4096
ALPHA = (2.0 * DEPTH) ** 0.25
LN_EPS = 1e-5
RMS_EPS = 1e-6
ROPE_BASE = 10000.0
ATT_SCALE = (NOPE + ROPE) ** -0.5
N_IN = 5536
P_GATE, P_GM, P_LX, P_LG, P_Q, P_KV, P_KA, P_KB, NP = 0, 3072, 4096, 4608, 5120, 5376, 5504, 5632, 5760
O_GM, O_Q, O_KV, O_KR, O_LX, O_LG, O_GATE = 0, 1024, 1280, 1408, 1440, 1952, 2464

ADAM_LR, ADAM_B1, ADAM_B2, ADAM_EPS, ADAM_WD, ADAM_STEP = 0.001, 0.9, 0.999, 1e-08, 0.01, 10

ROW_TILE = 512
ATT_TILE = 512
VMEM_LIMIT = 56 * 1024 * 1024
FLAT_W = 1024
FLAT_ALIGN = 32
RIDE_ALIGN = 1024


def _params(sem=None, vmem=VMEM_LIMIT):
    return pltpu.CompilerParams(dimension_semantics=sem, vmem_limit_bytes=vmem)


def _tile(dim, pref, mult=128):
    if dim <= pref:
        return dim
    t = (pref // mult) * mult
    while t >= mult:
        if dim % t == 0:
            return t
        t -= mult
    return dim


def _dot(a, b):
    return lax.dot_general(a, b, (((1,), (0,)), ((), ())), preferred_element_type=F32)


def _dot_nt(a, b):
    return lax.dot_general(a, b, (((1,), (1,)), ((), ())), preferred_element_type=F32)


def _dot_tn(a, b):
    return lax.dot_general(a, b, (((0,), (0,)), ((), ())), preferred_element_type=F32)


def _sigmoid(x):
    return 1.0 / (1.0 + jnp.exp(-x))


_GC = 0.7978845608028654


def _gelu(x):
    return 0.5 * x * (1.0 + jnp.tanh(_GC * (x + 0.044715 * x * x * x)))


def _gelu_grad(x):
    t = jnp.tanh(_GC * (x + 0.044715 * x * x * x))
    return 0.5 * (1.0 + t) + 0.5 * x * (1.0 - t * t) * _GC * (1.0 + 3.0 * 0.044715 * x * x)


def _ln_stats(x):
    mu = jnp.mean(x, axis=-1, keepdims=True)
    xc = x - mu
    r = lax.rsqrt(jnp.mean(xc * xc, axis=-1, keepdims=True) + LN_EPS)
    return xc * r, r


def _ln_bwd(dxh, xh, r):
    return r * (dxh - jnp.mean(dxh, axis=-1, keepdims=True) - xh * jnp.mean(dxh * xh, axis=-1, keepdims=True))


def _colsum(v):
    return jnp.sum(v, axis=0, keepdims=True)


def _acc(ref, first, val):
    @pl.when(first)
    def _():
        ref[...] = val

    @pl.when(jnp.logical_not(first))
    def _():
        ref[...] += val


def _row_spec(t, w, col=0):
    return pl.BlockSpec((t, w), lambda i, c=col: (i, c))


def _vec_spec(w):
    return pl.BlockSpec((1, w), lambda i: (0, 0))


def _full_spec(shape):
    nd = len(shape)
    return pl.BlockSpec(shape, lambda i, n=nd: (0,) * n)


def _mm(a, b, mode, name, out_dtypes=(F32,), extras=(), rows=(), epi=None, colsum=False, tm=1024, tn=1152, tk=2048):
    if mode == "nn":
        (M, K), N = a.shape, b.shape[1]
    elif mode == "nt":
        (M, K), N = a.shape, b.shape[0]
    else:
        (K, M), N = a.shape, b.shape[1]
    tm, tn, tk = _tile(M, tm), _tile(N, tn), _tile(K, tk)
    nk = K // tk
    dot = {"nn": _dot, "nt": _dot_nt, "tn": _dot_tn}[mode]
    a_spec = pl.BlockSpec((tk, tm), lambda i, j, k: (k, i)) if mode == "tn" else pl.BlockSpec((tm, tk), lambda i, j, k: (i, k))
    b_spec = pl.BlockSpec((tn, tk), lambda i, j, k: (j, k)) if mode == "nt" else pl.BlockSpec((tk, tn), lambda i, j, k: (k, j))
    o_spec = pl.BlockSpec((tm, tn), lambda i, j, k: (i, j))
    r_spec = pl.BlockSpec((1, tn), lambda i, j, k: (0, j))
    n_e, n_r, n_o = len(extras), len(rows), len(out_dtypes)
    if epi is None:
        epi = lambda acc: (acc,)

    def body(*refs):
        a_ref, b_ref = refs[0], refs[1]
        e_refs = refs[2:2 + n_e]
        r_refs = refs[2 + n_e:2 + n_e + n_r]
        o_refs = refs[2 + n_e + n_r:2 + n_e + n_r + n_o]
        p = dot(a_ref[...], b_ref[...])

        def finish(acc):
            outs = epi(acc, *[e[...] for e in e_refs], *[r[...] for r in r_refs])
            for o_ref, o in zip(o_refs, outs):
                o_ref[...] = o.astype(o_ref.dtype)
            if colsum:
                cs_ref = refs[2 + n_e + n_r + n_o]
                cols = pl.ds(pl.multiple_of(pl.program_id(1) * tn, 128), tn)
                _acc(cs_ref.at[:, cols], pl.program_id(0) == 0, _colsum(outs[0]))

        if nk == 1:
            finish(p)
        else:
            acc_ref = refs[-1]
            k = pl.program_id(2)
            _acc(acc_ref, k == 0, p)

            @pl.when(k == nk - 1)
            def _():
                finish(acc_ref[...])

    outs = pl.pallas_call(
        body, name=name, grid=(M // tm, N // tn, nk),
        in_specs=[a_spec, b_spec] + [o_spec] * n_e + [r_spec] * n_r,
        out_specs=[o_spec] * n_o + ([pl.BlockSpec((1, N), lambda i, j, k: (0, 0))] if colsum else []),
        out_shape=[jax.ShapeDtypeStruct((M, N), dt) for dt in out_dtypes] + ([jax.ShapeDtypeStruct((1, N), F32)] if colsum else []),
        scratch_shapes=[pltpu.VMEM((tm, tn), F32)] if nk > 1 else [],
        compiler_params=_params(("arbitrary",) * 3 if colsum else ("parallel", "parallel", "arbitrary")),
    )(a, b, *extras, *rows)
    return outs[0] if len(outs) == 1 else outs


def _lnmod_fwd(x, sc, sh, name):
    S = x.shape[0]
    t = min(ROW_TILE, S)

    def body(x_ref, sc_ref, sh_ref, h_ref):
        xh, _ = _ln_stats(x_ref[...])
        h_ref[...] = (xh * (1.0 + sc_ref[...]) + sh_ref[...]).astype(h_ref.dtype)

    return pl.pallas_call(
        body, name=name, grid=(S // t,),
        in_specs=[_row_spec(t, D), _vec_spec(D), _vec_spec(D)], out_specs=_row_spec(t, D),
        out_shape=jax.ShapeDtypeStruct((S, D), MXU), compiler_params=_params(("parallel",)),
    )(x, sc, sh)


def _res_ln_fwd(xprev, br, gvec, lng, lnb, name, mod=None):
    S = xprev.shape[0]
    t = min(ROW_TILE, S)
    with_h = mod is not None

    def body(*refs):
        xp_ref, br_ref, g_ref, lg_ref, lb_ref = refs[:5]
        u = ALPHA * xp_ref[...] + (1.0 + g_ref[...]) * br_ref[...]
        uh, _ = _ln_stats(u)
        xn = uh * lg_ref[...] + lb_ref[...]
        if with_h:
            sc_ref, sh_ref, xn_ref, h_ref = refs[5:]
            xh, _ = _ln_stats(xn)
            h_ref[...] = (xh * (1.0 + sc_ref[...]) + sh_ref[...]).astype(h_ref.dtype)
        else:
            xn_ref = refs[5]
        xn_ref[...] = xn

    ins = [xprev, br, gvec, lng, lnb] + (list(mod) if with_h else [])
    return pl.pallas_call(
        body, name=name, grid=(S // t,),
        in_specs=[_row_spec(t, D), _row_spec(t, D)] + [_vec_spec(D)] * (len(ins) - 2),
        out_specs=[_row_spec(t, D)] * (2 if with_h else 1),
        out_shape=[jax.ShapeDtypeStruct((S, D), F32)] + ([jax.ShapeDtypeStruct((S, D), MXU)] if with_h else []),
        compiler_params=_params(("parallel",)),
    )(*ins)


def _loss_fwd(y, tgt):
    S = y.shape[0]
    t = min(ROW_TILE, S)

    def body(y_ref, t_ref, dy_ref, sq_ref):
        d = y_ref[...] - t_ref[...]
        dy_ref[...] = d * (1.0 / D)
        _acc(sq_ref, pl.program_id(0) == 0, _colsum(d * d))

    return pl.pallas_call(
        body, name="loss_head", grid=(S // t,),
        in_specs=[_row_spec(t, D), _row_spec(t, D)], out_specs=[_row_spec(t, D), _vec_spec(D)],
        out_shape=[jax.ShapeDtypeStruct((S, D), F32), jax.ShapeDtypeStruct((1, D), F32)],
        compiler_params=_params(("arbitrary",)),
    )(y, tgt)


def _node_bwd(name, dres, *, hpath=None, upath=None):
    S = dres.shape[0]
    t = min(ROW_TILE, S)
    has_h, has_u = hpath is not None, upath is not None

    def body(*refs):
        refs = list(refs)
        first = pl.program_id(0) == 0
        dxs = refs.pop(0)[...]
        if has_h:
            dh = refs.pop(0)[...]
            xs = refs.pop(0)[...]
            sc = refs.pop(0)[...]
        if has_u:
            xp = refs.pop(0)[...]
            br = refs.pop(0)[...]
            gv = refs.pop(0)[...]
            lg = refs.pop(0)[...]
        if has_h:
            dsc_ref, dsh_ref = refs.pop(0), refs.pop(0)
            xh, r = _ln_stats(xs)
            _acc(dsc_ref, first, _colsum(dh * xh))
            _acc(dsh_ref, first, _colsum(dh))
            dxs = dxs + _ln_bwd(dh * (1.0 + sc), xh, r)
        if has_u:
            du_ref, dbr_ref, dlg_ref, dlb_ref, dg_ref, dbs_ref = refs
            uh, ru = _ln_stats(ALPHA * xp + (1.0 + gv) * br)
            _acc(dlg_ref, first, _colsum(dxs * uh))
            _acc(dlb_ref, first, _colsum(dxs))
            du = _ln_bwd(dxs * lg, uh, ru)
            _acc(dg_ref, first, _colsum(du * br))
            du_ref[...] = ALPHA * du
            dbr = (1.0 + gv) * du
            _acc(dbs_ref, first, _colsum(dbr))
            dbr_ref[...] = dbr.astype(dbr_ref.dtype)
        else:
            refs[0][...] = dxs

    ins, in_specs = [dres], [_row_spec(t, D)]
    outs, out_specs, names = [], [], []
    vec = jax.ShapeDtypeStruct((1, D), F32)
    if has_h:
        ins += list(hpath)
        in_specs += [_row_spec(t, D), _row_spec(t, D), _vec_spec(D)]
        outs += [vec, vec]
        out_specs += [_vec_spec(D), _vec_spec(D)]
        names += ["dsc", "dsh"]
    if has_u:
        ins += list(upath)
        in_specs += [_row_spec(t, D), _row_spec(t, D), _vec_spec(D), _vec_spec(D)]
        outs += [jax.ShapeDtypeStruct((S, D), F32), jax.ShapeDtypeStruct((S, D), MXU), vec, vec, vec, vec]
        out_specs += [_row_spec(t, D), _row_spec(t, D), _vec_spec(D), _vec_spec(D), _vec_spec(D), _vec_spec(D)]
        names += ["du", "dbr", "dlng", "dlnb", "dg", "dbrsum"]
    else:
        outs += [jax.ShapeDtypeStruct((S, D), F32)]
        out_specs += [_row_spec(t, D)]
        names += ["dx"]
    res = pl.pallas_call(
        body, name=name, grid=(S // t,), in_specs=in_specs, out_specs=out_specs, out_shape=outs,
        compiler_params=_params(("arbitrary",)),
    )(*ins)
    return dict(zip(names, res))


def _colsum_call(a, name):
    S, N = a.shape
    t, tn = min(ROW_TILE, S), _tile(N, 1152)

    def body(a_ref, o_ref):
        _acc(o_ref, pl.program_id(1) == 0, _colsum(a_ref[...].astype(F32)))

    return pl.pallas_call(
        body, name=name, grid=(N // tn, S // t),
        in_specs=[pl.BlockSpec((t, tn), lambda j, i: (i, j))], out_specs=pl.BlockSpec((1, tn), lambda j, i: (0, j)),
        out_shape=jax.ShapeDtypeStruct((1, N), F32), compiler_params=_params(("parallel", "arbitrary")),
    )(a)


def _gm_mask():
    i = lax.broadcasted_iota(jnp.int32, (128, 128), 0) // CHUNK
    j = lax.broadcasted_iota(jnp.int32, (128, 128), 1) // CHUNK
    return i >= j


def _gmlp_common(z, lng, lnb):
    gz = _gelu(z)
    u, v = gz[:, :GM_W], gz[:, GM_W:]
    vh, r = _ln_stats(v)
    return u, vh, r, vh * lng + lnb


def _gmlp_fwd(z, lng, lnb, ws, bst, name):
    S = z.shape[0]
    t = min(ROW_TILE, S)

    def body(z_ref, lg_ref, lb_ref, ws_ref, bs_ref, y_ref):
        u, _, _, vn = _gmlp_common(z_ref[...], lg_ref[...], lb_ref[...])
        mask = _gm_mask()
        vb = vn.astype(MXU)
        for g in range(GM_G):
            w = jnp.where(mask, ws_ref[g], jnp.zeros_like(ws_ref[g]))
            bias = bs_ref[:, g:g + 1]
            for blk in range(t // 128):
                rs, cs = slice(blk * 128, (blk + 1) * 128), slice(g * 128, (g + 1) * 128)
                f = _dot(w, vb[rs, cs]) + bias
                y_ref[rs, cs] = (u[rs, cs] * f).astype(y_ref.dtype)

    return pl.pallas_call(
        body, name=name, grid=(S // t,),
        in_specs=[_row_spec(t, 2 * GM_W, P_GM // (2 * GM_W)), _vec_spec(GM_W), _vec_spec(GM_W),
                  _full_spec((GM_G, 128, 128)), _full_spec((128, GM_G))],
        out_specs=_row_spec(t, GM_W), out_shape=jax.ShapeDtypeStruct((S, GM_W), MXU),
        compiler_params=_params(("parallel",)),
    )(z, lng, lnb, ws, bst)


def _gmlp_bwd(z, dya, lng, lnb, ws, wst, bst, dz_in, name):
    S = z.shape[0]
    t = min(ROW_TILE, S)

    def body(z_ref, dy_ref, lg_ref, lb_ref, ws_ref, wst_ref, bs_ref, _dz_in, dz_ref, dlg_ref, dlb_ref, dws_ref, dbs_ref, sum_ref):
        first = pl.program_id(0) == 0
        zz = z_ref[...]
        u, vh, r, vn = _gmlp_common(zz, lg_ref[...], lb_ref[...])
        dy = dy_ref[...]
        mask = _gm_mask()
        maskt = lax.broadcasted_iota(jnp.int32, (128, 128), 1) // CHUNK >= lax.broadcasted_iota(jnp.int32, (128, 128), 0) // CHUNK
        lane = lax.broadcasted_iota(jnp.int32, (128, 128), 1)
        vb = vn.astype(MXU)
        dfb = (dy * u).astype(MXU)
        df32 = dy * u
        dbs = jnp.zeros((128, 128), F32)
        du_cols, dvn_cols = [], []
        for g in range(GM_G):
            w = jnp.where(mask, ws_ref[g], jnp.zeros_like(ws_ref[g]))
            wt = jnp.where(maskt, wst_ref[g], jnp.zeros_like(wst_ref[g]))
            bias = bs_ref[:, g:g + 1]
            cs = slice(g * 128, (g + 1) * 128)
            dw = jnp.zeros((128, 128), F32)
            du_rows, dvn_rows = [], []
            for blk in range(t // 128):
                rs = slice(blk * 128, (blk + 1) * 128)
                f = _dot(w, vb[rs, cs]) + bias
                du_rows.append(dy[rs, cs] * f)
                dvn_rows.append(_dot(wt, dfb[rs, cs]))
                dw = dw + _dot_nt(dfb[rs, cs], vb[rs, cs])
                dbs = dbs + jnp.where(lane == g, jnp.sum(df32[rs, cs], axis=1, keepdims=True), 0.0)
            _acc(dws_ref.at[g], first, jnp.where(mask, dw, 0.0))
            du_cols.append(jnp.concatenate(du_rows, axis=0))
            dvn_cols.append(jnp.concatenate(dvn_rows, axis=0))
        _acc(dbs_ref, first, dbs)
        du = jnp.concatenate(du_cols, axis=1)
        dvn = jnp.concatenate(dvn_cols, axis=1)
        _acc(dlg_ref, first, _colsum(dvn * vh))
        _acc(dlb_ref, first, _colsum(dvn))
        dv = _ln_bwd(dvn * lg_ref[...], vh, r)
        dzz = jnp.concatenate([du, dv], axis=1) * _gelu_grad(zz)
        dz_ref[...] = dzz.astype(dz_ref.dtype)
        _acc(sum_ref, first, _colsum(dzz))

    vec = jax.ShapeDtypeStruct((1, GM_W), F32)
    return pl.pallas_call(
        body, name=name, grid=(S // t,),
        in_specs=[_row_spec(t, 2 * GM_W, P_GM // (2 * GM_W)), _row_spec(t, GM_W), _vec_spec(GM_W), _vec_spec(GM_W),
                  _full_spec((GM_G, 128, 128)), _full_spec((GM_G, 128, 128)), _full_spec((128, GM_G)), _ANY],
        out_specs=[_row_spec(t, 2 * GM_W, P_GM // (2 * GM_W)), _vec_spec(GM_W), _vec_spec(GM_W), _full_spec((GM_G, 128, 128)),
                   _full_spec((128, 128)), _vec_spec(2 * GM_W)],
        out_shape=[jax.ShapeDtypeStruct((S, NP), MXU), vec, vec,
                   jax.ShapeDtypeStruct((GM_G, 128, 128), F32), jax.ShapeDtypeStruct((128, 128), F32),
                   jax.ShapeDtypeStruct((1, 2 * GM_W), F32)],
        input_output_aliases=_dz_alias(8), compiler_params=_params(("arbitrary",)),
    )(z, dya, lng, lnb, ws, wst, bst, dz_in)


def _rms(x, g):
    r = lax.rsqrt(jnp.mean(x * x, axis=-1, keepdims=True) + RMS_EPS)
    xh = x * r
    return xh, r, xh * g


def _mla_specs(t):
    return [_row_spec(t, Q_RANK, P_Q // Q_RANK), _row_spec(t, KV_RANK, P_KV // 128), _row_spec(t, 128, P_KA // 128),
            _row_spec(t, 128, P_KB // 128), _row_spec(t, 128), _row_spec(t, 128), _vec_spec(Q_RANK), _vec_spec(KV_RANK),
            _full_spec((Q_RANK, D)), _full_spec((Q_RANK, D)), _full_spec((KV_RANK, D)), _full_spec((HEADS, KV_RANK, VD))]


def _mla_prep_fwd(z, cos, sin, gq, gkv, w1, w2, wk, wv, wvt, name):
    S = z.shape[0]
    t = min(ROW_TILE, S)

    def body(zq_ref, zkv_ref, zka_ref, zkb_ref, cos_ref, sin_ref, gq_ref, gkv_ref, w1_ref, w2_ref, wk_ref, wv_ref, wvt_ref,
             q_ref, k_ref, v_ref, vt_ref):
        cos_, sin_ = cos_ref[...], sin_ref[...]
        cq = cos_ + jnp.where(lax.broadcasted_iota(jnp.int32, cos_.shape, 1) < NOPE, 1.0, 0.0)
        qn = _rms(zq_ref[...], gq_ref[...])[2].astype(MXU)
        q1, q2 = _dot(qn, w1_ref[...]), _dot(qn, w2_ref[...])
        kvn = _rms(zkv_ref[...], gkv_ref[...])[2].astype(MXU)
        kn = _dot(kvn, wk_ref[...])
        krot = zka_ref[...] * cos_ + zkb_ref[...] * sin_
        for h in range(HEADS):
            hs = slice(h * 128, (h + 1) * 128)
            q_ref[h] = (q1[:, hs] * cq + q2[:, hs] * sin_).astype(q_ref.dtype)
            k_ref[h] = (kn[:, hs] + krot).astype(k_ref.dtype)
            v_ref[h] = _dot(kvn, wv_ref[h]).astype(v_ref.dtype)
            vt_ref[h] = _dot_nt(wvt_ref[h], kvn).astype(vt_ref.dtype)

    hspec = lambda w: pl.BlockSpec((HEADS, t, w), lambda i: (0, i, 0))
    return pl.pallas_call(
        body, name=name, grid=(S // t,), in_specs=_mla_specs(t) + [_full_spec((HEADS, VD, KV_RANK))],
        out_specs=[hspec(128), hspec(128), hspec(VD), pl.BlockSpec((HEADS, VD, t), lambda i: (0, 0, i))],
        out_shape=[jax.ShapeDtypeStruct((HEADS, S, 128), MXU), jax.ShapeDtypeStruct((HEADS, S, 128), MXU),
                   jax.ShapeDtypeStruct((HEADS, S, VD), MXU), jax.ShapeDtypeStruct((HEADS, VD, S), MXU)],
        compiler_params=_params(("parallel",)),
    )(z, z, z, z, cos, sin, gq, gkv, w1, w2, wk, wv, wvt)


def _mla_prep_bwd(z, cos, sin, gq, gkv, w1, w2, wk, wv, dq, dk, dv, dz_in, name):
    S = z.shape[0]
    t = min(ROW_TILE, S)

    def body(zq_ref, zkv_ref, zka_ref, zkb_ref, cos_ref, sin_ref, gq_ref, gkv_ref, w1_ref, w2_ref, wk_ref, wv_ref,
             dq_ref, dk_ref, dv_ref, _dz_in, dz_ref, dgq_ref, dgkv_ref, dw1_ref, dw2_ref, dwk_ref, dwv_ref, sum_ref):
        first = pl.program_id(0) == 0
        cos_, sin_ = cos_ref[...], sin_ref[...]
        cq = cos_ + jnp.where(lax.broadcasted_iota(jnp.int32, cos_.shape, 1) < NOPE, 1.0, 0.0)
        gq_, gkv_ = gq_ref[...], gkv_ref[...]
        xhq, rq, qn32 = _rms(zq_ref[...], gq_)
        qn = qn32.astype(MXU)
        dq1 = jnp.concatenate([dq_ref[h] * cq for h in range(HEADS)], axis=1).astype(MXU)
        dq2 = jnp.concatenate([dq_ref[h] * sin_ for h in range(HEADS)], axis=1).astype(MXU)
        dqn = _dot_nt(dq1, w1_ref[...]) + _dot_nt(dq2, w2_ref[...])
        _acc(dw1_ref, first, _dot_tn(qn, dq1))
        _acc(dw2_ref, first, _dot_tn(qn, dq2))
        _acc(dgq_ref, first, _colsum(dqn * xhq))
        dxn = dqn * gq_
        dzq = rq * (dxn - xhq * jnp.mean(dxn * xhq, axis=-1, keepdims=True))

        xhk, rk, kvn32 = _rms(zkv_ref[...], gkv_)
        kvn = kvn32.astype(MXU)
        dks = [dk_ref[h] for h in range(HEADS)]
        dkall = jnp.concatenate(dks, axis=1).astype(MXU)
        dkrot = functools.reduce(lambda p, q_: p + q_, dks)
        dkvn = _dot_nt(dkall, wk_ref[...])
        _acc(dwk_ref, first, _dot_tn(kvn, dkall))
        for h in range(HEADS):
            dvb = dv_ref[h].astype(MXU)
            dkvn = dkvn + _dot_nt(dvb, wv_ref[h])
            _acc(dwv_ref.at[h], first, _dot_tn(kvn, dvb))
        _acc(dgkv_ref, first, _colsum(dkvn * xhk))
        dxk = dkvn * gkv_
        dzkv = rk * (dxk - xhk * jnp.mean(dxk * xhk, axis=-1, keepdims=True))
        dzm = jnp.concatenate([dzq, dzkv, dkrot * cos_, dkrot * sin_], axis=1)
        dz_ref[...] = dzm.astype(dz_ref.dtype)
        _acc(sum_ref, first, _colsum(dzm))

    hspec = lambda w: pl.BlockSpec((HEADS, t, w), lambda i: (0, i, 0))
    sds = jax.ShapeDtypeStruct
    return pl.pallas_call(
        body, name=name, grid=(S // t,),
        in_specs=_mla_specs(t) + [hspec(128), hspec(128), hspec(VD), _ANY],
        out_specs=[_row_spec(t, 640, P_Q // 640), _vec_spec(Q_RANK), _vec_spec(KV_RANK), _full_spec((Q_RANK, D)),
                   _full_spec((Q_RANK, D)), _full_spec((KV_RANK, D)), _full_spec((HEADS, KV_RANK, VD)), _vec_spec(640)],
        out_shape=[sds((S, NP), MXU), sds((1, Q_RANK), F32), sds((1, KV_RANK), F32), sds((Q_RANK, D), F32),
                   sds((Q_RANK, D), F32), sds((KV_RANK, D), F32), sds((HEADS, KV_RANK, VD), F32), sds((1, 640), F32)],
        input_output_aliases=_dz_alias(16), compiler_params=_params(("arbitrary",)),
    )(z, z, z, z, cos, sin, gq, gkv, w1, w2, wk, wv, dq, dk, dv, dz_in)


def _chunk_mask(t, transposed):
    r = lax.broadcasted_iota(jnp.int32, (t, t), 0) // CHUNK
    c = lax.broadcasted_iota(jnp.int32, (t, t), 1) // CHUNK
    return (r <= c) if transposed else (c <= r)


def _attn_fwd(q, k, vt, name, gather=None):
    S = q.shape[1]
    t = min(ATT_TILE, S)
    n = S // t

    hb = 4
    nh = HEADS // hb

    def body(*refs):
        if gather is None:
            q_ref, k_ref, vt_ref, ob_ref, o_ref, lse_ref = refs
        else:
            q_ref, k_ref, vt_ref, w_ref, ob_ref, o_ref, lse_ref, g_ref, send_sems, recv_sems = refs
            hi, step = pl.program_id(0), pl.program_id(1)
            for stage, at in (("start", (0, 0)), ("forward", (nh // 2, 0))):
                pl.when((hi == at[0]) & (step == at[1]))(functools.partial(_ag_stage, stage, w_ref, g_ref, send_sems, recv_sems))
        qi = pl.program_id(1)
        qbs = [q_ref[g] for g in range(hb)]

        def block(j, carries, masked):
            cols = pl.ds(pl.multiple_of(j * t, t), t)
            out = []
            for g in range(hb):
                m, l, acc = carries[g]
                st = _dot_nt(k_ref[g, cols, :], qbs[g]) * ATT_SCALE
                if masked:
                    st = jnp.where(_chunk_mask(t, True), st, -jnp.inf)
                m_new = jnp.maximum(m, jnp.max(st, axis=0, keepdims=True))
                p = jnp.exp(st - m_new)
                alpha = jnp.exp(m - m_new)
                l = alpha * l + jnp.sum(p, axis=0, keepdims=True)
                acc = alpha * acc + _dot(vt_ref[g, :, cols], p.astype(MXU))
                out.append((m_new, l, acc))
            return tuple(out)

        init = tuple((jnp.full((1, t), -jnp.inf, F32), jnp.zeros((1, t), F32), jnp.zeros((VD, t), F32)) for _ in range(hb))
        carries = lax.fori_loop(0, qi, lambda j, c: block(j, c, False), init)
        for g, (m, l, acc) in enumerate(block(qi, carries, True)):
            o = (acc / l).T
            o_ref[g] = o
            ob_ref[g] = o.astype(ob_ref.dtype)
            lse_ref[g] = m + jnp.log(l)
        if gather is not None:
            pl.when((hi == nh - 1) & (step == n - 1))(functools.partial(_ag_stage, "finish", w_ref, g_ref, send_sems, recv_sems))

    qspec = lambda w: pl.BlockSpec((hb, t, w), lambda h, i: (h, i, 0))
    sds = jax.ShapeDtypeStruct
    comm = gather is not None
    return pl.pallas_call(
        body, name=name, grid=(nh, n),
        in_specs=[qspec(128), pl.BlockSpec((hb, S, 128), lambda h, i: (h, 0, 0)), pl.BlockSpec((hb, VD, S), lambda h, i: (h, 0, 0))]
        + ([_ANY] if comm else []),
        out_specs=[qspec(VD), qspec(VD), pl.BlockSpec((hb, 1, t), lambda h, i: (h, 0, i))] + ([_ANY] if comm else []),
        out_shape=[sds((HEADS, S, VD), MXU), sds((HEADS, S, VD), F32), sds((HEADS, 1, S), F32)]
        + ([sds((4,) + gather.shape, gather.dtype)] if comm else []),
        scratch_shapes=list(_AG_SEMS) if comm else [],
        compiler_params=_params(("arbitrary", "arbitrary") if comm else ("parallel", "arbitrary")),
    )(*((q, k, vt) + ((gather,) if comm else ())))


def _attn_delta(o, do, name):
    S = o.shape[1]
    t = min(ROW_TILE, S)

    def body(o_ref, do_ref, d_ref):
        for h in range(HEADS):
            d_ref[h] = jnp.sum(o_ref[h] * do_ref[h], axis=1, keepdims=True)

    hspec = lambda w: pl.BlockSpec((HEADS, t, w), lambda i: (0, i, 0))
    return pl.pallas_call(
        body, name=name, grid=(S // t,), in_specs=[hspec(VD), hspec(VD)], out_specs=hspec(1),
        out_shape=jax.ShapeDtypeStruct((HEADS, S, 1), F32), compiler_params=_params(("parallel",)),
    )(o, do)


def _scatter_stage(stage, g_ref, land_ref, send_sems, recv_sems):
    rh = g_ref.shape[1] // 2
    x, y, c = _coords()
    cps = []
    for p in range(1, 8):
        tx, ty, tc = x ^ ((p >> 2) & 1), y ^ ((p >> 1) & 1), c ^ (p & 1)
        rows = pl.ds(pl.multiple_of(tc * rh, FLAT_ALIGN // 2), rh)
        cps.append(pltpu.make_async_remote_copy(src_ref=g_ref.at[2 * tx + ty, rows, :], dst_ref=land_ref.at[p - 1],
                                                send_sem=send_sems.at[p - 1], recv_sem=recv_sems.at[p - 1], device_id=(tx, ty, tc),
                                                device_id_type=MESH))
    for cp in cps:
        if stage == "start":
            cp.start()
        else:
            cp.wait()


def _attn_bwd(q, k, v, do, lse_row, d_row, name, scatter=None):
    S = q.shape[1]
    t = min(ATT_TILE, S)
    n = S // t

    def body(*refs):
        if scatter is None:
            q_ref, do_ref, lse_ref, d_ref, k_ref, v_ref, dq_ref, dk_ref, dv_ref, dk_s, dv_s = refs
        else:
            q_ref, do_ref, lse_ref, d_ref, k_ref, v_ref, g_ref, dq_ref, dk_ref, dv_ref, land_ref, dk_s, dv_s, send_sems, recv_sems = refs
            pl.when((pl.program_id(0) == 0) & (pl.program_id(1) == 0))(
                functools.partial(_scatter_stage, "start", g_ref, land_ref, send_sems, recv_sems))
        ki = pl.program_id(1)

        @pl.when(ki == 0)
        def _():
            dq_ref[...] = jnp.zeros(dq_ref.shape, F32)

        dk_s[...] = jnp.zeros(dk_s.shape, F32)
        dv_s[...] = jnp.zeros(dv_s.shape, F32)
        kb, vb = k_ref[0], v_ref[0]

        def step(qi, masked):
            rows = pl.ds(pl.multiple_of(qi * t, t), t)
            qb, dob = q_ref[0, rows, :], do_ref[0, rows, :]
            st = _dot_nt(kb, qb) * ATT_SCALE
            if masked:
                st = jnp.where(_chunk_mask(t, True), st, -jnp.inf)
            pt = jnp.exp(st - lse_ref[0, :, rows])
            dv_s[...] += _dot(pt.astype(MXU), dob)
            dpt = _dot_nt(vb, dob)
            dst = (pt * (dpt - d_ref[0, :, rows]) * ATT_SCALE).astype(MXU)
            dk_s[...] += _dot(dst, qb)
            dq_ref[0, rows, :] += _dot_tn(dst, kb)

        step(ki, True)

        def loop(qi, c):
            step(qi, False)
            return c

        lax.fori_loop(ki + 1, n, loop, 0)
        dk_ref[0] = dk_s[...]
        dv_ref[0] = dv_s[...]
        if scatter is not None:
            pl.when((pl.program_id(0) == HEADS - 1) & (ki == n - 1))(
                functools.partial(_scatter_stage, "finish", g_ref, land_ref, send_sems, recv_sems))

    head = lambda *shape: pl.BlockSpec((1,) + shape, lambda h, j: (h, 0, 0))
    kmap = lambda h, j: (h, j, 0)
    sds = jax.ShapeDtypeStruct
    comm = scatter is not None
    return pl.pallas_call(
        body, name=name, grid=(HEADS, n),
        in_specs=[head(S, 128), head(S, VD), head(1, S), head(1, S), pl.BlockSpec((1, t, 128), kmap), pl.BlockSpec((1, t, VD), kmap)]
        + ([_ANY] if comm else []),
        out_specs=[head(S, 128), pl.BlockSpec((1, t, 128), kmap), pl.BlockSpec((1, t, VD), kmap)] + ([_ANY] if comm else []),
        out_shape=[sds((HEADS, S, 128), F32), sds((HEADS, S, 128), F32), sds((HEADS, S, VD), F32)]
        + ([sds((7, scatter.shape[1] // 2, scatter.shape[2]), scatter.dtype)] if comm else []),
        scratch_shapes=[pltpu.VMEM((t, 128), F32), pltpu.VMEM((t, VD), F32)]
        + ([pltpu.SemaphoreType.DMA((7,)), pltpu.SemaphoreType.DMA((7,))] if comm else []),
        compiler_params=_params(("arbitrary", "arbitrary") if comm else ("parallel", "arbitrary")),
    )(*((q, do, lse_row, d_row, k, v) + ((scatter,) if comm else ())))


def _shift_down(x, prev8, d):
    xr = pltpu.roll(x, d, 0)
    r8 = lax.broadcasted_iota(jnp.int32, prev8.shape, 0)
    top = jnp.where(r8 < d, pltpu.roll(prev8, d, 0), xr[0:8])
    return jnp.concatenate([top, xr[8:]], axis=0)


def _shift_up(x, next8, d):
    n = x.shape[0]
    xr = pltpu.roll(x, n - d, 0)
    r8 = lax.broadcasted_iota(jnp.int32, next8.shape, 0)
    bot = jnp.where(r8 >= 8 - d, pltpu.roll(next8, 8 - d, 0), xr[n - 8:])
    return jnp.concatenate([xr[:n - 8], bot], axis=0)


def _log1p(u):
    return jnp.where(u < 0.01, u * (1.0 - u * (0.5 - u * (1.0 / 3.0 - 0.25 * u))), jnp.log(1.0 + u))


def _neg_expm1(y):
    series = -y * (1.0 + 0.5 * y * (1.0 + (1.0 / 3.0) * y * (1.0 + 0.25 * y)))
    return jnp.where(y > -0.05, series, 1.0 - jnp.exp(y))


def _softplus_neg(lam):
    return jnp.maximum(-lam, 0.0) + _log1p(jnp.exp(-jnp.abs(lam)))


def _lru_gates(x, prev8, cw, cb, wr, br, wi, bi, lam):
    xs1, xs2, xs3 = _shift_down(x, prev8, 1), _shift_down(x, prev8, 2), _shift_down(x, prev8, 3)
    xc = cb + cw[0:1] * xs3 + cw[1:2] * xs2 + cw[2:3] * xs1 + cw[3:4] * x
    xcb = xc.astype(MXU)
    r = _sigmoid(_dot(xcb, wr) + br)
    ig = _sigmoid(_dot(xcb, wi) + bi)
    sp = _softplus_neg(lam)
    log_a = -8.0 * r * sp
    a = jnp.exp(log_a)
    gb = jnp.sqrt(_neg_expm1(2.0 * log_a))
    return (xs1, xs2, xs3), xc, xcb, r, ig, sp, a, gb


def _lru_fwd(z, cw, cb, wr, br, wi, bi, lam, name):
    S = z.shape[0]
    t = min(ROW_TILE, S)

    def body(zx_ref, zg_ref, cw_ref, cb_ref, wr_ref, br_ref, wi_ref, bi_ref, lam_ref, y_ref, h_ref, xp_s, hc_s):
        @pl.when(pl.program_id(0) == 0)
        def _():
            xp_s[...] = jnp.zeros(xp_s.shape, F32)
            hc_s[...] = jnp.zeros(hc_s.shape, F32)

        x = zx_ref[...]
        _, xc, _, _, ig, _, a, gb = _lru_gates(x, xp_s[...], cw_ref[...], cb_ref[...], wr_ref[...], br_ref[...],
                                               wi_ref[...], bi_ref[...], lam_ref[...])
        b = gb * (ig * xc)
        rows = lax.broadcasted_iota(jnp.int32, a.shape, 0)
        d = 1
        while d < t:
            ar, brr = pltpu.roll(a, d, 0), pltpu.roll(b, d, 0)
            ok = rows >= d
            b = jnp.where(ok, a * brr, 0.0) + b
            a = jnp.where(ok, a * ar, a)
            d *= 2
        h = a * hc_s[7:8, :] + b
        h_ref[...] = h
        y_ref[...] = (h * _gelu(zg_ref[...])).astype(y_ref.dtype)
        hc_s[...] = h[t - 8:, :]
        xp_s[...] = x[t - 8:, :]

    w = LRU_W
    return pl.pallas_call(
        body, name=name, grid=(S // t,),
        in_specs=[_row_spec(t, w, P_LX // w), _row_spec(t, w, P_LG // w), _full_spec((4, w)), _vec_spec(w), _full_spec((w, w)),
                  _vec_spec(w), _full_spec((w, w)), _vec_spec(w), _vec_spec(w)],
        out_specs=[_row_spec(t, w), _row_spec(t, w)],
        out_shape=[jax.ShapeDtypeStruct((S, w), MXU), jax.ShapeDtypeStruct((S, w), F32)],
        scratch_shapes=[pltpu.VMEM((8, w), F32), pltpu.VMEM((8, w), F32)],
        compiler_params=_params(("arbitrary",)),
    )(z, z, cw, cb, wr, br, wi, bi, lam)


def _lru_bwd(z, h, dy, cw, cb, wr, br, wi, bi, lam, dz_in, name):
    S = z.shape[0]
    t = min(ROW_TILE, S)
    n = S // t
    w = LRU_W

    def body(zx_ref, zxp_ref, zg_ref, h_ref, hp_ref, dy_ref, cw_ref, cb_ref, wr_ref, br_ref, wi_ref, bi_ref, lam_ref, _dz_in,
             dz_ref, dcw_ref, dcb_ref, dwr_ref, dbr_ref, dwi_ref, dbi_ref, dlam_ref, sum_ref, gc_s, dn_s):
        i = pl.program_id(0)
        first = i == 0
        j = n - 1 - i

        @pl.when(first)
        def _():
            gc_s[...] = jnp.zeros(gc_s.shape, F32)
            dn_s[...] = jnp.zeros(dn_s.shape, F32)

        live = (j > 0).astype(F32)
        xprev8, hprev8 = zxp_ref[...] * live, hp_ref[...] * live
        x, zg, hh, dy_ = zx_ref[...], zg_ref[...], h_ref[...], dy_ref[...]
        cw_, lam_ = cw_ref[...], lam_ref[...]
        (xs1, xs2, xs3), xc, xcb, r, ig, sp, a, gb = _lru_gates(x, xprev8, cw_, cb_ref[...], wr_ref[...], br_ref[...],
                                                                wi_ref[...], bi_ref[...], lam_)
        hm1 = _shift_down(hh, hprev8, 1)
        dh = dy_ * _gelu(zg)
        dzg = dy_ * hh * _gelu_grad(zg)
        rows = lax.broadcasted_iota(jnp.int32, a.shape, 0)
        last = rows == t - 1
        ca = jnp.where(last, 0.0, pltpu.roll(a, t - 1, 0))
        g = dh + jnp.where(last, gc_s[0:1, :], 0.0)
        d = 1
        while d < t:
            ok = rows < t - d
            g = g + jnp.where(ok, ca * pltpu.roll(g, t - d, 0), 0.0)
            ca = jnp.where(ok, ca * pltpu.roll(ca, t - d, 0), 0.0)
            d *= 2
        gc_s[...] = a[0:8, :] * g[0:8, :]
        da = g * hm1
        dgb = g * (ig * xc)
        dub = g * gb
        di = dub * xc
        dxc = dub * ig
        dlog_a = da * a - dgb * (a * a) / gb
        dr = dlog_a * (-8.0 * sp)
        _acc(dlam_ref, first, _colsum(dlog_a * (-8.0 * r)) * (-_sigmoid(-lam_)))
        dpr = dr * r * (1.0 - r)
        dpi = di * ig * (1.0 - ig)
        _acc(dbr_ref, first, _colsum(dpr))
        _acc(dbi_ref, first, _colsum(dpi))
        dprb, dpib = dpr.astype(MXU), dpi.astype(MXU)
        _acc(dwr_ref, first, _dot_tn(xcb, dprb))
        _acc(dwi_ref, first, _dot_tn(xcb, dpib))
        dxc = dxc + _dot_nt(dprb, wr_ref[...]) + _dot_nt(dpib, wi_ref[...])
        _acc(dcb_ref, first, _colsum(dxc))
        _acc(dcw_ref, first, jnp.concatenate([_colsum(dxc * xs3), _colsum(dxc * xs2), _colsum(dxc * xs1), _colsum(dxc * x)], axis=0))
        nxt = dn_s[...]
        dx = cw_[3:4] * dxc + cw_[2:3] * _shift_up(dxc, nxt, 1) + cw_[1:2] * _shift_up(dxc, nxt, 2) + cw_[0:1] * _shift_up(dxc, nxt, 3)
        dn_s[...] = dxc[0:8, :]
        dz_ref[...] = jnp.concatenate([dx, dzg], axis=1).astype(dz_ref.dtype)
        _acc(sum_ref, first, jnp.concatenate([_colsum(dx), _colsum(dzg)], axis=1))

    rev = lambda col: pl.BlockSpec((t, w), lambda i, c=col: (n - 1 - i, c))
    prev8 = lambda col: pl.BlockSpec((8, w), lambda i, c=col: (jnp.maximum((n - 1 - i) * (t // 8) - 1, 0), c))
    vec = jax.ShapeDtypeStruct((1, w), F32)
    sds = jax.ShapeDtypeStruct
    return pl.pallas_call(
        body, name=name, grid=(n,),
        in_specs=[rev(P_LX // w), prev8(P_LX // w), rev(P_LG // w), rev(0), prev8(0), rev(0), _full_spec((4, w)), _vec_spec(w),
                  _full_spec((w, w)), _vec_spec(w), _full_spec((w, w)), _vec_spec(w), _vec_spec(w), _ANY],
        out_specs=[pl.BlockSpec((t, 2 * w), lambda i: (n - 1 - i, P_LX // (2 * w))), _full_spec((4, w)), _vec_spec(w),
                   _full_spec((w, w)), _vec_spec(w), _full_spec((w, w)), _vec_spec(w), _vec_spec(w), _vec_spec(2 * w)],
        out_shape=[sds((S, NP), MXU), sds((4, w), F32), vec, sds((w, w), F32), vec, sds((w, w), F32), vec, vec, sds((1, 2 * w), F32)],
        scratch_shapes=[pltpu.VMEM((8, w), F32), pltpu.VMEM((8, w), F32)],
        input_output_aliases=_dz_alias(14), compiler_params=_params(("arbitrary",)),
    )(z, z, z, h, h, dy, cw, cb, wr, br, wi, bi, lam, dz_in)


def _branch_fwd(z, ya, o, yc, wa, wb, wc, name):
    S = z.shape[0]
    t = min(ROW_TILE, S)

    def body(ga_ref, gb_ref, gc_ref, ya_ref, o_ref, yc_ref, wa_ref, wb_ref, wc_ref, m_ref, pa_ref, pb_ref, pc_ref):
        pa = _dot(ya_ref[...], wa_ref[...])
        pc = _dot(yc_ref[...], wc_ref[...])
        pb = _dot(o_ref[0], wb_ref[0])
        for h in range(1, HEADS):
            pb = pb + _dot(o_ref[h], wb_ref[h])
        pa_ref[...], pb_ref[...], pc_ref[...] = pa, pb, pc
        m = _sigmoid(ga_ref[...]) * pa + _sigmoid(gb_ref[...]) * pb + _sigmoid(gc_ref[...]) * pc
        m_ref[...] = m.astype(m_ref.dtype)

    g0 = P_GATE // D
    sds = jax.ShapeDtypeStruct
    return pl.pallas_call(
        body, name=name, grid=(S // t,),
        in_specs=[_row_spec(t, D, g0), _row_spec(t, D, g0 + 1), _row_spec(t, D, g0 + 2), _row_spec(t, GM_W),
                  pl.BlockSpec((HEADS, t, VD), lambda i: (0, i, 0)), _row_spec(t, LRU_W),
                  _full_spec((GM_W, D)), _full_spec((HEADS, VD, D)), _full_spec((LRU_W, D))],
        out_specs=[_row_spec(t, D)] * 4,
        out_shape=[sds((S, D), MXU), sds((S, D), F32), sds((S, D), F32), sds((S, D), F32)],
        compiler_params=_params(("parallel",)),
    )(z, z, z, ya, o, yc, wa, wb, wc)


def _dz_alias(n_inputs):
    return {n_inputs - 1: 0}


def _branch_bwd(z, dm, pa, pb, pc, name):
    S = z.shape[0]
    t = min(ROW_TILE, S)

    def body(ga_ref, gb_ref, gc_ref, dm_ref, pa_ref, pb_ref, pc_ref, dz_ref, da_ref, db_ref, dc_ref, sum_ref):
        first = pl.program_id(0) == 0
        dm_ = dm_ref[...]
        for n_, (g_ref, p_ref, d_ref) in enumerate(((ga_ref, pa_ref, da_ref), (gb_ref, pb_ref, db_ref), (gc_ref, pc_ref, dc_ref))):
            gt = _sigmoid(g_ref[...])
            d_ref[...] = (dm_ * gt).astype(d_ref.dtype)
            dzg = dm_ * p_ref[...] * gt * (1.0 - gt)
            dz_ref[:, n_ * D:(n_ + 1) * D] = dzg.astype(dz_ref.dtype)
            _acc(sum_ref.at[:, n_ * D:(n_ + 1) * D], first, _colsum(dzg))

    g0 = P_GATE // D
    sds = jax.ShapeDtypeStruct
    return pl.pallas_call(
        body, name=name, grid=(S // t,),
        in_specs=[_row_spec(t, D, g0), _row_spec(t, D, g0 + 1), _row_spec(t, D, g0 + 2)] + [_row_spec(t, D)] * 4,
        out_specs=[_row_spec(t, 3 * D, P_GATE // (3 * D))] + [_row_spec(t, D)] * 3 + [_vec_spec(3 * D)],
        out_shape=[sds((S, NP), MXU)] + [sds((S, D), MXU)] * 3 + [sds((1, 3 * D), F32)],
        compiler_params=_params(("arbitrary",)),
    )(z, z, z, dm, pa, pb, pc)


def _heads_bwd(dpb, o, wb, name):
    S = dpb.shape[0]
    t = min(ROW_TILE, S)

    def body(dp_ref, o_ref, wb_ref, dob_ref, do_ref, dwb_ref):
        first = pl.program_id(0) == 0
        dp = dp_ref[...]
        for h in range(HEADS):
            do = _dot_nt(dp, wb_ref[h])
            do_ref[h] = do
            dob_ref[h] = do.astype(dob_ref.dtype)
            _acc(dwb_ref.at[h], first, _dot_tn(o_ref[h], dp))

    hspec = pl.BlockSpec((HEADS, t, VD), lambda i: (0, i, 0))
    sds = jax.ShapeDtypeStruct
    return pl.pallas_call(
        body, name=name, grid=(S // t,),
        in_specs=[_row_spec(t, D), hspec, _full_spec((HEADS, VD, D))],
        out_specs=[hspec, hspec, _full_spec((HEADS, VD, D))],
        out_shape=[sds((HEADS, S, VD), MXU), sds((HEADS, S, VD), F32), sds((HEADS, VD, D), F32)],
        compiler_params=_params(("arbitrary",)),
    )(dpb, o, wb)


def _mod_fwd(c_all, ada_w, name):
    n = ada_w.shape[2]

    def body(c_ref, w_ref, o_ref):
        c = c_ref[...]
        ca = (c * _sigmoid(c)).astype(MXU)
        for l in range(DEPTH):
            o_ref[l] = _dot(ca, w_ref[l].astype(MXU))

    return pl.pallas_call(body, name=name, out_shape=jax.ShapeDtypeStruct((DEPTH, 8, n), F32), compiler_params=_params())(c_all, ada_w)


def _ada_w_grad(c_all_t, dmod, name):
    n = dmod.shape[2]

    def body(c_ref, d_ref, o_ref):
        c = c_ref[...]
        ca = c * _sigmoid(c)
        for l in range(DEPTH):
            dm = d_ref[l]
            acc = ca[:, 0:1] * dm[0:1, :]
            for b in range(1, 8):
                acc = acc + ca[:, b:b + 1] * dm[b:b + 1, :]
            o_ref[l] = acc

    return pl.pallas_call(body, name=name, out_shape=jax.ShapeDtypeStruct((DEPTH, D, n), F32), compiler_params=_params())(c_all_t, dmod)


def _adamw(w, g, m, v, name):
    R, C = w.shape
    t = _tile(R, 256) if R % 8 == 0 else R
    c1, c2 = 1.0 - ADAM_B1 ** ADAM_STEP, 1.0 - ADAM_B2 ** ADAM_STEP

    def body(w_ref, g_ref, m_ref, v_ref, d_ref, nm_ref, nv_ref):
        g_ = g_ref[...]
        m_ = ADAM_B1 * m_ref[...] + (1.0 - ADAM_B1) * g_
        v_ = ADAM_B2 * v_ref[...] + (1.0 - ADAM_B2) * (g_ * g_)
        nm_ref[...] = m_
        nv_ref[...] = v_
        d_ref[...] = -ADAM_LR * ((m_ / c1) / (jnp.sqrt(v_ / c2) + ADAM_EPS) + ADAM_WD * w_ref[...])

    spec = pl.BlockSpec((t, C), lambda i: (i, 0))
    return pl.pallas_call(
        body, name=name, grid=(R // t,), in_specs=[spec] * 4, out_specs=[spec] * 3,
        out_shape=[jax.ShapeDtypeStruct((R, C), F32)] * 3, compiler_params=_params(("parallel",)),
    )(w, g, m, v)


def _sum8(a, name):
    _, R, C = a.shape
    t = _tile(R, 512) if R % 8 == 0 else R

    def body(a_ref, o_ref):
        s = a_ref[0]
        for k in range(1, 8):
            s = s + a_ref[k]
        o_ref[...] = s

    return pl.pallas_call(
        body, name=name, grid=(R // t,), in_specs=[pl.BlockSpec((8, t, C), lambda i: (0, i, 0))],
        out_specs=pl.BlockSpec((t, C), lambda i: (i, 0)), out_shape=jax.ShapeDtypeStruct((R, C), F32),
        compiler_params=_params(("parallel",)),
    )(a)


def _pair_add(shards, from_sib, half, name):
    _, rh, C = from_sib.shape
    t = _tile(rh, 384, 16)
    nb = rh // t

    def body(half_ref, a0, a1, a2, a3, b_ref, o_ref, ob_ref):
        for k, a_ref in enumerate((a0, a1, a2, a3)):
            s = a_ref[...] + b_ref[k]
            o_ref[k] = s
            ob_ref[k] = s.astype(ob_ref.dtype)

    a_spec = pl.BlockSpec((t, C), lambda i, half_ref: (half_ref[0] * nb + i, 0))
    s_spec = pl.BlockSpec((4, t, C), lambda i, half_ref: (0, i, 0))
    return pl.pallas_call(
        body, name=name,
        grid_spec=pltpu.PrefetchScalarGridSpec(num_scalar_prefetch=1, grid=(nb,), in_specs=[a_spec] * 4 + [s_spec],
                                               out_specs=[s_spec, s_spec]),
        out_shape=[jax.ShapeDtypeStruct((4, rh, C), F32), jax.ShapeDtypeStruct((4, rh, C), jnp.bfloat16)],
        compiler_params=_params(("parallel",)),
    )(half, *shards, from_sib)


def _land_sum(g, land, half, name):
    _, rh, C = land.shape
    t = _tile(rh, 256, 16)
    nb = rh // t

    def body(half_ref, g_ref, l_ref, o_ref):
        s = g_ref[...].astype(F32)
        for k in range(7):
            s = s + l_ref[k].astype(F32)
        o_ref[...] = s

    return pl.pallas_call(
        body, name=name,
        grid_spec=pltpu.PrefetchScalarGridSpec(
            num_scalar_prefetch=1, grid=(nb,),
            in_specs=[pl.BlockSpec((t, C), lambda i, h: (h[0] * nb + i, 0)), pl.BlockSpec((7, t, C), lambda i, h: (0, i, 0))],
            out_specs=pl.BlockSpec((t, C), lambda i, h: (h[0] * nb + i, 0))),
        out_shape=jax.ShapeDtypeStruct((2 * rh, C), F32), compiler_params=_params(("parallel",)),
    )(half, g, land)


def _chip_sum(own, recv, half, name):
    rh, C = own.shape
    t = _tile(rh, 512, 16)
    nb = rh // t

    def body(half_ref, a_ref, r_ref, o_ref):
        s = a_ref[...]
        for k in range(3):
            s = s + r_ref[k].astype(F32)
        o_ref[...] = s

    return pl.pallas_call(
        body, name=name,
        grid_spec=pltpu.PrefetchScalarGridSpec(
            num_scalar_prefetch=1, grid=(nb,),
            in_specs=[pl.BlockSpec((t, C), lambda i, h: (i, 0)), pl.BlockSpec((3, t, C), lambda i, h: (0, i, 0))],
            out_specs=pl.BlockSpec((t, C), lambda i, h: (h[0] * nb + i, 0))),
        out_shape=jax.ShapeDtypeStruct((2 * rh, C), F32), compiler_params=_params(("parallel",)),
    )(half, own, recv)


_ANY = pl.BlockSpec(memory_space=pl.ANY)


def _coords():
    return lax.axis_index("x"), lax.axis_index("y"), lax.axis_index("c")


def _allgather8(v, name):
    R, C = v.shape

    def body(v_ref, out_ref, send_sems, recv_sems):
        x, y, c = _coords()
        me = 4 * x + 2 * y + c
        sends = []
        for k in range(1, 8):
            bx, by, bc = (k >> 2) & 1, (k >> 1) & 1, k & 1
            peer = (x ^ bx, y ^ by, c ^ bc)
            cp = pltpu.make_async_remote_copy(src_ref=v_ref, dst_ref=out_ref.at[me], send_sem=send_sems.at[k - 1],
                                              recv_sem=recv_sems.at[k - 1], device_id=peer, device_id_type=MESH)
            cp.start()
            sends.append(cp)
        for k in range(1, 8):
            bx, by, bc = (k >> 2) & 1, (k >> 1) & 1, k & 1
            peer = (x ^ bx, y ^ by, c ^ bc)
            src = 4 * peer[0] + 2 * peer[1] + peer[2]
            pltpu.make_async_remote_copy(src_ref=v_ref, dst_ref=out_ref.at[src], send_sem=send_sems.at[k - 1],
                                         recv_sem=recv_sems.at[k - 1], device_id=peer, device_id_type=MESH).wait_recv()
        for cp in sends:
            cp.wait_send()

    out = pl.pallas_call(
        body, name=name, in_specs=[_ANY], out_specs=_ANY, out_shape=jax.ShapeDtypeStruct((8, R, C), v.dtype),
        scratch_shapes=[pltpu.SemaphoreType.DMA((7,)), pltpu.SemaphoreType.DMA((7,))],
    )(v)
    x, y, c = _coords()
    return lax.dynamic_update_slice(out, v[None], (4 * x + 2 * y + c, 0, 0))


def _chip_patterns(x, y):
    return [(1 - x, y), (x, 1 - y), (1 - x, 1 - y)]


def _ag_stage(stage, w_ref, out_ref, send_sems, recv_sems):
    rh = w_ref.shape[0] // 2
    x, y, c = _coords()
    kme = 2 * x + y
    sibling = (x, y, 1 - c)
    mine_half = pl.ds(pl.multiple_of(c * rh, FLAT_ALIGN // 2), rh)
    other_half = pl.ds(pl.multiple_of((1 - c) * rh, FLAT_ALIGN // 2), rh)
    chips = _chip_patterns(x, y)

    def copy(k, chip_idx, half, to, src=None):
        dst = out_ref.at[chip_idx, half, :]
        return pltpu.make_async_remote_copy(src_ref=dst if src is None else src, dst_ref=dst, send_sem=send_sems.at[k],
                                            recv_sem=recv_sems.at[k], device_id=to, device_id_type=MESH)

    first = [copy(j, kme, mine_half, (cx, cy, c), src=w_ref.at[mine_half, :]) for j, (cx, cy) in enumerate(chips)]
    own = pltpu.make_async_remote_copy(src_ref=w_ref, dst_ref=out_ref.at[kme], send_sem=send_sems.at[6], recv_sem=recv_sems.at[6],
                                       device_id=sibling, device_id_type=MESH)
    passed = [copy(3 + j, 2 * cx + cy, mine_half, sibling) for j, (cx, cy) in enumerate(chips)]
    if stage == "start":
        for cp in first + [own]:
            cp.start()
    elif stage == "forward":
        for j, (cx, cy) in enumerate(chips):
            copy(j, 2 * cx + cy, mine_half, (x, y, c)).wait_recv()
            passed[j].start()
    else:
        for j, (cx, cy) in enumerate(chips):
            copy(3 + j, 2 * cx + cy, other_half, (x, y, c)).wait_recv()
        own.wait_recv()
        for cp in first + [own] + passed:
            cp.wait_send()


_AG_SEMS = [pltpu.SemaphoreType.DMA((7,)), pltpu.SemaphoreType.DMA((7,))]


def _allgather_weights(w, name):
    R, C = w.shape

    def body(w_ref, out_ref, send_sems, recv_sems):
        for stage in ("start", "forward", "finish"):
            _ag_stage(stage, w_ref, out_ref, send_sems, recv_sems)

    return pl.pallas_call(
        body, name=name, in_specs=[_ANY], out_specs=_ANY, out_shape=jax.ShapeDtypeStruct((4, R, C), w.dtype),
        scratch_shapes=list(_AG_SEMS),
    )(w)


def _sibling_swap(shards, name):
    R, C = shards[0].shape
    rh = R // 2

    def body(g0, g1, g2, g3, out_ref, send_sems, recv_sems):
        x, y, c = _coords()
        other_half = pl.ds(pl.multiple_of((1 - c) * rh, 8), rh)
        cps = []
        for k, g_ref in enumerate((g0, g1, g2, g3)):
            cp = pltpu.make_async_remote_copy(src_ref=g_ref.at[other_half, :], dst_ref=out_ref.at[k], send_sem=send_sems.at[k],
                                              recv_sem=recv_sems.at[k], device_id=(x, y, 1 - c), device_id_type=MESH)
            cp.start()
            cps.append(cp)
        for cp in cps:
            cp.wait()

    return pl.pallas_call(
        body, name=name, in_specs=[_ANY] * 4, out_specs=_ANY, out_shape=jax.ShapeDtypeStruct((4, rh, C), shards[0].dtype),
        scratch_shapes=[pltpu.SemaphoreType.DMA((4,)), pltpu.SemaphoreType.DMA((4,))],
    )(*shards)


def _chip_scatter(a, name):
    _, R, C = a.shape

    def body(a_ref, out_ref, send_sems, recv_sems):
        x, y, c = _coords()
        cps = []
        for j, (cx, cy) in enumerate(_chip_patterns(x, y)):
            cp = pltpu.make_async_remote_copy(src_ref=a_ref.at[2 * cx + cy], dst_ref=out_ref.at[j], send_sem=send_sems.at[j],
                                              recv_sem=recv_sems.at[j], device_id=(cx, cy, c), device_id_type=MESH)
            cp.start()
            cps.append(cp)
        for cp in cps:
            cp.wait()

    return pl.pallas_call(
        body, name=name, in_specs=[_ANY], out_specs=_ANY, out_shape=jax.ShapeDtypeStruct((3, R, C), a.dtype),
        scratch_shapes=[pltpu.SemaphoreType.DMA((3,)), pltpu.SemaphoreType.DMA((3,))],
    )(a)


def _sibling_join(f, name):
    R, C = f.shape
    rh = R // 2

    def body(f_ref, out_ref, send_sem, recv_sem):
        x, y, c = _coords()
        mine_half = pl.ds(pl.multiple_of(c * rh, 8), rh)
        cp = pltpu.make_async_remote_copy(src_ref=f_ref.at[mine_half, :], dst_ref=out_ref.at[mine_half, :], send_sem=send_sem,
                                          recv_sem=recv_sem, device_id=(x, y, 1 - c), device_id_type=MESH)
        cp.start()
        cp.wait()

    return pl.pallas_call(
        body, name=name, in_specs=[_ANY], out_specs=_ANY, out_shape=jax.ShapeDtypeStruct((R, C), f.dtype),
        scratch_shapes=[pltpu.SemaphoreType.DMA(()), pltpu.SemaphoreType.DMA(())], input_output_aliases={0: 0},
    )(f)


_EARLY = (("in_w", 1), ("mla_wuq", 1), ("mla_wukv", 1))
_LATE = (("branch_w", 2), ("mix_out_w", 0), ("ffn_w1", 1), ("ffn_w2", 0))
_BIG = _EARLY + _LATE


def _pack(shards, group):
    flat = jnp.concatenate([shards[n].reshape(-1, FLAT_W) for n, _ in group], axis=0)
    pad = (-flat.shape[0]) % FLAT_ALIGN
    return jnp.pad(flat, ((0, pad), (0, 0)))


def _packed_rows(shapes, group):
    rows = sum(math.prod(shapes[n]) // FLAT_W for n, _ in group)
    return rows + (-rows) % FLAT_ALIGN


def _unpack(flat, shapes, group):
    out, r = {}, 0
    for n, _ in group:
        rows = math.prod(shapes[n]) // FLAT_W
        out[n] = flat[..., r:r + rows, :].reshape(flat.shape[:-2] + tuple(shapes[n]))
        r += rows
    return out


def _swap16(a):
    return jnp.concatenate([a[..., 16:32], a[..., 0:16]], axis=-1)


def _shard_cols(parts, a, b):
    w = parts[0].shape[-1]
    pieces = []
    for k, p in enumerate(parts):
        s, e = max(a, k * w), min(b, (k + 1) * w)
        if s < e:
            pieces.append(p[..., s - k * w:e - k * w])
    return pieces


def _pad_in_cols(parts):
    lead, dt = parts[0].shape[:-1], parts[0].dtype
    z = lambda n: [jnp.zeros(lead + (n,), dt)]
    seg = lambda a, b: _shard_cols(parts, a, b)
    kr = jnp.concatenate(seg(O_KR, O_LX), axis=-1)
    return jnp.concatenate(seg(O_GATE, N_IN) + seg(O_GM, O_Q) + seg(O_LX, O_LG) + seg(O_LG, O_GATE) + seg(O_Q, O_KV) + seg(O_KV, O_KR)
                           + z(64) + [kr] + z(32) + z(64) + [_swap16(kr)] + z(32), axis=-1)


_SEGS = ((O_GM, O_Q, P_GM), (O_Q, O_KV, P_Q), (O_KV, O_KR, P_KV), (O_KR, O_LX, None), (O_LX, O_LG, P_LX), (O_LG, O_GATE, P_LG),
         (O_GATE, N_IN, P_GATE))


def _unpad_in_cols(g, a=0, b=N_IN):
    pieces = []
    for lo, hi, p in _SEGS:
        s, e = max(a, lo), min(b, hi)
        if s >= e:
            continue
        if p is None:
            kr = g[..., P_KA + 64:P_KA + 96] + _swap16(g[..., P_KB + 64:P_KB + 96])
            pieces.append(kr[..., s - lo:e - lo])
        else:
            pieces.append(g[..., p + s - lo:p + e - lo])
    return jnp.concatenate(pieces, axis=-1)


def _flat128(vecs):
    flat = jnp.concatenate([v.reshape(-1) for v in vecs])
    pad = (-flat.shape[0]) % 1024
    return jnp.pad(flat, (0, pad)).reshape(-1, 128)


def _unflat(flat, shapes):
    flat = flat.reshape(-1)
    out, r = [], 0
    for s in shapes:
        n = math.prod(s)
        out.append(flat[r:r + n].reshape(s))
        r += n
    return out


_SMALL = ("ada_b", "in_b", "gm_ln_g", "gm_ln_b", "gm_ws", "gm_bs", "mla_qnorm_g", "mla_kvnorm_g", "lru_conv_w", "lru_conv_b",
          "lru_wr", "lru_br", "lru_wi", "lru_bi", "lru_lambda", "ffn_b1", "ffn_b2", "ln_g", "ln_b")
_NAMES = ("ada_w", "ada_b", "in_w", "in_b", "gm_ln_g", "gm_ln_b", "gm_ws", "gm_bs", "mla_qnorm_g", "mla_wuq", "mla_kvnorm_g",
          "mla_wukv", "lru_conv_w", "lru_conv_b", "lru_wr", "lru_br", "lru_wi", "lru_bi", "lru_lambda", "branch_w", "mix_out_w",
          "ffn_w1", "ffn_b1", "ffn_w2", "ffn_b2", "ln_g", "ln_b")


def kernel(x, c, ada_w, ada_b, in_w, in_b, gm_ln_g, gm_ln_b, gm_ws, gm_bs, mla_qnorm_g, mla_wuq, mla_kvnorm_g, mla_wukv, lru_conv_w, lru_conv_b, lru_wr, lru_br, lru_wi, lru_bi, lru_lambda, branch_w, mix_out_w, ffn_w1, ffn_b1, ffn_w2, ffn_b2, ln_g, ln_b, loss_target, m_ada_w, m_ada_b, m_in_w, m_in_b, m_gm_ln_g, m_gm_ln_b, m_gm_ws, m_gm_bs, m_mla_qnorm_g, m_mla_wuq, m_mla_kvnorm_g, m_mla_wukv, m_lru_conv_w, m_lru_conv_b, m_lru_wr, m_lru_br, m_lru_wi, m_lru_bi, m_lru_lambda, m_branch_w, m_mix_out_w, m_ffn_w1, m_ffn_b1, m_ffn_w2, m_ffn_b2, m_ln_g, m_ln_b, v_ada_w, v_ada_b, v_in_w, v_in_b, v_gm_ln_g, v_gm_ln_b, v_gm_ws, v_gm_bs, v_mla_qnorm_g, v_mla_wuq, v_mla_kvnorm_g, v_mla_wukv, v_lru_conv_w, v_lru_conv_b, v_lru_wr, v_lru_br, v_lru_wi, v_lru_bi, v_lru_lambda, v_branch_w, v_mix_out_w, v_ffn_w1, v_ffn_b1, v_ffn_w2, v_ffn_b2, v_ln_g, v_ln_b):
    W = dict(ada_w=ada_w, ada_b=ada_b, in_w=in_w, in_b=in_b, gm_ln_g=gm_ln_g, gm_ln_b=gm_ln_b, gm_ws=gm_ws, gm_bs=gm_bs,
             mla_qnorm_g=mla_qnorm_g, mla_wuq=mla_wuq, mla_kvnorm_g=mla_kvnorm_g, mla_wukv=mla_wukv, lru_conv_w=lru_conv_w,
             lru_conv_b=lru_conv_b, lru_wr=lru_wr, lru_br=lru_br, lru_wi=lru_wi, lru_bi=lru_bi, lru_lambda=lru_lambda,
             branch_w=branch_w, mix_out_w=mix_out_w, ffn_w1=ffn_w1, ffn_b1=ffn_b1, ffn_w2=ffn_w2, ffn_b2=ffn_b2, ln_g=ln_g, ln_b=ln_b)
    M = dict(ada_w=m_ada_w, ada_b=m_ada_b, in_w=m_in_w, in_b=m_in_b, gm_ln_g=m_gm_ln_g, gm_ln_b=m_gm_ln_b, gm_ws=m_gm_ws,
             gm_bs=m_gm_bs, mla_qnorm_g=m_mla_qnorm_g, mla_wuq=m_mla_wuq, mla_kvnorm_g=m_mla_kvnorm_g, mla_wukv=m_mla_wukv,
             lru_conv_w=m_lru_conv_w, lru_conv_b=m_lru_conv_b, lru_wr=m_lru_wr, lru_br=m_lru_br, lru_wi=m_lru_wi, lru_bi=m_lru_bi,
             lru_lambda=m_lru_lambda, branch_w=m_branch_w, mix_out_w=m_mix_out_w, ffn_w1=m_ffn_w1, ffn_b1=m_ffn_b1, ffn_w2=m_ffn_w2,
             ffn_b2=m_ffn_b2, ln_g=m_ln_g, ln_b=m_ln_b)
    V = dict(ada_w=v_ada_w, ada_b=v_ada_b, in_w=v_in_w, in_b=v_in_b, gm_ln_g=v_gm_ln_g, gm_ln_b=v_gm_ln_b, gm_ws=v_gm_ws,
             gm_bs=v_gm_bs, mla_qnorm_g=v_mla_qnorm_g, mla_wuq=v_mla_wuq, mla_kvnorm_g=v_mla_kvnorm_g, mla_wukv=v_mla_wukv,
             lru_conv_w=v_lru_conv_w, lru_conv_b=v_lru_conv_b, lru_wr=v_lru_wr, lru_br=v_lru_br, lru_wi=v_lru_wi, lru_bi=v_lru_bi,
             lru_lambda=v_lru_lambda, branch_w=v_branch_w, mix_out_w=v_mix_out_w, ffn_w1=v_ffn_w1, ffn_b1=v_ffn_b1, ffn_w2=v_ffn_w2,
             ffn_b2=v_ffn_b2, ln_g=v_ln_g, ln_b=v_ln_b)

    S = x.shape[1]
    xi, yi, ci = _coords()
    kme = 2 * xi + yi
    me = 4 * xi + 2 * yi + ci
    x0 = x[0]
    tgt = loss_target[0]

    assert DEPTH == 2
    shard_shapes = {n: W[n].shape[1:] for n, _ in _BIG}
    rows_e, rows_l = _packed_rows(shard_shapes, _EARLY), _packed_rows(shard_shapes, _LATE)
    wpack = lambda l, group: _pack({n: W[n][l].astype(MXU) for n, _ in group}, group)
    joined = lambda sh, n, ax: jnp.concatenate([sh[n][k] for k in range(4)], axis=ax)

    def early_weights(gathered):
        sh = _unpack(gathered, shard_shapes, _EARLY)
        wuq = joined(sh, "mla_wuq", 1).reshape(Q_RANK, HEADS, NOPE + ROPE)
        zq = lambda n: jnp.zeros((Q_RANK, HEADS, n), MXU)
        wukv = joined(sh, "mla_wukv", 1).reshape(KV_RANK, HEADS, NOPE + VD)
        return dict(
            win=_pad_in_cols([sh["in_w"][k] for k in range(4)]),
            w1q=jnp.concatenate([wuq, zq(32)], axis=-1).reshape(Q_RANK, D),
            w2q=jnp.concatenate([zq(64), _swap16(wuq[..., NOPE:]), zq(32)], axis=-1).reshape(Q_RANK, D),
            wk=jnp.concatenate([wukv[..., :NOPE], jnp.zeros((KV_RANK, HEADS, 64), MXU)], axis=-1).reshape(KV_RANK, D),
            wv=wukv[..., NOPE:].transpose(1, 0, 2), wvt=wukv[..., NOPE:].transpose(1, 2, 0))

    def late_weights(gathered):
        sh = _unpack(gathered, shard_shapes, _LATE)
        bw = joined(sh, "branch_w", 2)
        return dict(wa=bw[0], wb=bw[1].reshape(HEADS, VD, D), wc=bw[2], mix=joined(sh, "mix_out_w", 0),
                    w1=joined(sh, "ffn_w1", 1), w2=joined(sh, "ffn_w2", 0))

    LW = [early_weights(_allgather_weights(wpack(0, _EARLY), "ag_weights0")), None]
    ride_w = jnp.concatenate([wpack(0, _LATE), wpack(1, _EARLY), wpack(1, _LATE)], axis=0)
    ride_w = jnp.pad(ride_w, ((0, (-ride_w.shape[0]) % RIDE_ALIGN), (0, 0)))

    small1 = _allgather8(_flat128([c, ln_g, ln_b, lru_conv_w]), "ag_small")
    per_dev = [_unflat(small1[d], [c.shape, ln_g.shape, ln_b.shape, lru_conv_w.shape]) for d in range(8)]
    c_all = jnp.concatenate([p[0] for p in per_dev], axis=0)
    chip = lambda i: [per_dev[2 * k][i] for k in range(4)]
    ln_g_f, ln_b_f, conv_w_f = (jnp.concatenate(chip(1), axis=2), jnp.concatenate(chip(2), axis=2), jnp.concatenate(chip(3), axis=2))

    mod_sh = _mod_fwd(c_all, ada_w, "mod_fwd")
    mod_all = _allgather8(_flat128([mod_sh]), "ag_mod")
    mod_parts = [_unflat(mod_all[2 * k], [mod_sh.shape])[0] for k in range(4)]
    mod = jnp.concatenate(mod_parts, axis=2)
    mod = lax.dynamic_index_in_dim(mod, me, axis=1, keepdims=False) + ada_b

    pos = jnp.arange(S, dtype=F32)
    inv = ROPE_BASE ** (-jnp.arange(0, ROPE, 2, dtype=F32) / ROPE)
    ang = pos[:, None] * inv[None, :]
    cs, sn = jnp.cos(ang), jnp.sin(ang)
    zc = lambda n: jnp.zeros((S, n), F32)
    cos_t = jnp.concatenate([zc(64), cs, cs, zc(32)], axis=1)
    sin_t = jnp.concatenate([zc(64), -sn, sn, zc(32)], axis=1)

    bin_p = _pad_in_cols([in_b])
    wr_f =jnp.stack([block_diag(*[lru_wr[l, b] for b in range(LRU_NB)]) for l in range(DEPTH)]).astype(MXU)
    wi_f = jnp.stack([block_diag(*[lru_wi[l, b] for b in range(LRU_NB)]) for l in range(DEPTH)]).astype(MXU)
    ws_b = gm_ws.astype(MXU)
    wst_b = gm_ws.transpose(0, 1, 3, 2).astype(MXU)
    bs_t = gm_bs.transpose(0, 2, 1)
    row = lambda v: v.reshape(1, -1)

    saved = []
    xs = x0
    h1 = None
    for l in range(DEPTH):
        sh1, sc1, g1, sh2, sc2, g2 = [row(mod[l, i * D:(i + 1) * D]) for i in range(6)]
        if l == 0:
            h1 = _lnmod_fwd(xs, sc1, sh1, f"lnmod_fwd{l}")
        lw = LW[l]
        z = _mm(h1, lw["win"], "nn", f"in_proj{l}", rows=[row(bin_p[l])], epi=lambda acc, b: (acc + b,))
        ya = _gmlp_fwd(z, row(gm_ln_g[l]), row(gm_ln_b[l]), ws_b[l], bs_t[l], f"gmlp_fwd{l}")
        qf, kf, vf, vtf = _mla_prep_fwd(z, cos_t, sin_t, row(mla_qnorm_g[l]), row(mla_kvnorm_g[l]), lw["w1q"], lw["w2q"], lw["wk"],
                                        lw["wv"], lw["wvt"], f"mla_prep_fwd{l}")
        if l == 0:
            ob, o32, lse, rode = _attn_fwd(qf, kf, vtf, f"attn_fwd{l}", gather=ride_w)
            lw.update(late_weights(rode[:, :rows_l]))
            LW[1] = {**early_weights(rode[:, rows_l:rows_l + rows_e]), **late_weights(rode[:, rows_l + rows_e:rows_l + rows_e + rows_l])}
        else:
            ob, o32, lse = _attn_fwd(qf, kf, vtf, f"attn_fwd{l}")
        lru_args = (conv_w_f[l], row(lru_conv_b[l]), wr_f[l], row(lru_br[l]), wi_f[l], row(lru_bi[l]), row(lru_lambda[l]))
        yc, hl = _lru_fwd(z, *lru_args, f"lru_fwd{l}")
        merged, pa, pb, pc = _branch_fwd(z, ya, ob, yc, lw["wa"], lw["wb"], lw["wc"], f"branch_fwd{l}")
        mix = _mm(merged, lw["mix"], "nn", f"mix_out{l}")
        x1, h2 = _res_ln_fwd(xs, mix, g1, row(ln_g_f[l, 0]), row(ln_b_f[l, 0]), f"res_ln_a{l}", mod=(sc2, sh2))
        a1, r2 = _mm(h2, lw["w1"], "nn", f"ffn_up{l}", out_dtypes=(F32, MXU), rows=[row(ffn_b1[l])],
                     epi=lambda acc, b: (acc + b, jnp.square(jnp.maximum(acc + b, 0.0))))
        f = _mm(r2, lw["w2"], "nn", f"ffn_down{l}", rows=[row(ffn_b2[l])], epi=lambda acc, b: (acc + b,))
        if l + 1 < DEPTH:
            nsh1, nsc1 = row(mod[l + 1, 0:D]), row(mod[l + 1, D:2 * D])
            x2, h1n = _res_ln_fwd(x1, f, g2, row(ln_g_f[l, 1]), row(ln_b_f[l, 1]), f"res_ln_b{l}", mod=(nsc1, nsh1))
        else:
            (x2,), h1n = _res_ln_fwd(x1, f, g2, row(ln_g_f[l, 1]), row(ln_b_f[l, 1]), f"res_ln_b{l}"), None
        saved.append(dict(x_in=xs, h1=h1, z=z, ya=ya, qf=qf, kf=kf, vf=vf, ob=ob, o32=o32, lse=lse, yc=yc, hl=hl, merged=merged,
                          pa=pa, pb=pb, pc=pc, mix=mix, x1=x1, h2=h2, a1=a1, r2=r2, f=f, x2=x2, lru_args=lru_args,
                          mods=(sh1, sc1, g1, sh2, sc2, g2)))
        xs, h1 = x2, h1n

    dy, sq = _loss_fwd(xs, tgt)
    loss = lax.psum(0.5 / D * jnp.sum(sq), ("x", "y", "c"))

    G = {}
    dmods = []
    dres, hpath = dy, None
    wcols = in_w.shape[2]

    def grad_shards(gl, dtype, group):
        shard_of = lambda n, ax, k: (_unpad_in_cols(gl[n], k * wcols, (k + 1) * wcols) if n == "in_w" else jnp.split(gl[n], 4, axis=ax)[k])
        return [_pack({n: shard_of(n, ax, k).astype(dtype) for n, ax in group}, group) for k in range(4)]

    landed = None
    for l in reversed(range(DEPTH)):
        sv = saved[l]
        lw = LW[l]
        sh1, sc1, g1, sh2, sc2, g2 = sv["mods"]
        gl = {}
        nb = _node_bwd(f"node_b{l}", dres, hpath=hpath, upath=(sv["x1"], sv["f"], g2, row(ln_g_f[l, 1])))
        if hpath is not None:
            dmods[-1]["sc1"], dmods[-1]["sh1"] = nb["dsc"], nb["dsh"]
        dm = dict(g2=nb["dg"])
        ln_g_l1, ln_b_l1 = nb["dlng"], nb["dlnb"]
        df = nb["dbr"]
        gl["ffn_b2"] = nb["dbrsum"][0]
        gl["ffn_w2"] = _mm(sv["r2"], df, "tn", f"d_ffn_w2_{l}")
        da1, db1 = _mm(df, lw["w2"], "nt", f"d_ffn_act{l}", out_dtypes=(MXU,), extras=[sv["a1"]], colsum=True,
                       epi=lambda acc, a: (acc * (2.0 * jnp.maximum(a, 0.0)),))
        gl["ffn_b1"] = db1[0]
        gl["ffn_w1"] = _mm(sv["h2"], da1, "tn", f"d_ffn_w1_{l}")
        dh2 = _mm(da1, lw["w1"], "nt", f"d_ffn_in{l}")
        na = _node_bwd(f"node_a{l}", nb["du"], hpath=(dh2, sv["x1"], sc2), upath=(sv["x_in"], sv["mix"], g1, row(ln_g_f[l, 0])))
        dm.update(sc2=na["dsc"], sh2=na["dsh"], g1=na["dg"])
        gl["ln_g"] = jnp.concatenate([na["dlng"], ln_g_l1], axis=0)
        gl["ln_b"] = jnp.concatenate([na["dlnb"], ln_b_l1], axis=0)
        dmix = na["dbr"]
        gl["mix_out_w"] = _mm(sv["merged"], dmix, "tn", f"d_mix_w{l}")
        dmerged = _mm(dmix, lw["mix"], "nt", f"d_merged{l}")
        dz, dpa, dpb, dpc, s_gate = _branch_bwd(sv["z"], dmerged, sv["pa"], sv["pb"], sv["pc"], f"branch_bwd{l}")
        dya = _mm(dpa, lw["wa"], "nt", f"d_ya{l}")
        dyc = _mm(dpc, lw["wc"], "nt", f"d_yc{l}")
        dwa = _mm(sv["ya"], dpa, "tn", f"d_wa{l}")
        dwc = _mm(sv["yc"], dpc, "tn", f"d_wc{l}")
        dob, do32, dwb = _heads_bwd(dpb, sv["ob"], lw["wb"], f"heads_bwd{l}")
        gl["branch_w"] = jnp.stack([dwa, dwb.reshape(GM_W, D), dwc])
        dz, dcw, dcb, dwr, dbr_, dwi, dbi, dlam, s_lru = _lru_bwd(sv["z"], sv["hl"], dyc, *sv["lru_args"], dz, f"lru_bwd{l}")
        gl["lru_conv_w"], gl["lru_conv_b"], gl["lru_br"], gl["lru_bi"], gl["lru_lambda"] = dcw, dcb[0], dbr_[0], dbi[0], dlam[0]
        blocks = lambda m: jnp.stack([m[b * 64:(b + 1) * 64, b * 64:(b + 1) * 64] for b in range(LRU_NB)])
        gl["lru_wr"], gl["lru_wi"] = blocks(dwr), blocks(dwi)
        delta = _attn_delta(sv["o32"], do32, f"attn_delta{l}")
        attn_args = (sv["qf"], sv["kf"], sv["vf"], dob, sv["lse"], delta.reshape(HEADS, 1, S), f"attn_bwd{l}")
        if l == 0:
            pieces = zip(grad_shards(gl, jnp.bfloat16, _LATE), grad_shards(G[1], jnp.bfloat16, _EARLY), grad_shards(G[1], jnp.bfloat16, _LATE))
            sent = jnp.stack([jnp.concatenate(p, axis=0) for p in pieces])
            sent = jnp.pad(sent, ((0, 0), (0, (-sent.shape[1]) % RIDE_ALIGN), (0, 0)))
            dqf, dkf, dvf, landed = _attn_bwd(*attn_args, scatter=sent)
        else:
            dqf, dkf, dvf = _attn_bwd(*attn_args)
        dz, dgq, dgkv, dw1, dw2, dwk, dwv, s_mla = _mla_prep_bwd(sv["z"], cos_t, sin_t, row(mla_qnorm_g[l]), row(mla_kvnorm_g[l]),
                                                                 lw["w1q"], lw["w2q"], lw["wk"], lw["wv"], dqf, dkf, dvf, dz,
                                                                 f"mla_prep_bwd{l}")
        gl["mla_qnorm_g"], gl["mla_kvnorm_g"] = dgq[0], dgkv[0]
        dw1 = dw1.reshape(Q_RANK, HEADS, 128)
        dw2 = dw2.reshape(Q_RANK, HEADS, 128)
        gl["mla_wuq"] = jnp.concatenate([dw1[..., :NOPE], dw1[..., NOPE:NOPE + ROPE] + _swap16(dw2[..., NOPE:NOPE + ROPE])],
                                        axis=-1).reshape(Q_RANK, HEADS * (NOPE + ROPE))
        gl["mla_wukv"] = jnp.concatenate([dwk.reshape(KV_RANK, HEADS, 128)[..., :NOPE], dwv.transpose(1, 0, 2)],
                                         axis=-1).reshape(KV_RANK, HEADS * (NOPE + VD))
        dz, dglg, dglb, dws, dbs, s_gm = _gmlp_bwd(sv["z"], dya, row(gm_ln_g[l]), row(gm_ln_b[l]), ws_b[l], wst_b[l], bs_t[l], dz,
                                                   f"gmlp_bwd{l}")
        gl["gm_ln_g"], gl["gm_ln_b"], gl["gm_ws"], gl["gm_bs"] = dglg[0], dglb[0], dws, dbs[:, :GM_G].T
        gl["in_b"] = _unpad_in_cols(jnp.concatenate([s_gate, s_gm, s_lru, s_mla], axis=1))[0]
        gl["in_w"] = _mm(sv["h1"], dz, "tn", f"d_in_w{l}")
        dh1 = _mm(dz, lw["win"], "nt", f"d_h1_{l}")
        dmods.append(dm)
        dres, hpath = na["du"], (dh1, sv["x_in"], sc1)
        G[l] = gl
    n0 = _node_bwd("node_in", dres, hpath=hpath)
    dmods[-1]["sc1"], dmods[-1]["sh1"] = n0["dsc"], n0["dsh"]
    grad_x = n0["dx"][None]
    dmods = dmods[::-1]
    dmod = jnp.stack([jnp.concatenate([dmods[l][k] for k in ("sh1", "sc1", "g1", "sh2", "sc2", "g2")], axis=1)[0]
                      for l in range(DEPTH)])
    grads = {n: jnp.stack([G[l][n] for l in range(DEPTH)]) for n in _SMALL if n != "ada_b"}
    grads["ada_b"] = dmod

    half = ci.reshape(1).astype(jnp.int32)
    rode_sum = _land_sum(lax.dynamic_index_in_dim(sent, kme, axis=0, keepdims=False), landed, half, "rs_ride_sum")
    gsh = grad_shards(G[0], F32, _EARLY)
    from_sib = _sibling_swap(gsh, "rs_pair")
    pair, pair_b = _pair_add(gsh, from_sib, half, "rs_pair_add")
    from_chips = _chip_scatter(pair_b, "rs_chips")
    own = lax.dynamic_index_in_dim(pair, kme, axis=0, keepdims=False)
    early0_sum = _chip_sum(own, from_chips, half, "rs_chip_sum")
    rode_g = _sibling_join(rode_sum, "rs_join_ride")
    layer0 = {**_unpack(_sibling_join(early0_sum, "rs_join_early"), shard_shapes, _EARLY), **_unpack(rode_g[:rows_l], shard_shapes, _LATE)}
    layer1 = {**_unpack(rode_g[rows_l:rows_l + rows_e], shard_shapes, _EARLY),
              **_unpack(rode_g[rows_l + rows_e:rows_l + rows_e + rows_l], shard_shapes, _LATE)}
    gbig = {n: jnp.stack([layer0[n], layer1[n]]) for n, _ in _BIG}

    small_shapes = [grads[n].shape for n in _SMALL]
    gsm_all = _allgather8(_flat128([grads[n] for n in _SMALL]), "ag_small_grads")
    gsm = _unflat(_sum8(gsm_all, "sum_small_grads"), small_shapes)
    gsmall = dict(zip(_SMALL, gsm))
    n_mod = DEPTH * 6 * D
    dmod_all = gsm_all[:, :n_mod // 128].reshape(8, DEPTH, 6 * D).transpose(1, 0, 2)
    dmod_sh = lax.dynamic_slice_in_dim(dmod_all, kme * (6 * D // 4), 6 * D // 4, axis=2)
    g_ada_w = _ada_w_grad(c_all.T, dmod_sh, "d_ada_w")
    quarter = lambda g, ax: lax.dynamic_slice_in_dim(g, kme * (g.shape[ax] // 4), g.shape[ax] // 4, axis=ax)
    gsmall["lru_conv_w"] = quarter(gsmall["lru_conv_w"], 2)
    gsmall["ln_g"] = quarter(gsmall["ln_g"], 2)
    gsmall["ln_b"] = quarter(gsmall["ln_b"], 2)

    grad = dict(gbig)
    grad.update(gsmall)
    grad["ada_w"] = g_ada_w

    delta, new_m, new_v = {}, {}, {}
    for n in _NAMES:
        shp = W[n].shape
        two = lambda a: a.reshape(-1, shp[-1])
        d_, m_, v_ = _adamw(two(W[n]), two(grad[n]), two(M[n]), two(V[n]), f"adamw_{n}")
        delta[n], new_m[n], new_v[n] = d_.reshape(shp), m_.reshape(shp), v_.reshape(shp)

    return (loss, grad_x, *[grad[n] for n in _NAMES], *[delta[n] for n in _NAMES], *[new_m[n] for n in _NAMES],
            *[new_v[n] for n in _NAMES])
```

```python
import functools
import math

import jax
import jax.numpy as jnp
from jax import lax
from jax.experimental import pallas as pl
from jax.experimental.pallas import tpu as pltpu
from jax.scipy.linalg import block_diag

F32 = jnp.float32
MXU = jnp.bfloat16
MESH = pl.DeviceIdType.MESH

D = 1024
DEPTH = 2
CHUNK = 64
GM_W = 512
GM_G = 4
HEADS = 8
Q_RANK = 256
KV_RANK = 128
NOPE = 64
ROPE = 32
VD = 64
LRU_W = 512
LRU_NB = 8
D_FF = 4096
ALPHA = (2.0 * DEPTH) ** 0.25
LN_EPS = 1e-5
RMS_EPS = 1e-6
ROPE_BASE = 10000.0
ATT_SCALE = (NOPE + ROPE) ** -0.5
ATT_SCALE2 = ATT_SCALE * math.log2(math.e)
N_IN = 5536
P_GATE, P_GM, P_LX, P_LG, P_Q, P_KV, P_KA, P_KB, NP = 0, 3072, 4096, 4608, 5120, 5376, 5504, 5632, 5760
O_GM, O_Q, O_KV, O_KR, O_LX, O_LG, O_GATE = 0, 1024, 1280, 1408, 1440, 1952, 2464

ADAM_LR, ADAM_B1, ADAM_B2, ADAM_EPS, ADAM_WD, ADAM_STEP = 0.001, 0.9, 0.999, 1e-08, 0.01, 10

ROW_TILE = 512
ATT_TILE = 512
VMEM_LIMIT = 56 * 1024 * 1024
FLAT_W = 1024
FLAT_ALIGN = 32
RIDE_ALIGN = 1024


def _params(sem=None, vmem=VMEM_LIMIT):
    return pltpu.CompilerParams(dimension_semantics=sem, vmem_limit_bytes=vmem)


def _tile(dim, pref, mult=128):
    if dim <= pref:
        return dim
    t = (pref // mult) * mult
    while t >= mult:
        if dim % t == 0:
            return t
        t -= mult
    return dim


def _dot(a, b):
    return lax.dot_general(a, b, (((1,), (0,)), ((), ())), preferred_element_type=F32)


def _dot_nt(a, b):
    return lax.dot_general(a, b, (((1,), (1,)), ((), ())), preferred_element_type=F32)


def _dot_tn(a, b):
    return lax.dot_general(a, b, (((0,), (0,)), ((), ())), preferred_element_type=F32)


def _sigmoid(x):
    return 1.0 / (1.0 + jnp.exp(-x))


_GC = 0.7978845608028654


def _gelu(x):
    return 0.5 * x * (1.0 + jnp.tanh(_GC * (x + 0.044715 * x * x * x)))


def _gelu_grad(x):
    t = jnp.tanh(_GC * (x + 0.044715 * x * x * x))
    return 0.5 * (1.0 + t) + 0.5 * x * (1.0 - t * t) * _GC * (1.0 + 3.0 * 0.044715 * x * x)


def _ln_stats(x):
    mu = jnp.mean(x, axis=-1, keepdims=True)
    xc = x - mu
    r = lax.rsqrt(jnp.mean(xc * xc, axis=-1, keepdims=True) + LN_EPS)
    return xc * r, r


def _ln_bwd(dxh, xh, r):
    return r * (dxh - jnp.mean(dxh, axis=-1, keepdims=True) - xh * jnp.mean(dxh * xh, axis=-1, keepdims=True))


def _colsum(v):
    return jnp.sum(v, axis=0, keepdims=True)


def _acc(ref, first, val):
    @pl.when(first)
    def _():
        ref[...] = val

    @pl.when(jnp.logical_not(first))
    def _():
        ref[...] += val


def _row_spec(t, w, col=0):
    return pl.BlockSpec((t, w), lambda i, c=col: (i, c))


def _vec_spec(w):
    return pl.BlockSpec((1, w), lambda i: (0, 0))


def _full_spec(shape):
    nd = len(shape)
    return pl.BlockSpec(shape, lambda i, n=nd: (0,) * n)


def _mm(a, b, mode, name, out_dtypes=(F32,), extras=(), rows=(), epi=None, colsum=False, tm=1024, tn=1152, tk=2048):
    if mode == "nn":
        (M, K), N = a.shape, b.shape[1]
    elif mode == "nt":
        (M, K), N = a.shape, b.shape[0]
    else:
        (K, M), N = a.shape, b.shape[1]
    tm, tn, tk = _tile(M, tm), _tile(N, tn), _tile(K, tk)
    nk = K // tk
    dot = {"nn": _dot, "nt": _dot_nt, "tn": _dot_tn}[mode]
    a_spec = pl.BlockSpec((tk, tm), lambda i, j, k: (k, i)) if mode == "tn" else pl.BlockSpec((tm, tk), lambda i, j, k: (i, k))
    b_spec = pl.BlockSpec((tn, tk), lambda i, j, k: (j, k)) if mode == "nt" else pl.BlockSpec((tk, tn), lambda i, j, k: (k, j))
    o_spec = pl.BlockSpec((tm, tn), lambda i, j, k: (i, j))
    r_spec = pl.BlockSpec((1, tn), lambda i, j, k: (0, j))
    n_e, n_r, n_o = len(extras), len(rows), len(out_dtypes)
    if epi is None:
        epi = lambda acc: (acc,)

    def body(*refs):
        a_ref, b_ref = refs[0], refs[1]
        e_refs = refs[2:2 + n_e]
        r_refs = refs[2 + n_e:2 + n_e + n_r]
        o_refs = refs[2 + n_e + n_r:2 + n_e + n_r + n_o]
        p = dot(a_ref[...], b_ref[...])

        def finish(acc):
            outs = epi(acc, *[e[...] for e in e_refs], *[r[...] for r in r_refs])
            for o_ref, o in zip(o_refs, outs):
                o_ref[...] = o.astype(o_ref.dtype)
            if colsum:
                cs_ref = refs[2 + n_e + n_r + n_o]
                cols = pl.ds(pl.multiple_of(pl.program_id(1) * tn, 128), tn)
                _acc(cs_ref.at[:, cols], pl.program_id(0) == 0, _colsum(outs[0]))

        if nk == 1:
            finish(p)
        else:
            acc_ref = refs[-1]
            k = pl.program_id(2)
            _acc(acc_ref, k == 0, p)

            @pl.when(k == nk - 1)
            def _():
                finish(acc_ref[...])

    outs = pl.pallas_call(
        body, name=name, grid=(M // tm, N // tn, nk),
        in_specs=[a_spec, b_spec] + [o_spec] * n_e + [r_spec] * n_r,
        out_specs=[o_spec] * n_o + ([pl.BlockSpec((1, N), lambda i, j, k: (0, 0))] if colsum else []),
        out_shape=[jax.ShapeDtypeStruct((M, N), dt) for dt in out_dtypes] + ([jax.ShapeDtypeStruct((1, N), F32)] if colsum else []),
        scratch_shapes=[pltpu.VMEM((tm, tn), F32)] if nk > 1 else [],
        compiler_params=_params(("arbitrary",) * 3 if colsum else ("parallel", "parallel", "arbitrary")),
    )(a, b, *extras, *rows)
    return outs[0] if len(outs) == 1 else outs


def _lnmod_fwd(x, sc, sh, name):
    S = x.shape[0]
    t = min(ROW_TILE, S)

    def body(x_ref, sc_ref, sh_ref, h_ref):
        xh, _ = _ln_stats(x_ref[...])
        h_ref[...] = (xh * (1.0 + sc_ref[...]) + sh_ref[...]).astype(h_ref.dtype)

    return pl.pallas_call(
        body, name=name, grid=(S // t,),
        in_specs=[_row_spec(t, D), _vec_spec(D), _vec_spec(D)], out_specs=_row_spec(t, D),
        out_shape=jax.ShapeDtypeStruct((S, D), MXU), compiler_params=_params(("parallel",)),
    )(x, sc, sh)


def _res_ln_fwd(xprev, br, gvec, lng, lnb, name, mod=None):
    S = xprev.shape[0]
    t = min(ROW_TILE, S)
    with_h = mod is not None

    def body(*refs):
        xp_ref, br_ref, g_ref, lg_ref, lb_ref = refs[:5]
        u = ALPHA * xp_ref[...] + (1.0 + g_ref[...]) * br_ref[...]
        uh, _ = _ln_stats(u)
        xn = uh * lg_ref[...] + lb_ref[...]
        if with_h:
            sc_ref, sh_ref, xn_ref, h_ref = refs[5:]
            xh, _ = _ln_stats(xn)
            h_ref[...] = (xh * (1.0 + sc_ref[...]) + sh_ref[...]).astype(h_ref.dtype)
        else:
            xn_ref = refs[5]
        xn_ref[...] = xn

    ins = [xprev, br, gvec, lng, lnb] + (list(mod) if with_h else [])
    return pl.pallas_call(
        body, name=name, grid=(S // t,),
        in_specs=[_row_spec(t, D), _row_spec(t, D)] + [_vec_spec(D)] * (len(ins) - 2),
        out_specs=[_row_spec(t, D)] * (2 if with_h else 1),
        out_shape=[jax.ShapeDtypeStruct((S, D), F32)] + ([jax.ShapeDtypeStruct((S, D), MXU)] if with_h else []),
        compiler_params=_params(("parallel",)),
    )(*ins)


def _loss_fwd(y, tgt):
    S = y.shape[0]
    t = min(ROW_TILE, S)

    def body(y_ref, t_ref, dy_ref, sq_ref):
        d = y_ref[...] - t_ref[...]
        dy_ref[...] = d * (1.0 / D)
        _acc(sq_ref, pl.program_id(0) == 0, _colsum(d * d))

    return pl.pallas_call(
        body, name="loss_head", grid=(S // t,),
        in_specs=[_row_spec(t, D), _row_spec(t, D)], out_specs=[_row_spec(t, D), _vec_spec(D)],
        out_shape=[jax.ShapeDtypeStruct((S, D), F32), jax.ShapeDtypeStruct((1, D), F32)],
        compiler_params=_params(("arbitrary",)),
    )(y, tgt)


def _node_bwd(name, dres, *, hpath=None, upath=None):
    S = dres.shape[0]
    t = min(ROW_TILE, S)
    has_h, has_u = hpath is not None, upath is not None

    def body(*refs):
        refs = list(refs)
        first = pl.program_id(0) == 0
        dxs = refs.pop(0)[...]
        if has_h:
            dh = refs.pop(0)[...]
            xs = refs.pop(0)[...]
            sc = refs.pop(0)[...]
        if has_u:
            xp = refs.pop(0)[...]
            br = refs.pop(0)[...]
            gv = refs.pop(0)[...]
            lg = refs.pop(0)[...]
        if has_h:
            dsc_ref, dsh_ref = refs.pop(0), refs.pop(0)
            xh, r = _ln_stats(xs)
            _acc(dsc_ref, first, _colsum(dh * xh))
            _acc(dsh_ref, first, _colsum(dh))
            dxs = dxs + _ln_bwd(dh * (1.0 + sc), xh, r)
        if has_u:
            du_ref, dbr_ref, dlg_ref, dlb_ref, dg_ref, dbs_ref = refs
            uh, ru = _ln_stats(ALPHA * xp + (1.0 + gv) * br)
            _acc(dlg_ref, first, _colsum(dxs * uh))
            _acc(dlb_ref, first, _colsum(dxs))
            du = _ln_bwd(dxs * lg, uh, ru)
            _acc(dg_ref, first, _colsum(du * br))
            du_ref[...] = ALPHA * du
            dbr = (1.0 + gv) * du
            _acc(dbs_ref, first, _colsum(dbr))
            dbr_ref[...] = dbr.astype(dbr_ref.dtype)
        else:
            refs[0][...] = dxs

    ins, in_specs = [dres], [_row_spec(t, D)]
    outs, out_specs, names = [], [], []
    vec = jax.ShapeDtypeStruct((1, D), F32)
    if has_h:
        ins += list(hpath)
        in_specs += [_row_spec(t, D), _row_spec(t, D), _vec_spec(D)]
        outs += [vec, vec]
        out_specs += [_vec_spec(D), _vec_spec(D)]
        names += ["dsc", "dsh"]
    if has_u:
        ins += list(upath)
        in_specs += [_row_spec(t, D), _row_spec(t, D), _vec_spec(D), _vec_spec(D)]
        outs += [jax.ShapeDtypeStruct((S, D), F32), jax.ShapeDtypeStruct((S, D), MXU), vec, vec, vec, vec]
        out_specs += [_row_spec(t, D), _row_spec(t, D), _vec_spec(D), _vec_spec(D), _vec_spec(D), _vec_spec(D)]
        names += ["du", "dbr", "dlng", "dlnb", "dg", "dbrsum"]
    else:
        outs += [jax.ShapeDtypeStruct((S, D), F32)]
        out_specs += [_row_spec(t, D)]
        names += ["dx"]
    res = pl.pallas_call(
        body, name=name, grid=(S // t,), in_specs=in_specs, out_specs=out_specs, out_shape=outs,
        compiler_params=_params(("arbitrary",)),
    )(*ins)
    return dict(zip(names, res))


def _colsum_call(a, name):
    S, N = a.shape
    t, tn = min(ROW_TILE, S), _tile(N, 1152)

    def body(a_ref, o_ref):
        _acc(o_ref, pl.program_id(1) == 0, _colsum(a_ref[...].astype(F32)))

    return pl.pallas_call(
        body, name=name, grid=(N // tn, S // t),
        in_specs=[pl.BlockSpec((t, tn), lambda j, i: (i, j))], out_specs=pl.BlockSpec((1, tn), lambda j, i: (0, j)),
        out_shape=jax.ShapeDtypeStruct((1, N), F32), compiler_params=_params(("parallel", "arbitrary")),
    )(a)


def _gm_mask():
    i = lax.broadcasted_iota(jnp.int32, (128, 128), 0) // CHUNK
    j = lax.broadcasted_iota(jnp.int32, (128, 128), 1) // CHUNK
    return i >= j


def _gmlp_common(z, lng, lnb):
    gz = _gelu(z)
    u, v = gz[:, :GM_W], gz[:, GM_W:]
    vh, r = _ln_stats(v)
    return u, vh, r, vh * lng + lnb


def _gmlp_fwd(z, lng, lnb, ws, bst, name):
    S = z.shape[0]
    t = min(ROW_TILE, S)

    def body(z_ref, lg_ref, lb_ref, ws_ref, bs_ref, y_ref):
        u, _, _, vn = _gmlp_common(z_ref[...], lg_ref[...], lb_ref[...])
        mask = _gm_mask()
        vb = vn.astype(MXU)
        for g in range(GM_G):
            w = jnp.where(mask, ws_ref[g], jnp.zeros_like(ws_ref[g]))
            bias = bs_ref[:, g:g + 1]
            for blk in range(t // 128):
                rs, cs = slice(blk * 128, (blk + 1) * 128), slice(g * 128, (g + 1) * 128)
                f = _dot(w, vb[rs, cs]) + bias
                y_ref[rs, cs] = (u[rs, cs] * f).astype(y_ref.dtype)

    return pl.pallas_call(
        body, name=name, grid=(S // t,),
        in_specs=[_row_spec(t, 2 * GM_W, P_GM // (2 * GM_W)), _vec_spec(GM_W), _vec_spec(GM_W),
                  _full_spec((GM_G, 128, 128)), _full_spec((128, GM_G))],
        out_specs=_row_spec(t, GM_W), out_shape=jax.ShapeDtypeStruct((S, GM_W), MXU),
        compiler_params=_params(("parallel",)),
    )(z, lng, lnb, ws, bst)


def _gmlp_bwd(z, dya, lng, lnb, ws, wst, bst, dz_in, name):
    S = z.shape[0]
    t = min(ROW_TILE, S)

    def body(z_ref, dy_ref, lg_ref, lb_ref, ws_ref, wst_ref, bs_ref, _dz_in, dz_ref, dlg_ref, dlb_ref, dws_ref, dbs_ref, sum_ref):
        first = pl.program_id(0) == 0
        zz = z_ref[...]
        u, vh, r, vn = _gmlp_common(zz, lg_ref[...], lb_ref[...])
        dy = dy_ref[...]
        mask = _gm_mask()
        maskt = lax.broadcasted_iota(jnp.int32, (128, 128), 1) // CHUNK >= lax.broadcasted_iota(jnp.int32, (128, 128), 0) // CHUNK
        lane = lax.broadcasted_iota(jnp.int32, (128, 128), 1)
        vb = vn.astype(MXU)
        dfb = (dy * u).astype(MXU)
        df32 = dy * u
        dbs = jnp.zeros((128, 128), F32)
        du_cols, dvn_cols = [], []
        for g in range(GM_G):
            w = jnp.where(mask, ws_ref[g], jnp.zeros_like(ws_ref[g]))
            wt = jnp.where(maskt, wst_ref[g], jnp.zeros_like(wst_ref[g]))
            bias = bs_ref[:, g:g + 1]
            cs = slice(g * 128, (g + 1) * 128)
            dw = jnp.zeros((128, 128), F32)
            du_rows, dvn_rows = [], []
            for blk in range(t // 128):
                rs = slice(blk * 128, (blk + 1) * 128)
                f = _dot(w, vb[rs, cs]) + bias
                du_rows.append(dy[rs, cs] * f)
                dvn_rows.append(_dot(wt, dfb[rs, cs]))
                dw = dw + _dot_nt(dfb[rs, cs], vb[rs, cs])
                dbs = dbs + jnp.where(lane == g, jnp.sum(df32[rs, cs], axis=1, keepdims=True), 0.0)
            _acc(dws_ref.at[g], first, jnp.where(mask, dw, 0.0))
            du_cols.append(jnp.concatenate(du_rows, axis=0))
            dvn_cols.append(jnp.concatenate(dvn_rows, axis=0))
        _acc(dbs_ref, first, dbs)
        du = jnp.concatenate(du_cols, axis=1)
        dvn = jnp.concatenate(dvn_cols, axis=1)
        _acc(dlg_ref, first, _colsum(dvn * vh))
        _acc(dlb_ref, first, _colsum(dvn))
        dv = _ln_bwd(dvn * lg_ref[...], vh, r)
        dzz = jnp.concatenate([du, dv], axis=1) * _gelu_grad(zz)
        dz_ref[...] = dzz.astype(dz_ref.dtype)
        _acc(sum_ref, first, _colsum(dzz))

    vec = jax.ShapeDtypeStruct((1, GM_W), F32)
    return pl.pallas_call(
        body, name=name, grid=(S // t,),
        in_specs=[_row_spec(t, 2 * GM_W, P_GM // (2 * GM_W)), _row_spec(t, GM_W), _vec_spec(GM_W), _vec_spec(GM_W),
                  _full_spec((GM_G, 128, 128)), _full_spec((GM_G, 128, 128)), _full_spec((128, GM_G)), _ANY],
        out_specs=[_row_spec(t, 2 * GM_W, P_GM // (2 * GM_W)), _vec_spec(GM_W), _vec_spec(GM_W), _full_spec((GM_G, 128, 128)),
                   _full_spec((128, 128)), _vec_spec(2 * GM_W)],
        out_shape=[jax.ShapeDtypeStruct((S, NP), MXU), vec, vec,
                   jax.ShapeDtypeStruct((GM_G, 128, 128), F32), jax.ShapeDtypeStruct((128, 128), F32),
                   jax.ShapeDtypeStruct((1, 2 * GM_W), F32)],
        input_output_aliases=_dz_alias(8), compiler_params=_params(("arbitrary",)),
    )(z, dya, lng, lnb, ws, wst, bst, dz_in)


def _rms(x, g):
    r = lax.rsqrt(jnp.mean(x * x, axis=-1, keepdims=True) + RMS_EPS)
    xh = x * r
    return xh, r, xh * g


def _mla_specs(t):
    return [_row_spec(t, Q_RANK, P_Q // Q_RANK), _row_spec(t, KV_RANK, P_KV // 128), _row_spec(t, 128, P_KA // 128),
            _row_spec(t, 128, P_KB // 128), _row_spec(t, 128), _row_spec(t, 128), _vec_spec(Q_RANK), _vec_spec(KV_RANK),
            _full_spec((Q_RANK, D)), _full_spec((Q_RANK, D)), _full_spec((KV_RANK, D)), _full_spec((HEADS, KV_RANK, VD))]


def _mla_prep_fwd(z, cos, sin, gq, gkv, w1, w2, wk, wv, wvt, name):
    S = z.shape[0]
    t = min(ROW_TILE, S)

    def body(zq_ref, zkv_ref, zka_ref, zkb_ref, cos_ref, sin_ref, gq_ref, gkv_ref, w1_ref, w2_ref, wk_ref, wv_ref, wvt_ref,
             q_ref, k_ref, v_ref, vt_ref):
        cos_, sin_ = cos_ref[...], sin_ref[...]
        cq = cos_ + jnp.where(lax.broadcasted_iota(jnp.int32, cos_.shape, 1) < NOPE, 1.0, 0.0)
        qn = _rms(zq_ref[...], gq_ref[...])[2].astype(MXU)
        q1, q2 = _dot(qn, w1_ref[...]), _dot(qn, w2_ref[...])
        kvn = _rms(zkv_ref[...], gkv_ref[...])[2].astype(MXU)
        kn = _dot(kvn, wk_ref[...])
        krot = zka_ref[...] * cos_ + zkb_ref[...] * sin_
        for h in range(HEADS):
            hs = slice(h * 128, (h + 1) * 128)
            q_ref[h] = (q1[:, hs] * cq + q2[:, hs] * sin_).astype(q_ref.dtype)
            k_ref[h] = (kn[:, hs] + krot).astype(k_ref.dtype)
            v_ref[h] = _dot(kvn, wv_ref[h]).astype(v_ref.dtype)
            vt_ref[h] = _dot_nt(wvt_ref[h], kvn).astype(vt_ref.dtype)

    hspec = lambda w: pl.BlockSpec((HEADS, t, w), lambda i: (0, i, 0))
    return pl.pallas_call(
        body, name=name, grid=(S // t,), in_specs=_mla_specs(t) + [_full_spec((HEADS, VD, KV_RANK))],
        out_specs=[hspec(128), hspec(128), hspec(VD), pl.BlockSpec((HEADS, VD, t), lambda i: (0, 0, i))],
        out_shape=[jax.ShapeDtypeStruct((HEADS, S, 128), MXU), jax.ShapeDtypeStruct((HEADS, S, 128), MXU),
                   jax.ShapeDtypeStruct((HEADS, S, VD), MXU), jax.ShapeDtypeStruct((HEADS, VD, S), MXU)],
        compiler_params=_params(("parallel",)),
    )(z, z, z, z, cos, sin, gq, gkv, w1, w2, wk, wv, wvt)


def _mla_prep_bwd(z, cos, sin, gq, gkv, w1, w2, wk, wv, dq, dk, dv, dz_in, name):
    S = z.shape[0]
    t = min(ROW_TILE, S)

    def body(zq_ref, zkv_ref, zka_ref, zkb_ref, cos_ref, sin_ref, gq_ref, gkv_ref, w1_ref, w2_ref, wk_ref, wv_ref,
             dq_ref, dk_ref, dv_ref, _dz_in, dz_ref, dgq_ref, dgkv_ref, dw1_ref, dw2_ref, dwk_ref, dwv_ref, sum_ref):
        first = pl.program_id(0) == 0
        cos_, sin_ = cos_ref[...], sin_ref[...]
        cq = cos_ + jnp.where(lax.broadcasted_iota(jnp.int32, cos_.shape, 1) < NOPE, 1.0, 0.0)
        gq_, gkv_ = gq_ref[...], gkv_ref[...]
        xhq, rq, qn32 = _rms(zq_ref[...], gq_)
        qn = qn32.astype(MXU)
        dq1 = jnp.concatenate([dq_ref[h] * cq for h in range(HEADS)], axis=1).astype(MXU)
        dq2 = jnp.concatenate([dq_ref[h] * sin_ for h in range(HEADS)], axis=1).astype(MXU)
        dqn = _dot_nt(dq1, w1_ref[...]) + _dot_nt(dq2, w2_ref[...])
        _acc(dw1_ref, first, _dot_tn(qn, dq1))
        _acc(dw2_ref, first, _dot_tn(qn, dq2))
        _acc(dgq_ref, first, _colsum(dqn * xhq))
        dxn = dqn * gq_
        dzq = rq * (dxn - xhq * jnp.mean(dxn * xhq, axis=-1, keepdims=True))

        xhk, rk, kvn32 = _rms(zkv_ref[...], gkv_)
        kvn = kvn32.astype(MXU)
        dks = [dk_ref[h] for h in range(HEADS)]
        dkall = jnp.concatenate(dks, axis=1).astype(MXU)
        dkrot = functools.reduce(lambda p, q_: p + q_, dks)
        dkvn = _dot_nt(dkall, wk_ref[...])
        _acc(dwk_ref, first, _dot_tn(kvn, dkall))
        for h in range(HEADS):
            dvb = dv_ref[h].astype(MXU)
            dkvn = dkvn + _dot_nt(dvb, wv_ref[h])
            _acc(dwv_ref.at[h], first, _dot_tn(kvn, dvb))
        _acc(dgkv_ref, first, _colsum(dkvn * xhk))
        dxk = dkvn * gkv_
        dzkv = rk * (dxk - xhk * jnp.mean(dxk * xhk, axis=-1, keepdims=True))
        dzm = jnp.concatenate([dzq, dzkv, dkrot * cos_, dkrot * sin_], axis=1)
        dz_ref[...] = dzm.astype(dz_ref.dtype)
        _acc(sum_ref, first, _colsum(dzm))

    hspec = lambda w: pl.BlockSpec((HEADS, t, w), lambda i: (0, i, 0))
    sds = jax.ShapeDtypeStruct
    return pl.pallas_call(
        body, name=name, grid=(S // t,),
        in_specs=_mla_specs(t) + [hspec(128), hspec(128), hspec(VD), _ANY],
        out_specs=[_row_spec(t, 640, P_Q // 640), _vec_spec(Q_RANK), _vec_spec(KV_RANK), _full_spec((Q_RANK, D)),
                   _full_spec((Q_RANK, D)), _full_spec((KV_RANK, D)), _full_spec((HEADS, KV_RANK, VD)), _vec_spec(640)],
        out_shape=[sds((S, NP), MXU), sds((1, Q_RANK), F32), sds((1, KV_RANK), F32), sds((Q_RANK, D), F32),
                   sds((Q_RANK, D), F32), sds((KV_RANK, D), F32), sds((HEADS, KV_RANK, VD), F32), sds((1, 640), F32)],
        input_output_aliases=_dz_alias(16), compiler_params=_params(("arbitrary",)),
    )(z, z, z, z, cos, sin, gq, gkv, w1, w2, wk, wv, dq, dk, dv, dz_in)


def _chunk_mask(t, transposed):
    r = lax.broadcasted_iota(jnp.int32, (t, t), 0) // CHUNK
    c = lax.broadcasted_iota(jnp.int32, (t, t), 1) // CHUNK
    return (r <= c) if transposed else (c <= r)


def _attn_fwd(q, k, vt, name, gather=None):
    S = q.shape[1]
    t = min(ATT_TILE, S)
    n = S // t

    hb = 4
    nh = HEADS // hb

    def body(*refs):
        if gather is None:
            q_ref, k_ref, vt_ref, ob_ref, o_ref, lse_ref = refs
        else:
            q_ref, k_ref, vt_ref, w_ref, ob_ref, o_ref, lse_ref, g_ref, send_sems, recv_sems = refs
            hi, step = pl.program_id(0), pl.program_id(1)
            for stage, at in (("start", (0, 0)), ("forward", (nh // 2, 0))):
                pl.when((hi == at[0]) & (step == at[1]))(functools.partial(_ag_stage, stage, w_ref, g_ref, send_sems, recv_sems))
        qi = pl.program_id(1)
        qbs = [q_ref[g] for g in range(hb)]

        def block(j, carries, masked):
            cols = pl.ds(pl.multiple_of(j * t, t), t)
            out = []
            for g in range(hb):
                m, l, acc = carries[g]
                st = _dot_nt(k_ref[g, cols, :], qbs[g]) * ATT_SCALE2
                if masked:
                    st = jnp.where(_chunk_mask(t, True), st, -jnp.inf)
                m_new = jnp.maximum(m, jnp.max(st, axis=0, keepdims=True))
                p = jnp.exp2(st - m_new)
                alpha = jnp.exp2(m - m_new)
                l = alpha * l + jnp.sum(p, axis=0, keepdims=True)
                acc = alpha * acc + _dot(vt_ref[g, :, cols], p.astype(MXU))
                out.append((m_new, l, acc))
            return tuple(out)

        init = tuple((jnp.full((1, t), -jnp.inf, F32), jnp.zeros((1, t), F32), jnp.zeros((VD, t), F32)) for _ in range(hb))
        carries = lax.fori_loop(0, qi, lambda j, c: block(j, c, False), init)
        for g, (m, l, acc) in enumerate(block(qi, carries, True)):
            o = (acc / l).T
            o_ref[g] = o
            ob_ref[g] = o.astype(ob_ref.dtype)
            lse_ref[g] = m + jnp.log2(l)
        if gather is not None:
            pl.when((hi == nh - 1) & (step == n - 1))(functools.partial(_ag_stage, "finish", w_ref, g_ref, send_sems, recv_sems))

    qspec = lambda w: pl.BlockSpec((hb, t, w), lambda h, i: (h, i, 0))
    sds = jax.ShapeDtypeStruct
    comm = gather is not None
    return pl.pallas_call(
        body, name=name, grid=(nh, n),
        in_specs=[qspec(128), pl.BlockSpec((hb, S, 128), lambda h, i: (h, 0, 0)), pl.BlockSpec((hb, VD, S), lambda h, i: (h, 0, 0))]
        + ([_ANY] if comm else []),
        out_specs=[qspec(VD), qspec(VD), pl.BlockSpec((hb, 1, t), lambda h, i: (h, 0, i))] + ([_ANY] if comm else []),
        out_shape=[sds((HEADS, S, VD), MXU), sds((HEADS, S, VD), F32), sds((HEADS, 1, S), F32)]
        + ([sds((4,) + gather.shape, gather.dtype)] if comm else []),
        scratch_shapes=list(_AG_SEMS) if comm else [],
        compiler_params=_params(("arbitrary", "arbitrary") if comm else ("parallel", "arbitrary")),
    )(*((q, k, vt) + ((gather,) if comm else ())))


def _attn_delta(o, do, name):
    S = o.shape[1]
    t = min(ROW_TILE, S)

    def body(o_ref, do_ref, d_ref):
        for h in range(HEADS):
            d_ref[h] = jnp.sum(o_ref[h] * do_ref[h], axis=1, keepdims=True)

    hspec = lambda w: pl.BlockSpec((HEADS, t, w), lambda i: (0, i, 0))
    return pl.pallas_call(
        body, name=name, grid=(S // t,), in_specs=[hspec(VD), hspec(VD)], out_specs=hspec(1),
        out_shape=jax.ShapeDtypeStruct((HEADS, S, 1), F32), compiler_params=_params(("parallel",)),
    )(o, do)


def _scatter_stage(stage, g_ref, land_ref, send_sems, recv_sems):
    rh = g_ref.shape[1] // 2
    x, y, c = _coords()
    cps = []
    for p in range(1, 8):
        tx, ty, tc = x ^ ((p >> 2) & 1), y ^ ((p >> 1) & 1), c ^ (p & 1)
        rows = pl.ds(pl.multiple_of(tc * rh, FLAT_ALIGN // 2), rh)
        cps.append(pltpu.make_async_remote_copy(src_ref=g_ref.at[2 * tx + ty, rows, :], dst_ref=land_ref.at[p - 1],
                                                send_sem=send_sems.at[p - 1], recv_sem=recv_sems.at[p - 1], device_id=(tx, ty, tc),
                                                device_id_type=MESH))
    for cp in cps:
        if stage == "start":
            cp.start()
        else:
            cp.wait()


def _attn_bwd(q, k, v, do, lse_row, d_row, name, scatter=None):
    S = q.shape[1]
    t = min(ATT_TILE, S)
    n = S // t

    def body(*refs):
        if scatter is None:
            q_ref, do_ref, lse_ref, d_ref, k_ref, v_ref, dq_ref, dk_ref, dv_ref, dk_s, dv_s = refs
        else:
            q_ref, do_ref, lse_ref, d_ref, k_ref, v_ref, g_ref, dq_ref, dk_ref, dv_ref, land_ref, dk_s, dv_s, send_sems, recv_sems = refs
            pl.when((pl.program_id(0) == 0) & (pl.program_id(1) == 0))(
                functools.partial(_scatter_stage, "start", g_ref, land_ref, send_sems, recv_sems))
        ki = pl.program_id(1)

        @pl.when(ki == 0)
        def _():
            dq_ref[...] = jnp.zeros(dq_ref.shape, F32)

        dk_s[...] = jnp.zeros(dk_s.shape, F32)
        dv_s[...] = jnp.zeros(dv_s.shape, F32)
        kb, vb = k_ref[0], v_ref[0]

        def step(qi, masked):
            rows = pl.ds(pl.multiple_of(qi * t, t), t)
            qb, dob = q_ref[0, rows, :], do_ref[0, rows, :]
            st = _dot_nt(kb, qb) * ATT_SCALE2
            if masked:
                st = jnp.where(_chunk_mask(t, True), st, -jnp.inf)
            pt = jnp.exp2(st - lse_ref[0, :, rows])
            dv_s[...] += _dot(pt.astype(MXU), dob)
            dpt = _dot_nt(vb, dob)
            dst = (pt * (dpt - d_ref[0, :, rows]) * ATT_SCALE).astype(MXU)
            dk_s[...] += _dot(dst, qb)
            dq_ref[0, rows, :] += _dot_tn(dst, kb)

        step(ki, True)

        def loop(qi, c):
            step(qi, False)
            return c

        lax.fori_loop(ki + 1, n, loop, 0)
        dk_ref[0] = dk_s[...]
        dv_ref[0] = dv_s[...]
        if scatter is not None:
            pl.when((pl.program_id(0) == HEADS - 1) & (ki == n - 1))(
                functools.partial(_scatter_stage, "finish", g_ref, land_ref, send_sems, recv_sems))

    head = lambda *shape: pl.BlockSpec((1,) + shape, lambda h, j: (h, 0, 0))
    kmap = lambda h, j: (h, j, 0)
    sds = jax.ShapeDtypeStruct
    comm = scatter is not None
    return pl.pallas_call(
        body, name=name, grid=(HEADS, n),
        in_specs=[head(S, 128), head(S, VD), head(1, S), head(1, S), pl.BlockSpec((1, t, 128), kmap), pl.BlockSpec((1, t, VD), kmap)]
        + ([_ANY] if comm else []),
        out_specs=[head(S, 128), pl.BlockSpec((1, t, 128), kmap), pl.BlockSpec((1, t, VD), kmap)] + ([_ANY] if comm else []),
        out_shape=[sds((HEADS, S, 128), F32), sds((HEADS, S, 128), F32), sds((HEADS, S, VD), F32)]
        + ([sds((7, scatter.shape[1] // 2, scatter.shape[2]), scatter.dtype)] if comm else []),
        scratch_shapes=[pltpu.VMEM((t, 128), F32), pltpu.VMEM((t, VD), F32)]
        + ([pltpu.SemaphoreType.DMA((7,)), pltpu.SemaphoreType.DMA((7,))] if comm else []),
        compiler_params=_params(("arbitrary", "arbitrary") if comm else ("parallel", "arbitrary")),
    )(*((q, do, lse_row, d_row, k, v) + ((scatter,) if comm else ())))


def _shift_down(x, prev8, d):
    xr = pltpu.roll(x, d, 0)
    r8 = lax.broadcasted_iota(jnp.int32, prev8.shape, 0)
    top = jnp.where(r8 < d, pltpu.roll(prev8, d, 0), xr[0:8])
    return jnp.concatenate([top, xr[8:]], axis=0)


def _shift_up(x, next8, d):
    n = x.shape[0]
    xr = pltpu.roll(x, n - d, 0)
    r8 = lax.broadcasted_iota(jnp.int32, next8.shape, 0)
    bot = jnp.where(r8 >= 8 - d, pltpu.roll(next8, 8 - d, 0), xr[n - 8:])
    return jnp.concatenate([xr[:n - 8], bot], axis=0)


def _log1p(u):
    return jnp.where(u < 0.01, u * (1.0 - u * (0.5 - u * (1.0 / 3.0 - 0.25 * u))), jnp.log(1.0 + u))


def _neg_expm1(y):
    series = -y * (1.0 + 0.5 * y * (1.0 + (1.0 / 3.0) * y * (1.0 + 0.25 * y)))
    return jnp.where(y > -0.05, series, 1.0 - jnp.exp(y))


def _softplus_neg(lam):
    return jnp.maximum(-lam, 0.0) + _log1p(jnp.exp(-jnp.abs(lam)))


def _lru_gates(x, prev8, cw, cb, wr, br, wi, bi, lam):
    xs1, xs2, xs3 = _shift_down(x, prev8, 1), _shift_down(x, prev8, 2), _shift_down(x, prev8, 3)
    xc = cb + cw[0:1] * xs3 + cw[1:2] * xs2 + cw[2:3] * xs1 + cw[3:4] * x
    xcb = xc.astype(MXU)
    r = _sigmoid(_dot(xcb, wr) + br)
    ig = _sigmoid(_dot(xcb, wi) + bi)
    sp = _softplus_neg(lam)
    log_a = -8.0 * r * sp
    a = jnp.exp(log_a)
    gb = jnp.sqrt(_neg_expm1(2.0 * log_a))
    return (xs1, xs2, xs3), xc, xcb, r, ig, sp, a, gb


def _lru_fwd(z, cw, cb, wr, br, wi, bi, lam, name):
    S = z.shape[0]
    t = min(ROW_TILE, S)

    def body(zx_ref, zg_ref, cw_ref, cb_ref, wr_ref, br_ref, wi_ref, bi_ref, lam_ref, y_ref, h_ref, xp_s, hc_s):
        @pl.when(pl.program_id(0) == 0)
        def _():
            xp_s[...] = jnp.zeros(xp_s.shape, F32)
            hc_s[...] = jnp.zeros(hc_s.shape, F32)

        x = zx_ref[...]
        _, xc, _, _, ig, _, a, gb = _lru_gates(x, xp_s[...], cw_ref[...], cb_ref[...], wr_ref[...], br_ref[...],
                                               wi_ref[...], bi_ref[...], lam_ref[...])
        b = gb * (ig * xc)
        rows = lax.broadcasted_iota(jnp.int32, a.shape, 0)
        d = 1
        while d < t:
            ar, brr = pltpu.roll(a, d, 0), pltpu.roll(b, d, 0)
            ok = rows >= d
            b = jnp.where(ok, a * brr, 0.0) + b
            a = jnp.where(ok, a * ar, a)
            d *= 2
        h = a * hc_s[7:8, :] + b
        h_ref[...] = h
        y_ref[...] = (h * _gelu(zg_ref[...])).astype(y_ref.dtype)
        hc_s[...] = h[t - 8:, :]
        xp_s[...] = x[t - 8:, :]

    w = LRU_W
    return pl.pallas_call(
        body, name=name, grid=(S // t,),
        in_specs=[_row_spec(t, w, P_LX // w), _row_spec(t, w, P_LG // w), _full_spec((4, w)), _vec_spec(w), _full_spec((w, w)),
                  _vec_spec(w), _full_spec((w, w)), _vec_spec(w), _vec_spec(w)],
        out_specs=[_row_spec(t, w), _row_spec(t, w)],
        out_shape=[jax.ShapeDtypeStruct((S, w), MXU), jax.ShapeDtypeStruct((S, w), F32)],
        scratch_shapes=[pltpu.VMEM((8, w), F32), pltpu.VMEM((8, w), F32)],
        compiler_params=_params(("arbitrary",)),
    )(z, z, cw, cb, wr, br, wi, bi, lam)


def _lru_bwd(z, h, dy, cw, cb, wr, br, wi, bi, lam, dz_in, name):
    S = z.shape[0]
    t = min(ROW_TILE, S)
    n = S // t
    w = LRU_W

    def body(zx_ref, zxp_ref, zg_ref, h_ref, hp_ref, dy_ref, cw_ref, cb_ref, wr_ref, br_ref, wi_ref, bi_ref, lam_ref, _dz_in,
             dz_ref, dcw_ref, dcb_ref, dwr_ref, dbr_ref, dwi_ref, dbi_ref, dlam_ref, sum_ref, gc_s, dn_s):
        i = pl.program_id(0)
        first = i == 0
        j = n - 1 - i

        @pl.when(first)
        def _():
            gc_s[...] = jnp.zeros(gc_s.shape, F32)
            dn_s[...] = jnp.zeros(dn_s.shape, F32)

        live = (j > 0).astype(F32)
        xprev8, hprev8 = zxp_ref[...] * live, hp_ref[...] * live
        x, zg, hh, dy_ = zx_ref[...], zg_ref[...], h_ref[...], dy_ref[...]
        cw_, lam_ = cw_ref[...], lam_ref[...]
        (xs1, xs2, xs3), xc, xcb, r, ig, sp, a, gb = _lru_gates(x, xprev8, cw_, cb_ref[...], wr_ref[...], br_ref[...],
                                                                wi_ref[...], bi_ref[...], lam_)
        hm1 = _shift_down(hh, hprev8, 1)
        dh = dy_ * _gelu(zg)
        dzg = dy_ * hh * _gelu_grad(zg)
        rows = lax.broadcasted_iota(jnp.int32, a.shape, 0)
        last = rows == t - 1
        ca = jnp.where(last, 0.0, pltpu.roll(a, t - 1, 0))
        g = dh + jnp.where(last, gc_s[0:1, :], 0.0)
        d = 1
        while d < t:
            ok = rows < t - d
            g = g + jnp.where(ok, ca * pltpu.roll(g, t - d, 0), 0.0)
            ca = jnp.where(ok, ca * pltpu.roll(ca, t - d, 0), 0.0)
            d *= 2
        gc_s[...] = a[0:8, :] * g[0:8, :]
        da = g * hm1
        dgb = g * (ig * xc)
        dub = g * gb
        di = dub * xc
        dxc = dub * ig
        dlog_a = da * a - dgb * (a * a) / gb
        dr = dlog_a * (-8.0 * sp)
        _acc(dlam_ref, first, _colsum(dlog_a * (-8.0 * r)) * (-_sigmoid(-lam_)))
        dpr = dr * r * (1.0 - r)
        dpi = di * ig * (1.0 - ig)
        _acc(dbr_ref, first, _colsum(dpr))
        _acc(dbi_ref, first, _colsum(dpi))
        dprb, dpib = dpr.astype(MXU), dpi.astype(MXU)
        _acc(dwr_ref, first, _dot_tn(xcb, dprb))
        _acc(dwi_ref, first, _dot_tn(xcb, dpib))
        dxc = dxc + _dot_nt(dprb, wr_ref[...]) + _dot_nt(dpib, wi_ref[...])
        _acc(dcb_ref, first, _colsum(dxc))
        _acc(dcw_ref, first, jnp.concatenate([_colsum(dxc * xs3), _colsum(dxc * xs2), _colsum(dxc * xs1), _colsum(dxc * x)], axis=0))
        nxt = dn_s[...]
        dx = cw_[3:4] * dxc + cw_[2:3] * _shift_up(dxc, nxt, 1) + cw_[1:2] * _shift_up(dxc, nxt, 2) + cw_[0:1] * _shift_up(dxc, nxt, 3)
        dn_s[...] = dxc[0:8, :]
        dz_ref[...] = jnp.concatenate([dx, dzg], axis=1).astype(dz_ref.dtype)
        _acc(sum_ref, first, jnp.concatenate([_colsum(dx), _colsum(dzg)], axis=1))

    rev = lambda col: pl.BlockSpec((t, w), lambda i, c=col: (n - 1 - i, c))
    prev8 = lambda col: pl.BlockSpec((8, w), lambda i, c=col: (jnp.maximum((n - 1 - i) * (t // 8) - 1, 0), c))
    vec = jax.ShapeDtypeStruct((1, w), F32)
    sds = jax.ShapeDtypeStruct
    return pl.pallas_call(
        body, name=name, grid=(n,),
        in_specs=[rev(P_LX // w), prev8(P_LX // w), rev(P_LG // w), rev(0), prev8(0), rev(0), _full_spec((4, w)), _vec_spec(w),
                  _full_spec((w, w)), _vec_spec(w), _full_spec((w, w)), _vec_spec(w), _vec_spec(w), _ANY],
        out_specs=[pl.BlockSpec((t, 2 * w), lambda i: (n - 1 - i, P_LX // (2 * w))), _full_spec((4, w)), _vec_spec(w),
                   _full_spec((w, w)), _vec_spec(w), _full_spec((w, w)), _vec_spec(w), _vec_spec(w), _vec_spec(2 * w)],
        out_shape=[sds((S, NP), MXU), sds((4, w), F32), vec, sds((w, w), F32), vec, sds((w, w), F32), vec, vec, sds((1, 2 * w), F32)],
        scratch_shapes=[pltpu.VMEM((8, w), F32), pltpu.VMEM((8, w), F32)],
        input_output_aliases=_dz_alias(14), compiler_params=_params(("arbitrary",)),
    )(z, z, z, h, h, dy, cw, cb, wr, br, wi, bi, lam, dz_in)


def _branch_fwd(z, ya, o, yc, wa, wb, wc, name):
    S = z.shape[0]
    t = min(ROW_TILE, S)

    def body(ga_ref, gb_ref, gc_ref, ya_ref, o_ref, yc_ref, wa_ref, wb_ref, wc_ref, m_ref, pa_ref, pb_ref, pc_ref):
        pa = _dot(ya_ref[...], wa_ref[...])
        pc = _dot(yc_ref[...], wc_ref[...])
        pb = _dot(o_ref[0], wb_ref[0])
        for h in range(1, HEADS):
            pb = pb + _dot(o_ref[h], wb_ref[h])
        pa_ref[...], pb_ref[...], pc_ref[...] = pa, pb, pc
        m = _sigmoid(ga_ref[...]) * pa + _sigmoid(gb_ref[...]) * pb + _sigmoid(gc_ref[...]) * pc
        m_ref[...] = m.astype(m_ref.dtype)

    g0 = P_GATE // D
    sds = jax.ShapeDtypeStruct
    return pl.pallas_call(
        body, name=name, grid=(S // t,),
        in_specs=[_row_spec(t, D, g0), _row_spec(t, D, g0 + 1), _row_spec(t, D, g0 + 2), _row_spec(t, GM_W),
                  pl.BlockSpec((HEADS, t, VD), lambda i: (0, i, 0)), _row_spec(t, LRU_W),
                  _full_spec((GM_W, D)), _full_spec((HEADS, VD, D)), _full_spec((LRU_W, D))],
        out_specs=[_row_spec(t, D)] * 4,
        out_shape=[sds((S, D), MXU), sds((S, D), F32), sds((S, D), F32), sds((S, D), F32)],
        compiler_params=_params(("parallel",)),
    )(z, z, z, ya, o, yc, wa, wb, wc)


def _dz_alias(n_inputs):
    return {n_inputs - 1: 0}


def _branch_bwd(z, dm, pa, pb, pc, name):
    S = z.shape[0]
    t = min(ROW_TILE, S)

    def body(ga_ref, gb_ref, gc_ref, dm_ref, pa_ref, pb_ref, pc_ref, dz_ref, da_ref, db_ref, dc_ref, sum_ref):
        first = pl.program_id(0) == 0
        dm_ = dm_ref[...]
        for n_, (g_ref, p_ref, d_ref) in enumerate(((ga_ref, pa_ref, da_ref), (gb_ref, pb_ref, db_ref), (gc_ref, pc_ref, dc_ref))):
            gt = _sigmoid(g_ref[...])
            d_ref[...] = (dm_ * gt).astype(d_ref.dtype)
            dzg = dm_ * p_ref[...] * gt * (1.0 - gt)
            dz_ref[:, n_ * D:(n_ + 1) * D] = dzg.astype(dz_ref.dtype)
            _acc(sum_ref.at[:, n_ * D:(n_ + 1) * D], first, _colsum(dzg))

    g0 = P_GATE // D
    sds = jax.ShapeDtypeStruct
    return pl.pallas_call(
        body, name=name, grid=(S // t,),
        in_specs=[_row_spec(t, D, g0), _row_spec(t, D, g0 + 1), _row_spec(t, D, g0 + 2)] + [_row_spec(t, D)] * 4,
        out_specs=[_row_spec(t, 3 * D, P_GATE // (3 * D))] + [_row_spec(t, D)] * 3 + [_vec_spec(3 * D)],
        out_shape=[sds((S, NP), MXU)] + [sds((S, D), MXU)] * 3 + [sds((1, 3 * D), F32)],
        compiler_params=_params(("arbitrary",)),
    )(z, z, z, dm, pa, pb, pc)


def _heads_bwd(dpb, o, wb, name):
    S = dpb.shape[0]
    t = min(ROW_TILE, S)

    def body(dp_ref, o_ref, wb_ref, dob_ref, do_ref, dwb_ref):
        first = pl.program_id(0) == 0
        dp = dp_ref[...]
        for h in range(HEADS):
            do = _dot_nt(dp, wb_ref[h])
            do_ref[h] = do
            dob_ref[h] = do.astype(dob_ref.dtype)
            _acc(dwb_ref.at[h], first, _dot_tn(o_ref[h], dp))

    hspec = pl.BlockSpec((HEADS, t, VD), lambda i: (0, i, 0))
    sds = jax.ShapeDtypeStruct
    return pl.pallas_call(
        body, name=name, grid=(S // t,),
        in_specs=[_row_spec(t, D), hspec, _full_spec((HEADS, VD, D))],
        out_specs=[hspec, hspec, _full_spec((HEADS, VD, D))],
        out_shape=[sds((HEADS, S, VD), MXU), sds((HEADS, S, VD), F32), sds((HEADS, VD, D), F32)],
        compiler_params=_params(("arbitrary",)),
    )(dpb, o, wb)


def _mod_fwd(c_all, ada_w, name):
    n = ada_w.shape[2]

    def body(c_ref, w_ref, o_ref):
        c = c_ref[...]
        ca = (c * _sigmoid(c)).astype(MXU)
        for l in range(DEPTH):
            o_ref[l] = _dot(ca, w_ref[l].astype(MXU))

    return pl.pallas_call(body, name=name, out_shape=jax.ShapeDtypeStruct((DEPTH, 8, n), F32), compiler_params=_params())(c_all, ada_w)


def _ada_w_grad(c_all_t, dmod, name):
    n = dmod.shape[2]

    def body(c_ref, d_ref, o_ref):
        c = c_ref[...]
        ca = c * _sigmoid(c)
        for l in range(DEPTH):
            dm = d_ref[l]
            acc = ca[:, 0:1] * dm[0:1, :]
            for b in range(1, 8):
                acc = acc + ca[:, b:b + 1] * dm[b:b + 1, :]
            o_ref[l] = acc

    return pl.pallas_call(body, name=name, out_shape=jax.ShapeDtypeStruct((DEPTH, D, n), F32), compiler_params=_params())(c_all_t, dmod)


def _adamw(w, g, m, v, name):
    R, C = w.shape
    t = _tile(R, 256) if R % 8 == 0 else R
    c1, c2 = 1.0 - ADAM_B1 ** ADAM_STEP, 1.0 - ADAM_B2 ** ADAM_STEP

    def body(w_ref, g_ref, m_ref, v_ref, d_ref, nm_ref, nv_ref):
        g_ = g_ref[...]
        m_ = ADAM_B1 * m_ref[...] + (1.0 - ADAM_B1) * g_
        v_ = ADAM_B2 * v_ref[...] + (1.0 - ADAM_B2) * (g_ * g_)
        nm_ref[...] = m_
        nv_ref[...] = v_
        d_ref[...] = -ADAM_LR * ((m_ / c1) / (jnp.sqrt(v_ / c2) + ADAM_EPS) + ADAM_WD * w_ref[...])

    spec = pl.BlockSpec((t, C), lambda i: (i, 0))
    return pl.pallas_call(
        body, name=name, grid=(R // t,), in_specs=[spec] * 4, out_specs=[spec] * 3,
        out_shape=[jax.ShapeDtypeStruct((R, C), F32)] * 3, compiler_params=_params(("parallel",)),
    )(w, g, m, v)


def _sum8(a, name):
    _, R, C = a.shape
    t = _tile(R, 512) if R % 8 == 0 else R

    def body(a_ref, o_ref):
        s = a_ref[0]
        for k in range(1, 8):
            s = s + a_ref[k]
        o_ref[...] = s

    return pl.pallas_call(
        body, name=name, grid=(R // t,), in_specs=[pl.BlockSpec((8, t, C), lambda i: (0, i, 0))],
        out_specs=pl.BlockSpec((t, C), lambda i: (i, 0)), out_shape=jax.ShapeDtypeStruct((R, C), F32),
        compiler_params=_params(("parallel",)),
    )(a)


def _pair_add(shards, from_sib, half, name):
    _, rh, C = from_sib.shape
    t = _tile(rh, 384, 16)
    nb = rh // t

    def body(half_ref, a0, a1, a2, a3, b_ref, o_ref, ob_ref):
        for k, a_ref in enumerate((a0, a1, a2, a3)):
            s = a_ref[...] + b_ref[k]
            o_ref[k] = s
            ob_ref[k] = s.astype(ob_ref.dtype)

    a_spec = pl.BlockSpec((t, C), lambda i, half_ref: (half_ref[0] * nb + i, 0))
    s_spec = pl.BlockSpec((4, t, C), lambda i, half_ref: (0, i, 0))
    return pl.pallas_call(
        body, name=name,
        grid_spec=pltpu.PrefetchScalarGridSpec(num_scalar_prefetch=1, grid=(nb,), in_specs=[a_spec] * 4 + [s_spec],
                                               out_specs=[s_spec, s_spec]),
        out_shape=[jax.ShapeDtypeStruct((4, rh, C), F32), jax.ShapeDtypeStruct((4, rh, C), jnp.bfloat16)],
        compiler_params=_params(("parallel",)),
    )(half, *shards, from_sib)


def _land_sum(g, land, half, name):
    _, rh, C = land.shape
    t = _tile(rh, 256, 16)
    nb = rh // t

    def body(half_ref, g_ref, l_ref, o_ref):
        s = g_ref[...].astype(F32)
        for k in range(7):
            s = s + l_ref[k].astype(F32)
        o_ref[...] = s

    return pl.pallas_call(
        body, name=name,
        grid_spec=pltpu.PrefetchScalarGridSpec(
            num_scalar_prefetch=1, grid=(nb,),
            in_specs=[pl.BlockSpec((t, C), lambda i, h: (h[0] * nb + i, 0)), pl.BlockSpec((7, t, C), lambda i, h: (0, i, 0))],
            out_specs=pl.BlockSpec((t, C), lambda i, h: (h[0] * nb + i, 0))),
        out_shape=jax.ShapeDtypeStruct((2 * rh, C), F32), compiler_params=_params(("parallel",)),
    )(half, g, land)


def _chip_sum(own, recv, half, name):
    rh, C = own.shape
    t = _tile(rh, 512, 16)
    nb = rh // t

    def body(half_ref, a_ref, r_ref, o_ref):
        s = a_ref[...]
        for k in range(3):
            s = s + r_ref[k].astype(F32)
        o_ref[...] = s

    return pl.pallas_call(
        body, name=name,
        grid_spec=pltpu.PrefetchScalarGridSpec(
            num_scalar_prefetch=1, grid=(nb,),
            in_specs=[pl.BlockSpec((t, C), lambda i, h: (i, 0)), pl.BlockSpec((3, t, C), lambda i, h: (0, i, 0))],
            out_specs=pl.BlockSpec((t, C), lambda i, h: (h[0] * nb + i, 0))),
        out_shape=jax.ShapeDtypeStruct((2 * rh, C), F32), compiler_params=_params(("parallel",)),
    )(half, own, recv)


_ANY = pl.BlockSpec(memory_space=pl.ANY)


def _coords():
    return lax.axis_index("x"), lax.axis_index("y"), lax.axis_index("c")


def _allgather8(v, name):
    R, C = v.shape

    def body(v_ref, out_ref, send_sems, recv_sems):
        x, y, c = _coords()
        me = 4 * x + 2 * y + c
        sends = []
        for k in range(1, 8):
            bx, by, bc = (k >> 2) & 1, (k >> 1) & 1, k & 1
            peer = (x ^ bx, y ^ by, c ^ bc)
            cp = pltpu.make_async_remote_copy(src_ref=v_ref, dst_ref=out_ref.at[me], send_sem=send_sems.at[k - 1],
                                              recv_sem=recv_sems.at[k - 1], device_id=peer, device_id_type=MESH)
            cp.start()
            sends.append(cp)
        for k in range(1, 8):
            bx, by, bc = (k >> 2) & 1, (k >> 1) & 1, k & 1
            peer = (x ^ bx, y ^ by, c ^ bc)
            src = 4 * peer[0] + 2 * peer[1] + peer[2]
            pltpu.make_async_remote_copy(src_ref=v_ref, dst_ref=out_ref.at[src], send_sem=send_sems.at[k - 1],
                                         recv_sem=recv_sems.at[k - 1], device_id=peer, device_id_type=MESH).wait_recv()
        for cp in sends:
            cp.wait_send()

    out = pl.pallas_call(
        body, name=name, in_specs=[_ANY], out_specs=_ANY, out_shape=jax.ShapeDtypeStruct((8, R, C), v.dtype),
        scratch_shapes=[pltpu.SemaphoreType.DMA((7,)), pltpu.SemaphoreType.DMA((7,))],
    )(v)
    x, y, c = _coords()
    return lax.dynamic_update_slice(out, v[None], (4 * x + 2 * y + c, 0, 0))


def _chip_patterns(x, y):
    return [(1 - x, y), (x, 1 - y), (1 - x, 1 - y)]


def _ag_stage(stage, w_ref, out_ref, send_sems, recv_sems):
    rh = w_ref.shape[0] // 2
    x, y, c = _coords()
    kme = 2 * x + y
    sibling = (x, y, 1 - c)
    mine_half = pl.ds(pl.multiple_of(c * rh, FLAT_ALIGN // 2), rh)
    other_half = pl.ds(pl.multiple_of((1 - c) * rh, FLAT_ALIGN // 2), rh)
    chips = _chip_patterns(x, y)

    def copy(k, chip_idx, half, to, src=None):
        dst = out_ref.at[chip_idx, half, :]
        return pltpu.make_async_remote_copy(src_ref=dst if src is None else src, dst_ref=dst, send_sem=send_sems.at[k],
                                            recv_sem=recv_sems.at[k], device_id=to, device_id_type=MESH)

    first = [copy(j, kme, mine_half, (cx, cy, c), src=w_ref.at[mine_half, :]) for j, (cx, cy) in enumerate(chips)]
    own = pltpu.make_async_remote_copy(src_ref=w_ref, dst_ref=out_ref.at[kme], send_sem=send_sems.at[6], recv_sem=recv_sems.at[6],
                                       device_id=sibling, device_id_type=MESH)
    passed = [copy(3 + j, 2 * cx + cy, mine_half, sibling) for j, (cx, cy) in enumerate(chips)]
    if stage == "start":
        for cp in first + [own]:
            cp.start()
    elif stage == "forward":
        for j, (cx, cy) in enumerate(chips):
            copy(j, 2 * cx + cy, mine_half, (x, y, c)).wait_recv()
            passed[j].start()
    else:
        for j, (cx, cy) in enumerate(chips):
            copy(3 + j, 2 * cx + cy, other_half, (x, y, c)).wait_recv()
        own.wait_recv()
        for cp in first + [own] + passed:
            cp.wait_send()


_AG_SEMS = [pltpu.SemaphoreType.DMA((7,)), pltpu.SemaphoreType.DMA((7,))]


def _allgather_weights(w, name):
    R, C = w.shape

    def body(w_ref, out_ref, send_sems, recv_sems):
        for stage in ("start", "forward", "finish"):
            _ag_stage(stage, w_ref, out_ref, send_sems, recv_sems)

    return pl.pallas_call(
        body, name=name, in_specs=[_ANY], out_specs=_ANY, out_shape=jax.ShapeDtypeStruct((4, R, C), w.dtype),
        scratch_shapes=list(_AG_SEMS),
    )(w)


def _sibling_swap(shards, name):
    R, C = shards[0].shape
    rh = R // 2

    def body(g0, g1, g2, g3, out_ref, send_sems, recv_sems):
        x, y, c = _coords()
        other_half = pl.ds(pl.multiple_of((1 - c) * rh, 8), rh)
        cps = []
        for k, g_ref in enumerate((g0, g1, g2, g3)):
            cp = pltpu.make_async_remote_copy(src_ref=g_ref.at[other_half, :], dst_ref=out_ref.at[k], send_sem=send_sems.at[k],
                                              recv_sem=recv_sems.at[k], device_id=(x, y, 1 - c), device_id_type=MESH)
            cp.start()
            cps.append(cp)
        for cp in cps:
            cp.wait()

    return pl.pallas_call(
        body, name=name, in_specs=[_ANY] * 4, out_specs=_ANY, out_shape=jax.ShapeDtypeStruct((4, rh, C), shards[0].dtype),
        scratch_shapes=[pltpu.SemaphoreType.DMA((4,)), pltpu.SemaphoreType.DMA((4,))],
    )(*shards)


def _chip_scatter(a, name):
    _, R, C = a.shape

    def body(a_ref, out_ref, send_sems, recv_sems):
        x, y, c = _coords()
        cps = []
        for j, (cx, cy) in enumerate(_chip_patterns(x, y)):
            cp = pltpu.make_async_remote_copy(src_ref=a_ref.at[2 * cx + cy], dst_ref=out_ref.at[j], send_sem=send_sems.at[j],
                                              recv_sem=recv_sems.at[j], device_id=(cx, cy, c), device_id_type=MESH)
            cp.start()
            cps.append(cp)
        for cp in cps:
            cp.wait()

    return pl.pallas_call(
        body, name=name, in_specs=[_ANY], out_specs=_ANY, out_shape=jax.ShapeDtypeStruct((3, R, C), a.dtype),
        scratch_shapes=[pltpu.SemaphoreType.DMA((3,)), pltpu.SemaphoreType.DMA((3,))],
    )(a)


def _sibling_join(f, name):
    R, C = f.shape
    rh = R // 2

    def body(f_ref, out_ref, send_sem, recv_sem):
        x, y, c = _coords()
        mine_half = pl.ds(pl.multiple_of(c * rh, 8), rh)
        cp = pltpu.make_async_remote_copy(src_ref=f_ref.at[mine_half, :], dst_ref=out_ref.at[mine_half, :], send_sem=send_sem,
                                          recv_sem=recv_sem, device_id=(x, y, 1 - c), device_id_type=MESH)
        cp.start()
        cp.wait()

    return pl.pallas_call(
        body, name=name, in_specs=[_ANY], out_specs=_ANY, out_shape=jax.ShapeDtypeStruct((R, C), f.dtype),
        scratch_shapes=[pltpu.SemaphoreType.DMA(()), pltpu.SemaphoreType.DMA(())], input_output_aliases={0: 0},
    )(f)


_EARLY = (("in_w", 1), ("mla_wuq", 1), ("mla_wukv", 1))
_LATE = (("branch_w", 2), ("mix_out_w", 0), ("ffn_w1", 1), ("ffn_w2", 0))
_BIG = _EARLY + _LATE


def _pack(shards, group):
    flat = jnp.concatenate([shards[n].reshape(-1, FLAT_W) for n, _ in group], axis=0)
    pad = (-flat.shape[0]) % FLAT_ALIGN
    return jnp.pad(flat, ((0, pad), (0, 0)))


def _packed_rows(shapes, group):
    rows = sum(math.prod(shapes[n]) // FLAT_W for n, _ in group)
    return rows + (-rows) % FLAT_ALIGN


def _unpack(flat, shapes, group):
    out, r = {}, 0
    for n, _ in group:
        rows = math.prod(shapes[n]) // FLAT_W
        out[n] = flat[..., r:r + rows, :].reshape(flat.shape[:-2] + tuple(shapes[n]))
        r += rows
    return out


def _swap16(a):
    return jnp.concatenate([a[..., 16:32], a[..., 0:16]], axis=-1)


def _shard_cols(parts, a, b):
    w = parts[0].shape[-1]
    pieces = []
    for k, p in enumerate(parts):
        s, e = max(a, k * w), min(b, (k + 1) * w)
        if s < e:
            pieces.append(p[..., s - k * w:e - k * w])
    return pieces


def _pad_in_cols(parts):
    lead, dt = parts[0].shape[:-1], parts[0].dtype
    z = lambda n: [jnp.zeros(lead + (n,), dt)]
    seg = lambda a, b: _shard_cols(parts, a, b)
    kr = jnp.concatenate(seg(O_KR, O_LX), axis=-1)
    return jnp.concatenate(seg(O_GATE, N_IN) + seg(O_GM, O_Q) + seg(O_LX, O_LG) + seg(O_LG, O_GATE) + seg(O_Q, O_KV) + seg(O_KV, O_KR)
                           + z(64) + [kr] + z(32) + z(64) + [_swap16(kr)] + z(32), axis=-1)


_SEGS = ((O_GM, O_Q, P_GM), (O_Q, O_KV, P_Q), (O_KV, O_KR, P_KV), (O_KR, O_LX, None), (O_LX, O_LG, P_LX), (O_LG, O_GATE, P_LG),
         (O_GATE, N_IN, P_GATE))


def _unpad_in_cols(g, a=0, b=N_IN):
    pieces = []
    for lo, hi, p in _SEGS:
        s, e = max(a, lo), min(b, hi)
        if s >= e:
            continue
        if p is None:
            kr = g[..., P_KA + 64:P_KA + 96] + _swap16(g[..., P_KB + 64:P_KB + 96])
            pieces.append(kr[..., s - lo:e - lo])
        else:
            pieces.append(g[..., p + s - lo:p + e - lo])
    return jnp.concatenate(pieces, axis=-1)


def _flat128(vecs):
    flat = jnp.concatenate([v.reshape(-1) for v in vecs])
    pad = (-flat.shape[0]) % 1024
    return jnp.pad(flat, (0, pad)).reshape(-1, 128)


def _unflat(flat, shapes):
    flat = flat.reshape(-1)
    out, r = [], 0
    for s in shapes:
        n = math.prod(s)
        out.append(flat[r:r + n].reshape(s))
        r += n
    return out


_SMALL = ("ada_b", "in_b", "gm_ln_g", "gm_ln_b", "gm_ws", "gm_bs", "mla_qnorm_g", "mla_kvnorm_g", "lru_conv_w", "lru_conv_b",
          "lru_wr", "lru_br", "lru_wi", "lru_bi", "lru_lambda", "ffn_b1", "ffn_b2", "ln_g", "ln_b")
_NAMES = ("ada_w", "ada_b", "in_w", "in_b", "gm_ln_g", "gm_ln_b", "gm_ws", "gm_bs", "mla_qnorm_g", "mla_wuq", "mla_kvnorm_g",
          "mla_wukv", "lru_conv_w", "lru_conv_b", "lru_wr", "lru_br", "lru_wi", "lru_bi", "lru_lambda", "branch_w", "mix_out_w",
          "ffn_w1", "ffn_b1", "ffn_w2", "ffn_b2", "ln_g", "ln_b")


def kernel(x, c, ada_w, ada_b, in_w, in_b, gm_ln_g, gm_ln_b, gm_ws, gm_bs, mla_qnorm_g, mla_wuq, mla_kvnorm_g, mla_wukv, lru_conv_w, lru_conv_b, lru_wr, lru_br, lru_wi, lru_bi, lru_lambda, branch_w, mix_out_w, ffn_w1, ffn_b1, ffn_w2, ffn_b2, ln_g, ln_b, loss_target, m_ada_w, m_ada_b, m_in_w, m_in_b, m_gm_ln_g, m_gm_ln_b, m_gm_ws, m_gm_bs, m_mla_qnorm_g, m_mla_wuq, m_mla_kvnorm_g, m_mla_wukv, m_lru_conv_w, m_lru_conv_b, m_lru_wr, m_lru_br, m_lru_wi, m_lru_bi, m_lru_lambda, m_branch_w, m_mix_out_w, m_ffn_w1, m_ffn_b1, m_ffn_w2, m_ffn_b2, m_ln_g, m_ln_b, v_ada_w, v_ada_b, v_in_w, v_in_b, v_gm_ln_g, v_gm_ln_b, v_gm_ws, v_gm_bs, v_mla_qnorm_g, v_mla_wuq, v_mla_kvnorm_g, v_mla_wukv, v_lru_conv_w, v_lru_conv_b, v_lru_wr, v_lru_br, v_lru_wi, v_lru_bi, v_lru_lambda, v_branch_w, v_mix_out_w, v_ffn_w1, v_ffn_b1, v_ffn_w2, v_ffn_b2, v_ln_g, v_ln_b):
    W = dict(ada_w=ada_w, ada_b=ada_b, in_w=in_w, in_b=in_b, gm_ln_g=gm_ln_g, gm_ln_b=gm_ln_b, gm_ws=gm_ws, gm_bs=gm_bs,
             mla_qnorm_g=mla_qnorm_g, mla_wuq=mla_wuq, mla_kvnorm_g=mla_kvnorm_g, mla_wukv=mla_wukv, lru_conv_w=lru_conv_w,
             lru_conv_b=lru_conv_b, lru_wr=lru_wr, lru_br=lru_br, lru_wi=lru_wi, lru_bi=lru_bi, lru_lambda=lru_lambda,
             branch_w=branch_w, mix_out_w=mix_out_w, ffn_w1=ffn_w1, ffn_b1=ffn_b1, ffn_w2=ffn_w2, ffn_b2=ffn_b2, ln_g=ln_g, ln_b=ln_b)
    M = dict(ada_w=m_ada_w, ada_b=m_ada_b, in_w=m_in_w, in_b=m_in_b, gm_ln_g=m_gm_ln_g, gm_ln_b=m_gm_ln_b, gm_ws=m_gm_ws,
             gm_bs=m_gm_bs, mla_qnorm_g=m_mla_qnorm_g, mla_wuq=m_mla_wuq, mla_kvnorm_g=m_mla_kvnorm_g, mla_wukv=m_mla_wukv,
             lru_conv_w=m_lru_conv_w, lru_conv_b=m_lru_conv_b, lru_wr=m_lru_wr, lru_br=m_lru_br, lru_wi=m_lru_wi, lru_bi=m_lru_bi,
             lru_lambda=m_lru_lambda, branch_w=m_branch_w, mix_out_w=m_mix_out_w, ffn_w1=m_ffn_w1, ffn_b1=m_ffn_b1, ffn_w2=m_ffn_w2,
             ffn_b2=m_ffn_b2, ln_g=m_ln_g, ln_b=m_ln_b)
    V = dict(ada_w=v_ada_w, ada_b=v_ada_b, in_w=v_in_w, in_b=v_in_b, gm_ln_g=v_gm_ln_g, gm_ln_b=v_gm_ln_b, gm_ws=v_gm_ws,
             gm_bs=v_gm_bs, mla_qnorm_g=v_mla_qnorm_g, mla_wuq=v_mla_wuq, mla_kvnorm_g=v_mla_kvnorm_g, mla_wukv=v_mla_wukv,
             lru_conv_w=v_lru_conv_w, lru_conv_b=v_lru_conv_b, lru_wr=v_lru_wr, lru_br=v_lru_br, lru_wi=v_lru_wi, lru_bi=v_lru_bi,
             lru_lambda=v_lru_lambda, branch_w=v_branch_w, mix_out_w=v_mix_out_w, ffn_w1=v_ffn_w1, ffn_b1=v_ffn_b1, ffn_w2=v_ffn_w2,
             ffn_b2=v_ffn_b2, ln_g=v_ln_g, ln_b=v_ln_b)

    S = x.shape[1]
    xi, yi, ci = _coords()
    kme = 2 * xi + yi
    me = 4 * xi + 2 * yi + ci
    x0 = x[0]
    tgt = loss_target[0]

    assert DEPTH == 2
    shard_shapes = {n: W[n].shape[1:] for n, _ in _BIG}
    rows_e, rows_l = _packed_rows(shard_shapes, _EARLY), _packed_rows(shard_shapes, _LATE)
    wpack = lambda l, group: _pack({n: W[n][l].astype(MXU) for n, _ in group}, group)
    joined = lambda sh, n, ax: jnp.concatenate([sh[n][k] for k in range(4)], axis=ax)

    def early_weights(gathered):
        sh = _unpack(gathered, shard_shapes, _EARLY)
        wuq = joined(sh, "mla_wuq", 1).reshape(Q_RANK, HEADS, NOPE + ROPE)
        zq = lambda n: jnp.zeros((Q_RANK, HEADS, n), MXU)
        wukv = joined(sh, "mla_wukv", 1).reshape(KV_RANK, HEADS, NOPE + VD)
        return dict(
            win=_pad_in_cols([sh["in_w"][k] for k in range(4)]),
            w1q=jnp.concatenate([wuq, zq(32)], axis=-1).reshape(Q_RANK, D),
            w2q=jnp.concatenate([zq(64), _swap16(wuq[..., NOPE:]), zq(32)], axis=-1).reshape(Q_RANK, D),
            wk=jnp.concatenate([wukv[..., :NOPE], jnp.zeros((KV_RANK, HEADS, 64), MXU)], axis=-1).reshape(KV_RANK, D),
            wv=wukv[..., NOPE:].transpose(1, 0, 2), wvt=wukv[..., NOPE:].transpose(1, 2, 0))

    def late_weights(gathered):
        sh = _unpack(gathered, shard_shapes, _LATE)
        bw = joined(sh, "branch_w", 2)
        return dict(wa=bw[0], wb=bw[1].reshape(HEADS, VD, D), wc=bw[2], mix=joined(sh, "mix_out_w", 0),
                    w1=joined(sh, "ffn_w1", 1), w2=joined(sh, "ffn_w2", 0))

    LW = [early_weights(_allgather_weights(wpack(0, _EARLY), "ag_weights0")), None]
    ride_w = jnp.concatenate([wpack(0, _LATE), wpack(1, _EARLY), wpack(1, _LATE)], axis=0)
    ride_w = jnp.pad(ride_w, ((0, (-ride_w.shape[0]) % RIDE_ALIGN), (0, 0)))

    small1 = _allgather8(_flat128([c, ln_g, ln_b, lru_conv_w]), "ag_small")
    per_dev = [_unflat(small1[d], [c.shape, ln_g.shape, ln_b.shape, lru_conv_w.shape]) for d in range(8)]
    c_all = jnp.concatenate([p[0] for p in per_dev], axis=0)
    chip = lambda i: [per_dev[2 * k][i] for k in range(4)]
    ln_g_f, ln_b_f, conv_w_f = (jnp.concatenate(chip(1), axis=2), jnp.concatenate(chip(2), axis=2), jnp.concatenate(chip(3), axis=2))

    mod_sh = _mod_fwd(c_all, ada_w, "mod_fwd")
    mod_all = _allgather8(_flat128([mod_sh]), "ag_mod")
    mod_parts = [_unflat(mod_all[2 * k], [mod_sh.shape])[0] for k in range(4)]
    mod = jnp.concatenate(mod_parts, axis=2)
    mod = lax.dynamic_index_in_dim(mod, me, axis=1, keepdims=False) + ada_b

    pos = jnp.arange(S, dtype=F32)
    inv = ROPE_BASE ** (-jnp.arange(0, ROPE, 2, dtype=F32) / ROPE)
    ang = pos[:, None] * inv[None, :]
    cs, sn = jnp.cos(ang), jnp.sin(ang)
    zc = lambda n: jnp.zeros((S, n), F32)
    cos_t = jnp.concatenate([zc(64), cs, cs, zc(32)], axis=1)
    sin_t = jnp.concatenate([zc(64), -sn, sn, zc(32)], axis=1)

    bin_p = _pad_in_cols([in_b])
    wr_f =jnp.stack([block_diag(*[lru_wr[l, b] for b in range(LRU_NB)]) for l in range(DEPTH)]).astype(MXU)
    wi_f = jnp.stack([block_diag(*[lru_wi[l, b] for b in range(LRU_NB)]) for l in range(DEPTH)]).astype(MXU)
    ws_b = gm_ws.astype(MXU)
    wst_b = gm_ws.transpose(0, 1, 3, 2).astype(MXU)
    bs_t = gm_bs.transpose(0, 2, 1)
    row = lambda v: v.reshape(1, -1)

    saved = []
    xs = x0
    h1 = None
    for l in range(DEPTH):
        sh1, sc1, g1, sh2, sc2, g2 = [row(mod[l, i * D:(i + 1) * D]) for i in range(6)]
        if l == 0:
            h1 = _lnmod_fwd(xs, sc1, sh1, f"lnmod_fwd{l}")
        lw = LW[l]
        z = _mm(h1, lw["win"], "nn", f"in_proj{l}", rows=[row(bin_p[l])], epi=lambda acc, b: (acc + b,))
        ya = _gmlp_fwd(z, row(gm_ln_g[l]), row(gm_ln_b[l]), ws_b[l], bs_t[l], f"gmlp_fwd{l}")
        qf, kf, vf, vtf = _mla_prep_fwd(z, cos_t, sin_t, row(mla_qnorm_g[l]), row(mla_kvnorm_g[l]), lw["w1q"], lw["w2q"], lw["wk"],
                                        lw["wv"], lw["wvt"], f"mla_prep_fwd{l}")
        if l == 0:
            ob, o32, lse, rode = _attn_fwd(qf, kf, vtf, f"attn_fwd{l}", gather=ride_w)
            lw.update(late_weights(rode[:, :rows_l]))
            LW[1] = {**early_weights(rode[:, rows_l:rows_l + rows_e]), **late_weights(rode[:, rows_l + rows_e:rows_l + rows_e + rows_l])}
        else:
            ob, o32, lse = _attn_fwd(qf, kf, vtf, f"attn_fwd{l}")
        lru_args = (conv_w_f[l], row(lru_conv_b[l]), wr_f[l], row(lru_br[l]), wi_f[l], row(lru_bi[l]), row(lru_lambda[l]))
        yc, hl = _lru_fwd(z, *lru_args, f"lru_fwd{l}")
        merged, pa, pb, pc = _branch_fwd(z, ya, ob, yc, lw["wa"], lw["wb"], lw["wc"], f"branch_fwd{l}")
        mix = _mm(merged, lw["mix"], "nn", f"mix_out{l}")
        x1, h2 = _res_ln_fwd(xs, mix, g1, row(ln_g_f[l, 0]), row(ln_b_f[l, 0]), f"res_ln_a{l}", mod=(sc2, sh2))
        a1, r2 = _mm(h2, lw["w1"], "nn", f"ffn_up{l}", out_dtypes=(F32, MXU), rows=[row(ffn_b1[l])],
                     epi=lambda acc, b: (acc + b, jnp.square(jnp.maximum(acc + b, 0.0))))
        f = _mm(r2, lw["w2"], "nn", f"ffn_down{l}", rows=[row(ffn_b2[l])], epi=lambda acc, b: (acc + b,))
        if l + 1 < DEPTH:
            nsh1, nsc1 = row(mod[l + 1, 0:D]), row(mod[l + 1, D:2 * D])
            x2, h1n = _res_ln_fwd(x1, f, g2, row(ln_g_f[l, 1]), row(ln_b_f[l, 1]), f"res_ln_b{l}", mod=(nsc1, nsh1))
        else:
            (x2,), h1n = _res_ln_fwd(x1, f, g2, row(ln_g_f[l, 1]), row(ln_b_f[l, 1]), f"res_ln_b{l}"), None
        saved.append(dict(x_in=xs, h1=h1, z=z, ya=ya, qf=qf, kf=kf, vf=vf, ob=ob, o32=o32, lse=lse, yc=yc, hl=hl, merged=merged,
                          pa=pa, pb=pb, pc=pc, mix=mix, x1=x1, h2=h2, a1=a1, r2=r2, f=f, x2=x2, lru_args=lru_args,
                          mods=(sh1, sc1, g1, sh2, sc2, g2)))
        xs, h1 = x2, h1n

    dy, sq = _loss_fwd(xs, tgt)
    loss = lax.psum(0.5 / D * jnp.sum(sq), ("x", "y", "c"))

    G = {}
    dmods = []
    dres, hpath = dy, None
    wcols = in_w.shape[2]

    def grad_shards(gl, dtype, group):
        shard_of = lambda n, ax, k: (_unpad_in_cols(gl[n], k * wcols, (k + 1) * wcols) if n == "in_w" else jnp.split(gl[n], 4, axis=ax)[k])
        return [_pack({n: shard_of(n, ax, k).astype(dtype) for n, ax in group}, group) for k in range(4)]

    landed = None
    for l in reversed(range(DEPTH)):
        sv = saved[l]
        lw = LW[l]
        sh1, sc1, g1, sh2, sc2, g2 = sv["mods"]
        gl = {}
        nb = _node_bwd(f"node_b{l}", dres, hpath=hpath, upath=(sv["x1"], sv["f"], g2, row(ln_g_f[l, 1])))
        if hpath is not None:
            dmods[-1]["sc1"], dmods[-1]["sh1"] = nb["dsc"], nb["dsh"]
        dm = dict(g2=nb["dg"])
        ln_g_l1, ln_b_l1 = nb["dlng"], nb["dlnb"]
        df = nb["dbr"]
        gl["ffn_b2"] = nb["dbrsum"][0]
        gl["ffn_w2"] = _mm(sv["r2"], df, "tn", f"d_ffn_w2_{l}")
        da1, db1 = _mm(df, lw["w2"], "nt", f"d_ffn_act{l}", out_dtypes=(MXU,), extras=[sv["a1"]], colsum=True,
                       epi=lambda acc, a: (acc * (2.0 * jnp.maximum(a, 0.0)),))
        gl["ffn_b1"] = db1[0]
        gl["ffn_w1"] = _mm(sv["h2"], da1, "tn", f"d_ffn_w1_{l}")
        dh2 = _mm(da1, lw["w1"], "nt", f"d_ffn_in{l}")
        na = _node_bwd(f"node_a{l}", nb["du"], hpath=(dh2, sv["x1"], sc2), upath=(sv["x_in"], sv["mix"], g1, row(ln_g_f[l, 0])))
        dm.update(sc2=na["dsc"], sh2=na["dsh"], g1=na["dg"])
        gl["ln_g"] = jnp.concatenate([na["dlng"], ln_g_l1], axis=0)
        gl["ln_b"] = jnp.concatenate([na["dlnb"], ln_b_l1], axis=0)
        dmix = na["dbr"]
        gl["mix_out_w"] = _mm(sv["merged"], dmix, "tn", f"d_mix_w{l}")
        dmerged = _mm(dmix, lw["mix"], "nt", f"d_merged{l}")
        dz, dpa, dpb, dpc, s_gate = _branch_bwd(sv["z"], dmerged, sv["pa"], sv["pb"], sv["pc"], f"branch_bwd{l}")
        dya = _mm(dpa, lw["wa"], "nt", f"d_ya{l}")
        dyc = _mm(dpc, lw["wc"], "nt", f"d_yc{l}")
        dwa = _mm(sv["ya"], dpa, "tn", f"d_wa{l}")
        dwc = _mm(sv["yc"], dpc, "tn", f"d_wc{l}")
        dob, do32, dwb = _heads_bwd(dpb, sv["ob"], lw["wb"], f"heads_bwd{l}")
        gl["branch_w"] = jnp.stack([dwa, dwb.reshape(GM_W, D), dwc])
        dz, dcw, dcb, dwr, dbr_, dwi, dbi, dlam, s_lru = _lru_bwd(sv["z"], sv["hl"], dyc, *sv["lru_args"], dz, f"lru_bwd{l}")
        gl["lru_conv_w"], gl["lru_conv_b"], gl["lru_br"], gl["lru_bi"], gl["lru_lambda"] = dcw, dcb[0], dbr_[0], dbi[0], dlam[0]
        blocks = lambda m: jnp.stack([m[b * 64:(b + 1) * 64, b * 64:(b + 1) * 64] for b in range(LRU_NB)])
        gl["lru_wr"], gl["lru_wi"] = blocks(dwr), blocks(dwi)
        delta = _attn_delta(sv["o32"], do32, f"attn_delta{l}")
        attn_args = (sv["qf"], sv["kf"], sv["vf"], dob, sv["lse"], delta.reshape(HEADS, 1, S), f"attn_bwd{l}")
        if l == 0:
            pieces = zip(grad_shards(gl, jnp.bfloat16, _LATE), grad_shards(G[1], jnp.bfloat16, _EARLY), grad_shards(G[1], jnp.bfloat16, _LATE))
            sent = jnp.stack([jnp.concatenate(p, axis=0) for p in pieces])
            sent = jnp.pad(sent, ((0, 0), (0, (-sent.shape[1]) % RIDE_ALIGN), (0, 0)))
            dqf, dkf, dvf, landed = _attn_bwd(*attn_args, scatter=sent)
        else:
            dqf, dkf, dvf = _attn_bwd(*attn_args)
        dz, dgq, dgkv, dw1, dw2, dwk, dwv, s_mla = _mla_prep_bwd(sv["z"], cos_t, sin_t, row(mla_qnorm_g[l]), row(mla_kvnorm_g[l]),
                                                                 lw["w1q"], lw["w2q"], lw["wk"], lw["wv"], dqf, dkf, dvf, dz,
                                                                 f"mla_prep_bwd{l}")
        gl["mla_qnorm_g"], gl["mla_kvnorm_g"] = dgq[0], dgkv[0]
        dw1 = dw1.reshape(Q_RANK, HEADS, 128)
        dw2 = dw2.reshape(Q_RANK, HEADS, 128)
        gl["mla_wuq"] = jnp.concatenate([dw1[..., :NOPE], dw1[..., NOPE:NOPE + ROPE] + _swap16(dw2[..., NOPE:NOPE + ROPE])],
                                        axis=-1).reshape(Q_RANK, HEADS * (NOPE + ROPE))
        gl["mla_wukv"] = jnp.concatenate([dwk.reshape(KV_RANK, HEADS, 128)[..., :NOPE], dwv.transpose(1, 0, 2)],
                                         axis=-1).reshape(KV_RANK, HEADS * (NOPE + VD))
        dz, dglg, dglb, dws, dbs, s_gm = _gmlp_bwd(sv["z"], dya, row(gm_ln_g[l]), row(gm_ln_b[l]), ws_b[l], wst_b[l], bs_t[l], dz,
                                                   f"gmlp_bwd{l}")
        gl["gm_ln_g"], gl["gm_ln_b"], gl["gm_ws"], gl["gm_bs"] = dglg[0], dglb[0], dws, dbs[:, :GM_G].T
        gl["in_b"] = _unpad_in_cols(jnp.concatenate([s_gate, s_gm, s_lru, s_mla], axis=1))[0]
        gl["in_w"] = _mm(sv["h1"], dz, "tn", f"d_in_w{l}")
        dh1 = _mm(dz, lw["win"], "nt", f"d_h1_{l}")
        dmods.append(dm)
        dres, hpath = na["du"], (dh1, sv["x_in"], sc1)
        G[l] = gl
    n0 = _node_bwd("node_in", dres, hpath=hpath)
    dmods[-1]["sc1"], dmods[-1]["sh1"] = n0["dsc"], n0["dsh"]
    grad_x = n0["dx"][None]
    dmods = dmods[::-1]
    dmod = jnp.stack([jnp.concatenate([dmods[l][k] for k in ("sh1", "sc1", "g1", "sh2", "sc2", "g2")], axis=1)[0]
                      for l in range(DEPTH)])
    grads = {n: jnp.stack([G[l][n] for l in range(DEPTH)]) for n in _SMALL if n != "ada_b"}
    grads["ada_b"] = dmod

    half = ci.reshape(1).astype(jnp.int32)
    rode_sum = _land_sum(lax.dynamic_index_in_dim(sent, kme, axis=0, keepdims=False), landed, half, "rs_ride_sum")
    gsh = grad_shards(G[0], F32, _EARLY)
    from_sib = _sibling_swap(gsh, "rs_pair")
    pair, pair_b = _pair_add(gsh, from_sib, half, "rs_pair_add")
    from_chips = _chip_scatter(pair_b, "rs_chips")
    own = lax.dynamic_index_in_dim(pair, kme, axis=0, keepdims=False)
    early0_sum = _chip_sum(own, from_chips, half, "rs_chip_sum")
    rode_g = _sibling_join(rode_sum, "rs_join_ride")
    layer0 = {**_unpack(_sibling_join(early0_sum, "rs_join_early"), shard_shapes, _EARLY), **_unpack(rode_g[:rows_l], shard_shapes, _LATE)}
    layer1 = {**_unpack(rode_g[rows_l:rows_l + rows_e], shard_shapes, _EARLY),
              **_unpack(rode_g[rows_l + rows_e:rows_l + rows_e + rows_l], shard_shapes, _LATE)}
    gbig = {n: jnp.stack([layer0[n], layer1[n]]) for n, _ in _BIG}

    small_shapes = [grads[n].shape for n in _SMALL]
    gsm_all = _allgather8(_flat128([grads[n] for n in _SMALL]), "ag_small_grads")
    gsm = _unflat(_sum8(gsm_all, "sum_small_grads"), small_shapes)
    gsmall = dict(zip(_SMALL, gsm))
    n_mod = DEPTH * 6 * D
    dmod_all = gsm_all[:, :n_mod // 128].reshape(8, DEPTH, 6 * D).transpose(1, 0, 2)
    dmod_sh = lax.dynamic_slice_in_dim(dmod_all, kme * (6 * D // 4), 6 * D // 4, axis=2)
    g_ada_w = _ada_w_grad(c_all.T, dmod_sh, "d_ada_w")
    quarter = lambda g, ax: lax.dynamic_slice_in_dim(g, kme * (g.shape[ax] // 4), g.shape[ax] // 4, axis=ax)
    gsmall["lru_conv_w"] = quarter(gsmall["lru_conv_w"], 2)
    gsmall["ln_g"] = quarter(gsmall["ln_g"], 2)
    gsmall["ln_b"] = quarter(gsmall["ln_b"], 2)

    grad = dict(gbig)
    grad.update(gsmall)
    grad["ada_w"] = g_ada_w

    delta, new_m, new_v = {}, {}, {}
    for n in _NAMES:
        shp = W[n].shape
        two = lambda a: a.reshape(-1, shp[-1])
        d_, m_, v_ = _adamw(two(W[n]), two(grad[n]), two(M[n]), two(V[n]), f"adamw_{n}")
        delta[n], new_m[n], new_v[n] = d_.reshape(shp), m_.reshape(shp), v_.reshape(shp)

    return (loss, grad_x, *[grad[n] for n in _NAMES], *[delta[n] for n in _NAMES], *[new_m[n] for n in _NAMES],
            *[new_v[n] for n in _NAMES])
```

```python
import functools
import math

import jax
import jax.numpy as jnp
from jax import lax
from jax.experimental import pallas as pl
from jax.experimental.pallas import tpu as pltpu
from jax.scipy.linalg import block_diag

F32 = jnp.float32
MXU = jnp.bfloat16
MESH = pl.DeviceIdType.MESH

D = 1024
DEPTH = 2
CHUNK = 64
GM_W = 512
GM_G = 4
HEADS = 8
Q_RANK = 256
KV_RANK = 128
NOPE = 64
ROPE = 32
VD = 64
LRU_W = 512
LRU_NB = 8
D_FF = 4096
ALPHA = (2.0 * DEPTH) ** 0.25
LN_EPS = 1e-5
RMS_EPS = 1e-6
ROPE_BASE = 10000.0
ATT_SCALE = (NOPE + ROPE) ** -0.5
ATT_SCALE2 = ATT_SCALE * math.log2(math.e)
N_IN = 5536
P_GATE, P_GM, P_LX, P_LG, P_Q, P_KV, P_KA, P_KB, NP = 0, 3072, 4096, 4608, 5120, 5376, 5504, 5632, 5760
O_GM, O_Q, O_KV, O_KR, O_LX, O_LG, O_GATE = 0, 1024, 1280, 1408, 1440, 1952, 2464

ADAM_LR, ADAM_B1, ADAM_B2, ADAM_EPS, ADAM_WD, ADAM_STEP = 0.001, 0.9, 0.999, 1e-08, 0.01, 10

ROW_TILE = 512
ATT_TILE = 512
VMEM_LIMIT = 56 * 1024 * 1024
FLAT_W = 1024
FLAT_ALIGN = 32
RIDE_ALIGN = 1024


def _params(sem=None, vmem=VMEM_LIMIT):
    return pltpu.CompilerParams(dimension_semantics=sem, vmem_limit_bytes=vmem)


def _tile(dim, pref, mult=128):
    if dim <= pref:
        return dim
    t = (pref // mult) * mult
    while t >= mult:
        if dim % t == 0:
            return t
        t -= mult
    return dim


def _dot(a, b):
    return lax.dot_general(a, b, (((1,), (0,)), ((), ())), preferred_element_type=F32)


def _dot_nt(a, b):
    return lax.dot_general(a, b, (((1,), (1,)), ((), ())), preferred_element_type=F32)


def _dot_tn(a, b):
    return lax.dot_general(a, b, (((0,), (0,)), ((), ())), preferred_element_type=F32)


def _sigmoid(x):
    return 1.0 / (1.0 + jnp.exp(-x))


_GC = 0.7978845608028654


def _gelu(x):
    return 0.5 * x * (1.0 + jnp.tanh(_GC * (x + 0.044715 * x * x * x)))


def _gelu_grad(x):
    t = jnp.tanh(_GC * (x + 0.044715 * x * x * x))
    return 0.5 * (1.0 + t) + 0.5 * x * (1.0 - t * t) * _GC * (1.0 + 3.0 * 0.044715 * x * x)


def _ln_stats(x):
    mu = jnp.mean(x, axis=-1, keepdims=True)
    xc = x - mu
    r = lax.rsqrt(jnp.mean(xc * xc, axis=-1, keepdims=True) + LN_EPS)
    return xc * r, r


def _ln_bwd(dxh, xh, r):
    return r * (dxh - jnp.mean(dxh, axis=-1, keepdims=True) - xh * jnp.mean(dxh * xh, axis=-1, keepdims=True))


def _colsum(v):
    return jnp.sum(v, axis=0, keepdims=True)


def _acc(ref, first, val):
    @pl.when(first)
    def _():
        ref[...] = val

    @pl.when(jnp.logical_not(first))
    def _():
        ref[...] += val


def _row_spec(t, w, col=0):
    return pl.BlockSpec((t, w), lambda i, c=col: (i, c))


def _vec_spec(w):
    return pl.BlockSpec((1, w), lambda i: (0, 0))


def _full_spec(shape):
    nd = len(shape)
    return pl.BlockSpec(shape, lambda i, n=nd: (0,) * n)


def _mm(a, b, mode, name, out_dtypes=(F32,), extras=(), rows=(), epi=None, colsum=False, tm=1024, tn=1152, tk=2048):
    if mode == "nn":
        (M, K), N = a.shape, b.shape[1]
    elif mode == "nt":
        (M, K), N = a.shape, b.shape[0]
    else:
        (K, M), N = a.shape, b.shape[1]
    tm, tn, tk = _tile(M, tm), _tile(N, tn), _tile(K, tk)
    nk = K // tk
    dot = {"nn": _dot, "nt": _dot_nt, "tn": _dot_tn}[mode]
    a_spec = pl.BlockSpec((tk, tm), lambda i, j, k: (k, i)) if mode == "tn" else pl.BlockSpec((tm, tk), lambda i, j, k: (i, k))
    b_spec = pl.BlockSpec((tn, tk), lambda i, j, k: (j, k)) if mode == "nt" else pl.BlockSpec((tk, tn), lambda i, j, k: (k, j))
    o_spec = pl.BlockSpec((tm, tn), lambda i, j, k: (i, j))
    r_spec = pl.BlockSpec((1, tn), lambda i, j, k: (0, j))
    n_e, n_r, n_o = len(extras), len(rows), len(out_dtypes)
    if epi is None:
        epi = lambda acc: (acc,)

    def body(*refs):
        a_ref, b_ref = refs[0], refs[1]
        e_refs = refs[2:2 + n_e]
        r_refs = refs[2 + n_e:2 + n_e + n_r]
        o_refs = refs[2 + n_e + n_r:2 + n_e + n_r + n_o]
        p = dot(a_ref[...], b_ref[...])

        def finish(acc):
            outs = epi(acc, *[e[...] for e in e_refs], *[r[...] for r in r_refs])
            for o_ref, o in zip(o_refs, outs):
                o_ref[...] = o.astype(o_ref.dtype)
            if colsum:
                cs_ref = refs[2 + n_e + n_r + n_o]
                cols = pl.ds(pl.multiple_of(pl.program_id(1) * tn, 128), tn)
                _acc(cs_ref.at[:, cols], pl.program_id(0) == 0, _colsum(outs[0]))

        if nk == 1:
            finish(p)
        else:
            acc_ref = refs[-1]
            k = pl.program_id(2)
            _acc(acc_ref, k == 0, p)

            @pl.when(k == nk - 1)
            def _():
                finish(acc_ref[...])

    outs = pl.pallas_call(
        body, name=name, grid=(M // tm, N // tn, nk),
        in_specs=[a_spec, b_spec] + [o_spec] * n_e + [r_spec] * n_r,
        out_specs=[o_spec] * n_o + ([pl.BlockSpec((1, N), lambda i, j, k: (0, 0))] if colsum else []),
        out_shape=[jax.ShapeDtypeStruct((M, N), dt) for dt in out_dtypes] + ([jax.ShapeDtypeStruct((1, N), F32)] if colsum else []),
        scratch_shapes=[pltpu.VMEM((tm, tn), F32)] if nk > 1 else [],
        compiler_params=_params(("arbitrary",) * 3 if colsum else ("parallel", "parallel", "arbitrary")),
    )(a, b, *extras, *rows)
    return outs[0] if len(outs) == 1 else outs


def _lnmod_fwd(x, sc, sh, name):
    S = x.shape[0]
    t = min(ROW_TILE, S)

    def body(x_ref, sc_ref, sh_ref, h_ref):
        xh, _ = _ln_stats(x_ref[...])
        h_ref[...] = (xh * (1.0 + sc_ref[...]) + sh_ref[...]).astype(h_ref.dtype)

    return pl.pallas_call(
        body, name=name, grid=(S // t,),
        in_specs=[_row_spec(t, D), _vec_spec(D), _vec_spec(D)], out_specs=_row_spec(t, D),
        out_shape=jax.ShapeDtypeStruct((S, D), MXU), compiler_params=_params(("parallel",)),
    )(x, sc, sh)


def _res_ln_fwd(xprev, br, gvec, lng, lnb, name, mod=None):
    S = xprev.shape[0]
    t = min(ROW_TILE, S)
    with_h = mod is not None

    def body(*refs):
        xp_ref, br_ref, g_ref, lg_ref, lb_ref = refs[:5]
        u = ALPHA * xp_ref[...] + (1.0 + g_ref[...]) * br_ref[...]
        uh, _ = _ln_stats(u)
        xn = uh * lg_ref[...] + lb_ref[...]
        if with_h:
            sc_ref, sh_ref, xn_ref, h_ref = refs[5:]
            xh, _ = _ln_stats(xn)
            h_ref[...] = (xh * (1.0 + sc_ref[...]) + sh_ref[...]).astype(h_ref.dtype)
        else:
            xn_ref = refs[5]
        xn_ref[...] = xn

    ins = [xprev, br, gvec, lng, lnb] + (list(mod) if with_h else [])
    return pl.pallas_call(
        body, name=name, grid=(S // t,),
        in_specs=[_row_spec(t, D), _row_spec(t, D)] + [_vec_spec(D)] * (len(ins) - 2),
        out_specs=[_row_spec(t, D)] * (2 if with_h else 1),
        out_shape=[jax.ShapeDtypeStruct((S, D), F32)] + ([jax.ShapeDtypeStruct((S, D), MXU)] if with_h else []),
        compiler_params=_params(("parallel",)),
    )(*ins)


def _loss_fwd(y, tgt):
    S = y.shape[0]
    t = min(ROW_TILE, S)

    def body(y_ref, t_ref, dy_ref, sq_ref):
        d = y_ref[...] - t_ref[...]
        dy_ref[...] = d * (1.0 / D)
        _acc(sq_ref, pl.program_id(0) == 0, _colsum(d * d))

    return pl.pallas_call(
        body, name="loss_head", grid=(S // t,),
        in_specs=[_row_spec(t, D), _row_spec(t, D)], out_specs=[_row_spec(t, D), _vec_spec(D)],
        out_shape=[jax.ShapeDtypeStruct((S, D), F32), jax.ShapeDtypeStruct((1, D), F32)],
        compiler_params=_params(("arbitrary",)),
    )(y, tgt)


def _node_bwd(name, dres, *, hpath=None, upath=None):
    S = dres.shape[0]
    t = min(ROW_TILE, S)
    has_h, has_u = hpath is not None, upath is not None

    def body(*refs):
        refs = list(refs)
        first = pl.program_id(0) == 0
        dxs = refs.pop(0)[...]
        if has_h:
            dh = refs.pop(0)[...]
            xs = refs.pop(0)[...]
            sc = refs.pop(0)[...]
        if has_u:
            xp = refs.pop(0)[...]
            br = refs.pop(0)[...]
            gv = refs.pop(0)[...]
            lg = refs.pop(0)[...]
        if has_h:
            dsc_ref, dsh_ref = refs.pop(0), refs.pop(0)
            xh, r = _ln_stats(xs)
            _acc(dsc_ref, first, _colsum(dh * xh))
            _acc(dsh_ref, first, _colsum(dh))
            dxs = dxs + _ln_bwd(dh * (1.0 + sc), xh, r)
        if has_u:
            du_ref, dbr_ref, dlg_ref, dlb_ref, dg_ref, dbs_ref = refs
            uh, ru = _ln_stats(ALPHA * xp + (1.0 + gv) * br)
            _acc(dlg_ref, first, _colsum(dxs * uh))
            _acc(dlb_ref, first, _colsum(dxs))
            du = _ln_bwd(dxs * lg, uh, ru)
            _acc(dg_ref, first, _colsum(du * br))
            du_ref[...] = ALPHA * du
            dbr = (1.0 + gv) * du
            _acc(dbs_ref, first, _colsum(dbr))
            dbr_ref[...] = dbr.astype(dbr_ref.dtype)
        else:
            refs[0][...] = dxs

    ins, in_specs = [dres], [_row_spec(t, D)]
    outs, out_specs, names = [], [], []
    vec = jax.ShapeDtypeStruct((1, D), F32)
    if has_h:
        ins += list(hpath)
        in_specs += [_row_spec(t, D), _row_spec(t, D), _vec_spec(D)]
        outs += [vec, vec]
        out_specs += [_vec_spec(D), _vec_spec(D)]
        names += ["dsc", "dsh"]
    if has_u:
        ins += list(upath)
        in_specs += [_row_spec(t, D), _row_spec(t, D), _vec_spec(D), _vec_spec(D)]
        outs += [jax.ShapeDtypeStruct((S, D), F32), jax.ShapeDtypeStruct((S, D), MXU), vec, vec, vec, vec]
        out_specs += [_row_spec(t, D), _row_spec(t, D), _vec_spec(D), _vec_spec(D), _vec_spec(D), _vec_spec(D)]
        names += ["du", "dbr", "dlng", "dlnb", "dg", "dbrsum"]
    else:
        outs += [jax.ShapeDtypeStruct((S, D), F32)]
        out_specs += [_row_spec(t, D)]
        names += ["dx"]
    res = pl.pallas_call(
        body, name=name, grid=(S // t,), in_specs=in_specs, out_specs=out_specs, out_shape=outs,
        compiler_params=_params(("arbitrary",)),
    )(*ins)
    return dict(zip(names, res))


def _colsum_call(a, name):
    S, N = a.shape
    t, tn = min(ROW_TILE, S), _tile(N, 1152)

    def body(a_ref, o_ref):
        _acc(o_ref, pl.program_id(1) == 0, _colsum(a_ref[...].astype(F32)))

    return pl.pallas_call(
        body, name=name, grid=(N // tn, S // t),
        in_specs=[pl.BlockSpec((t, tn), lambda j, i: (i, j))], out_specs=pl.BlockSpec((1, tn), lambda j, i: (0, j)),
        out_shape=jax.ShapeDtypeStruct((1, N), F32), compiler_params=_params(("parallel", "arbitrary")),
    )(a)


def _gm_mask():
    i = lax.broadcasted_iota(jnp.int32, (128, 128), 0) // CHUNK
    j = lax.broadcasted_iota(jnp.int32, (128, 128), 1) // CHUNK
    return i >= j


def _gmlp_common(z, lng, lnb):
    gz = _gelu(z)
    u, v = gz[:, :GM_W], gz[:, GM_W:]
    vh, r = _ln_stats(v)
    return u, vh, r, vh * lng + lnb


def _gmlp_fwd(z, lng, lnb, ws, bst, name):
    S = z.shape[0]
    t = min(ROW_TILE, S)

    def body(z_ref, lg_ref, lb_ref, ws_ref, bs_ref, y_ref):
        u, _, _, vn = _gmlp_common(z_ref[...], lg_ref[...], lb_ref[...])
        mask = _gm_mask()
        vb = vn.astype(MXU)
        for g in range(GM_G):
            w = jnp.where(mask, ws_ref[g], jnp.zeros_like(ws_ref[g]))
            bias = bs_ref[:, g:g + 1]
            for blk in range(t // 128):
                rs, cs = slice(blk * 128, (blk + 1) * 128), slice(g * 128, (g + 1) * 128)
                f = _dot(w, vb[rs, cs]) + bias
                y_ref[rs, cs] = (u[rs, cs] * f).astype(y_ref.dtype)

    return pl.pallas_call(
        body, name=name, grid=(S // t,),
        in_specs=[_row_spec(t, 2 * GM_W, P_GM // (2 * GM_W)), _vec_spec(GM_W), _vec_spec(GM_W),
                  _full_spec((GM_G, 128, 128)), _full_spec((128, GM_G))],
        out_specs=_row_spec(t, GM_W), out_shape=jax.ShapeDtypeStruct((S, GM_W), MXU),
        compiler_params=_params(("parallel",)),
    )(z, lng, lnb, ws, bst)


def _gmlp_bwd(z, dya, lng, lnb, ws, wst, bst, dz_in, name):
    S = z.shape[0]
    t = min(ROW_TILE, S)

    def body(z_ref, dy_ref, lg_ref, lb_ref, ws_ref, wst_ref, bs_ref, _dz_in, dz_ref, dlg_ref, dlb_ref, dws_ref, dbs_ref, sum_ref):
        first = pl.program_id(0) == 0
        zz = z_ref[...]
        u, vh, r, vn = _gmlp_common(zz, lg_ref[...], lb_ref[...])
        dy = dy_ref[...]
        mask = _gm_mask()
        maskt = lax.broadcasted_iota(jnp.int32, (128, 128), 1) // CHUNK >= lax.broadcasted_iota(jnp.int32, (128, 128), 0) // CHUNK
        lane = lax.broadcasted_iota(jnp.int32, (128, 128), 1)
        vb = vn.astype(MXU)
        dfb = (dy * u).astype(MXU)
        df32 = dy * u
        dbs = jnp.zeros((128, 128), F32)
        du_cols, dvn_cols = [], []
        for g in range(GM_G):
            w = jnp.where(mask, ws_ref[g], jnp.zeros_like(ws_ref[g]))
            wt = jnp.where(maskt, wst_ref[g], jnp.zeros_like(wst_ref[g]))
            bias = bs_ref[:, g:g + 1]
            cs = slice(g * 128, (g + 1) * 128)
            dw = jnp.zeros((128, 128), F32)
            du_rows, dvn_rows = [], []
            for blk in range(t // 128):
                rs = slice(blk * 128, (blk + 1) * 128)
                f = _dot(w, vb[rs, cs]) + bias
                du_rows.append(dy[rs, cs] * f)
                dvn_rows.append(_dot(wt, dfb[rs, cs]))
                dw = dw + _dot_nt(dfb[rs, cs], vb[rs, cs])
                dbs = dbs + jnp.where(lane == g, jnp.sum(df32[rs, cs], axis=1, keepdims=True), 0.0)
            _acc(dws_ref.at[g], first, jnp.where(mask, dw, 0.0))
            du_cols.append(jnp.concatenate(du_rows, axis=0))
            dvn_cols.append(jnp.concatenate(dvn_rows, axis=0))
        _acc(dbs_ref, first, dbs)
        du = jnp.concatenate(du_cols, axis=1)
        dvn = jnp.concatenate(dvn_cols, axis=1)
        _acc(dlg_ref, first, _colsum(dvn * vh))
        _acc(dlb_ref, first, _colsum(dvn))
        dv = _ln_bwd(dvn * lg_ref[...], vh, r)
        dzz = jnp.concatenate([du, dv], axis=1) * _gelu_grad(zz)
        dz_ref[...] = dzz.astype(dz_ref.dtype)
        _acc(sum_ref, first, _colsum(dzz))

    vec = jax.ShapeDtypeStruct((1, GM_W), F32)
    return pl.pallas_call(
        body, name=name, grid=(S // t,),
        in_specs=[_row_spec(t, 2 * GM_W, P_GM // (2 * GM_W)), _row_spec(t, GM_W), _vec_spec(GM_W), _vec_spec(GM_W),
                  _full_spec((GM_G, 128, 128)), _full_spec((GM_G, 128, 128)), _full_spec((128, GM_G)), _ANY],
        out_specs=[_row_spec(t, 2 * GM_W, P_GM // (2 * GM_W)), _vec_spec(GM_W), _vec_spec(GM_W), _full_spec((GM_G, 128, 128)),
                   _full_spec((128, 128)), _vec_spec(2 * GM_W)],
        out_shape=[jax.ShapeDtypeStruct((S, NP), MXU), vec, vec,
                   jax.ShapeDtypeStruct((GM_G, 128, 128), F32), jax.ShapeDtypeStruct((128, 128), F32),
                   jax.ShapeDtypeStruct((1, 2 * GM_W), F32)],
        input_output_aliases=_dz_alias(8), compiler_params=_params(("arbitrary",)),
    )(z, dya, lng, lnb, ws, wst, bst, dz_in)


def _rms(x, g):
    r = lax.rsqrt(jnp.mean(x * x, axis=-1, keepdims=True) + RMS_EPS)
    xh = x * r
    return xh, r, xh * g


def _mla_specs(t):
    return [_row_spec(t, Q_RANK, P_Q // Q_RANK), _row_spec(t, KV_RANK, P_KV // 128), _row_spec(t, 128, P_KA // 128),
            _row_spec(t, 128, P_KB // 128), _row_spec(t, 128), _row_spec(t, 128), _vec_spec(Q_RANK), _vec_spec(KV_RANK),
            _full_spec((Q_RANK, D)), _full_spec((Q_RANK, D)), _full_spec((KV_RANK, D)), _full_spec((HEADS, KV_RANK, VD))]


def _mla_prep_fwd(z, cos, sin, gq, gkv, w1, w2, wk, wv, wvt, name):
    S = z.shape[0]
    t = min(ROW_TILE, S)

    def body(zq_ref, zkv_ref, zka_ref, zkb_ref, cos_ref, sin_ref, gq_ref, gkv_ref, w1_ref, w2_ref, wk_ref, wv_ref, wvt_ref,
             q_ref, k_ref, v_ref, vt_ref):
        cos_, sin_ = cos_ref[...], sin_ref[...]
        cq = cos_ + jnp.where(lax.broadcasted_iota(jnp.int32, cos_.shape, 1) < NOPE, 1.0, 0.0)
        qn = _rms(zq_ref[...], gq_ref[...])[2].astype(MXU)
        q1, q2 = _dot(qn, w1_ref[...]), _dot(qn, w2_ref[...])
        kvn = _rms(zkv_ref[...], gkv_ref[...])[2].astype(MXU)
        kn = _dot(kvn, wk_ref[...])
        krot = zka_ref[...] * cos_ + zkb_ref[...] * sin_
        for h in range(HEADS):
            hs = slice(h * 128, (h + 1) * 128)
            q_ref[h] = (q1[:, hs] * cq + q2[:, hs] * sin_).astype(q_ref.dtype)
            k_ref[h] = (kn[:, hs] + krot).astype(k_ref.dtype)
            v_ref[h] = _dot(kvn, wv_ref[h]).astype(v_ref.dtype)
            vt_ref[h] = _dot_nt(wvt_ref[h], kvn).astype(vt_ref.dtype)

    hspec = lambda w: pl.BlockSpec((HEADS, t, w), lambda i: (0, i, 0))
    return pl.pallas_call(
        body, name=name, grid=(S // t,), in_specs=_mla_specs(t) + [_full_spec((HEADS, VD, KV_RANK))],
        out_specs=[hspec(128), hspec(128), hspec(VD), pl.BlockSpec((HEADS, VD, t), lambda i: (0, 0, i))],
        out_shape=[jax.ShapeDtypeStruct((HEADS, S, 128), MXU), jax.ShapeDtypeStruct((HEADS, S, 128), MXU),
                   jax.ShapeDtypeStruct((HEADS, S, VD), MXU), jax.ShapeDtypeStruct((HEADS, VD, S), MXU)],
        compiler_params=_params(("parallel",)),
    )(z, z, z, z, cos, sin, gq, gkv, w1, w2, wk, wv, wvt)


def _mla_prep_bwd(z, cos, sin, gq, gkv, w1, w2, wk, wv, dq, dk, dv, dz_in, name):
    S = z.shape[0]
    t = min(ROW_TILE, S)

    def body(zq_ref, zkv_ref, zka_ref, zkb_ref, cos_ref, sin_ref, gq_ref, gkv_ref, w1_ref, w2_ref, wk_ref, wv_ref,
             dq_ref, dk_ref, dv_ref, _dz_in, dz_ref, dgq_ref, dgkv_ref, dw1_ref, dw2_ref, dwk_ref, dwv_ref, sum_ref):
        first = pl.program_id(0) == 0
        cos_, sin_ = cos_ref[...], sin_ref[...]
        cq = cos_ + jnp.where(lax.broadcasted_iota(jnp.int32, cos_.shape, 1) < NOPE, 1.0, 0.0)
        gq_, gkv_ = gq_ref[...], gkv_ref[...]
        xhq, rq, qn32 = _rms(zq_ref[...], gq_)
        qn = qn32.astype(MXU)
        dq1 = jnp.concatenate([dq_ref[h] * cq for h in range(HEADS)], axis=1).astype(MXU)
        dq2 = jnp.concatenate([dq_ref[h] * sin_ for h in range(HEADS)], axis=1).astype(MXU)
        dqn = _dot_nt(dq1, w1_ref[...]) + _dot_nt(dq2, w2_ref[...])
        _acc(dw1_ref, first, _dot_tn(qn, dq1))
        _acc(dw2_ref, first, _dot_tn(qn, dq2))
        _acc(dgq_ref, first, _colsum(dqn * xhq))
        dxn = dqn * gq_
        dzq = rq * (dxn - xhq * jnp.mean(dxn * xhq, axis=-1, keepdims=True))

        xhk, rk, kvn32 = _rms(zkv_ref[...], gkv_)
        kvn = kvn32.astype(MXU)
        dks = [dk_ref[h] for h in range(HEADS)]
        dkall = jnp.concatenate(dks, axis=1).astype(MXU)
        dkrot = functools.reduce(lambda p, q_: p + q_, dks)
        dkvn = _dot_nt(dkall, wk_ref[...])
        _acc(dwk_ref, first, _dot_tn(kvn, dkall))
        for h in range(HEADS):
            dvb = dv_ref[h].astype(MXU)
            dkvn = dkvn + _dot_nt(dvb, wv_ref[h])
            _acc(dwv_ref.at[h], first, _dot_tn(kvn, dvb))
        _acc(dgkv_ref, first, _colsum(dkvn * xhk))
        dxk = dkvn * gkv_
        dzkv = rk * (dxk - xhk * jnp.mean(dxk * xhk, axis=-1, keepdims=True))
        dzm = jnp.concatenate([dzq, dzkv, dkrot * cos_, dkrot * sin_], axis=1)
        dz_ref[...] = dzm.astype(dz_ref.dtype)
        _acc(sum_ref, first, _colsum(dzm))

    hspec = lambda w: pl.BlockSpec((HEADS, t, w), lambda i: (0, i, 0))
    sds = jax.ShapeDtypeStruct
    return pl.pallas_call(
        body, name=name, grid=(S // t,),
        in_specs=_mla_specs(t) + [hspec(128), hspec(128), hspec(VD), _ANY],
        out_specs=[_row_spec(t, 640, P_Q // 640), _vec_spec(Q_RANK), _vec_spec(KV_RANK), _full_spec((Q_RANK, D)),
                   _full_spec((Q_RANK, D)), _full_spec((KV_RANK, D)), _full_spec((HEADS, KV_RANK, VD)), _vec_spec(640)],
        out_shape=[sds((S, NP), MXU), sds((1, Q_RANK), F32), sds((1, KV_RANK), F32), sds((Q_RANK, D), F32),
                   sds((Q_RANK, D), F32), sds((KV_RANK, D), F32), sds((HEADS, KV_RANK, VD), F32), sds((1, 640), F32)],
        input_output_aliases=_dz_alias(16), compiler_params=_params(("arbitrary",)),
    )(z, z, z, z, cos, sin, gq, gkv, w1, w2, wk, wv, dq, dk, dv, dz_in)


def _chunk_mask(t, transposed):
    r = lax.broadcasted_iota(jnp.int32, (t, t), 0) // CHUNK
    c = lax.broadcasted_iota(jnp.int32, (t, t), 1) // CHUNK
    return (r <= c) if transposed else (c <= r)


def _attn_fwd(q, k, vt, name, gather=None):
    S = q.shape[1]
    t = min(ATT_TILE, S)
    n = S // t

    hb = 4
    nh = HEADS // hb

    def body(*refs):
        if gather is None:
            q_ref, k_ref, vt_ref, ob_ref, o_ref, lse_ref = refs
        else:
            q_ref, k_ref, vt_ref, w_ref, ob_ref, o_ref, lse_ref, g_ref, send_sems, recv_sems = refs
            hi, step = pl.program_id(0), pl.program_id(1)
            for stage, at in (("start", (0, 0)), ("forward", (nh // 2, 0))):
                pl.when((hi == at[0]) & (step == at[1]))(functools.partial(_ag_stage, stage, w_ref, g_ref, send_sems, recv_sems))
        qi = pl.program_id(1)
        qbs = [q_ref[g] for g in range(hb)]

        def block(j, carries, masked):
            cols = pl.ds(pl.multiple_of(j * t, t), t)
            out = []
            for g in range(hb):
                m, l, acc = carries[g]
                st = _dot_nt(k_ref[g, cols, :], qbs[g]) * ATT_SCALE2
                if masked:
                    st = jnp.where(_chunk_mask(t, True), st, -jnp.inf)
                m_new = jnp.maximum(m, jnp.max(st, axis=0, keepdims=True))
                p = jnp.exp2(st - m_new)
                alpha = jnp.exp2(m - m_new)
                l = alpha * l + jnp.sum(p, axis=0, keepdims=True)
                acc = alpha * acc + _dot(vt_ref[g, :, cols], p.astype(MXU))
                out.append((m_new, l, acc))
            return tuple(out)

        init = tuple((jnp.full((1, t), -jnp.inf, F32), jnp.zeros((1, t), F32), jnp.zeros((VD, t), F32)) for _ in range(hb))
        carries = lax.fori_loop(0, qi, lambda j, c: block(j, c, False), init)
        for g, (m, l, acc) in enumerate(block(qi, carries, True)):
            o = (acc / l).T
            o_ref[g] = o
            ob_ref[g] = o.astype(ob_ref.dtype)
            lse_ref[g] = m + jnp.log2(l)
        if gather is not None:
            pl.when((hi == nh - 1) & (step == n - 1))(functools.partial(_ag_stage, "finish", w_ref, g_ref, send_sems, recv_sems))

    qspec = lambda w: pl.BlockSpec((hb, t, w), lambda h, i: (h, i, 0))
    sds = jax.ShapeDtypeStruct
    comm = gather is not None
    return pl.pallas_call(
        body, name=name, grid=(nh, n),
        in_specs=[qspec(128), pl.BlockSpec((hb, S, 128), lambda h, i: (h, 0, 0)), pl.BlockSpec((hb, VD, S), lambda h, i: (h, 0, 0))]
        + ([_ANY] if comm else []),
        out_specs=[qspec(VD), qspec(VD), pl.BlockSpec((hb, 1, t), lambda h, i: (h, 0, i))] + ([_ANY] if comm else []),
        out_shape=[sds((HEADS, S, VD), MXU), sds((HEADS, S, VD), F32), sds((HEADS, 1, S), F32)]
        + ([sds((4,) + gather.shape, gather.dtype)] if comm else []),
        scratch_shapes=list(_AG_SEMS) if comm else [],
        compiler_params=_params(("arbitrary", "arbitrary") if comm else ("parallel", "arbitrary")),
    )(*((q, k, vt) + ((gather,) if comm else ())))


def _attn_delta(o, do, name):
    S = o.shape[1]
    t = min(ROW_TILE, S)

    def body(o_ref, do_ref, d_ref):
        for h in range(HEADS):
            d_ref[h] = jnp.sum(o_ref[h] * do_ref[h], axis=1, keepdims=True)

    hspec = lambda w: pl.BlockSpec((HEADS, t, w), lambda i: (0, i, 0))
    return pl.pallas_call(
        body, name=name, grid=(S // t,), in_specs=[hspec(VD), hspec(VD)], out_specs=hspec(1),
        out_shape=jax.ShapeDtypeStruct((HEADS, S, 1), F32), compiler_params=_params(("parallel",)),
    )(o, do)


def _scatter_stage(stage, g_ref, land_ref, send_sems, recv_sems):
    rh = g_ref.shape[1] // 2
    x, y, c = _coords()
    cps = []
    for p in range(1, 8):
        tx, ty, tc = x ^ ((p >> 2) & 1), y ^ ((p >> 1) & 1), c ^ (p & 1)
        rows = pl.ds(pl.multiple_of(tc * rh, FLAT_ALIGN // 2), rh)
        cps.append(pltpu.make_async_remote_copy(src_ref=g_ref.at[2 * tx + ty, rows, :], dst_ref=land_ref.at[p - 1],
                                                send_sem=send_sems.at[p - 1], recv_sem=recv_sems.at[p - 1], device_id=(tx, ty, tc),
                                                device_id_type=MESH))
    for cp in cps:
        if stage == "start":
            cp.start()
        else:
            cp.wait()


def _attn_bwd(q, k, v, do, lse_row, d_row, name, scatter=None):
    S = q.shape[1]
    t = min(ATT_TILE, S)
    n = S // t

    def body(*refs):
        if scatter is None:
            q_ref, do_ref, lse_ref, d_ref, k_ref, v_ref, dq_ref, dk_ref, dv_ref, dk_s, dv_s = refs
        else:
            q_ref, do_ref, lse_ref, d_ref, k_ref, v_ref, g_ref, dq_ref, dk_ref, dv_ref, land_ref, dk_s, dv_s, send_sems, recv_sems = refs
            pl.when((pl.program_id(0) == 0) & (pl.program_id(1) == 0))(
                functools.partial(_scatter_stage, "start", g_ref, land_ref, send_sems, recv_sems))
        ki = pl.program_id(1)

        @pl.when(ki == 0)
        def _():
            dq_ref[...] = jnp.zeros(dq_ref.shape, F32)

        dk_s[...] = jnp.zeros(dk_s.shape, F32)
        dv_s[...] = jnp.zeros(dv_s.shape, F32)
        kbs, vbs = [k_ref[g] for g in range(hb)], [v_ref[g] for g in range(hb)]

        def step(qi, masked):
            rows = pl.ds(pl.multiple_of(qi * t, t), t)
            for g in range(hb):
                qb, dob = q_ref[g, rows, :], do_ref[g, rows, :]
                st = _dot_nt(kbs[g], qb) * ATT_SCALE2
                if masked:
                    st = jnp.where(_chunk_mask(t, True), st, -jnp.inf)
                pt = jnp.exp2(st - lse_ref[g, :, rows])
                dv_s[g] += _dot(pt.astype(MXU), dob)
                dpt = _dot_nt(vbs[g], dob)
                dst = (pt * (dpt - d_ref[g, :, rows]) * ATT_SCALE).astype(MXU)
                dk_s[g] += _dot(dst, qb)
                dq_ref[g, rows, :] += _dot_tn(dst, kbs[g])

        step(ki, True)

        def loop(qi, c):
            step(qi, False)
            return c

        lax.fori_loop(ki + 1, n, loop, 0)
        dk_ref[...] = dk_s[...]
        dv_ref[...] = dv_s[...]
        if scatter is not None:
            pl.when((pl.program_id(0) == HEADS // hb - 1) & (ki == n - 1))(
                functools.partial(_scatter_stage, "finish", g_ref, land_ref, send_sems, recv_sems))

    hb = 2
    head = lambda *shape: pl.BlockSpec((hb,) + shape, lambda h, j: (h, 0, 0))
    kmap = lambda h, j: (h, j, 0)
    sds = jax.ShapeDtypeStruct
    comm = scatter is not None
    return pl.pallas_call(
        body, name=name, grid=(HEADS // hb, n),
        in_specs=[head(S, 128), head(S, VD), head(1, S), head(1, S), pl.BlockSpec((hb, t, 128), kmap), pl.BlockSpec((hb, t, VD), kmap)]
        + ([_ANY] if comm else []),
        out_specs=[head(S, 128), pl.BlockSpec((hb, t, 128), kmap), pl.BlockSpec((hb, t, VD), kmap)] + ([_ANY] if comm else []),
        out_shape=[sds((HEADS, S, 128), F32), sds((HEADS, S, 128), F32), sds((HEADS, S, VD), F32)]
        + ([sds((7, scatter.shape[1] // 2, scatter.shape[2]), scatter.dtype)] if comm else []),
        scratch_shapes=[pltpu.VMEM((hb, t, 128), F32), pltpu.VMEM((hb, t, VD), F32)]
        + ([pltpu.SemaphoreType.DMA((7,)), pltpu.SemaphoreType.DMA((7,))] if comm else []),
        compiler_params=_params(("arbitrary", "arbitrary") if comm else ("parallel", "arbitrary")),
    )(*((q, do, lse_row, d_row, k, v) + ((scatter,) if comm else ())))


def _shift_down(x, prev8, d):
    xr = pltpu.roll(x, d, 0)
    r8 = lax.broadcasted_iota(jnp.int32, prev8.shape, 0)
    top = jnp.where(r8 < d, pltpu.roll(prev8, d, 0), xr[0:8])
    return jnp.concatenate([top, xr[8:]], axis=0)


def _shift_up(x, next8, d):
    n = x.shape[0]
    xr = pltpu.roll(x, n - d, 0)
    r8 = lax.broadcasted_iota(jnp.int32, next8.shape, 0)
    bot = jnp.where(r8 >= 8 - d, pltpu.roll(next8, 8 - d, 0), xr[n - 8:])
    return jnp.concatenate([xr[:n - 8], bot], axis=0)


def _log1p(u):
    return jnp.where(u < 0.01, u * (1.0 - u * (0.5 - u * (1.0 / 3.0 - 0.25 * u))), jnp.log(1.0 + u))


def _neg_expm1(y):
    series = -y * (1.0 + 0.5 * y * (1.0 + (1.0 / 3.0) * y * (1.0 + 0.25 * y)))
    return jnp.where(y > -0.05, series, 1.0 - jnp.exp(y))


def _softplus_neg(lam):
    return jnp.maximum(-lam, 0.0) + _log1p(jnp.exp(-jnp.abs(lam)))


def _lru_gates(x, prev8, cw, cb, wr, br, wi, bi, lam):
    xs1, xs2, xs3 = _shift_down(x, prev8, 1), _shift_down(x, prev8, 2), _shift_down(x, prev8, 3)
    xc = cb + cw[0:1] * xs3 + cw[1:2] * xs2 + cw[2:3] * xs1 + cw[3:4] * x
    xcb = xc.astype(MXU)
    r = _sigmoid(_dot(xcb, wr) + br)
    ig = _sigmoid(_dot(xcb, wi) + bi)
    sp = _softplus_neg(lam)
    log_a = -8.0 * r * sp
    a = jnp.exp(log_a)
    gb = jnp.sqrt(_neg_expm1(2.0 * log_a))
    return (xs1, xs2, xs3), xc, xcb, r, ig, sp, a, gb


def _lru_fwd(z, cw, cb, wr, br, wi, bi, lam, name):
    S = z.shape[0]
    t = min(ROW_TILE, S)

    def body(zx_ref, zg_ref, cw_ref, cb_ref, wr_ref, br_ref, wi_ref, bi_ref, lam_ref, y_ref, h_ref, xp_s, hc_s):
        @pl.when(pl.program_id(0) == 0)
        def _():
            xp_s[...] = jnp.zeros(xp_s.shape, F32)
            hc_s[...] = jnp.zeros(hc_s.shape, F32)

        x = zx_ref[...]
        _, xc, _, _, ig, _, a, gb = _lru_gates(x, xp_s[...], cw_ref[...], cb_ref[...], wr_ref[...], br_ref[...],
                                               wi_ref[...], bi_ref[...], lam_ref[...])
        b = gb * (ig * xc)
        rows = lax.broadcasted_iota(jnp.int32, a.shape, 0)
        d = 1
        while d < t:
            ar, brr = pltpu.roll(a, d, 0), pltpu.roll(b, d, 0)
            ok = rows >= d
            b = jnp.where(ok, a * brr, 0.0) + b
            a = jnp.where(ok, a * ar, a)
            d *= 2
        h = a * hc_s[7:8, :] + b
        h_ref[...] = h
        y_ref[...] = (h * _gelu(zg_ref[...])).astype(y_ref.dtype)
        hc_s[...] = h[t - 8:, :]
        xp_s[...] = x[t - 8:, :]

    w = LRU_W
    return pl.pallas_call(
        body, name=name, grid=(S // t,),
        in_specs=[_row_spec(t, w, P_LX // w), _row_spec(t, w, P_LG // w), _full_spec((4, w)), _vec_spec(w), _full_spec((w, w)),
                  _vec_spec(w), _full_spec((w, w)), _vec_spec(w), _vec_spec(w)],
        out_specs=[_row_spec(t, w), _row_spec(t, w)],
        out_shape=[jax.ShapeDtypeStruct((S, w), MXU), jax.ShapeDtypeStruct((S, w), F32)],
        scratch_shapes=[pltpu.VMEM((8, w), F32), pltpu.VMEM((8, w), F32)],
        compiler_params=_params(("arbitrary",)),
    )(z, z, cw, cb, wr, br, wi, bi, lam)


def _lru_bwd(z, h, dy, cw, cb, wr, br, wi, bi, lam, dz_in, name):
    S = z.shape[0]
    t = min(ROW_TILE, S)
    n = S // t
    w = LRU_W

    def body(zx_ref, zxp_ref, zg_ref, h_ref, hp_ref, dy_ref, cw_ref, cb_ref, wr_ref, br_ref, wi_ref, bi_ref, lam_ref, _dz_in,
             dz_ref, dcw_ref, dcb_ref, dwr_ref, dbr_ref, dwi_ref, dbi_ref, dlam_ref, sum_ref, gc_s, dn_s):
        i = pl.program_id(0)
        first = i == 0
        j = n - 1 - i

        @pl.when(first)
        def _():
            gc_s[...] = jnp.zeros(gc_s.shape, F32)
            dn_s[...] = jnp.zeros(dn_s.shape, F32)

        live = (j > 0).astype(F32)
        xprev8, hprev8 = zxp_ref[...] * live, hp_ref[...] * live
        x, zg, hh, dy_ = zx_ref[...], zg_ref[...], h_ref[...], dy_ref[...]
        cw_, lam_ = cw_ref[...], lam_ref[...]
        (xs1, xs2, xs3), xc, xcb, r, ig, sp, a, gb = _lru_gates(x, xprev8, cw_, cb_ref[...], wr_ref[...], br_ref[...],
                                                                wi_ref[...], bi_ref[...], lam_)
        hm1 = _shift_down(hh, hprev8, 1)
        dh = dy_ * _gelu(zg)
        dzg = dy_ * hh * _gelu_grad(zg)
        rows = lax.broadcasted_iota(jnp.int32, a.shape, 0)
        last = rows == t - 1
        ca = jnp.where(last, 0.0, pltpu.roll(a, t - 1, 0))
        g = dh + jnp.where(last, gc_s[0:1, :], 0.0)
        d = 1
        while d < t:
            ok = rows < t - d
            g = g + jnp.where(ok, ca * pltpu.roll(g, t - d, 0), 0.0)
            ca = jnp.where(ok, ca * pltpu.roll(ca, t - d, 0), 0.0)
            d *= 2
        gc_s[...] = a[0:8, :] * g[0:8, :]
        da = g * hm1
        dgb = g * (ig * xc)
        dub = g * gb
        di = dub * xc
        dxc = dub * ig
        dlog_a = da * a - dgb * (a * a) / gb
        dr = dlog_a * (-8.0 * sp)
        _acc(dlam_ref, first, _colsum(dlog_a * (-8.0 * r)) * (-_sigmoid(-lam_)))
        dpr = dr * r * (1.0 - r)
        dpi = di * ig * (1.0 - ig)
        _acc(dbr_ref, first, _colsum(dpr))
        _acc(dbi_ref, first, _colsum(dpi))
        dprb, dpib = dpr.astype(MXU), dpi.astype(MXU)
        _acc(dwr_ref, first, _dot_tn(xcb, dprb))
        _acc(dwi_ref, first, _dot_tn(xcb, dpib))
        dxc = dxc + _dot_nt(dprb, wr_ref[...]) + _dot_nt(dpib, wi_ref[...])
        _acc(dcb_ref, first, _colsum(dxc))
        _acc(dcw_ref, first, jnp.concatenate([_colsum(dxc * xs3), _colsum(dxc * xs2), _colsum(dxc * xs1), _colsum(dxc * x)], axis=0))
        nxt = dn_s[...]
        dx = cw_[3:4] * dxc + cw_[2:3] * _shift_up(dxc, nxt, 1) + cw_[1:2] * _shift_up(dxc, nxt, 2) + cw_[0:1] * _shift_up(dxc, nxt, 3)
        dn_s[...] = dxc[0:8, :]
        dz_ref[...] = jnp.concatenate([dx, dzg], axis=1).astype(dz_ref.dtype)
        _acc(sum_ref, first, jnp.concatenate([_colsum(dx), _colsum(dzg)], axis=1))

    rev = lambda col: pl.BlockSpec((t, w), lambda i, c=col: (n - 1 - i, c))
    prev8 = lambda col: pl.BlockSpec((8, w), lambda i, c=col: (jnp.maximum((n - 1 - i) * (t // 8) - 1, 0), c))
    vec = jax.ShapeDtypeStruct((1, w), F32)
    sds = jax.ShapeDtypeStruct
    return pl.pallas_call(
        body, name=name, grid=(n,),
        in_specs=[rev(P_LX // w), prev8(P_LX // w), rev(P_LG // w), rev(0), prev8(0), rev(0), _full_spec((4, w)), _vec_spec(w),
                  _full_spec((w, w)), _vec_spec(w), _full_spec((w, w)), _vec_spec(w), _vec_spec(w), _ANY],
        out_specs=[pl.BlockSpec((t, 2 * w), lambda i: (n - 1 - i, P_LX // (2 * w))), _full_spec((4, w)), _vec_spec(w),
                   _full_spec((w, w)), _vec_spec(w), _full_spec((w, w)), _vec_spec(w), _vec_spec(w), _vec_spec(2 * w)],
        out_shape=[sds((S, NP), MXU), sds((4, w), F32), vec, sds((w, w), F32), vec, sds((w, w), F32), vec, vec, sds((1, 2 * w), F32)],
        scratch_shapes=[pltpu.VMEM((8, w), F32), pltpu.VMEM((8, w), F32)],
        input_output_aliases=_dz_alias(14), compiler_params=_params(("arbitrary",)),
    )(z, z, z, h, h, dy, cw, cb, wr, br, wi, bi, lam, dz_in)


def _branch_fwd(z, ya, o, yc, wa, wb, wc, name):
    S = z.shape[0]
    t = min(ROW_TILE, S)

    def body(ga_ref, gb_ref, gc_ref, ya_ref, o_ref, yc_ref, wa_ref, wb_ref, wc_ref, m_ref, pa_ref, pb_ref, pc_ref):
        pa = _dot(ya_ref[...], wa_ref[...])
        pc = _dot(yc_ref[...], wc_ref[...])
        pb = _dot(o_ref[0], wb_ref[0])
        for h in range(1, HEADS):
            pb = pb + _dot(o_ref[h], wb_ref[h])
        pa_ref[...], pb_ref[...], pc_ref[...] = pa, pb, pc
        m = _sigmoid(ga_ref[...]) * pa + _sigmoid(gb_ref[...]) * pb + _sigmoid(gc_ref[...]) * pc
        m_ref[...] = m.astype(m_ref.dtype)

    g0 = P_GATE // D
    sds = jax.ShapeDtypeStruct
    return pl.pallas_call(
        body, name=name, grid=(S // t,),
        in_specs=[_row_spec(t, D, g0), _row_spec(t, D, g0 + 1), _row_spec(t, D, g0 + 2), _row_spec(t, GM_W),
                  pl.BlockSpec((HEADS, t, VD), lambda i: (0, i, 0)), _row_spec(t, LRU_W),
                  _full_spec((GM_W, D)), _full_spec((HEADS, VD, D)), _full_spec((LRU_W, D))],
        out_specs=[_row_spec(t, D)] * 4,
        out_shape=[sds((S, D), MXU), sds((S, D), F32), sds((S, D), F32), sds((S, D), F32)],
        compiler_params=_params(("parallel",)),
    )(z, z, z, ya, o, yc, wa, wb, wc)


def _dz_alias(n_inputs):
    return {n_inputs - 1: 0}


def _branch_bwd(z, dm, pa, pb, pc, name):
    S = z.shape[0]
    t = min(ROW_TILE, S)

    def body(ga_ref, gb_ref, gc_ref, dm_ref, pa_ref, pb_ref, pc_ref, dz_ref, da_ref, db_ref, dc_ref, sum_ref):
        first = pl.program_id(0) == 0
        dm_ = dm_ref[...]
        for n_, (g_ref, p_ref, d_ref) in enumerate(((ga_ref, pa_ref, da_ref), (gb_ref, pb_ref, db_ref), (gc_ref, pc_ref, dc_ref))):
            gt = _sigmoid(g_ref[...])
            d_ref[...] = (dm_ * gt).astype(d_ref.dtype)
            dzg = dm_ * p_ref[...] * gt * (1.0 - gt)
            dz_ref[:, n_ * D:(n_ + 1) * D] = dzg.astype(dz_ref.dtype)
            _acc(sum_ref.at[:, n_ * D:(n_ + 1) * D], first, _colsum(dzg))

    g0 = P_GATE // D
    sds = jax.ShapeDtypeStruct
    return pl.pallas_call(
        body, name=name, grid=(S // t,),
        in_specs=[_row_spec(t, D, g0), _row_spec(t, D, g0 + 1), _row_spec(t, D, g0 + 2)] + [_row_spec(t, D)] * 4,
        out_specs=[_row_spec(t, 3 * D, P_GATE // (3 * D))] + [_row_spec(t, D)] * 3 + [_vec_spec(3 * D)],
        out_shape=[sds((S, NP), MXU)] + [sds((S, D), MXU)] * 3 + [sds((1, 3 * D), F32)],
        compiler_params=_params(("arbitrary",)),
    )(z, z, z, dm, pa, pb, pc)


def _heads_bwd(dpb, o, wb, name):
    S = dpb.shape[0]
    t = min(ROW_TILE, S)

    def body(dp_ref, o_ref, wb_ref, dob_ref, do_ref, dwb_ref):
        first = pl.program_id(0) == 0
        dp = dp_ref[...]
        for h in range(HEADS):
            do = _dot_nt(dp, wb_ref[h])
            do_ref[h] = do
            dob_ref[h] = do.astype(dob_ref.dtype)
            _acc(dwb_ref.at[h], first, _dot_tn(o_ref[h], dp))

    hspec = pl.BlockSpec((HEADS, t, VD), lambda i: (0, i, 0))
    sds = jax.ShapeDtypeStruct
    return pl.pallas_call(
        body, name=name, grid=(S // t,),
        in_specs=[_row_spec(t, D), hspec, _full_spec((HEADS, VD, D))],
        out_specs=[hspec, hspec, _full_spec((HEADS, VD, D))],
        out_shape=[sds((HEADS, S, VD), MXU), sds((HEADS, S, VD), F32), sds((HEADS, VD, D), F32)],
        compiler_params=_params(("arbitrary",)),
    )(dpb, o, wb)


def _mod_fwd(c_all, ada_w, name):
    n = ada_w.shape[2]

    def body(c_ref, w_ref, o_ref):
        c = c_ref[...]
        ca = (c * _sigmoid(c)).astype(MXU)
        for l in range(DEPTH):
            o_ref[l] = _dot(ca, w_ref[l].astype(MXU))

    return pl.pallas_call(body, name=name, out_shape=jax.ShapeDtypeStruct((DEPTH, 8, n), F32), compiler_params=_params())(c_all, ada_w)


def _ada_w_grad(c_all_t, dmod, name):
    n = dmod.shape[2]

    def body(c_ref, d_ref, o_ref):
        c = c_ref[...]
        ca = c * _sigmoid(c)
        for l in range(DEPTH):
            dm = d_ref[l]
            acc = ca[:, 0:1] * dm[0:1, :]
            for b in range(1, 8):
                acc = acc + ca[:, b:b + 1] * dm[b:b + 1, :]
            o_ref[l] = acc

    return pl.pallas_call(body, name=name, out_shape=jax.ShapeDtypeStruct((DEPTH, D, n), F32), compiler_params=_params())(c_all_t, dmod)


def _adamw(w, g, m, v, name):
    R, C = w.shape
    t = _tile(R, 256) if R % 8 == 0 else R
    c1, c2 = 1.0 - ADAM_B1 ** ADAM_STEP, 1.0 - ADAM_B2 ** ADAM_STEP

    def body(w_ref, g_ref, m_ref, v_ref, d_ref, nm_ref, nv_ref):
        g_ = g_ref[...]
        m_ = ADAM_B1 * m_ref[...] + (1.0 - ADAM_B1) * g_
        v_ = ADAM_B2 * v_ref[...] + (1.0 - ADAM_B2) * (g_ * g_)
        nm_ref[...] = m_
        nv_ref[...] = v_
        d_ref[...] = -ADAM_LR * ((m_ / c1) / (jnp.sqrt(v_ / c2) + ADAM_EPS) + ADAM_WD * w_ref[...])

    spec = pl.BlockSpec((t, C), lambda i: (i, 0))
    return pl.pallas_call(
        body, name=name, grid=(R // t,), in_specs=[spec] * 4, out_specs=[spec] * 3,
        out_shape=[jax.ShapeDtypeStruct((R, C), F32)] * 3, compiler_params=_params(("parallel",)),
    )(w, g, m, v)


def _sum8(a, name):
    _, R, C = a.shape
    t = _tile(R, 512) if R % 8 == 0 else R

    def body(a_ref, o_ref):
        s = a_ref[0]
        for k in range(1, 8):
            s = s + a_ref[k]
        o_ref[...] = s

    return pl.pallas_call(
        body, name=name, grid=(R // t,), in_specs=[pl.BlockSpec((8, t, C), lambda i: (0, i, 0))],
        out_specs=pl.BlockSpec((t, C), lambda i: (i, 0)), out_shape=jax.ShapeDtypeStruct((R, C), F32),
        compiler_params=_params(("parallel",)),
    )(a)


def _pair_add(shards, from_sib, half, name):
    _, rh, C = from_sib.shape
    t = _tile(rh, 384, 16)
    nb = rh // t

    def body(half_ref, a0, a1, a2, a3, b_ref, o_ref, ob_ref):
        for k, a_ref in enumerate((a0, a1, a2, a3)):
            s = a_ref[...] + b_ref[k]
            o_ref[k] = s
            ob_ref[k] = s.astype(ob_ref.dtype)

    a_spec = pl.BlockSpec((t, C), lambda i, half_ref: (half_ref[0] * nb + i, 0))
    s_spec = pl.BlockSpec((4, t, C), lambda i, half_ref: (0, i, 0))
    return pl.pallas_call(
        body, name=name,
        grid_spec=pltpu.PrefetchScalarGridSpec(num_scalar_prefetch=1, grid=(nb,), in_specs=[a_spec] * 4 + [s_spec],
                                               out_specs=[s_spec, s_spec]),
        out_shape=[jax.ShapeDtypeStruct((4, rh, C), F32), jax.ShapeDtypeStruct((4, rh, C), jnp.bfloat16)],
        compiler_params=_params(("parallel",)),
    )(half, *shards, from_sib)


def _land_sum(g, land, half, name):
    _, rh, C = land.shape
    t = _tile(rh, 256, 16)
    nb = rh // t

    def body(half_ref, g_ref, l_ref, o_ref):
        s = g_ref[...].astype(F32)
        for k in range(7):
            s = s + l_ref[k].astype(F32)
        o_ref[...] = s

    return pl.pallas_call(
        body, name=name,
        grid_spec=pltpu.PrefetchScalarGridSpec(
            num_scalar_prefetch=1, grid=(nb,),
            in_specs=[pl.BlockSpec((t, C), lambda i, h: (h[0] * nb + i, 0)), pl.BlockSpec((7, t, C), lambda i, h: (0, i, 0))],
            out_specs=pl.BlockSpec((t, C), lambda i, h: (h[0] * nb + i, 0))),
        out_shape=jax.ShapeDtypeStruct((2 * rh, C), F32), compiler_params=_params(("parallel",)),
    )(half, g, land)


def _chip_sum(own, recv, half, name):
    rh, C = own.shape
    t = _tile(rh, 512, 16)
    nb = rh // t

    def body(half_ref, a_ref, r_ref, o_ref):
        s = a_ref[...]
        for k in range(3):
            s = s + r_ref[k].astype(F32)
        o_ref[...] = s

    return pl.pallas_call(
        body, name=name,
        grid_spec=pltpu.PrefetchScalarGridSpec(
            num_scalar_prefetch=1, grid=(nb,),
            in_specs=[pl.BlockSpec((t, C), lambda i, h: (i, 0)), pl.BlockSpec((3, t, C), lambda i, h: (0, i, 0))],
            out_specs=pl.BlockSpec((t, C), lambda i, h: (h[0] * nb + i, 0))),
        out_shape=jax.ShapeDtypeStruct((2 * rh, C), F32), compiler_params=_params(("parallel",)),
    )(half, own, recv)


_ANY = pl.BlockSpec(memory_space=pl.ANY)


def _coords():
    return lax.axis_index("x"), lax.axis_index("y"), lax.axis_index("c")


def _allgather8(v, name):
    R, C = v.shape

    def body(v_ref, out_ref, send_sems, recv_sems):
        x, y, c = _coords()
        me = 4 * x + 2 * y + c
        sends = []
        for k in range(1, 8):
            bx, by, bc = (k >> 2) & 1, (k >> 1) & 1, k & 1
            peer = (x ^ bx, y ^ by, c ^ bc)
            cp = pltpu.make_async_remote_copy(src_ref=v_ref, dst_ref=out_ref.at[me], send_sem=send_sems.at[k - 1],
                                              recv_sem=recv_sems.at[k - 1], device_id=peer, device_id_type=MESH)
            cp.start()
            sends.append(cp)
        for k in range(1, 8):
            bx, by, bc = (k >> 2) & 1, (k >> 1) & 1, k & 1
            peer = (x ^ bx, y ^ by, c ^ bc)
            src = 4 * peer[0] + 2 * peer[1] + peer[2]
            pltpu.make_async_remote_copy(src_ref=v_ref, dst_ref=out_ref.at[src], send_sem=send_sems.at[k - 1],
                                         recv_sem=recv_sems.at[k - 1], device_id=peer, device_id_type=MESH).wait_recv()
        for cp in sends:
            cp.wait_send()

    out = pl.pallas_call(
        body, name=name, in_specs=[_ANY], out_specs=_ANY, out_shape=jax.ShapeDtypeStruct((8, R, C), v.dtype),
        scratch_shapes=[pltpu.SemaphoreType.DMA((7,)), pltpu.SemaphoreType.DMA((7,))],
    )(v)
    x, y, c = _coords()
    return lax.dynamic_update_slice(out, v[None], (4 * x + 2 * y + c, 0, 0))


def _chip_patterns(x, y):
    return [(1 - x, y), (x, 1 - y), (1 - x, 1 - y)]


def _ag_stage(stage, w_ref, out_ref, send_sems, recv_sems):
    rh = w_ref.shape[0] // 2
    x, y, c = _coords()
    kme = 2 * x + y
    sibling = (x, y, 1 - c)
    mine_half = pl.ds(pl.multiple_of(c * rh, FLAT_ALIGN // 2), rh)
    other_half = pl.ds(pl.multiple_of((1 - c) * rh, FLAT_ALIGN // 2), rh)
    chips = _chip_patterns(x, y)

    def copy(k, chip_idx, half, to, src=None):
        dst = out_ref.at[chip_idx, half, :]
        return pltpu.make_async_remote_copy(src_ref=dst if src is None else src, dst_ref=dst, send_sem=send_sems.at[k],
                                            recv_sem=recv_sems.at[k], device_id=to, device_id_type=MESH)

    first = [copy(j, kme, mine_half, (cx, cy, c), src=w_ref.at[mine_half, :]) for j, (cx, cy) in enumerate(chips)]
    own = pltpu.make_async_remote_copy(src_ref=w_ref, dst_ref=out_ref.at[kme], send_sem=send_sems.at[6], recv_sem=recv_sems.at[6],
                                       device_id=sibling, device_id_type=MESH)
    passed = [copy(3 + j, 2 * cx + cy, mine_half, sibling) for j, (cx, cy) in enumerate(chips)]
    if stage == "start":
        for cp in first + [own]:
            cp.start()
    elif stage == "forward":
        for j, (cx, cy) in enumerate(chips):
            copy(j, 2 * cx + cy, mine_half, (x, y, c)).wait_recv()
            passed[j].start()
    else:
        for j, (cx, cy) in enumerate(chips):
            copy(3 + j, 2 * cx + cy, other_half, (x, y, c)).wait_recv()
        own.wait_recv()
        for cp in first + [own] + passed:
            cp.wait_send()


_AG_SEMS = [pltpu.SemaphoreType.DMA((7,)), pltpu.SemaphoreType.DMA((7,))]


def _allgather_weights(w, name):
    R, C = w.shape

    def body(w_ref, out_ref, send_sems, recv_sems):
        for stage in ("start", "forward", "finish"):
            _ag_stage(stage, w_ref, out_ref, send_sems, recv_sems)

    return pl.pallas_call(
        body, name=name, in_specs=[_ANY], out_specs=_ANY, out_shape=jax.ShapeDtypeStruct((4, R, C), w.dtype),
        scratch_shapes=list(_AG_SEMS),
    )(w)


def _sibling_swap(shards, name):
    R, C = shards[0].shape
    rh = R // 2

    def body(g0, g1, g2, g3, out_ref, send_sems, recv_sems):
        x, y, c = _coords()
        other_half = pl.ds(pl.multiple_of((1 - c) * rh, 8), rh)
        cps = []
        for k, g_ref in enumerate((g0, g1, g2, g3)):
            cp = pltpu.make_async_remote_copy(src_ref=g_ref.at[other_half, :], dst_ref=out_ref.at[k], send_sem=send_sems.at[k],
                                              recv_sem=recv_sems.at[k], device_id=(x, y, 1 - c), device_id_type=MESH)
            cp.start()
            cps.append(cp)
        for cp in cps:
            cp.wait()

    return pl.pallas_call(
        body, name=name, in_specs=[_ANY] * 4, out_specs=_ANY, out_shape=jax.ShapeDtypeStruct((4, rh, C), shards[0].dtype),
        scratch_shapes=[pltpu.SemaphoreType.DMA((4,)), pltpu.SemaphoreType.DMA((4,))],
    )(*shards)


def _chip_scatter(a, name):
    _, R, C = a.shape

    def body(a_ref, out_ref, send_sems, recv_sems):
        x, y, c = _coords()
        cps = []
        for j, (cx, cy) in enumerate(_chip_patterns(x, y)):
            cp = pltpu.make_async_remote_copy(src_ref=a_ref.at[2 * cx + cy], dst_ref=out_ref.at[j], send_sem=send_sems.at[j],
                                              recv_sem=recv_sems.at[j], device_id=(cx, cy, c), device_id_type=MESH)
            cp.start()
            cps.append(cp)
        for cp in cps:
            cp.wait()

    return pl.pallas_call(
        body, name=name, in_specs=[_ANY], out_specs=_ANY, out_shape=jax.ShapeDtypeStruct((3, R, C), a.dtype),
        scratch_shapes=[pltpu.SemaphoreType.DMA((3,)), pltpu.SemaphoreType.DMA((3,))],
    )(a)


def _sibling_join(f, name):
    R, C = f.shape
    rh = R // 2

    def body(f_ref, out_ref, send_sem, recv_sem):
        x, y, c = _coords()
        mine_half = pl.ds(pl.multiple_of(c * rh, 8), rh)
        cp = pltpu.make_async_remote_copy(src_ref=f_ref.at[mine_half, :], dst_ref=out_ref.at[mine_half, :], send_sem=send_sem,
                                          recv_sem=recv_sem, device_id=(x, y, 1 - c), device_id_type=MESH)
        cp.start()
        cp.wait()

    return pl.pallas_call(
        body, name=name, in_specs=[_ANY], out_specs=_ANY, out_shape=jax.ShapeDtypeStruct((R, C), f.dtype),
        scratch_shapes=[pltpu.SemaphoreType.DMA(()), pltpu.SemaphoreType.DMA(())], input_output_aliases={0: 0},
    )(f)


_EARLY = (("in_w", 1), ("mla_wuq", 1), ("mla_wukv", 1))
_LATE = (("branch_w", 2), ("mix_out_w", 0), ("ffn_w1", 1), ("ffn_w2", 0))
_BIG = _EARLY + _LATE


def _pack(shards, group):
    flat = jnp.concatenate([shards[n].reshape(-1, FLAT_W) for n, _ in group], axis=0)
    pad = (-flat.shape[0]) % FLAT_ALIGN
    return jnp.pad(flat, ((0, pad), (0, 0)))


def _packed_rows(shapes, group):
    rows = sum(math.prod(shapes[n]) // FLAT_W for n, _ in group)
    return rows + (-rows) % FLAT_ALIGN


def _unpack(flat, shapes, group):
    out, r = {}, 0
    for n, _ in group:
        rows = math.prod(shapes[n]) // FLAT_W
        out[n] = flat[..., r:r + rows, :].reshape(flat.shape[:-2] + tuple(shapes[n]))
        r += rows
    return out


def _swap16(a):
    return jnp.concatenate([a[..., 16:32], a[..., 0:16]], axis=-1)


def _shard_cols(parts, a, b):
    w = parts[0].shape[-1]
    pieces = []
    for k, p in enumerate(parts):
        s, e = max(a, k * w), min(b, (k + 1) * w)
        if s < e:
            pieces.append(p[..., s - k * w:e - k * w])
    return pieces


def _pad_in_cols(parts):
    lead, dt = parts[0].shape[:-1], parts[0].dtype
    z = lambda n: [jnp.zeros(lead + (n,), dt)]
    seg = lambda a, b: _shard_cols(parts, a, b)
    kr = jnp.concatenate(seg(O_KR, O_LX), axis=-1)
    return jnp.concatenate(seg(O_GATE, N_IN) + seg(O_GM, O_Q) + seg(O_LX, O_LG) + seg(O_LG, O_GATE) + seg(O_Q, O_KV) + seg(O_KV, O_KR)
                           + z(64) + [kr] + z(32) + z(64) + [_swap16(kr)] + z(32), axis=-1)


_SEGS = ((O_GM, O_Q, P_GM), (O_Q, O_KV, P_Q), (O_KV, O_KR, P_KV), (O_KR, O_LX, None), (O_LX, O_LG, P_LX), (O_LG, O_GATE, P_LG),
         (O_GATE, N_IN, P_GATE))


def _unpad_in_cols(g, a=0, b=N_IN):
    pieces = []
    for lo, hi, p in _SEGS:
        s, e = max(a, lo), min(b, hi)
        if s >= e:
            continue
        if p is None:
            kr = g[..., P_KA + 64:P_KA + 96] + _swap16(g[..., P_KB + 64:P_KB + 96])
            pieces.append(kr[..., s - lo:e - lo])
        else:
            pieces.append(g[..., p + s - lo:p + e - lo])
    return jnp.concatenate(pieces, axis=-1)


def _flat128(vecs):
    flat = jnp.concatenate([v.reshape(-1) for v in vecs])
    pad = (-flat.shape[0]) % 1024
    return jnp.pad(flat, (0, pad)).reshape(-1, 128)


def _unflat(flat, shapes):
    flat = flat.reshape(-1)
    out, r = [], 0
    for s in shapes:
        n = math.prod(s)
        out.append(flat[r:r + n].reshape(s))
        r += n
    return out


_SMALL = ("ada_b", "in_b", "gm_ln_g", "gm_ln_b", "gm_ws", "gm_bs", "mla_qnorm_g", "mla_kvnorm_g", "lru_conv_w", "lru_conv_b",
          "lru_wr", "lru_br", "lru_wi", "lru_bi", "lru_lambda", "ffn_b1", "ffn_b2", "ln_g", "ln_b")
_NAMES = ("ada_w", "ada_b", "in_w", "in_b", "gm_ln_g", "gm_ln_b", "gm_ws", "gm_bs", "mla_qnorm_g", "mla_wuq", "mla_kvnorm_g",
          "mla_wukv", "lru_conv_w", "lru_conv_b", "lru_wr", "lru_br", "lru_wi", "lru_bi", "lru_lambda", "branch_w", "mix_out_w",
          "ffn_w1", "ffn_b1", "ffn_w2", "ffn_b2", "ln_g", "ln_b")


def kernel(x, c, ada_w, ada_b, in_w, in_b, gm_ln_g, gm_ln_b, gm_ws, gm_bs, mla_qnorm_g, mla_wuq, mla_kvnorm_g, mla_wukv, lru_conv_w, lru_conv_b, lru_wr, lru_br, lru_wi, lru_bi, lru_lambda, branch_w, mix_out_w, ffn_w1, ffn_b1, ffn_w2, ffn_b2, ln_g, ln_b, loss_target, m_ada_w, m_ada_b, m_in_w, m_in_b, m_gm_ln_g, m_gm_ln_b, m_gm_ws, m_gm_bs, m_mla_qnorm_g, m_mla_wuq, m_mla_kvnorm_g, m_mla_wukv, m_lru_conv_w, m_lru_conv_b, m_lru_wr, m_lru_br, m_lru_wi, m_lru_bi, m_lru_lambda, m_branch_w, m_mix_out_w, m_ffn_w1, m_ffn_b1, m_ffn_w2, m_ffn_b2, m_ln_g, m_ln_b, v_ada_w, v_ada_b, v_in_w, v_in_b, v_gm_ln_g, v_gm_ln_b, v_gm_ws, v_gm_bs, v_mla_qnorm_g, v_mla_wuq, v_mla_kvnorm_g, v_mla_wukv, v_lru_conv_w, v_lru_conv_b, v_lru_wr, v_lru_br, v_lru_wi, v_lru_bi, v_lru_lambda, v_branch_w, v_mix_out_w, v_ffn_w1, v_ffn_b1, v_ffn_w2, v_ffn_b2, v_ln_g, v_ln_b):
    W = dict(ada_w=ada_w, ada_b=ada_b, in_w=in_w, in_b=in_b, gm_ln_g=gm_ln_g, gm_ln_b=gm_ln_b, gm_ws=gm_ws, gm_bs=gm_bs,
             mla_qnorm_g=mla_qnorm_g, mla_wuq=mla_wuq, mla_kvnorm_g=mla_kvnorm_g, mla_wukv=mla_wukv, lru_conv_w=lru_conv_w,
             lru_conv_b=lru_conv_b, lru_wr=lru_wr, lru_br=lru_br, lru_wi=lru_wi, lru_bi=lru_bi, lru_lambda=lru_lambda,
             branch_w=branch_w, mix_out_w=mix_out_w, ffn_w1=ffn_w1, ffn_b1=ffn_b1, ffn_w2=ffn_w2, ffn_b2=ffn_b2, ln_g=ln_g, ln_b=ln_b)
    M = dict(ada_w=m_ada_w, ada_b=m_ada_b, in_w=m_in_w, in_b=m_in_b, gm_ln_g=m_gm_ln_g, gm_ln_b=m_gm_ln_b, gm_ws=m_gm_ws,
             gm_bs=m_gm_bs, mla_qnorm_g=m_mla_qnorm_g, mla_wuq=m_mla_wuq, mla_kvnorm_g=m_mla_kvnorm_g, mla_wukv=m_mla_wukv,
             lru_conv_w=m_lru_conv_w, lru_conv_b=m_lru_conv_b, lru_wr=m_lru_wr, lru_br=m_lru_br, lru_wi=m_lru_wi, lru_bi=m_lru_bi,
             lru_lambda=m_lru_lambda, branch_w=m_branch_w, mix_out_w=m_mix_out_w, ffn_w1=m_ffn_w1, ffn_b1=m_ffn_b1, ffn_w2=m_ffn_w2,
             ffn_b2=m_ffn_b2, ln_g=m_ln_g, ln_b=m_ln_b)
    V = dict(ada_w=v_ada_w, ada_b=v_ada_b, in_w=v_in_w, in_b=v_in_b, gm_ln_g=v_gm_ln_g, gm_ln_b=v_gm_ln_b, gm_ws=v_gm_ws,
             gm_bs=v_gm_bs, mla_qnorm_g=v_mla_qnorm_g, mla_wuq=v_mla_wuq, mla_kvnorm_g=v_mla_kvnorm_g, mla_wukv=v_mla_wukv,
             lru_conv_w=v_lru_conv_w, lru_conv_b=v_lru_conv_b, lru_wr=v_lru_wr, lru_br=v_lru_br, lru_wi=v_lru_wi, lru_bi=v_lru_bi,
             lru_lambda=v_lru_lambda, branch_w=v_branch_w, mix_out_w=v_mix_out_w, ffn_w1=v_ffn_w1, ffn_b1=v_ffn_b1, ffn_w2=v_ffn_w2,
             ffn_b2=v_ffn_b2, ln_g=v_ln_g, ln_b=v_ln_b)

    S = x.shape[1]
    xi, yi, ci = _coords()
    kme = 2 * xi + yi
    me = 4 * xi + 2 * yi + ci
    x0 = x[0]
    tgt = loss_target[0]

    assert DEPTH == 2
    shard_shapes = {n: W[n].shape[1:] for n, _ in _BIG}
    rows_e, rows_l = _packed_rows(shard_shapes, _EARLY), _packed_rows(shard_shapes, _LATE)
    wpack = lambda l, group: _pack({n: W[n][l].astype(MXU) for n, _ in group}, group)
    joined = lambda sh, n, ax: jnp.concatenate([sh[n][k] for k in range(4)], axis=ax)

    def early_weights(gathered):
        sh = _unpack(gathered, shard_shapes, _EARLY)
        wuq = joined(sh, "mla_wuq", 1).reshape(Q_RANK, HEADS, NOPE + ROPE)
        zq = lambda n: jnp.zeros((Q_RANK, HEADS, n), MXU)
        wukv = joined(sh, "mla_wukv", 1).reshape(KV_RANK, HEADS, NOPE + VD)
        return dict(
            win=_pad_in_cols([sh["in_w"][k] for k in range(4)]),
            w1q=jnp.concatenate([wuq, zq(32)], axis=-1).reshape(Q_RANK, D),
            w2q=jnp.concatenate([zq(64), _swap16(wuq[..., NOPE:]), zq(32)], axis=-1).reshape(Q_RANK, D),
            wk=jnp.concatenate([wukv[..., :NOPE], jnp.zeros((KV_RANK, HEADS, 64), MXU)], axis=-1).reshape(KV_RANK, D),
            wv=wukv[..., NOPE:].transpose(1, 0, 2), wvt=wukv[..., NOPE:].transpose(1, 2, 0))

    def late_weights(gathered):
        sh = _unpack(gathered, shard_shapes, _LATE)
        bw = joined(sh, "branch_w", 2)
        return dict(wa=bw[0], wb=bw[1].reshape(HEADS, VD, D), wc=bw[2], mix=joined(sh, "mix_out_w", 0),
                    w1=joined(sh, "ffn_w1", 1), w2=joined(sh, "ffn_w2", 0))

    LW = [early_weights(_allgather_weights(wpack(0, _EARLY), "ag_weights0")), None]
    ride_w = jnp.concatenate([wpack(0, _LATE), wpack(1, _EARLY), wpack(1, _LATE)], axis=0)
    ride_w = jnp.pad(ride_w, ((0, (-ride_w.shape[0]) % RIDE_ALIGN), (0, 0)))

    small1 = _allgather8(_flat128([c, ln_g, ln_b, lru_conv_w]), "ag_small")
    per_dev = [_unflat(small1[d], [c.shape, ln_g.shape, ln_b.shape, lru_conv_w.shape]) for d in range(8)]
    c_all = jnp.concatenate([p[0] for p in per_dev], axis=0)
    chip = lambda i: [per_dev[2 * k][i] for k in range(4)]
    ln_g_f, ln_b_f, conv_w_f = (jnp.concatenate(chip(1), axis=2), jnp.concatenate(chip(2), axis=2), jnp.concatenate(chip(3), axis=2))

    mod_sh = _mod_fwd(c_all, ada_w, "mod_fwd")
    mod_all = _allgather8(_flat128([mod_sh]), "ag_mod")
    mod_parts = [_unflat(mod_all[2 * k], [mod_sh.shape])[0] for k in range(4)]
    mod = jnp.concatenate(mod_parts, axis=2)
    mod = lax.dynamic_index_in_dim(mod, me, axis=1, keepdims=False) + ada_b

    pos = jnp.arange(S, dtype=F32)
    inv = ROPE_BASE ** (-jnp.arange(0, ROPE, 2, dtype=F32) / ROPE)
    ang = pos[:, None] * inv[None, :]
    cs, sn = jnp.cos(ang), jnp.sin(ang)
    zc = lambda n: jnp.zeros((S, n), F32)
    cos_t = jnp.concatenate([zc(64), cs, cs, zc(32)], axis=1)
    sin_t = jnp.concatenate([zc(64), -sn, sn, zc(32)], axis=1)

    bin_p = _pad_in_cols([in_b])
    wr_f =jnp.stack([block_diag(*[lru_wr[l, b] for b in range(LRU_NB)]) for l in range(DEPTH)]).astype(MXU)
    wi_f = jnp.stack([block_diag(*[lru_wi[l, b] for b in range(LRU_NB)]) for l in range(DEPTH)]).astype(MXU)
    ws_b = gm_ws.astype(MXU)
    wst_b = gm_ws.transpose(0, 1, 3, 2).astype(MXU)
    bs_t = gm_bs.transpose(0, 2, 1)
    row = lambda v: v.reshape(1, -1)

    saved = []
    xs = x0
    h1 = None
    for l in range(DEPTH):
        sh1, sc1, g1, sh2, sc2, g2 = [row(mod[l, i * D:(i + 1) * D]) for i in range(6)]
        if l == 0:
            h1 = _lnmod_fwd(xs, sc1, sh1, f"lnmod_fwd{l}")
        lw = LW[l]
        z = _mm(h1, lw["win"], "nn", f"in_proj{l}", rows=[row(bin_p[l])], epi=lambda acc, b: (acc + b,))
        ya = _gmlp_fwd(z, row(gm_ln_g[l]), row(gm_ln_b[l]), ws_b[l], bs_t[l], f"gmlp_fwd{l}")
        qf, kf, vf, vtf = _mla_prep_fwd(z, cos_t, sin_t, row(mla_qnorm_g[l]), row(mla_kvnorm_g[l]), lw["w1q"], lw["w2q"], lw["wk"],
                                        lw["wv"], lw["wvt"], f"mla_prep_fwd{l}")
        if l == 0:
            ob, o32, lse, rode = _attn_fwd(qf, kf, vtf, f"attn_fwd{l}", gather=ride_w)
            lw.update(late_weights(rode[:, :rows_l]))
            LW[1] = {**early_weights(rode[:, rows_l:rows_l + rows_e]), **late_weights(rode[:, rows_l + rows_e:rows_l + rows_e + rows_l])}
        else:
            ob, o32, lse = _attn_fwd(qf, kf, vtf, f"attn_fwd{l}")
        lru_args = (conv_w_f[l], row(lru_conv_b[l]), wr_f[l], row(lru_br[l]), wi_f[l], row(lru_bi[l]), row(lru_lambda[l]))
        yc, hl = _lru_fwd(z, *lru_args, f"lru_fwd{l}")
        merged, pa, pb, pc = _branch_fwd(z, ya, ob, yc, lw["wa"], lw["wb"], lw["wc"], f"branch_fwd{l}")
        mix = _mm(merged, lw["mix"], "nn", f"mix_out{l}")
        x1, h2 = _res_ln_fwd(xs, mix, g1, row(ln_g_f[l, 0]), row(ln_b_f[l, 0]), f"res_ln_a{l}", mod=(sc2, sh2))
        a1, r2 = _mm(h2, lw["w1"], "nn", f"ffn_up{l}", out_dtypes=(F32, MXU), rows=[row(ffn_b1[l])],
                     epi=lambda acc, b: (acc + b, jnp.square(jnp.maximum(acc + b, 0.0))))
        f = _mm(r2, lw["w2"], "nn", f"ffn_down{l}", rows=[row(ffn_b2[l])], epi=lambda acc, b: (acc + b,))
        if l + 1 < DEPTH:
            nsh1, nsc1 = row(mod[l + 1, 0:D]), row(mod[l + 1, D:2 * D])
            x2, h1n = _res_ln_fwd(x1, f, g2, row(ln_g_f[l, 1]), row(ln_b_f[l, 1]), f"res_ln_b{l}", mod=(nsc1, nsh1))
        else:
            (x2,), h1n = _res_ln_fwd(x1, f, g2, row(ln_g_f[l, 1]), row(ln_b_f[l, 1]), f"res_ln_b{l}"), None
        saved.append(dict(x_in=xs, h1=h1, z=z, ya=ya, qf=qf, kf=kf, vf=vf, ob=ob, o32=o32, lse=lse, yc=yc, hl=hl, merged=merged,
                          pa=pa, pb=pb, pc=pc, mix=mix, x1=x1, h2=h2, a1=a1, r2=r2, f=f, x2=x2, lru_args=lru_args,
                          mods=(sh1, sc1, g1, sh2, sc2, g2)))
        xs, h1 = x2, h1n

    dy, sq = _loss_fwd(xs, tgt)
    loss = lax.psum(0.5 / D * jnp.sum(sq), ("x", "y", "c"))

    G = {}
    dmods = []
    dres, hpath = dy, None
    wcols = in_w.shape[2]

    def grad_shards(gl, dtype, group):
        shard_of = lambda n, ax, k: (_unpad_in_cols(gl[n], k * wcols, (k + 1) * wcols) if n == "in_w" else jnp.split(gl[n], 4, axis=ax)[k])
        return [_pack({n: shard_of(n, ax, k).astype(dtype) for n, ax in group}, group) for k in range(4)]

    landed = None
    for l in reversed(range(DEPTH)):
        sv = saved[l]
        lw = LW[l]
        sh1, sc1, g1, sh2, sc2, g2 = sv["mods"]
        gl = {}
        nb = _node_bwd(f"node_b{l}", dres, hpath=hpath, upath=(sv["x1"], sv["f"], g2, row(ln_g_f[l, 1])))
        if hpath is not None:
            dmods[-1]["sc1"], dmods[-1]["sh1"] = nb["dsc"], nb["dsh"]
        dm = dict(g2=nb["dg"])
        ln_g_l1, ln_b_l1 = nb["dlng"], nb["dlnb"]
        df = nb["dbr"]
        gl["ffn_b2"] = nb["dbrsum"][0]
        gl["ffn_w2"] = _mm(sv["r2"], df, "tn", f"d_ffn_w2_{l}")
        da1, db1 = _mm(df, lw["w2"], "nt", f"d_ffn_act{l}", out_dtypes=(MXU,), extras=[sv["a1"]], colsum=True,
                       epi=lambda acc, a: (acc * (2.0 * jnp.maximum(a, 0.0)),))
        gl["ffn_b1"] = db1[0]
        gl["ffn_w1"] = _mm(sv["h2"], da1, "tn", f"d_ffn_w1_{l}")
        dh2 = _mm(da1, lw["w1"], "nt", f"d_ffn_in{l}")
        na = _node_bwd(f"node_a{l}", nb["du"], hpath=(dh2, sv["x1"], sc2), upath=(sv["x_in"], sv["mix"], g1, row(ln_g_f[l, 0])))
        dm.update(sc2=na["dsc"], sh2=na["dsh"], g1=na["dg"])
        gl["ln_g"] = jnp.concatenate([na["dlng"], ln_g_l1], axis=0)
        gl["ln_b"] = jnp.concatenate([na["dlnb"], ln_b_l1], axis=0)
        dmix = na["dbr"]
        gl["mix_out_w"] = _mm(sv["merged"], dmix, "tn", f"d_mix_w{l}")
        dmerged = _mm(dmix, lw["mix"], "nt", f"d_merged{l}")
        dz, dpa, dpb, dpc, s_gate = _branch_bwd(sv["z"], dmerged, sv["pa"], sv["pb"], sv["pc"], f"branch_bwd{l}")
        dya = _mm(dpa, lw["wa"], "nt", f"d_ya{l}")
        dyc = _mm(dpc, lw["wc"], "nt", f"d_yc{l}")
        dwa = _mm(sv["ya"], dpa, "tn", f"d_wa{l}")
        dwc = _mm(sv["yc"], dpc, "tn", f"d_wc{l}")
        dob, do32, dwb = _heads_bwd(dpb, sv["ob"], lw["wb"], f"heads_bwd{l}")
        gl["branch_w"] = jnp.stack([dwa, dwb.reshape(GM_W, D), dwc])
        dz, dcw, dcb, dwr, dbr_, dwi, dbi, dlam, s_lru = _lru_bwd(sv["z"], sv["hl"], dyc, *sv["lru_args"], dz, f"lru_bwd{l}")
        gl["lru_conv_w"], gl["lru_conv_b"], gl["lru_br"], gl["lru_bi"], gl["lru_lambda"] = dcw, dcb[0], dbr_[0], dbi[0], dlam[0]
        blocks = lambda m: jnp.stack([m[b * 64:(b + 1) * 64, b * 64:(b + 1) * 64] for b in range(LRU_NB)])
        gl["lru_wr"], gl["lru_wi"] = blocks(dwr), blocks(dwi)
        delta = _attn_delta(sv["o32"], do32, f"attn_delta{l}")
        attn_args = (sv["qf"], sv["kf"], sv["vf"], dob, sv["lse"], delta.reshape(HEADS, 1, S), f"attn_bwd{l}")
        if l == 0:
            pieces = zip(grad_shards(gl, jnp.bfloat16, _LATE), grad_shards(G[1], jnp.bfloat16, _EARLY), grad_shards(G[1], jnp.bfloat16, _LATE))
            sent = jnp.stack([jnp.concatenate(p, axis=0) for p in pieces])
            sent = jnp.pad(sent, ((0, 0), (0, (-sent.shape[1]) % RIDE_ALIGN), (0, 0)))
            dqf, dkf, dvf, landed = _attn_bwd(*attn_args, scatter=sent)
        else:
            dqf, dkf, dvf = _attn_bwd(*attn_args)
        dz, dgq, dgkv, dw1, dw2, dwk, dwv, s_mla = _mla_prep_bwd(sv["z"], cos_t, sin_t, row(mla_qnorm_g[l]), row(mla_kvnorm_g[l]),
                                                                 lw["w1q"], lw["w2q"], lw["wk"], lw["wv"], dqf, dkf, dvf, dz,
                                                                 f"mla_prep_bwd{l}")
        gl["mla_qnorm_g"], gl["mla_kvnorm_g"] = dgq[0], dgkv[0]
        dw1 = dw1.reshape(Q_RANK, HEADS, 128)
        dw2 = dw2.reshape(Q_RANK, HEADS, 128)
        gl["mla_wuq"] = jnp.concatenate([dw1[..., :NOPE], dw1[..., NOPE:NOPE + ROPE] + _swap16(dw2[..., NOPE:NOPE + ROPE])],
                                        axis=-1).reshape(Q_RANK, HEADS * (NOPE + ROPE))
        gl["mla_wukv"] = jnp.concatenate([dwk.reshape(KV_RANK, HEADS, 128)[..., :NOPE], dwv.transpose(1, 0, 2)],
                                         axis=-1).reshape(KV_RANK, HEADS * (NOPE + VD))
        dz, dglg, dglb, dws, dbs, s_gm = _gmlp_bwd(sv["z"], dya, row(gm_ln_g[l]), row(gm_ln_b[l]), ws_b[l], wst_b[l], bs_t[l], dz,
                                                   f"gmlp_bwd{l}")
        gl["gm_ln_g"], gl["gm_ln_b"], gl["gm_ws"], gl["gm_bs"] = dglg[0], dglb[0], dws, dbs[:, :GM_G].T
        gl["in_b"] = _unpad_in_cols(jnp.concatenate([s_gate, s_gm, s_lru, s_mla], axis=1))[0]
        gl["in_w"] = _mm(sv["h1"], dz, "tn", f"d_in_w{l}")
        dh1 = _mm(dz, lw["win"], "nt", f"d_h1_{l}")
        dmods.append(dm)
        dres, hpath = na["du"], (dh1, sv["x_in"], sc1)
        G[l] = gl
    n0 = _node_bwd("node_in", dres, hpath=hpath)
    dmods[-1]["sc1"], dmods[-1]["sh1"] = n0["dsc"], n0["dsh"]
    grad_x = n0["dx"][None]
    dmods = dmods[::-1]
    dmod = jnp.stack([jnp.concatenate([dmods[l][k] for k in ("sh1", "sc1", "g1", "sh2", "sc2", "g2")], axis=1)[0]
                      for l in range(DEPTH)])
    grads = {n: jnp.stack([G[l][n] for l in range(DEPTH)]) for n in _SMALL if n != "ada_b"}
    grads["ada_b"] = dmod

    half = ci.reshape(1).astype(jnp.int32)
    rode_sum = _land_sum(lax.dynamic_index_in_dim(sent, kme, axis=0, keepdims=False), landed, half, "rs_ride_sum")
    gsh = grad_shards(G[0], F32, _EARLY)
    from_sib = _sibling_swap(gsh, "rs_pair")
    pair, pair_b = _pair_add(gsh, from_sib, half, "rs_pair_add")
    from_chips = _chip_scatter(pair_b, "rs_chips")
    own = lax.dynamic_index_in_dim(pair, kme, axis=0, keepdims=False)
    early0_sum = _chip_sum(own, from_chips, half, "rs_chip_sum")
    rode_g = _sibling_join(rode_sum, "rs_join_ride")
    layer0 = {**_unpack(_sibling_join(early0_sum, "rs_join_early"), shard_shapes, _EARLY), **_unpack(rode_g[:rows_l], shard_shapes, _LATE)}
    layer1 = {**_unpack(rode_g[rows_l:rows_l + rows_e], shard_shapes, _EARLY),
              **_unpack(rode_g[rows_l + rows_e:rows_l + rows_e + rows_l], shard_shapes, _LATE)}
    gbig = {n: jnp.stack([layer0[n], layer1[n]]) for n, _ in _BIG}

    small_shapes = [grads[n].shape for n in _SMALL]
    gsm_all = _allgather8(_flat128([grads[n] for n in _SMALL]), "ag_small_grads")
    gsm = _unflat(_sum8(gsm_all, "sum_small_grads"), small_shapes)
    gsmall = dict(zip(_SMALL, gsm))
    n_mod = DEPTH * 6 * D
    dmod_all = gsm_all[:, :n_mod // 128].reshape(8, DEPTH, 6 * D).transpose(1, 0, 2)
    dmod_sh = lax.dynamic_slice_in_dim(dmod_all, kme * (6 * D // 4), 6 * D // 4, axis=2)
    g_ada_w = _ada_w_grad(c_all.T, dmod_sh, "d_ada_w")
    quarter = lambda g, ax: lax.dynamic_slice_in_dim(g, kme * (g.shape[ax] // 4), g.shape[ax] // 4, axis=ax)
    gsmall["lru_conv_w"] = quarter(gsmall["lru_conv_w"], 2)
    gsmall["ln_g"] = quarter(gsmall["ln_g"], 2)
    gsmall["ln_b"] = quarter(gsmall["ln_b"], 2)

    grad = dict(gbig)
    grad.update(gsmall)
    grad["ada_w"] = g_ada_w

    delta, new_m, new_v = {}, {}, {}
    for n in _NAMES:
        shp = W[n].shape
        two = lambda a: a.reshape(-1, shp[-1])
        d_, m_, v_ = _adamw(two(W[n]), two(grad[n]), two(M[n]), two(V[n]), f"adamw_{n}")
        delta[n], new_m[n], new_v[n] = d_.reshape(shp), m_.reshape(shp), v_.reshape(shp)

    return (loss, grad_x, *[grad[n] for n in _NAMES], *[delta[n] for n in _NAMES], *[new_m[n] for n in _NAMES],
            *[new_v[n] for n in _NAMES])
```
